```python
import math
import jax, jax.numpy as jnp
from jax import lax
import numpy as np

D_MODEL = 1024
BATCH = 4
SEQ = 4096
DEPTH = 2
DEC_BATCH = 128
DEC_SEQ = 4
PAST_LEN = 2048
PAGE_SIZE = 128

HEAD_DIM = 64
N_MIXERS = 4
HEADS_PER_MIXER = D_MODEL // (N_MIXERS * HEAD_DIM)
N_HEADS = N_MIXERS * HEADS_PER_MIXER
MIX_WIDTH = HEADS_PER_MIXER * HEAD_DIM
D_QKV = N_HEADS * HEAD_DIM
DIFF_HALF = HEAD_DIM // 2
N_IN = 3 * D_QKV + HEADS_PER_MIXER + N_MIXERS * D_MODEL
D_FF = 4 * D_MODEL
MOBA_BLOCK = 256
MOBA_TOPK = 3
MOBA_QBLK = 32
QBLK = 128
N_BUCKETS = 32
MAX_DISTANCE = 128
N_BIAS_HEADS = 2 * HEADS_PER_MIXER
FORGET_BIAS_INIT = 2.0
EPS = 1e-6
NEG = -1e30
F32 = jnp.float32

kernel_name = 'hybrid_gated_moba_fox_stick_diff_decode'


def rms_normalize(a):
    a32 = a.astype(F32)
    return (a32 * lax.rsqrt(jnp.mean(a32 * a32, axis=-1, keepdims=True) + EPS)).astype(a.dtype)


def rmsnorm(a, g):
    return rms_normalize(a) * g


def half_rms(a):
    b, t, h, dd = a.shape
    return rms_normalize(a.reshape(b, t, h, 2, DIFF_HALF)).reshape(b, t, h, dd)


def t5_bucket(rel):
    n = jnp.maximum(rel, 0)
    max_exact = N_BUCKETS // 2
    nf = jnp.maximum(n, 1).astype(F32)
    large = max_exact + (jnp.log(nf / max_exact) / math.log(MAX_DISTANCE / max_exact)
                         * (N_BUCKETS - max_exact)).astype(jnp.int32)
    return jnp.where(n < max_exact, n, jnp.minimum(large, N_BUCKETS - 1))


def sweep_queries(fn, q, q_pos, blk):
    b, t = q.shape[:2]
    blk = math.gcd(t, blk)
    n = t // blk
    qb = jnp.moveaxis(q.reshape((b, n, blk) + q.shape[2:]), 1, 0)
    out = lax.map(lambda a: fn(a[0], a[1]), (qb, q_pos.reshape(n, blk)))
    out = jnp.moveaxis(out, 0, 1)
    return out.reshape((b, t) + out.shape[3:])


def moba_attention(q, k, v, q_pos, bias_table):
    b, tk, h, d = k.shape
    nb = -(-tk // MOBA_BLOCK)
    pad = ((0, 0), (0, nb * MOBA_BLOCK - tk), (0, 0), (0, 0))
    kb = jnp.moveaxis(jnp.pad(k, pad).reshape(b, nb, MOBA_BLOCK, h, d), 3, 1)
    vb = jnp.moveaxis(jnp.pad(v, pad).reshape(b, nb, MOBA_BLOCK, h, d), 3, 1)
    means = jnp.mean(kb.astype(F32), axis=3)
    n_sel = min(MOBA_TOPK, nb)
    bi = jnp.arange(b)[:, None, None, None]
    hi = jnp.arange(h)[None, :, None, None]
    offs = jnp.arange(MOBA_BLOCK, dtype=jnp.int32)
    blk_ids = jnp.arange(nb, dtype=jnp.int32)
    scale = d ** -0.5

    def block(qb, qp):
        tb = qp.shape[0]
        own = qp // MOBA_BLOCK
        gate = jnp.einsum('bqhd,bhnd->bhqn', qb.astype(F32), means)
        gate = jnp.where(blk_ids[None, :] < own[:, None], gate, NEG)
        _, top = lax.top_k(gate, n_sel)
        own_b = jnp.broadcast_to(own[None, None, :, None], (b, h, tb, 1))
        idx = jnp.concatenate([top.astype(jnp.int32), own_b], axis=-1)
        valid = jnp.concatenate([top < own[None, None, :, None],
                                 jnp.ones(own_b.shape, dtype=bool)], axis=-1)
        gk = kb[bi, hi, idx]
        gv = vb[bi, hi, idx]
        rel = qp[None, None, :, None, None] - (idx[..., None] * MOBA_BLOCK + offs)
        bias = bias_table[hi[..., None], t5_bucket(rel)]
        s = jnp.einsum('bqhd,bhqnkd->bhqnk', qb, gk, preferred_element_type=F32) * scale + bias
        s = jnp.where(valid[..., None] & (rel >= 0), s, NEG)
        p = jax.nn.softmax(s.reshape(b, h, tb, -1), axis=-1).reshape(s.shape)
        return jnp.einsum('bhqnk,bhqnkd->bqhd', p.astype(gv.dtype), gv)

    return sweep_queries(block, q, q_pos, MOBA_QBLK)


def fox_attention(q, k, v, logf, q_pos):
    tk = k.shape[1]
    k_pos = jnp.arange(tk, dtype=jnp.int32)
    c = jnp.moveaxis(jnp.cumsum(logf.astype(F32), axis=1), 1, 2)
    scale = HEAD_DIM ** -0.5

    def block(qb, qp):
        s = jnp.einsum('bqhd,bkhd->bhqk', qb, k, preferred_element_type=F32) * scale
        s = s + jnp.take(c, qp, axis=2)[..., None] - c[:, :, None, :]
        s = jnp.where(k_pos[None, :] <= qp[:, None], s, NEG)
        p = jax.nn.softmax(s, axis=-1)
        return jnp.einsum('bhqk,bkhd->bqhd', p.astype(v.dtype), v)

    return sweep_queries(block, q, q_pos, QBLK)


def stick_breaking_attention(q, k, v, q_pos):
    tk = k.shape[1]
    k_pos = jnp.arange(tk, dtype=jnp.int32)
    scale = HEAD_DIM ** -0.5

    def block(qb, qp):
        z = jnp.einsum('bqhd,bkhd->bhqk', qb, k, preferred_element_type=F32) * scale
        past = k_pos[None, :] < qp[:, None]
        log_beta = jax.nn.log_sigmoid(z)
        log_keep = jnp.where(past, jax.nn.log_sigmoid(-z), 0.0)
        later = lax.cumsum(log_keep, axis=3, reverse=True) - log_keep
        a = jnp.where(past, jnp.exp(log_beta + later), 0.0)
        return jnp.einsum('bhqk,bkhd->bqhd', a.astype(v.dtype), v)

    return sweep_queries(block, q, q_pos, QBLK)


def diff_attention(q, k, v, q_pos, bias_table, lam, lam_init, g_sub):
    tk = k.shape[1]
    k_pos = jnp.arange(tk, dtype=jnp.int32)
    k1, k2 = k[..., :DIFF_HALF], k[..., DIFF_HALF:]
    scale = DIFF_HALF ** -0.5

    def block(qb, qp):
        rel = qp[:, None] - k_pos[None, :]
        bias = bias_table[:, t5_bucket(rel)][None]
        causal = rel >= 0

        def probs(qh, kh):
            s = jnp.einsum('bqhd,bkhd->bhqk', qh, kh, preferred_element_type=F32) * scale + bias
            return jax.nn.softmax(jnp.where(causal, s, NEG), axis=-1)

        p = probs(qb[..., :DIFF_HALF], k1) - lam * probs(qb[..., DIFF_HALF:], k2)
        return jnp.einsum('bhqk,bkhd->bqhd', p.astype(v.dtype), v)

    o = sweep_queries(block, q, q_pos, QBLK)
    return rms_normalize(o) * g_sub * (1.0 - lam_init)


def with_past(new, pool, l, page_table, h0, h1):
    if pool is None:
        return new
    past = pool[l, page_table, :, h0:h1]
    past = past.reshape((past.shape[0], -1) + past.shape[3:])
    return jnp.concatenate([past.astype(new.dtype), new], axis=1)


def trunk_layer(x, q_pos, l, past, weights):
    (rel_bias, w_in, b_forget, g_q, g_k, lam, g_sub, w_branch, w_o,
     g_mix, g_mlp, w_up, w_down) = weights
    ck, cv, cf, pt = past
    b, t, _ = x.shape
    hpm = HEADS_PER_MIXER
    h = rmsnorm(x, g_mix[l])
    proj = h @ w_in[l]
    q = proj[..., :D_QKV].reshape(b, t, N_HEADS, HEAD_DIM)
    k = proj[..., D_QKV:2 * D_QKV].reshape(b, t, N_HEADS, HEAD_DIM)
    v = proj[..., 2 * D_QKV:3 * D_QKV].reshape(b, t, N_HEADS, HEAD_DIM)
    logf = jax.nn.log_sigmoid(proj[..., 3 * D_QKV:3 * D_QKV + hpm] + b_forget[l])
    gates = jax.nn.sigmoid(proj[..., 3 * D_QKV + hpm:].reshape(b, t, N_MIXERS, D_MODEL))

    def heads(a, g):
        return a[:, :, g * hpm:(g + 1) * hpm]

    q_moba, k_moba = rms_normalize(heads(q, 0)) * g_q[l, 0], rms_normalize(heads(k, 0)) * g_k[l, 0]
    q_fox, k_fox = rms_normalize(heads(q, 1)) * g_q[l, 1], rms_normalize(heads(k, 1)) * g_k[l, 1]
    q_sb, k_sb = heads(q, 2), heads(k, 2)
    q_dif, k_dif = half_rms(heads(q, 3)) * g_q[l, 2], half_rms(heads(k, 3)) * g_k[l, 2]
    k_new = jnp.concatenate([k_moba, k_fox, k_sb, k_dif], axis=2)

    def kv(g, k_g):
        h0, h1 = g * hpm, (g + 1) * hpm
        return with_past(k_g, ck, l, pt, h0, h1), with_past(heads(v, g), cv, l, pt, h0, h1)

    ka, va = kv(0, k_moba)
    kb_, vb_ = kv(1, k_fox)
    kc, vc = kv(2, k_sb)
    kd, vd = kv(3, k_dif)
    fa = with_past(logf, cf, l, pt, 0, hpm)

    lm = lam[l].astype(F32)
    lam_init = 0.8 - 0.6 * math.exp(-0.3 * l)
    lam_val = jnp.exp(jnp.sum(lm[0] * lm[1])) - jnp.exp(jnp.sum(lm[2] * lm[3])) + lam_init

    o_moba = moba_attention(q_moba, ka, va, q_pos, rel_bias[:, :hpm].T)
    o_fox = fox_attention(q_fox, kb_, vb_, fa, q_pos)
    o_sb = stick_breaking_attention(q_sb, kc, vc, q_pos)
    o_dif = diff_attention(q_dif, kd, vd, q_pos, rel_bias[:, hpm:2 * hpm].T, lam_val, lam_init, g_sub[l])

    o = jnp.stack([o_moba, o_fox, o_sb, o_dif], axis=2).reshape(b, t, N_MIXERS, MIX_WIDTH)
    branch = jnp.einsum('btmc,mcd->btmd', o, w_branch[l])
    x = x + jnp.sum(gates * branch, axis=2) @ w_o[l]
    hm = rmsnorm(x, g_mlp[l])
    x = x + jnp.square(jax.nn.relu(hm @ w_up[l])) @ w_down[l]
    return x, k_new, v, logf


def setup_inputs(seed: int = 0) -> dict:
    key = jax.random.key(seed)
    ks = jax.random.split(key, 20)
    nrm = jax.random.normal
    n_pages = PAST_LEN // PAGE_SIZE
    n_used = DEC_BATCH * n_pages
    n_phys = n_used + max(1, n_used // 4)
    x_prompt = nrm(ks[0], (BATCH, SEQ, D_MODEL), F32)
    x_sample = nrm(ks[1], (DEC_BATCH, DEC_SEQ, D_MODEL), F32)
    cache_k = nrm(ks[2], (DEPTH, n_phys, PAGE_SIZE, N_HEADS, HEAD_DIM), F32)
    cache_v = nrm(ks[3], (DEPTH, n_phys, PAGE_SIZE, N_HEADS, HEAD_DIM), F32)
    cache_logf = jax.nn.log_sigmoid(FORGET_BIAS_INIT + nrm(ks[4], (DEPTH, n_phys, PAGE_SIZE, HEADS_PER_MIXER), F32))
    page_table = jax.random.permutation(ks[5], n_phys)[:n_used].reshape(DEC_BATCH, n_pages).astype(jnp.int32)
    rel_bias = 0.5 * nrm(ks[6], (N_BUCKETS, N_BIAS_HEADS), F32)
    w_in = nrm(ks[7], (DEPTH, D_MODEL, N_IN), F32) * D_MODEL ** -0.5
    b_forget = FORGET_BIAS_INIT + 0.1 * nrm(ks[8], (DEPTH, HEADS_PER_MIXER), F32)
    g_q = 1.0 + 0.02 * nrm(ks[9], (DEPTH, 3, HEAD_DIM), F32)
    g_k = 1.0 + 0.02 * nrm(ks[10], (DEPTH, 3, HEAD_DIM), F32)
    lam = 0.1 * nrm(ks[11], (DEPTH, 4, DIFF_HALF), F32)
    g_sub = 1.0 + 0.02 * nrm(ks[12], (DEPTH, HEAD_DIM), F32)
    w_branch = nrm(ks[13], (DEPTH, N_MIXERS, MIX_WIDTH, D_MODEL), F32) * MIX_WIDTH ** -0.5
    w_o = nrm(ks[14], (DEPTH, D_MODEL, D_MODEL), F32) * D_MODEL ** -0.5
    g_mix = 1.0 + 0.02 * nrm(ks[15], (DEPTH, D_MODEL), F32)
    g_mlp = 1.0 + 0.02 * nrm(ks[16], (DEPTH, D_MODEL), F32)
    w_up = nrm(ks[17], (DEPTH, D_MODEL, D_FF), F32) * D_MODEL ** -0.5
    w_down = nrm(ks[18], (DEPTH, D_FF, D_MODEL), F32) * D_FF ** -0.5
    return {'x_prompt': x_prompt, 'x_sample': x_sample, 'cache_k': cache_k, 'cache_v': cache_v,
            'cache_logf': cache_logf, 'page_table': page_table, 'rel_bias': rel_bias, 'w_in': w_in,
            'b_forget': b_forget, 'g_q': g_q, 'g_k': g_k, 'lam': lam, 'g_sub': g_sub,
            'w_branch': w_branch, 'w_o': w_o, 'g_mix': g_mix, 'g_mlp': g_mlp, 'w_up': w_up,
            'w_down': w_down}


def reference(x_prompt, x_sample, cache_k, cache_v, cache_logf, page_table, rel_bias, w_in,
              b_forget, g_q, g_k, lam, g_sub, w_branch, w_o, g_mix, g_mlp, w_up, w_down):
    weights = (rel_bias, w_in, b_forget, g_q, g_k, lam, g_sub, w_branch, w_o,
               g_mix, g_mlp, w_up, w_down)
    past_len = page_table.shape[1] * cache_k.shape[2]
    pos_p = jnp.arange(x_prompt.shape[1], dtype=jnp.int32)
    pos_s = past_len + jnp.arange(x_sample.shape[1], dtype=jnp.int32)
    no_past = (None, None, None, None)
    paged = (cache_k, cache_v, cache_logf, page_table)
    hp, hs = x_prompt, x_sample
    kps, vps, fps, kss, vss, fss = [], [], [], [], [], []
    for l in range(DEPTH):
        hp, kp, vp, fp = trunk_layer(hp, pos_p, l, no_past, weights)
        hs, ksm, vsm, fsm = trunk_layer(hs, pos_s, l, paged, weights)
        kps.append(kp); vps.append(vp); fps.append(fp)
        kss.append(ksm); vss.append(vsm); fss.append(fsm)
    return (hp, hs, jnp.stack(kps), jnp.stack(vps), jnp.stack(fps),
            jnp.stack(kss), jnp.stack(vss), jnp.stack(fss))
```

```python
import functools
import math

import numpy as np
import jax
import jax.numpy as jnp
from jax import lax
from jax.experimental import pallas as pl
from jax.experimental.pallas import tpu as pltpu

F32 = jnp.float32
BF16 = jnp.bfloat16

D_MODEL = 1024
HEAD_DIM = 64
N_MIXERS = 4
HPM = 4
N_HEADS = 16
MIX_WIDTH = HPM * HEAD_DIM
D_QKV = N_HEADS * HEAD_DIM
DIFF_HALF = HEAD_DIM // 2
D_FF = 4 * D_MODEL
MOBA_BLOCK = 256
MOBA_TOPK = 3
N_BUCKETS = 32
MAX_DISTANCE = 128
EPS = 1e-6
NEG = -1e30
LANES = 128
PAIR = 2 * HEAD_DIM
VMEM_LIMIT = 56 * 1024 * 1024


def _params(*sem):
    return pltpu.CompilerParams(dimension_semantics=sem, vmem_limit_bytes=VMEM_LIMIT)


def _nt(a, b):
    return lax.dot_general(a, b, (((1,), (1,)), ((), ())), preferred_element_type=F32)


def _nn(a, b):
    return jnp.dot(a, b, preferred_element_type=F32)


def _split2(x):
    hi = x.astype(BF16)
    lo = (x - hi.astype(F32)).astype(BF16)
    return hi, lo


def _split3(x):
    hi = x.astype(BF16)
    r = x - hi.astype(F32)
    mid = r.astype(BF16)
    lo = (r - mid.astype(F32)).astype(BF16)
    return hi, mid, lo


def _nn_split(x, w, parts):
    pieces = _split2(x) if parts == 2 else _split3(x)
    out = _nn(pieces[0], w)
    for p in pieces[1:]:
        out = out + _nn(p, w)
    return out


def _log_sigmoid(z):
    return jnp.minimum(z, 0.0) - jnp.log1p(jnp.exp(-jnp.abs(z)))


def _const_spec(shape):
    nd = len(shape)
    return pl.BlockSpec(shape, lambda *_: (0,) * nd)


def _t5_bucket_np(rel):
    n = np.maximum(rel, 0)
    max_exact = N_BUCKETS // 2
    nf = np.maximum(n, 1).astype(np.float32)
    large = max_exact + (np.log(nf / np.float32(max_exact)) / np.float32(math.log(MAX_DISTANCE / max_exact))
                         * np.float32(N_BUCKETS - max_exact)).astype(np.int32)
    return np.where(n < max_exact, n, np.minimum(large, N_BUCKETS - 1)).astype(np.int32)


def _qk_group_mats():
    g1 = np.zeros((D_QKV, LANES), np.float32)
    g2 = np.zeros((LANES, D_QKV), np.float32)
    for c in range(D_QKV):
        h = c // HEAD_DIM
        if h < 2 * HPM:
            gid, size = h, HEAD_DIM
        elif h < 3 * HPM:
            continue
        else:
            gid, size = 2 * HPM + (c - 3 * MIX_WIDTH) // DIFF_HALF, DIFF_HALF
        g1[c, gid] = 1.0 / size
        g2[gid, c] = 1.0
    return g1, g2


def _strict_lower(n):
    r = np.arange(n)
    return (r[:, None] > r[None, :]).astype(np.float32)


def _upper_incl(n):
    r = np.arange(n)
    return (r[:, None] <= r[None, :]).astype(np.float32)


def _inproj_kernel(x_ref, gmix_ref, wqkv_ref, wf_ref, bf_ref, g1_ref, g2_ref,
                   qmul_ref, qadd_ref, kmul_ref, kadd_ref,
                   qb_ref, kb_ref, vb_ref, kf_ref, vf_ref, lf_ref):
    x = x_ref[...]
    h = (x * lax.rsqrt(jnp.mean(x * x, axis=-1, keepdims=True) + EPS)) * gmix_ref[...]
    hb = h.astype(BF16)
    g1 = g1_ref[...]
    g2 = g2_ref[...]

    def group_norm(a, mul_ref, add_ref):
        ms = _nn_split(a * a, g1, 2)
        inv = lax.rsqrt(ms + EPS)
        bc = _nn_split(inv, g2, 3)
        return a * (bc * mul_ref[...] + add_ref[...])

    q = _nn(hb, wqkv_ref[:, 0:D_QKV])
    qb_ref[...] = group_norm(q, qmul_ref, qadd_ref).astype(BF16)
    k = _nn(hb, wqkv_ref[:, D_QKV:2 * D_QKV])
    kn = group_norm(k, kmul_ref, kadd_ref)
    kf_ref[...] = kn
    kb_ref[...] = kn.astype(BF16)
    v = _nn(hb, wqkv_ref[:, 2 * D_QKV:3 * D_QKV])
    vf_ref[...] = v
    vb_ref[...] = v.astype(BF16)
    z = _nn(hb, wf_ref[...]) + bf_ref[...]
    lf_ref[...] = _log_sigmoid(z)[:, 0:HPM]


def _inproj(x, gmix, wqkv, wf, bf, g1, g2, qmul, qadd, kmul, kadd, tm):
    rows = x.shape[0]
    row = lambda w: pl.BlockSpec((tm, w), lambda i: (i, 0))
    vec = _const_spec((1, D_QKV))
    return pl.pallas_call(
        _inproj_kernel,
        grid=(rows // tm,),
        in_specs=[row(D_MODEL), _const_spec((1, D_MODEL)), _const_spec((D_MODEL, 3 * D_QKV)),
                  _const_spec((D_MODEL, LANES)), _const_spec((1, LANES)),
                  _const_spec((D_QKV, LANES)), _const_spec((LANES, D_QKV)), vec, vec, vec, vec],
        out_specs=[row(D_QKV), row(D_QKV), row(D_QKV), row(D_QKV), row(D_QKV), row(HPM)],
        out_shape=[jax.ShapeDtypeStruct((rows, D_QKV), BF16)] * 3
        + [jax.ShapeDtypeStruct((rows, D_QKV), F32)] * 2 + [jax.ShapeDtypeStruct((rows, HPM), F32)],
        compiler_params=_params("parallel"),
        name="inproj",
    )(x, gmix, wqkv, wf, bf, g1, g2, qmul, qadd, kmul, kadd)


def _cumsum_kernel(lf_ref, tri_ref, c_ref, *, t, blk):
    tri = tri_ref[...]
    carry = jnp.zeros((8, 1), F32)
    for i in range(t // blk):
        cs = _nn_split(lf_ref[:, i * blk:(i + 1) * blk], tri, 3) + carry
        c_ref[:, i * blk:(i + 1) * blk] = cs
        carry = cs[:, blk - 1:blk]


def _cumsum(lf_t, tri):
    b, r, t = lf_t.shape
    blk = tri.shape[0]
    return pl.pallas_call(
        functools.partial(_cumsum_kernel, t=t, blk=blk),
        grid=(b,),
        in_specs=[pl.BlockSpec((None, r, t), lambda i: (i, 0, 0)), _const_spec((blk, blk))],
        out_specs=pl.BlockSpec((None, r, t), lambda i: (i, 0, 0)),
        out_shape=jax.ShapeDtypeStruct((b, r, t), F32),
        compiler_params=_params("parallel"),
        name="logf_cumsum",
    )(lf_t, tri)


def _means_kernel(k_ref, o_ref, *, nb):
    for n in range(nb):
        o_ref[n:n + 1, :] = jnp.mean(k_ref[n * MOBA_BLOCK:(n + 1) * MOBA_BLOCK, :], axis=0, keepdims=True)


def _block_means(kf):
    b, t, _ = kf.shape
    nb = t // MOBA_BLOCK
    return pl.pallas_call(
        functools.partial(_means_kernel, nb=nb),
        grid=(b,),
        in_specs=[pl.BlockSpec((None, t, MIX_WIDTH), lambda i: (i, 0, 0))],
        out_specs=pl.BlockSpec((None, nb, MIX_WIDTH), lambda i: (i, 0, 0)),
        out_shape=jax.ShapeDtypeStruct((b, nb, MIX_WIDTH), F32),
        compiler_params=_params("parallel"),
        name="moba_block_means",
    )(kf)


def _bias_kernel(tab_ref, idxp_ref, idxs_ref, bp_ref, bs_ref, *, dec_seq):
    idxp = idxp_ref[...]
    idxs = idxs_ref[...]
    row = lax.broadcasted_iota(jnp.int32, idxs.shape, 1)
    for h in range(2 * HPM):
        acc = jnp.zeros(idxp.shape, F32)
        for b in range(N_BUCKETS):
            acc = jnp.where(idxp == b, tab_ref[h, b], acc)
        bp_ref[h] = acc
    for m in range(2):
        acc = jnp.zeros(idxs.shape, F32)
        for hl in range(HPM):
            for b in range(N_BUCKETS):
                acc = jnp.where((idxs == b) & (row // dec_seq == hl), tab_ref[m * HPM + hl, b], acc)
        bs_ref[m] = acc


def _bias_tiles(tab, idxp, idxs, dec_seq):
    return pl.pallas_call(
        functools.partial(_bias_kernel, dec_seq=dec_seq),
        in_specs=[pl.BlockSpec(memory_space=pltpu.SMEM), pl.BlockSpec(memory_space=pltpu.VMEM),
                  pl.BlockSpec(memory_space=pltpu.VMEM)],
        out_specs=[pl.BlockSpec(memory_space=pltpu.VMEM), pl.BlockSpec(memory_space=pltpu.VMEM)],
        out_shape=[jax.ShapeDtypeStruct((2 * HPM,) + idxp.shape, F32),
                   jax.ShapeDtypeStruct((2,) + idxs.shape, F32)],
        compiler_params=pltpu.CompilerParams(vmem_limit_bytes=VMEM_LIMIT),
        name="t5_bias_tiles",
    )(tab, idxp, idxs)


def _lane_is_a():
    return lax.broadcasted_iota(jnp.int32, (1, PAIR), 1) < HEAD_DIM


def _causal(tq, strict):
    r = lax.broadcasted_iota(jnp.int32, (tq, tq), 0)
    c = lax.broadcasted_iota(jnp.int32, (tq, tq), 1)
    return (c < r) if strict else (c <= r)


def _softmax_update(x, s, v, m_sc, l_sc):
    m_old = m_sc[x]
    m_new = jnp.maximum(m_old, jnp.max(s, axis=-1, keepdims=True))
    p = jnp.exp(s - m_new)
    alpha = jnp.exp(m_old - m_new)
    l_sc[x] = alpha * l_sc[x] + jnp.sum(p, axis=-1, keepdims=True)
    m_sc[x] = m_new
    return alpha, _nn(p.astype(BF16), v)


def _fox_kernel(q_ref, k_ref, v_ref, cca_ref, ccb_ref, cra_ref, crb_ref, o_ref, m_sc, l_sc, acc_sc, *, tq):
    qi = pl.program_id(2)
    is_a = _lane_is_a()
    q = q_ref[...]
    qs = (jnp.where(is_a, q, jnp.zeros_like(q)), jnp.where(is_a, jnp.zeros_like(q), q))
    cc = (cca_ref[...], ccb_ref[...])
    cr = (cra_ref, crb_ref)
    m_sc[...] = jnp.full(m_sc.shape, NEG, F32)
    l_sc[...] = jnp.zeros(l_sc.shape, F32)
    acc_sc[...] = jnp.zeros(acc_sc.shape, F32)

    def step(j, masked):
        r0 = pl.multiple_of(j * tq, tq)
        k = k_ref[pl.ds(r0, tq), :]
        v = v_ref[pl.ds(r0, tq), :]
        alphas, pvs = [], []
        for x in range(2):
            s = _nt(qs[x], k) + cc[x] - cr[x][j]
            if masked:
                s = jnp.where(_causal(tq, False), s, NEG)
            a, pv = _softmax_update(x, s, v, m_sc, l_sc)
            alphas.append(a)
            pvs.append(pv)
        acc_sc[...] = acc_sc[...] * jnp.where(is_a, alphas[0], alphas[1]) + jnp.where(is_a, pvs[0], pvs[1])

    step(qi, True)

    def body(j, c):
        step(j, False)
        return c

    lax.fori_loop(0, qi, body, 0)
    o_ref[...] = (acc_sc[...] / jnp.where(is_a, l_sc[0], l_sc[1])).astype(o_ref.dtype)


def _moba_kernel(tab_ref, q_ref, k_ref, v_ref, mean_ref, bta_ref, btb_ref, o_ref,
                 m_sc, l_sc, acc_sc, sel_sc, *, tq, nb):
    hp = pl.program_id(1)
    qi = pl.program_id(2)
    is_a = _lane_is_a()
    q = q_ref[...]
    qs = (jnp.where(is_a, q, jnp.zeros_like(q)), jnp.where(is_a, jnp.zeros_like(q), q))
    bt = (bta_ref, btb_ref)
    far = (tab_ref[2 * hp, N_BUCKETS - 1], tab_ref[2 * hp + 1, N_BUCKETS - 1])
    m_sc[...] = jnp.full(m_sc.shape, NEG, F32)
    l_sc[...] = jnp.zeros(l_sc.shape, F32)
    acc_sc[...] = jnp.zeros(acc_sc.shape, F32)

    means = mean_ref[...]
    mh, ml = _split2(means)
    lane = lax.broadcasted_iota(jnp.int32, (1, LANES), 1)
    for x in range(2):
        gate = _nt(qs[x], mh) + _nt(qs[x], ml)
        cnt = jnp.zeros((tq, LANES), jnp.int32)
        for n in range(nb):
            col = gate[:, n:n + 1]
            beats = (col > gate) | ((col == gate) & (n < lane))
            cnt = cnt + jnp.where(beats, jnp.where(n < qi, 1, 0), 0)
        sel_sc[x] = jnp.where((cnt < MOBA_TOPK) & (lane < qi), 1.0, 0.0)

    def step(j, kind):
        r0 = pl.multiple_of(j * tq, tq)
        k = k_ref[pl.ds(r0, tq), :]
        v = v_ref[pl.ds(r0, tq), :]
        alphas, pvs = [], []
        for x in range(2):
            s = _nt(qs[x], k)
            if kind == "diag":
                s = jnp.where(_causal(tq, False), s + bt[x][0], NEG)
            else:
                s = s + (bt[x][1] if kind == "sub" else far[x])
                picked = jnp.max(jnp.where(lane == j, sel_sc[x], 0.0), axis=-1, keepdims=True)
                s = jnp.where(picked > 0.0, s, NEG)
            a, pv = _softmax_update(x, s, v, m_sc, l_sc)
            alphas.append(a)
            pvs.append(pv)
        acc_sc[...] = acc_sc[...] * jnp.where(is_a, alphas[0], alphas[1]) + jnp.where(is_a, pvs[0], pvs[1])

    step(qi, "diag")

    @pl.when(qi >= 1)
    def _():
        step(qi - 1, "sub")

    def body(j, c):
        step(j, "far")
        return c

    lax.fori_loop(0, jnp.maximum(qi - 1, 0), body, 0)
    o_ref[...] = (acc_sc[...] / jnp.where(is_a, l_sc[0], l_sc[1])).astype(o_ref.dtype)


def _sb_kernel(q_ref, k_ref, v_ref, sl_ref, o_ref, r_sc, acc_sc, *, tq):
    qi = pl.program_id(2)
    is_a = _lane_is_a()
    q = q_ref[...]
    qs = (jnp.where(is_a, q, jnp.zeros_like(q)), jnp.where(is_a, jnp.zeros_like(q), q))
    sl = sl_ref[...]
    r_sc[...] = jnp.zeros(r_sc.shape, F32)
    acc_sc[...] = jnp.zeros(acc_sc.shape, F32)

    def step(j, masked):
        r0 = pl.multiple_of(j * tq, tq)
        k = k_ref[pl.ds(r0, tq), :]
        v = v_ref[pl.ds(r0, tq), :]
        pvs = []
        for x in range(2):
            z = _nt(qs[x], k)
            log_beta = _log_sigmoid(z)
            log_keep = log_beta - z
            if masked:
                past = _causal(tq, True)
                log_keep = jnp.where(past, log_keep, 0.0)
            later = _nn_split(log_keep, sl, 2) + r_sc[x]
            a = jnp.exp(log_beta + later)
            if masked:
                a = jnp.where(past, a, 0.0)
            r_sc[x] = r_sc[x] + jnp.sum(log_keep, axis=-1, keepdims=True)
            pvs.append(_nn(a.astype(BF16), v))
        acc_sc[...] = acc_sc[...] + jnp.where(is_a, pvs[0], pvs[1])

    step(qi, True)

    def body(i, c):
        step(qi - 1 - i, False)
        return c

    lax.fori_loop(0, qi, body, 0)
    o_ref[...] = acc_sc[...].astype(o_ref.dtype)


def _lam_value(lam_ref, lam_init):
    lm = lam_ref[...]
    a = jnp.sum(lm[0:1] * lm[1:2], axis=-1, keepdims=True)
    b = jnp.sum(lm[2:3] * lm[3:4], axis=-1, keepdims=True)
    return jnp.exp(a) - jnp.exp(b) + lam_init


def _diff_kernel(tab_ref, q_ref, k_ref, v_ref, bta_ref, btb_ref, lam_ref, gsub_ref, o_ref,
                 m_sc, l_sc, acc_sc, *, tq, lam_init):
    hp = pl.program_id(1)
    qi = pl.program_id(2)
    lane = lax.broadcasted_iota(jnp.int32, (1, PAIR), 1)
    is_a = lane < HEAD_DIM
    q = q_ref[...]
    zero = jnp.zeros_like(q)
    qs = [jnp.where((lane >= i * DIFF_HALF) & (lane < (i + 1) * DIFF_HALF), q, zero) for i in range(4)]
    bt = (bta_ref, btb_ref)
    far = (tab_ref[HPM + 2 * hp, N_BUCKETS - 1], tab_ref[HPM + 2 * hp + 1, N_BUCKETS - 1])
    m_sc[...] = jnp.full(m_sc.shape, NEG, F32)
    l_sc[...] = jnp.zeros(l_sc.shape, F32)
    acc_sc[...] = jnp.zeros(acc_sc.shape, F32)

    def step(j, kind):
        r0 = pl.multiple_of(j * tq, tq)
        k = k_ref[pl.ds(r0, tq), :]
        v = v_ref[pl.ds(r0, tq), :]
        alphas, pvs = [], []
        for i in range(4):
            x = i // 2
            s = _nt(qs[i], k)
            if kind == "diag":
                s = jnp.where(_causal(tq, False), s + bt[x][0], NEG)
            else:
                s = s + (bt[x][1] if kind == "sub" else far[x])
            a, pv = _softmax_update(i, s, v, m_sc, l_sc)
            alphas.append(a)
            pvs.append(pv)
        for mp in range(2):
            acc_sc[mp] = (acc_sc[mp] * jnp.where(is_a, alphas[mp], alphas[2 + mp])
                          + jnp.where(is_a, pvs[mp], pvs[2 + mp]))

    step(qi, "diag")

    @pl.when(qi >= 1)
    def _():
        step(qi - 1, "sub")

    def body(j, c):
        step(j, "far")
        return c

    lax.fori_loop(0, jnp.maximum(qi - 1, 0), body, 0)
    lam_val = _lam_value(lam_ref, lam_init)
    o = (acc_sc[0] / jnp.where(is_a, l_sc[0], l_sc[2])
         - lam_val * (acc_sc[1] / jnp.where(is_a, l_sc[1], l_sc[3])))
    sq = o * o
    ms_a = jnp.sum(jnp.where(is_a, sq, 0.0), axis=-1, keepdims=True) * (1.0 / HEAD_DIM)
    ms_b = jnp.sum(jnp.where(is_a, 0.0, sq), axis=-1, keepdims=True) * (1.0 / HEAD_DIM)
    inv = jnp.where(is_a, lax.rsqrt(ms_a + EPS), lax.rsqrt(ms_b + EPS))
    o_ref[...] = (((o * inv) * gsub_ref[...]) * (1.0 - lam_init)).astype(o_ref.dtype)


def _pair_specs(t, tq, col0):
    qspec = pl.BlockSpec((None, tq, PAIR), lambda b, hp, qi: (b, qi, col0 + hp))
    kvspec = pl.BlockSpec((None, t, PAIR), lambda b, hp, qi: (b, 0, col0 + hp))
    ospec = pl.BlockSpec((None, tq, PAIR), lambda b, hp, qi: (b, qi, hp))
    return qspec, kvspec, ospec


def _attn_out(b, t):
    return jax.ShapeDtypeStruct((b, t, MIX_WIDTH), BF16)


def _fox_prompt(qb, kb, vb, c_col, c_row, tq):
    b, t, _ = qb.shape
    nq = t // tq
    qspec, kvspec, ospec = _pair_specs(t, tq, 2)
    cc = lambda x: pl.BlockSpec((None, None, tq, 1), lambda bi, hp, qi: (bi, 2 * hp + x, qi, 0))
    cr = lambda x: pl.BlockSpec((None, None, nq, 1, tq), lambda bi, hp, qi: (bi, 2 * hp + x, 0, 0, 0))
    return pl.pallas_call(
        functools.partial(_fox_kernel, tq=tq),
        grid=(b, 2, nq),
        in_specs=[qspec, kvspec, kvspec, cc(0), cc(1), cr(0), cr(1)],
        out_specs=ospec,
        out_shape=_attn_out(b, t),
        scratch_shapes=[pltpu.VMEM((2, tq, 1), F32), pltpu.VMEM((2, tq, 1), F32), pltpu.VMEM((tq, PAIR), F32)],
        compiler_params=_params("parallel", "parallel", "arbitrary"),
        name="fox_prompt",
    )(qb, kb, vb, c_col, c_col, c_row, c_row)


def _moba_prompt(tab, qb, kb, vb, means, bp, tq):
    b, t, _ = qb.shape
    nq = t // tq
    nb = t // MOBA_BLOCK
    qspec, kvspec, ospec = _pair_specs(t, tq, 0)
    mspec = pl.BlockSpec((None, LANES, PAIR), lambda bi, hp, qi: (bi, 0, hp))
    bt = lambda x: pl.BlockSpec((None, 2, tq, tq), lambda bi, hp, qi: (2 * hp + x, 0, 0, 0))
    return pl.pallas_call(
        functools.partial(_moba_kernel, tq=tq, nb=nb),
        grid=(b, 2, nq),
        in_specs=[pl.BlockSpec(memory_space=pltpu.SMEM), qspec, kvspec, kvspec, mspec, bt(0), bt(1)],
        out_specs=ospec,
        out_shape=_attn_out(b, t),
        scratch_shapes=[pltpu.VMEM((2, tq, 1), F32), pltpu.VMEM((2, tq, 1), F32), pltpu.VMEM((tq, PAIR), F32),
                        pltpu.VMEM((2, tq, LANES), F32)],
        compiler_params=_params("parallel", "parallel", "arbitrary"),
        name="moba_prompt",
    )(tab, qb, kb, vb, means, bp, bp)


def _sb_prompt(qb, kb, vb, sl, tq):
    b, t, _ = qb.shape
    qspec, kvspec, ospec = _pair_specs(t, tq, 4)
    return pl.pallas_call(
        functools.partial(_sb_kernel, tq=tq),
        grid=(b, 2, t // tq),
        in_specs=[qspec, kvspec, kvspec, _const_spec((tq, tq))],
        out_specs=ospec,
        out_shape=_attn_out(b, t),
        scratch_shapes=[pltpu.VMEM((2, tq, 1), F32), pltpu.VMEM((tq, PAIR), F32)],
        compiler_params=_params("parallel", "parallel", "arbitrary"),
        name="sb_prompt",
    )(qb, kb, vb, sl)


def _diff_prompt(tab, qb, kb, vb, bp, lam, gsub_pair, lam_init, tq):
    b, t, _ = qb.shape
    qspec, kvspec, ospec = _pair_specs(t, tq, 6)
    bt = lambda x: pl.BlockSpec((None, 2, tq, tq), lambda bi, hp, qi: (HPM + 2 * hp + x, 0, 0, 0))
    return pl.pallas_call(
        functools.partial(_diff_kernel, tq=tq, lam_init=lam_init),
        grid=(b, 2, t // tq),
        in_specs=[pl.BlockSpec(memory_space=pltpu.SMEM), qspec, kvspec, kvspec, bt(0), bt(1),
                  _const_spec((4, DIFF_HALF)), _const_spec((1, PAIR))],
        out_specs=ospec,
        out_shape=_attn_out(b, t),
        scratch_shapes=[pltpu.VMEM((4, tq, 1), F32), pltpu.VMEM((4, tq, 1), F32), pltpu.VMEM((2, tq, PAIR), F32)],
        compiler_params=_params("parallel", "parallel", "arbitrary"),
        name="diff_prompt",
    )(tab, qb, kb, vb, bp, bp, lam, gsub_pair)


N_SLOTS = 20


def _sample_masks(dec_seq):
    rows = N_SLOTS * dec_seq
    qmask = np.zeros((rows, D_QKV), np.float32)
    for r in range(rows):
        slot = r // dec_seq
        if slot < 3 * HPM:
            qmask[r, slot * HEAD_DIM:(slot + 1) * HEAD_DIM] = 1.0
        elif slot < 4 * HPM:
            qmask[r, slot * HEAD_DIM:slot * HEAD_DIM + DIFF_HALF] = 1.0
        else:
            c0 = (slot - HPM) * HEAD_DIM + DIFF_HALF
            qmask[r, c0:c0 + DIFF_HALF] = 1.0
    omask = np.zeros((N_HEADS * dec_seq, D_QKV), np.float32)
    for r in range(N_HEADS * dec_seq):
        omask[r, (r // dec_seq) * HEAD_DIM:(r // dec_seq + 1) * HEAD_DIM] = 1.0
    return qmask, omask


def _expand_rows(a, dec_seq):
    rows = HPM * dec_seq
    r = lax.broadcasted_iota(jnp.int32, (rows, 1), 0)
    out = jnp.broadcast_to(a[HPM - 1:HPM, :], (rows, a.shape[1]))
    for h in range(HPM - 2, -1, -1):
        out = jnp.where(r < (h + 1) * dec_seq, a[h:h + 1, :], out)
    return out


def _softmax_pages(s_pages):
    m = s_pages[0].max(axis=-1, keepdims=True)
    for s in s_pages[1:]:
        m = jnp.maximum(m, s.max(axis=-1, keepdims=True))
    ps = [jnp.exp(s - m) for s in s_pages]
    l = ps[0].sum(axis=-1, keepdims=True)
    for p in ps[1:]:
        l = l + p.sum(axis=-1, keepdims=True)
    return [p / l for p in ps]


def _sample_kernel(*refs, n_pages, dec_seq, lam_init):
    pt_ref = refs[0]
    del pt_ref
    (q_ref, kn_ref, vn_ref, lfn_ref, bs_ref, qmask_ref, omask_ref, sl_ref, lam_ref,
     gd1_ref, gd2_ref, gmul_ref, gadd_ref) = refs[1:14]
    k_refs = refs[14:14 + n_pages]
    v_refs = refs[14 + n_pages:14 + 2 * n_pages]
    lf_refs = refs[14 + 2 * n_pages:14 + 3 * n_pages]
    o_ref = refs[14 + 3 * n_pages]
    s_sc, p_sc = refs[14 + 3 * n_pages + 1:]

    g = HPM * dec_seq
    npg = n_pages + 1
    q = q_ref[...]
    qbd = (jnp.concatenate([q] * (N_SLOTS * dec_seq // 8), axis=0) * qmask_ref[...]).astype(BF16)
    pad = jnp.zeros((LANES - kn_ref.shape[0], D_QKV), BF16)
    for p in range(npg):
        kp = k_refs[p][...].astype(BF16) if p < n_pages else jnp.concatenate([kn_ref[...], pad], axis=0)
        s_sc[p] = _nt(qbd, kp)

    lane = lax.broadcasted_iota(jnp.int32, (1, LANES), 1)
    qrow = lax.broadcasted_iota(jnp.int32, (g, 1), 0) % dec_seq
    new_ok = lane <= qrow
    new_past = lane < qrow
    sl = sl_ref[...]

    def pages(r0):
        return [s_sc[p, r0:r0 + g, :] for p in range(npg)]

    sm = pages(0)
    ppb = MOBA_BLOCK // LANES
    nblk = n_pages // ppb
    gates = []
    for n in range(nblk):
        tot = sm[n * ppb]
        for i in range(1, ppb):
            tot = tot + sm[n * ppb + i]
        gates.append(jnp.sum(tot, axis=-1, keepdims=True))
    s_pages = []
    for n in range(nblk):
        cnt = jnp.zeros((g, 1), jnp.int32)
        for n2 in range(nblk):
            if n2 == n:
                continue
            beats = (gates[n2] > gates[n]) | ((gates[n2] == gates[n]) & (n2 < n))
            cnt = cnt + jnp.where(beats, 1, 0)
        for i in range(ppb):
            p = n * ppb + i
            s_pages.append(jnp.where(cnt < MOBA_TOPK, sm[p] + bs_ref[0, p], NEG))
    s_pages.append(jnp.where(new_ok, sm[n_pages] + bs_ref[0, n_pages], NEG))
    for p, pr in enumerate(_softmax_pages(s_pages)):
        p_sc[p, 0:g, :] = pr.astype(BF16)

    lfn = lfn_ref[...]
    cn_row = jnp.zeros((g, 1), F32)
    bias_new = jnp.zeros((g, LANES), F32)
    for n in range(dec_seq):
        col = lfn[:, n:n + 1]
        cn_row = cn_row + jnp.where(n <= qrow, col, 0.0)
        bias_new = bias_new + jnp.where((lane < n) & (n <= qrow), col, 0.0)
    lfe = [_expand_rows(lf_refs[p][...], dec_seq) for p in range(n_pages)]
    suffix = _nn_split(jnp.concatenate(lfe, axis=0), sl, 3)
    sf = pages(g)
    s_pages = [None] * npg
    s_pages[n_pages] = jnp.where(new_ok, sf[n_pages] + bias_new, NEG)
    after = cn_row
    for p in range(n_pages - 1, -1, -1):
        s_pages[p] = sf[p] + (suffix[p * g:(p + 1) * g, :] + after)
        after = after + jnp.sum(lfe[p], axis=-1, keepdims=True)
    for p, pr in enumerate(_softmax_pages(s_pages)):
        p_sc[p, g:2 * g, :] = pr.astype(BF16)

    ss = pages(2 * g)
    lbs, lks = [], []
    for p in range(npg):
        lb = _log_sigmoid(ss[p])
        lk = lb - ss[p]
        if p == n_pages:
            lk = jnp.where(new_past, lk, 0.0)
        lbs.append(lb)
        lks.append(lk)
    later_in = _nn_split(jnp.concatenate(lks, axis=0), sl, 2)
    after = jnp.zeros((g, 1), F32)
    for p in range(npg - 1, -1, -1):
        a = jnp.exp(lbs[p] + later_in[p * g:(p + 1) * g, :] + after)
        if p == n_pages:
            a = jnp.where(new_past, a, 0.0)
        p_sc[p, 2 * g:3 * g, :] = a.astype(BF16)
        after = after + jnp.sum(lks[p], axis=-1, keepdims=True)

    lam_val = _lam_value(lam_ref, lam_init)
    maps = []
    for mp in range(2):
        sd = pages((3 + mp) * g)
        s_pages = [sd[p] + bs_ref[1, p] for p in range(n_pages)]
        s_pages.append(jnp.where(new_ok, sd[n_pages] + bs_ref[1, n_pages], NEG))
        maps.append(_softmax_pages(s_pages))
    for p in range(npg):
        p_sc[p, 3 * g:4 * g, :] = (maps[0][p] - lam_val * maps[1][p]).astype(BF16)

    padv = jnp.zeros((LANES - vn_ref.shape[0], D_QKV), BF16)
    acc = _nn(p_sc[n_pages], jnp.concatenate([vn_ref[...], padv], axis=0))
    for p in range(n_pages):
        acc = acc + _nn(p_sc[p], v_refs[p][...].astype(BF16))
    rm = acc * omask_ref[...]
    y = rm[0:8, :]
    for i in range(1, N_HEADS * dec_seq // 8):
        y = y + rm[8 * i:8 * (i + 1), :]
    o8 = y + pltpu.roll(y, dec_seq, 0)
    ms = _nn_split(o8 * o8, gd1_ref[...], 2)
    bc = _nn_split(lax.rsqrt(ms + EPS), gd2_ref[...], 3)
    o8 = o8 * (bc * gmul_ref[...] + gadd_ref[...])
    o_ref[...] = o8[0:dec_seq, :]


def _sample_attention(layer, page_table, q8, kn, vn, lfn, bs, consts, lam, gmul, gadd,
                      cache_k, cache_v, cache_lft, lam_init):
    db, n_pages = page_table.shape
    dec_seq = lfn.shape[1] // HPM
    page = cache_k.shape[2]
    qmask, omask, sl, gd1, gd2 = consts
    rows = N_SLOTS * dec_seq

    def batch_spec(shape):
        nd = len(shape)
        return pl.BlockSpec((None,) + shape, lambda b, pt: (b,) + (0,) * nd)

    def const(shape):
        nd = len(shape)
        return pl.BlockSpec(shape, lambda b, pt: (0,) * nd)

    def paged(width_rows, width_cols):
        return [pl.BlockSpec((None, None, width_rows, width_cols),
                             functools.partial(lambda b, pt, j: (layer, pt[b, j], 0, 0), j=j))
                for j in range(n_pages)]

    in_specs = ([batch_spec((8, D_QKV)), batch_spec(kn.shape[1:]), batch_spec(vn.shape[1:]),
                 batch_spec(lfn.shape[1:]), const(bs.shape), const(qmask.shape), const(omask.shape),
                 const(sl.shape), const(lam.shape), const(gd1.shape), const(gd2.shape),
                 const(gmul.shape), const(gadd.shape)]
                + paged(page, D_QKV) + paged(page, D_QKV) + paged(HPM, page))
    grid_spec = pltpu.PrefetchScalarGridSpec(
        num_scalar_prefetch=1,
        grid=(db,),
        in_specs=in_specs,
        out_specs=pl.BlockSpec((None, dec_seq, D_QKV), lambda b, pt: (b, 0, 0)),
        scratch_shapes=[pltpu.VMEM((n_pages + 1, rows, LANES), F32),
                        pltpu.VMEM((n_pages + 1, N_HEADS * dec_seq, LANES), BF16)],
    )
    return pl.pallas_call(
        functools.partial(_sample_kernel, n_pages=n_pages, dec_seq=dec_seq, lam_init=lam_init),
        grid_spec=grid_spec,
        out_shape=jax.ShapeDtypeStruct((db, dec_seq, D_QKV), F32),
        compiler_params=_params("arbitrary"),
        name="sample_attention",
    )(page_table, q8, kn, vn, lfn, bs, qmask, omask, sl, lam, gd1, gd2, gmul, gadd,
      *([cache_k] * n_pages), *([cache_v] * n_pages), *([cache_lft] * n_pages))


def _merge_kernel(x_ref, o0_ref, o1_ref, o2_ref, o3_ref, gmix_ref, wg_ref, wbr_ref, wo_ref, y_ref):
    x = x_ref[...]
    h = (x * lax.rsqrt(jnp.mean(x * x, axis=-1, keepdims=True) + EPS)) * gmix_ref[...]
    hb = h.astype(BF16)
    acc = jnp.zeros(x.shape, F32)
    for m, o_ref in enumerate((o0_ref, o1_ref, o2_ref, o3_ref)):
        gate = 1.0 / (1.0 + jnp.exp(-_nn(hb, wg_ref[:, m * D_MODEL:(m + 1) * D_MODEL])))
        acc = acc + gate * _nn(o_ref[...].astype(BF16), wbr_ref[m])
    y_ref[...] = x + _nn(acc.astype(BF16), wo_ref[...])


def _merge(x, os_, gmix, wg, wbr, wo, tm):
    rows = x.shape[0]
    row = lambda w: pl.BlockSpec((tm, w), lambda i: (i, 0))
    return pl.pallas_call(
        _merge_kernel,
        grid=(rows // tm,),
        in_specs=[row(D_MODEL)] + [row(MIX_WIDTH)] * 4
        + [_const_spec((1, D_MODEL)), _const_spec((D_MODEL, N_MIXERS * D_MODEL)),
           _const_spec((N_MIXERS, MIX_WIDTH, D_MODEL)), _const_spec((D_MODEL, D_MODEL))],
        out_specs=row(D_MODEL),
        out_shape=jax.ShapeDtypeStruct((rows, D_MODEL), F32),
        compiler_params=_params("parallel"),
        name="merge",
    )(x, *os_, gmix, wg, wbr, wo)


def _mlp_kernel(x_ref, g_ref, wup_ref, wdn_ref, y_ref, *, chunk):
    x = x_ref[...]
    h = ((x * lax.rsqrt(jnp.mean(x * x, axis=-1, keepdims=True) + EPS)) * g_ref[...]).astype(BF16)
    acc = x
    for c in range(D_FF // chunk):
        u = jnp.maximum(_nn(h, wup_ref[:, c * chunk:(c + 1) * chunk]), 0.0)
        acc = acc + _nn((u * u).astype(BF16), wdn_ref[c * chunk:(c + 1) * chunk, :])
    y_ref[...] = acc


def _mlp(x, g, wup, wdn, tm):
    rows = x.shape[0]
    row = pl.BlockSpec((tm, D_MODEL), lambda i: (i, 0))
    return pl.pallas_call(
        functools.partial(_mlp_kernel, chunk=1024),
        grid=(rows // tm,),
        in_specs=[row, _const_spec((1, D_MODEL)), _const_spec((D_MODEL, D_FF)), _const_spec((D_FF, D_MODEL))],
        out_specs=row,
        out_shape=jax.ShapeDtypeStruct((rows, D_MODEL), F32),
        compiler_params=_params("parallel"),
        name="mlp",
    )(x, g, wup, wdn)


def _qk_vectors(gq, gk):
    z = jnp.zeros((MIX_WIDTH,), F32)
    one = jnp.ones((MIX_WIDTH,), F32)
    t4 = lambda a: jnp.tile(a, HPM)
    sc = HEAD_DIM ** -0.5
    qmul = jnp.concatenate([t4(gq[0]) * sc, t4(gq[1]) * sc, z, t4(gq[2]) * (DIFF_HALF ** -0.5)])
    qadd = jnp.concatenate([z, z, one * sc, z])
    kmul = jnp.concatenate([t4(gk[0]), t4(gk[1]), z, t4(gk[2])])
    kadd = jnp.concatenate([z, z, one, z])
    return [a.reshape(1, D_QKV) for a in (qmul, qadd, kmul, kadd)]


def _bias_index_tiles(tq, n_pages, page, dec_seq):
    a = np.arange(tq)
    diag = _t5_bucket_np(a[:, None] - a[None, :])
    sub = _t5_bucket_np(tq + a[:, None] - a[None, :])
    idxp = np.stack([diag, sub]).astype(np.int32)
    past_len = n_pages * page
    qpos = past_len + (np.arange(HPM * dec_seq) % dec_seq)
    kpos = np.concatenate([np.arange(past_len), past_len + np.arange(page)])
    idxs = _t5_bucket_np(qpos[None, :, None] - kpos.reshape(n_pages + 1, 1, page))
    return idxp, idxs.astype(np.int32)


def _layer(l, x, q_start, past, w, consts, tq, tm):
    (tab, bp, bs, g1, g2, sl_q, tri_c, sl_p, sample_consts, gd) = consts
    (w_in, b_forget, g_q, g_k, lam, g_sub, w_branch, w_o, g_mix, g_mlp, w_up, w_down) = w
    b, t, _ = x.shape
    rows = b * t
    lam_init = 0.8 - 0.6 * math.exp(-0.3 * l)
    wqkv = w_in[l, :, :3 * D_QKV].astype(BF16)
    wf = jnp.pad(w_in[l, :, 3 * D_QKV:3 * D_QKV + HPM], ((0, 0), (0, LANES - HPM))).astype(BF16)
    bf = jnp.pad(b_forget[l], (0, LANES - HPM)).reshape(1, LANES)
    wg = w_in[l, :, 3 * D_QKV + HPM:].astype(BF16)
    gmix = g_mix[l].reshape(1, D_MODEL)
    qk_vecs = _qk_vectors(g_q[l], g_k[l])
    x2 = x.reshape(rows, D_MODEL)
    qb, kb, vb, kf, vf, lf = _inproj(x2, gmix, wqkv, wf, bf, g1, g2, *qk_vecs, tm=min(tm, rows))

    if past is None:
        qb3, kb3, vb3 = (a.reshape(b, t, D_QKV) for a in (qb, kb, vb))
        nq = t // tq
        lf_t = jnp.pad(jnp.swapaxes(lf.reshape(b, t, HPM), 1, 2), ((0, 0), (0, 8 - HPM), (0, 0)))
        c_t = _cumsum(lf_t, tri_c)[:, :HPM]
        means = _block_means(kf.reshape(b, t, D_QKV))
        means = jnp.pad(means, ((0, 0), (0, LANES - means.shape[1]), (0, 0)))
        gsub_pair = jnp.tile(g_sub[l], 2).reshape(1, PAIR)
        o_moba = _moba_prompt(tab, qb3, kb3, vb3, means, bp, tq)
        o_fox = _fox_prompt(qb3, kb3, vb3, c_t.reshape(b, HPM, t, 1), c_t.reshape(b, HPM, nq, 1, tq), tq)
        o_sb = _sb_prompt(qb3, kb3, vb3, sl_q, tq)
        o_dif = _diff_prompt(tab, qb3, kb3, vb3, bp, lam[l], gsub_pair, lam_init, tq)
        outs = [a.reshape(rows, MIX_WIDTH) for a in (o_moba, o_fox, o_sb, o_dif)]
    else:
        cache_k, cache_v, cache_lft, page_table = past
        q4 = qb.astype(F32).reshape(b, t, D_QKV)
        q8 = jnp.concatenate([q4] * (8 // t), axis=1)
        padn = ((0, 0), (0, 16 - t), (0, 0))
        kn = jnp.pad(kb.reshape(b, t, D_QKV), padn)
        vn = jnp.pad(vb.reshape(b, t, D_QKV), padn)
        lfn = jnp.swapaxes(lf.reshape(b, t, HPM), 1, 2)
        lfn = jnp.pad(jnp.repeat(lfn, t, axis=1), ((0, 0), (0, 0), (0, 8 - t)))
        gd1, gd2 = gd
        z3 = jnp.zeros((3 * MIX_WIDTH,), F32)
        gmul = jnp.concatenate([z3, jnp.tile(g_sub[l], HPM) * (1.0 - lam_init)]).reshape(1, D_QKV)
        gadd = jnp.concatenate([z3 + 1.0, jnp.zeros((MIX_WIDTH,), F32)]).reshape(1, D_QKV)
        o = _sample_attention(l, page_table, q8, kn, vn, lfn, bs, (*sample_consts, sl_p, gd1, gd2),
                              lam[l], gmul, gadd, cache_k, cache_v, cache_lft, lam_init)
        o = o.reshape(rows, D_QKV)
        outs = [o[:, m * MIX_WIDTH:(m + 1) * MIX_WIDTH] for m in range(N_MIXERS)]

    y = _merge(x2, outs, gmix, wg, w_branch[l].astype(BF16), w_o[l].astype(BF16), tm=min(tm, rows))
    y = _mlp(y, g_mlp[l].reshape(1, D_MODEL), w_up[l].astype(BF16), w_down[l].astype(BF16), tm=min(tm, rows))
    return (y.reshape(b, t, D_MODEL), kf.reshape(b, t, N_HEADS, HEAD_DIM),
            vf.reshape(b, t, N_HEADS, HEAD_DIM), lf.reshape(b, t, HPM))


def kernel(x_prompt, x_sample, cache_k, cache_v, cache_logf, page_table, rel_bias, w_in, b_forget,
           g_q, g_k, lam, g_sub, w_branch, w_o, g_mix, g_mlp, w_up, w_down):
    depth = w_in.shape[0]
    tq = MOBA_BLOCK
    tm = 512
    n_pages = page_table.shape[1]
    page = cache_k.shape[2]
    dec_seq = x_sample.shape[1]
    n_phys = cache_k.shape[1]

    idxp, idxs = _bias_index_tiles(tq, n_pages, page, dec_seq)
    tab = rel_bias.T
    bp, bs = _bias_tiles(tab, jnp.asarray(idxp), jnp.asarray(idxs), dec_seq)
    g1, g2 = _qk_group_mats()
    qmask, omask = _sample_masks(dec_seq)
    gd1 = np.zeros((D_QKV, LANES), np.float32)
    gd2 = np.zeros((LANES, D_QKV), np.float32)
    for c in range(3 * MIX_WIDTH, D_QKV):
        gd1[c, (c - 3 * MIX_WIDTH) // HEAD_DIM] = 1.0 / HEAD_DIM
        gd2[(c - 3 * MIX_WIDTH) // HEAD_DIM, c] = 1.0
    bf = lambda a: jnp.asarray(a, BF16)
    consts = (tab, bp, bs, bf(g1), bf(g2), bf(_strict_lower(tq)), bf(_upper_incl(tq)), bf(_strict_lower(page)),
              (jnp.asarray(qmask), jnp.asarray(omask)), (bf(gd1), bf(gd2)))
    w = (w_in, b_forget, g_q, g_k, lam, g_sub, w_branch, w_o, g_mix, g_mlp, w_up, w_down)
    past = (cache_k.reshape(depth, n_phys, page, D_QKV), cache_v.reshape(depth, n_phys, page, D_QKV),
            jnp.swapaxes(cache_logf, 2, 3), page_table)

    hp, hs = x_prompt, x_sample
    outs = [[] for _ in range(6)]
    for l in range(depth):
        hp, kp, vp, fp = _layer(l, hp, 0, None, w, consts, tq, tm)
        hs, ks, vs, fs = _layer(l, hs, n_pages * page, past, w, consts, tq, tm)
        for lst, a in zip(outs, (kp, vp, fp, ks, vs, fs)):
            lst.append(a)
    return (hp, hs) + tuple(jnp.stack(a) for a in outs)
```

```python
import functools
import math

import numpy as np
import jax
import jax.numpy as jnp
from jax import lax
from jax.experimental import pallas as pl
from jax.experimental.pallas import tpu as pltpu

F32 = jnp.float32
BF16 = jnp.bfloat16

D_MODEL = 1024
HEAD_DIM = 64
N_MIXERS = 4
HPM = 4
N_HEADS = 16
MIX_WIDTH = HPM * HEAD_DIM
D_QKV = N_HEADS * HEAD_DIM
DIFF_HALF = HEAD_DIM // 2
D_FF = 4 * D_MODEL
MOBA_BLOCK = 256
MOBA_TOPK = 3
N_BUCKETS = 32
MAX_DISTANCE = 128
EPS = 1e-6
NEG = -1e30
LOG2E = 1.4426950408889634
LANES = 128
PAIR = 2 * HEAD_DIM
VMEM_LIMIT = 56 * 1024 * 1024


def _params(*sem):
    return pltpu.CompilerParams(dimension_semantics=sem, vmem_limit_bytes=VMEM_LIMIT)


def _nt(a, b):
    return lax.dot_general(a, b, (((1,), (1,)), ((), ())), preferred_element_type=F32)


def _nn(a, b):
    return jnp.dot(a, b, preferred_element_type=F32)


def _split2(x):
    hi = x.astype(BF16)
    lo = (x - hi.astype(F32)).astype(BF16)
    return hi, lo


def _split3(x):
    hi = x.astype(BF16)
    r = x - hi.astype(F32)
    mid = r.astype(BF16)
    lo = (r - mid.astype(F32)).astype(BF16)
    return hi, mid, lo


def _nn_split(x, w, parts):
    pieces = _split2(x) if parts == 2 else _split3(x)
    out = _nn(pieces[0], w)
    for p in pieces[1:]:
        out = out + _nn(p, w)
    return out


def _log_sigmoid(z):
    return jnp.minimum(z, 0.0) - jnp.log1p(jnp.exp(-jnp.abs(z)))


def _const_spec(shape):
    nd = len(shape)
    return pl.BlockSpec(shape, lambda *_: (0,) * nd)


def _t5_bucket_np(rel):
    n = np.maximum(rel, 0)
    max_exact = N_BUCKETS // 2
    nf = np.maximum(n, 1).astype(np.float32)
    large = max_exact + (np.log(nf / np.float32(max_exact)) / np.float32(math.log(MAX_DISTANCE / max_exact))
                         * np.float32(N_BUCKETS - max_exact)).astype(np.int32)
    return np.where(n < max_exact, n, np.minimum(large, N_BUCKETS - 1)).astype(np.int32)


def _qk_group_mats():
    g1 = np.zeros((D_QKV, LANES), np.float32)
    g2 = np.zeros((LANES, D_QKV), np.float32)
    for c in range(D_QKV):
        h = c // HEAD_DIM
        if h < 2 * HPM:
            gid, size = h, HEAD_DIM
        elif h < 3 * HPM:
            continue
        else:
            gid, size = 2 * HPM + (c - 3 * MIX_WIDTH) // DIFF_HALF, DIFF_HALF
        g1[c, gid] = 1.0 / size
        g2[gid, c] = 1.0
    return g1, g2


def _strict_lower(n):
    r = np.arange(n)
    return (r[:, None] > r[None, :]).astype(np.float32)


def _upper_incl(n):
    r = np.arange(n)
    return (r[:, None] <= r[None, :]).astype(np.float32)


def _inproj_kernel(x_ref, gmix_ref, wqkv_ref, wf_ref, bf_ref, g1_ref, g2_ref,
                   qmul_ref, qadd_ref, kmul_ref, kadd_ref,
                   qb_ref, kb_ref, vb_ref, kf_ref, vf_ref, lf_ref):
    x = x_ref[...]
    h = (x * lax.rsqrt(jnp.mean(x * x, axis=-1, keepdims=True) + EPS)) * gmix_ref[...]
    hb = h.astype(BF16)
    g1 = g1_ref[...]
    g2 = g2_ref[...]

    def group_norm(a, mul_ref, add_ref):
        ms = _nn_split(a * a, g1, 2)
        inv = lax.rsqrt(ms + EPS)
        bc = _nn_split(inv, g2, 3)
        return a * (bc * mul_ref[...] + add_ref[...])

    q = _nn(hb, wqkv_ref[:, 0:D_QKV])
    qb_ref[...] = group_norm(q, qmul_ref, qadd_ref).astype(BF16)
    k = _nn(hb, wqkv_ref[:, D_QKV:2 * D_QKV])
    kn = group_norm(k, kmul_ref, kadd_ref)
    kf_ref[...] = kn
    kb_ref[...] = kn.astype(BF16)
    v = _nn(hb, wqkv_ref[:, 2 * D_QKV:3 * D_QKV])
    vf_ref[...] = v
    vb_ref[...] = v.astype(BF16)
    z = _nn(hb, wf_ref[...]) + bf_ref[...]
    lf_ref[...] = _log_sigmoid(z)[:, 0:HPM]


def _inproj(x, gmix, wqkv, wf, bf, g1, g2, qmul, qadd, kmul, kadd, tm):
    rows = x.shape[0]
    row = lambda w: pl.BlockSpec((tm, w), lambda i: (i, 0))
    vec = _const_spec((1, D_QKV))
    return pl.pallas_call(
        _inproj_kernel,
        grid=(rows // tm,),
        in_specs=[row(D_MODEL), _const_spec((1, D_MODEL)), _const_spec((D_MODEL, 3 * D_QKV)),
                  _const_spec((D_MODEL, LANES)), _const_spec((1, LANES)),
                  _const_spec((D_QKV, LANES)), _const_spec((LANES, D_QKV)), vec, vec, vec, vec],
        out_specs=[row(D_QKV), row(D_QKV), row(D_QKV), row(D_QKV), row(D_QKV), row(HPM)],
        out_shape=[jax.ShapeDtypeStruct((rows, D_QKV), BF16)] * 3
        + [jax.ShapeDtypeStruct((rows, D_QKV), F32)] * 2 + [jax.ShapeDtypeStruct((rows, HPM), F32)],
        compiler_params=_params("parallel"),
        name="inproj",
    )(x, gmix, wqkv, wf, bf, g1, g2, qmul, qadd, kmul, kadd)


def _cumsum_kernel(lf_ref, tri_ref, c_ref, *, t, blk):
    tri = tri_ref[...]
    carry = jnp.zeros((8, 1), F32)
    for i in range(t // blk):
        cs = _nn_split(lf_ref[:, i * blk:(i + 1) * blk], tri, 3) + carry
        c_ref[:, i * blk:(i + 1) * blk] = cs
        carry = cs[:, blk - 1:blk]


def _cumsum(lf_t, tri):
    b, r, t = lf_t.shape
    blk = tri.shape[0]
    return pl.pallas_call(
        functools.partial(_cumsum_kernel, t=t, blk=blk),
        grid=(b,),
        in_specs=[pl.BlockSpec((None, r, t), lambda i: (i, 0, 0)), _const_spec((blk, blk))],
        out_specs=pl.BlockSpec((None, r, t), lambda i: (i, 0, 0)),
        out_shape=jax.ShapeDtypeStruct((b, r, t), F32),
        compiler_params=_params("parallel"),
        name="logf_cumsum",
    )(lf_t, tri)


def _means_kernel(k_ref, o_ref, *, nb):
    for n in range(nb):
        o_ref[n:n + 1, :] = jnp.mean(k_ref[n * MOBA_BLOCK:(n + 1) * MOBA_BLOCK, :], axis=0, keepdims=True)


def _block_means(kf):
    b, t, _ = kf.shape
    nb = t // MOBA_BLOCK
    return pl.pallas_call(
        functools.partial(_means_kernel, nb=nb),
        grid=(b,),
        in_specs=[pl.BlockSpec((None, t, MIX_WIDTH), lambda i: (i, 0, 0))],
        out_specs=pl.BlockSpec((None, nb, MIX_WIDTH), lambda i: (i, 0, 0)),
        out_shape=jax.ShapeDtypeStruct((b, nb, MIX_WIDTH), F32),
        compiler_params=_params("parallel"),
        name="moba_block_means",
    )(kf)


def _bias_kernel(tab_ref, idxp_ref, idxs_ref, bp_ref, bs_ref, *, dec_seq):
    idxp = idxp_ref[...]
    idxs = idxs_ref[...]
    row = lax.broadcasted_iota(jnp.int32, idxs.shape, 1)
    for h in range(2 * HPM):
        acc = jnp.zeros(idxp.shape, F32)
        for b in range(N_BUCKETS):
            acc = jnp.where(idxp == b, tab_ref[h, b] * LOG2E, acc)
        bp_ref[h] = acc
    for m in range(2):
        acc = jnp.zeros(idxs.shape, F32)
        for hl in range(HPM):
            for b in range(N_BUCKETS):
                acc = jnp.where((idxs == b) & (row // dec_seq == hl), tab_ref[m * HPM + hl, b], acc)
        bs_ref[m] = acc


def _bias_tiles(tab, idxp, idxs, dec_seq):
    return pl.pallas_call(
        functools.partial(_bias_kernel, dec_seq=dec_seq),
        in_specs=[pl.BlockSpec(memory_space=pltpu.SMEM), pl.BlockSpec(memory_space=pltpu.VMEM),
                  pl.BlockSpec(memory_space=pltpu.VMEM)],
        out_specs=[pl.BlockSpec(memory_space=pltpu.VMEM), pl.BlockSpec(memory_space=pltpu.VMEM)],
        out_shape=[jax.ShapeDtypeStruct((2 * HPM,) + idxp.shape, F32),
                   jax.ShapeDtypeStruct((2,) + idxs.shape, F32)],
        compiler_params=pltpu.CompilerParams(vmem_limit_bytes=VMEM_LIMIT),
        name="t5_bias_tiles",
    )(tab, idxp, idxs)


ONES_ROWS = 16
ACC_ROWS = HEAD_DIM + ONES_ROWS


def _rows(n):
    return lax.broadcasted_iota(jnp.int32, (n, 1), 0)


def _key_before_query(tk, tq, strict):
    s = lax.broadcasted_iota(jnp.int32, (tk, tq), 0)
    t = lax.broadcasted_iota(jnp.int32, (tk, tq), 1)
    return (s < t) if strict else (s <= t)


def _row_range(qt, lo, hi):
    r = _rows(qt.shape[0])
    return jnp.where((r >= lo) & (r < hi), qt, jnp.zeros_like(qt))


def _v_aug(vt, x):
    return jnp.concatenate([vt[x * HEAD_DIM:(x + 1) * HEAD_DIM, :],
                            jnp.ones((ONES_ROWS, vt.shape[1]), BF16)], axis=0)


def _extra_rows(by_row, n, width):
    r = _rows(n)
    out = jnp.zeros((n, width), F32)
    for row, val in by_row.items():
        out = jnp.where(r == row, val, out)
    return out


def _pieces(by_row_f32, base):
    out = {}
    for i, val in enumerate(by_row_f32):
        for j, piece in enumerate(_split3(val)):
            out[base + 3 * i + j] = piece.astype(F32)
    return out


def _softmax_step(i, s, va, m_sc, acc_sc, sel=None):
    m_old = m_sc[i]
    bm = jnp.max(s, axis=0, keepdims=True)
    if sel is not None:
        bm = jnp.where(sel, bm, NEG)
    m_new = jnp.maximum(m_old, bm)
    m_sub = m_new if sel is None else jnp.where(sel, m_new, -NEG)
    p = jnp.exp2(s - m_sub)
    acc_sc[i] = acc_sc[i] * jnp.exp2(m_old - m_new) + _nn(va, p.astype(BF16))
    m_sc[i] = m_new


def _normalized(acc_sc, i):
    a = acc_sc[i]
    return a[0:HEAD_DIM, :] / a[HEAD_DIM:HEAD_DIM + 1, :]


def _init(m_sc, acc_sc):
    m_sc[...] = jnp.full(m_sc.shape, NEG, F32)
    acc_sc[...] = jnp.zeros(acc_sc.shape, F32)


def _sweep(qi, group, near, far):
    @pl.when(qi == 0)
    def _():
        near(1)

    @pl.when(qi > 0)
    def _():
        near(2)

    nf = jnp.maximum(qi - 1, 0)

    def body(i, c):
        top = nf - 1 - group * i
        far([top - g for g in range(group)])
        return c

    lax.fori_loop(0, nf // group, body, 0)
    rem = nf % group
    size = group // 2
    while size >= 1:
        @pl.when((rem & size) != 0)
        def _(size=size):
            top = (rem & (2 * size - 1)) - 1
            far([top - g for g in range(size)])
        size //= 2


def _softmax_blocks(items, scores, v_of, m_sc, acc_sc, tq):
    ss = [scores(j, kind, i) for (j, kind, i, x, sel) in items]
    for s, (j, kind, i, x, sel) in zip(ss, items):
        if kind == "diag":
            s = jnp.where(_key_before_query(tq, tq, False), s, NEG)
        _softmax_step(i, s, _v_aug(v_of(j), x), m_sc, acc_sc, sel(j, x) if sel is not None else None)


def _fox_kernel(qt_ref, k_ref, vt_ref, cq_ref, call_ref, o_ref, kx_sc, m_sc, acc_sc, *, tq):
    hp = pl.program_id(1)
    qi = pl.program_id(2)
    r8 = _rows(8)

    def head_row(c, x):
        return jnp.sum(jnp.where(r8 == 2 * hp + x, c, 0.0), axis=0, keepdims=True)

    @pl.when(qi == 0)
    def _():
        c = call_ref[...] * LOG2E
        by_row = _pieces([head_row(c, 0), head_row(c, 1)], 0)
        by_row.update({6: 1.0, 7: 1.0, 8: 1.0})
        e = _extra_rows(by_row, 16, c.shape[1])
        e = jnp.concatenate([e, jnp.zeros((PAIR - 16, c.shape[1]), F32)], axis=0)
        kx_sc[...] = e.T.astype(BF16)

    cq = cq_ref[...] * LOG2E
    qt = qt_ref[...]
    qps = []
    for x in range(2):
        by_row = {3 * x: -1.0, 3 * x + 1: -1.0, 3 * x + 2: -1.0}
        by_row.update(_pieces([head_row(cq, x)], 6))
        qx = _extra_rows(by_row, PAIR, tq).astype(BF16)
        qps.append(jnp.concatenate([_row_range(qt, x * HEAD_DIM, (x + 1) * HEAD_DIM), qx], axis=0))
    _init(m_sc, acc_sc)

    def scores(j, kind, x):
        r0 = pl.multiple_of(j * tq, tq)
        kp = jnp.concatenate([k_ref[pl.ds(r0, tq), :], kx_sc[pl.ds(r0, tq), :]], axis=1)
        return _nn(kp, qps[x])

    def run(blocks):
        items = [(j, kind, x, x, None) for j, kind in blocks for x in range(2)]
        _softmax_blocks(items, scores, lambda j: vt_ref[j], m_sc, acc_sc, tq)

    _sweep(qi, 4,
           lambda n: run([(qi, "diag")] + ([(qi - 1, "far")] if n == 2 else [])),
           lambda js: run([(j, "far") for j in js]))
    o = jnp.concatenate([_normalized(acc_sc, 0), _normalized(acc_sc, 1)], axis=0)
    o_ref[...] = o.T.astype(o_ref.dtype)


def _moba_kernel(tab_ref, qt_ref, k_ref, vt_ref, mean_ref, bta_ref, btb_ref, o_ref,
                 m_sc, acc_sc, sel_sc, *, tq, nb):
    hp = pl.program_id(1)
    qi = pl.program_id(2)
    qt = qt_ref[...]
    bt = (bta_ref, btb_ref)
    ones_k = jnp.ones((tq, PAIR), BF16)
    nbp = sel_sc.shape[1]
    rb = _rows(nbp)
    mh, ml = _split2(mean_ref[...])
    qs, qps = [], []
    for x in range(2):
        qx = _row_range(qt, x * HEAD_DIM, (x + 1) * HEAD_DIM)
        qs.append(qx)
        far = jnp.full((1, tq), tab_ref[2 * hp + x, N_BUCKETS - 1] * LOG2E, F32)
        qps.append(jnp.concatenate([qx, _extra_rows(_pieces([far], 0), PAIR, tq).astype(BF16)], axis=0))
        gate = (_nn(mh, qx) + _nn(ml, qx))[0:nbp, :]
        cnt = jnp.zeros((nbp, tq), jnp.int32)
        for n in range(nb):
            gn = gate[n:n + 1, :]
            beats = (gn > gate) | ((gn == gate) & (n < rb))
            cnt = cnt + jnp.where(beats, jnp.where(n < qi, 1, 0), 0)
        sel_sc[x] = jnp.where((cnt < MOBA_TOPK) & (rb < qi), 1.0, 0.0)
    _init(m_sc, acc_sc)

    def scores(j, kind, x):
        k = k_ref[pl.ds(pl.multiple_of(j * tq, tq), tq), :]
        if kind == "far":
            return _nn(jnp.concatenate([k, ones_k], axis=1), qps[x])
        return _nn(k, qs[x]) + bt[x][0 if kind == "diag" else 1]

    def run(blocks):
        items = [(j, kind, x, x, None if kind == "diag" else (lambda j, x: sel_sc[x, pl.ds(j, 1), :] > 0.0))
                 for j, kind in blocks for x in range(2)]
        _softmax_blocks(items, scores, lambda j: vt_ref[j], m_sc, acc_sc, tq)

    _sweep(qi, 4,
           lambda n: run([(qi, "diag")] + ([(qi - 1, "sub")] if n == 2 else [])),
           lambda js: run([(j, "far") for j in js]))
    o = jnp.concatenate([_normalized(acc_sc, 0), _normalized(acc_sc, 1)], axis=0)
    o_ref[...] = o.T.astype(o_ref.dtype)


def _sb_kernel(qt_ref, k_ref, vt_ref, su_ref, o_ref, r_sc, acc_sc, *, tq):
    qi = pl.program_id(2)
    qt = qt_ref[...]
    qs = [_row_range(qt, x * HEAD_DIM, (x + 1) * HEAD_DIM) for x in range(2)]
    su = su_ref[...]
    r_sc[...] = jnp.zeros(r_sc.shape, F32)
    acc_sc[...] = jnp.zeros(acc_sc.shape, F32)

    def run(blocks):
        tiles = [(j, masked, x) for j, masked in blocks for x in range(2)]
        zs = [_nn(k_ref[pl.ds(pl.multiple_of(j * tq, tq), tq), :], qs[x]) for j, masked, x in tiles]
        lbs, lats = [], []
        for z, (j, masked, x) in zip(zs, tiles):
            log_beta = jnp.minimum(z, 0.0) - jnp.log2(1.0 + jnp.exp2(-jnp.abs(z)))
            log_keep = log_beta - z
            if masked:
                log_keep = jnp.where(_key_before_query(tq, tq, True), log_keep, 0.0)
            hi, lo = _split2(log_keep)
            lbs.append(log_beta)
            lats.append(_nn(su, hi) + _nn(su, lo))
        r = [r_sc[0], r_sc[1]]
        pvs = [jnp.zeros((HEAD_DIM, tq), F32)] * 2
        for log_beta, lat, (j, masked, x) in zip(lbs, lats, tiles):
            a = jnp.exp2(log_beta + lat[0:tq, :] + r[x])
            if masked:
                a = jnp.where(_key_before_query(tq, tq, True), a, 0.0)
            r[x] = r[x] + lat[tq:tq + 1, :]
            pvs[x] = pvs[x] + _nn(vt_ref[j][x * HEAD_DIM:(x + 1) * HEAD_DIM, :], a.astype(BF16))
        for x in range(2):
            r_sc[x] = r[x]
            acc_sc[x] = acc_sc[x] + pvs[x]

    _sweep(qi, 4,
           lambda n: run([(qi, True)] + ([(qi - 1, False)] if n == 2 else [])),
           lambda js: run([(j, False) for j in js]))
    o = jnp.concatenate([acc_sc[0], acc_sc[1]], axis=0)
    o_ref[...] = o.T.astype(o_ref.dtype)


def _lam_value(lam_ref, lam_init):
    lm = lam_ref[...]
    a = jnp.sum(lm[0:1] * lm[1:2], axis=-1, keepdims=True)
    b = jnp.sum(lm[2:3] * lm[3:4], axis=-1, keepdims=True)
    return jnp.exp(a) - jnp.exp(b) + lam_init


def _diff_kernel(tab_ref, qt_ref, k_ref, vt_ref, bta_ref, btb_ref, lam_ref, gsub_ref, o_ref,
                 m_sc, acc_sc, *, tq, lam_init):
    hp = pl.program_id(1)
    qi = pl.program_id(2)
    qt = qt_ref[...]
    bt = (bta_ref, btb_ref)
    ones_k = jnp.ones((tq, PAIR), BF16)
    qs, qps = [], []
    for i in range(4):
        qx = _row_range(qt, i * DIFF_HALF, (i + 1) * DIFF_HALF)
        qs.append(qx)
        far = jnp.full((1, tq), tab_ref[HPM + 2 * hp + i // 2, N_BUCKETS - 1] * LOG2E, F32)
        qps.append(jnp.concatenate([qx, _extra_rows(_pieces([far], 0), PAIR, tq).astype(BF16)], axis=0))
    _init(m_sc, acc_sc)

    def scores(j, kind, i):
        k = k_ref[pl.ds(pl.multiple_of(j * tq, tq), tq), :]
        if kind == "far":
            return _nn(jnp.concatenate([k, ones_k], axis=1), qps[i])
        return _nn(k, qs[i]) + bt[i // 2][0 if kind == "diag" else 1]

    def run(blocks):
        items = [(j, kind, i, i // 2, None) for j, kind in blocks for i in range(4)]
        _softmax_blocks(items, scores, lambda j: vt_ref[j], m_sc, acc_sc, tq)

    _sweep(qi, 2,
           lambda n: run([(qi, "diag")] + ([(qi - 1, "sub")] if n == 2 else [])),
           lambda js: run([(j, "far") for j in js]))
    lam_val = _lam_value(lam_ref, lam_init)
    gcol = jnp.concatenate([gsub_ref[...]] * (tq // LANES), axis=1)
    outs = []
    for x in range(2):
        o = _normalized(acc_sc, 2 * x) - lam_val * _normalized(acc_sc, 2 * x + 1)
        inv = lax.rsqrt(jnp.mean(o * o, axis=0, keepdims=True) + EPS)
        outs.append(((o * inv) * gcol) * (1.0 - lam_init))
    o_ref[...] = jnp.concatenate(outs, axis=0).T.astype(o_ref.dtype)


def _pair_specs(t, tq, col0):
    nk = t // tq
    qspec = pl.BlockSpec((None, PAIR, tq), lambda b, hp, qi: (b, col0 + hp, qi))
    kspec = pl.BlockSpec((None, t, PAIR), lambda b, hp, qi: (b, 0, col0 + hp))
    vspec = pl.BlockSpec((None, nk, PAIR, tq), lambda b, hp, qi: (b, 0, col0 + hp, 0))
    ospec = pl.BlockSpec((None, tq, PAIR), lambda b, hp, qi: (b, qi, hp))
    return qspec, kspec, vspec, ospec


def _attn_out(b, t):
    return jax.ShapeDtypeStruct((b, t, MIX_WIDTH), BF16)


def _stat_scratch(n_maps, tq, rows):
    return [pltpu.VMEM((n_maps, 1, tq), F32), pltpu.VMEM((n_maps, rows, tq), F32)]


def _fox_prompt(qt, kb, vt, c, tq):
    b, t, _ = kb.shape
    qspec, kspec, vspec, ospec = _pair_specs(t, tq, 2)
    cq = pl.BlockSpec((None, 8, tq), lambda bi, hp, qi: (bi, 0, qi))
    call = pl.BlockSpec((None, 8, t), lambda bi, hp, qi: (bi, 0, 0))
    return pl.pallas_call(
        functools.partial(_fox_kernel, tq=tq),
        grid=(b, 2, t // tq),
        in_specs=[qspec, kspec, vspec, cq, call],
        out_specs=ospec,
        out_shape=_attn_out(b, t),
        scratch_shapes=[pltpu.VMEM((t, PAIR), BF16)] + _stat_scratch(2, tq, ACC_ROWS),
        compiler_params=_params("parallel", "parallel", "arbitrary"),
        name="fox_prompt",
    )(qt, kb, vt, c, c)


def _moba_prompt(tab, qt, kb, vt, means, bp, tq):
    b, t, _ = kb.shape
    nb = t // MOBA_BLOCK
    nbp = -(-nb // 8) * 8
    qspec, kspec, vspec, ospec = _pair_specs(t, tq, 0)
    mspec = pl.BlockSpec((None, LANES, PAIR), lambda bi, hp, qi: (bi, 0, hp))
    bt = lambda x: pl.BlockSpec((None, 2, tq, tq), lambda bi, hp, qi: (2 * hp + x, 0, 0, 0))
    return pl.pallas_call(
        functools.partial(_moba_kernel, tq=tq, nb=nb),
        grid=(b, 2, t // tq),
        in_specs=[pl.BlockSpec(memory_space=pltpu.SMEM), qspec, kspec, vspec, mspec, bt(0), bt(1)],
        out_specs=ospec,
        out_shape=_attn_out(b, t),
        scratch_shapes=_stat_scratch(2, tq, ACC_ROWS) + [pltpu.VMEM((2, nbp, tq), F32)],
        compiler_params=_params("parallel", "parallel", "arbitrary"),
        name="moba_prompt",
    )(tab, qt, kb, vt, means, bp, bp)


def _sb_prompt(qt, kb, vt, su, tq):
    b, t, _ = kb.shape
    qspec, kspec, vspec, ospec = _pair_specs(t, tq, 4)
    return pl.pallas_call(
        functools.partial(_sb_kernel, tq=tq),
        grid=(b, 2, t // tq),
        in_specs=[qspec, kspec, vspec, _const_spec(su.shape)],
        out_specs=ospec,
        out_shape=_attn_out(b, t),
        scratch_shapes=_stat_scratch(2, tq, HEAD_DIM),
        compiler_params=_params("parallel", "parallel", "arbitrary"),
        name="sb_prompt",
    )(qt, kb, vt, su)


def _diff_prompt(tab, qt, kb, vt, bp, lam, gsub_col, lam_init, tq):
    b, t, _ = kb.shape
    qspec, kspec, vspec, ospec = _pair_specs(t, tq, 6)
    bt = lambda x: pl.BlockSpec((None, 2, tq, tq), lambda bi, hp, qi: (HPM + 2 * hp + x, 0, 0, 0))
    return pl.pallas_call(
        functools.partial(_diff_kernel, tq=tq, lam_init=lam_init),
        grid=(b, 2, t // tq),
        in_specs=[pl.BlockSpec(memory_space=pltpu.SMEM), qspec, kspec, vspec, bt(0), bt(1),
                  _const_spec((4, DIFF_HALF)), _const_spec((HEAD_DIM, LANES))],
        out_specs=ospec,
        out_shape=_attn_out(b, t),
        scratch_shapes=_stat_scratch(4, tq, ACC_ROWS),
        compiler_params=_params("parallel", "parallel", "arbitrary"),
        name="diff_prompt",
    )(tab, qt, kb, vt, bp, bp, lam, gsub_col)


N_SLOTS = 20


def _sample_masks(dec_seq):
    rows = N_SLOTS * dec_seq
    qmask = np.zeros((rows, D_QKV), np.float32)
    for r in range(rows):
        slot = r // dec_seq
        if slot < 3 * HPM:
            qmask[r, slot * HEAD_DIM:(slot + 1) * HEAD_DIM] = 1.0
        elif slot < 4 * HPM:
            qmask[r, slot * HEAD_DIM:slot * HEAD_DIM + DIFF_HALF] = 1.0
        else:
            c0 = (slot - HPM) * HEAD_DIM + DIFF_HALF
            qmask[r, c0:c0 + DIFF_HALF] = 1.0
    omask = np.zeros((N_HEADS * dec_seq, D_QKV), np.float32)
    for r in range(N_HEADS * dec_seq):
        omask[r, (r // dec_seq) * HEAD_DIM:(r // dec_seq + 1) * HEAD_DIM] = 1.0
    return qmask, omask


def _expand_rows(a, dec_seq):
    rows = HPM * dec_seq
    r = lax.broadcasted_iota(jnp.int32, (rows, 1), 0)
    out = jnp.broadcast_to(a[HPM - 1:HPM, :], (rows, a.shape[1]))
    for h in range(HPM - 2, -1, -1):
        out = jnp.where(r < (h + 1) * dec_seq, a[h:h + 1, :], out)
    return out


def _softmax_pages(s_pages):
    m = s_pages[0].max(axis=-1, keepdims=True)
    for s in s_pages[1:]:
        m = jnp.maximum(m, s.max(axis=-1, keepdims=True))
    ps = [jnp.exp(s - m) for s in s_pages]
    l = ps[0].sum(axis=-1, keepdims=True)
    for p in ps[1:]:
        l = l + p.sum(axis=-1, keepdims=True)
    return [p / l for p in ps]


def _sample_kernel(*refs, n_pages, dec_seq, lam_init):
    pt_ref = refs[0]
    del pt_ref
    (q_ref, kn_ref, vn_ref, lfn_ref, bs_ref, qmask_ref, omask_ref, sl_ref, lam_ref,
     gd1_ref, gd2_ref, gmul_ref, gadd_ref) = refs[1:14]
    k_refs = refs[14:14 + n_pages]
    v_refs = refs[14 + n_pages:14 + 2 * n_pages]
    lf_refs = refs[14 + 2 * n_pages:14 + 3 * n_pages]
    o_ref = refs[14 + 3 * n_pages]
    s_sc, p_sc = refs[14 + 3 * n_pages + 1:]

    g = HPM * dec_seq
    npg = n_pages + 1
    q = q_ref[...]
    qbd = (jnp.concatenate([q] * (N_SLOTS * dec_seq // 8), axis=0) * qmask_ref[...]).astype(BF16)
    pad = jnp.zeros((LANES - kn_ref.shape[0], D_QKV), BF16)
    for p in range(n_pages):
        s_sc[p] = _nn(qbd, k_refs[p][...].astype(BF16))
    s_sc[n_pages] = _nt(qbd, jnp.concatenate([kn_ref[...], pad], axis=0))

    lane = lax.broadcasted_iota(jnp.int32, (1, LANES), 1)
    qrow = lax.broadcasted_iota(jnp.int32, (g, 1), 0) % dec_seq
    new_ok = lane <= qrow
    new_past = lane < qrow
    sl = sl_ref[...]

    def pages(r0):
        return [s_sc[p, r0:r0 + g, :] for p in range(npg)]

    sm = pages(0)
    ppb = MOBA_BLOCK // LANES
    nblk = n_pages // ppb
    gates = []
    for n in range(nblk):
        tot = sm[n * ppb]
        for i in range(1, ppb):
            tot = tot + sm[n * ppb + i]
        gates.append(jnp.sum(tot, axis=-1, keepdims=True))
    s_pages = []
    for n in range(nblk):
        cnt = jnp.zeros((g, 1), jnp.int32)
        for n2 in range(nblk):
            if n2 == n:
                continue
            beats = (gates[n2] > gates[n]) | ((gates[n2] == gates[n]) & (n2 < n))
            cnt = cnt + jnp.where(beats, 1, 0)
        for i in range(ppb):
            p = n * ppb + i
            s_pages.append(jnp.where(cnt < MOBA_TOPK, sm[p] + bs_ref[0, p], NEG))
    s_pages.append(jnp.where(new_ok, sm[n_pages] + bs_ref[0, n_pages], NEG))
    for p, pr in enumerate(_softmax_pages(s_pages)):
        p_sc[p, 0:g, :] = pr.astype(BF16)

    lfn = lfn_ref[...]
    cn_row = jnp.zeros((g, 1), F32)
    bias_new = jnp.zeros((g, LANES), F32)
    for n in range(dec_seq):
        col = lfn[:, n:n + 1]
        cn_row = cn_row + jnp.where(n <= qrow, col, 0.0)
        bias_new = bias_new + jnp.where((lane < n) & (n <= qrow), col, 0.0)
    lfe = [_expand_rows(lf_refs[p][...], dec_seq) for p in range(n_pages)]
    suffix = _nn_split(jnp.concatenate(lfe, axis=0), sl, 3)
    sf = pages(g)
    s_pages = [None] * npg
    s_pages[n_pages] = jnp.where(new_ok, sf[n_pages] + bias_new, NEG)
    after = cn_row
    for p in range(n_pages - 1, -1, -1):
        s_pages[p] = sf[p] + (suffix[p * g:(p + 1) * g, :] + after)
        after = after + jnp.sum(lfe[p], axis=-1, keepdims=True)
    for p, pr in enumerate(_softmax_pages(s_pages)):
        p_sc[p, g:2 * g, :] = pr.astype(BF16)

    ss = pages(2 * g)
    lbs, lks = [], []
    for p in range(npg):
        lb = _log_sigmoid(ss[p])
        lk = lb - ss[p]
        if p == n_pages:
            lk = jnp.where(new_past, lk, 0.0)
        lbs.append(lb)
        lks.append(lk)
    later_in = _nn_split(jnp.concatenate(lks, axis=0), sl, 2)
    after = jnp.zeros((g, 1), F32)
    for p in range(npg - 1, -1, -1):
        a = jnp.exp(lbs[p] + later_in[p * g:(p + 1) * g, :] + after)
        if p == n_pages:
            a = jnp.where(new_past, a, 0.0)
        p_sc[p, 2 * g:3 * g, :] = a.astype(BF16)
        after = after + jnp.sum(lks[p], axis=-1, keepdims=True)

    lam_val = _lam_value(lam_ref, lam_init)
    maps = []
    for mp in range(2):
        sd = pages((3 + mp) * g)
        s_pages = [sd[p] + bs_ref[1, p] for p in range(n_pages)]
        s_pages.append(jnp.where(new_ok, sd[n_pages] + bs_ref[1, n_pages], NEG))
        maps.append(_softmax_pages(s_pages))
    for p in range(npg):
        p_sc[p, 3 * g:4 * g, :] = (maps[0][p] - lam_val * maps[1][p]).astype(BF16)

    padv = jnp.zeros((LANES - vn_ref.shape[0], D_QKV), BF16)
    acc = _nn(p_sc[n_pages], jnp.concatenate([vn_ref[...], padv], axis=0))
    for p in range(n_pages):
        acc = acc + _nt(p_sc[p], v_refs[p][...].astype(BF16))
    rm = acc * omask_ref[...]
    y = rm[0:8, :]
    for i in range(1, N_HEADS * dec_seq // 8):
        y = y + rm[8 * i:8 * (i + 1), :]
    o8 = y + pltpu.roll(y, dec_seq, 0)
    ms = _nn_split(o8 * o8, gd1_ref[...], 2)
    bc = _nn_split(lax.rsqrt(ms + EPS), gd2_ref[...], 3)
    o8 = o8 * (bc * gmul_ref[...] + gadd_ref[...])
    o_ref[...] = o8[0:dec_seq, :]


def _sample_attention(layer, page_table, q8, kn, vn, lfn, bs, consts, lam, gmul, gadd,
                      cache_k, cache_v, cache_lft, lam_init):
    db, n_pages = page_table.shape
    dec_seq = lfn.shape[1] // HPM
    page = cache_k.shape[3]
    qmask, omask, sl, gd1, gd2 = consts
    rows = N_SLOTS * dec_seq

    def batch_spec(shape):
        nd = len(shape)
        return pl.BlockSpec((None,) + shape, lambda b, pt: (b,) + (0,) * nd)

    def const(shape):
        nd = len(shape)
        return pl.BlockSpec(shape, lambda b, pt: (0,) * nd)

    def paged(width_rows, width_cols):
        return [pl.BlockSpec((None, None, width_rows, width_cols),
                             functools.partial(lambda b, pt, j: (layer, pt[b, j], 0, 0), j=j))
                for j in range(n_pages)]

    in_specs = ([batch_spec((8, D_QKV)), batch_spec(kn.shape[1:]), batch_spec(vn.shape[1:]),
                 batch_spec(lfn.shape[1:]), const(bs.shape), const(qmask.shape), const(omask.shape),
                 const(sl.shape), const(lam.shape), const(gd1.shape), const(gd2.shape),
                 const(gmul.shape), const(gadd.shape)]
                + paged(D_QKV, page) + paged(D_QKV, page) + paged(HPM, page))
    grid_spec = pltpu.PrefetchScalarGridSpec(
        num_scalar_prefetch=1,
        grid=(db,),
        in_specs=in_specs,
        out_specs=pl.BlockSpec((None, dec_seq, D_QKV), lambda b, pt: (b, 0, 0)),
        scratch_shapes=[pltpu.VMEM((n_pages + 1, rows, LANES), F32),
                        pltpu.VMEM((n_pages + 1, N_HEADS * dec_seq, LANES), BF16)],
    )
    return pl.pallas_call(
        functools.partial(_sample_kernel, n_pages=n_pages, dec_seq=dec_seq, lam_init=lam_init),
        grid_spec=grid_spec,
        out_shape=jax.ShapeDtypeStruct((db, dec_seq, D_QKV), F32),
        compiler_params=_params("arbitrary"),
        name="sample_attention",
    )(page_table, q8, kn, vn, lfn, bs, qmask, omask, sl, lam, gd1, gd2, gmul, gadd,
      *([cache_k] * n_pages), *([cache_v] * n_pages), *([cache_lft] * n_pages))


def _merge_kernel(x_ref, o0_ref, o1_ref, o2_ref, o3_ref, gmix_ref, wg_ref, wbr_ref, wo_ref, y_ref):
    x = x_ref[...]
    h = (x * lax.rsqrt(jnp.mean(x * x, axis=-1, keepdims=True) + EPS)) * gmix_ref[...]
    hb = h.astype(BF16)
    acc = jnp.zeros(x.shape, F32)
    for m, o_ref in enumerate((o0_ref, o1_ref, o2_ref, o3_ref)):
        gate = 1.0 / (1.0 + jnp.exp(-_nn(hb, wg_ref[:, m * D_MODEL:(m + 1) * D_MODEL])))
        acc = acc + gate * _nn(o_ref[...].astype(BF16), wbr_ref[m])
    y_ref[...] = x + _nn(acc.astype(BF16), wo_ref[...])


def _merge(x, os_, gmix, wg, wbr, wo, tm):
    rows = x.shape[0]
    row = lambda w: pl.BlockSpec((tm, w), lambda i: (i, 0))
    return pl.pallas_call(
        _merge_kernel,
        grid=(rows // tm,),
        in_specs=[row(D_MODEL)] + [row(MIX_WIDTH)] * 4
        + [_const_spec((1, D_MODEL)), _const_spec((D_MODEL, N_MIXERS * D_MODEL)),
           _const_spec((N_MIXERS, MIX_WIDTH, D_MODEL)), _const_spec((D_MODEL, D_MODEL))],
        out_specs=row(D_MODEL),
        out_shape=jax.ShapeDtypeStruct((rows, D_MODEL), F32),
        compiler_params=_params("parallel"),
        name="merge",
    )(x, *os_, gmix, wg, wbr, wo)


def _mlp_kernel(x_ref, g_ref, wup_ref, wdn_ref, y_ref, *, chunk):
    x = x_ref[...]
    h = ((x * lax.rsqrt(jnp.mean(x * x, axis=-1, keepdims=True) + EPS)) * g_ref[...]).astype(BF16)
    acc = x
    for c in range(D_FF // chunk):
        u = jnp.maximum(_nn(h, wup_ref[:, c * chunk:(c + 1) * chunk]), 0.0)
        acc = acc + _nn((u * u).astype(BF16), wdn_ref[c * chunk:(c + 1) * chunk, :])
    y_ref[...] = acc


def _mlp(x, g, wup, wdn, tm):
    rows = x.shape[0]
    row = pl.BlockSpec((tm, D_MODEL), lambda i: (i, 0))
    return pl.pallas_call(
        functools.partial(_mlp_kernel, chunk=1024),
        grid=(rows // tm,),
        in_specs=[row, _const_spec((1, D_MODEL)), _const_spec((D_MODEL, D_FF)), _const_spec((D_FF, D_MODEL))],
        out_specs=row,
        out_shape=jax.ShapeDtypeStruct((rows, D_MODEL), F32),
        compiler_params=_params("parallel"),
        name="mlp",
    )(x, g, wup, wdn)


def _qk_vectors(gq, gk, qscale):
    z = jnp.zeros((MIX_WIDTH,), F32)
    one = jnp.ones((MIX_WIDTH,), F32)
    t4 = lambda a: jnp.tile(a, HPM)
    sc = HEAD_DIM ** -0.5 * qscale
    qmul = jnp.concatenate([t4(gq[0]) * sc, t4(gq[1]) * sc, z, t4(gq[2]) * (DIFF_HALF ** -0.5 * qscale)])
    qadd = jnp.concatenate([z, z, one * sc, z])
    kmul = jnp.concatenate([t4(gk[0]), t4(gk[1]), z, t4(gk[2])])
    kadd = jnp.concatenate([z, z, one, z])
    return [a.reshape(1, D_QKV) for a in (qmul, qadd, kmul, kadd)]


def _bias_index_tiles(tq, n_pages, page, dec_seq):
    a = np.arange(tq)
    diag = _t5_bucket_np(a[None, :] - a[:, None])
    sub = _t5_bucket_np(tq + a[None, :] - a[:, None])
    idxp = np.stack([diag, sub]).astype(np.int32)
    past_len = n_pages * page
    qpos = past_len + (np.arange(HPM * dec_seq) % dec_seq)
    kpos = np.concatenate([np.arange(past_len), past_len + np.arange(page)])
    idxs = _t5_bucket_np(qpos[None, :, None] - kpos.reshape(n_pages + 1, 1, page))
    return idxp, idxs.astype(np.int32)


def _layer(l, x, past, w, consts, tq, tm):
    (tab, bp, bs, g1, g2, su_q, tri_c, sl_p, sample_consts, gd) = consts
    (w_in, b_forget, g_q, g_k, lam, g_sub, w_branch, w_o, g_mix, g_mlp, w_up, w_down) = w
    b, t, _ = x.shape
    rows = b * t
    lam_init = 0.8 - 0.6 * math.exp(-0.3 * l)
    wqkv = w_in[l, :, :3 * D_QKV].astype(BF16)
    wf = jnp.pad(w_in[l, :, 3 * D_QKV:3 * D_QKV + HPM], ((0, 0), (0, LANES - HPM))).astype(BF16)
    bf = jnp.pad(b_forget[l], (0, LANES - HPM)).reshape(1, LANES)
    wg = w_in[l, :, 3 * D_QKV + HPM:].astype(BF16)
    gmix = g_mix[l].reshape(1, D_MODEL)
    qk_vecs = _qk_vectors(g_q[l], g_k[l], LOG2E if past is None else 1.0)
    x2 = x.reshape(rows, D_MODEL)
    qb, kb, vb, kf, vf, lf = _inproj(x2, gmix, wqkv, wf, bf, g1, g2, *qk_vecs, tm=min(tm, rows))

    if past is None:
        nq = t // tq
        qt = jnp.swapaxes(qb.reshape(b, t, D_QKV), 1, 2)
        kb3 = kb.reshape(b, t, D_QKV)
        vt = jnp.transpose(vb.reshape(b, nq, tq, D_QKV), (0, 1, 3, 2))
        lf_t = jnp.pad(jnp.swapaxes(lf.reshape(b, t, HPM), 1, 2), ((0, 0), (0, 8 - HPM), (0, 0)))
        c = _cumsum(lf_t, tri_c)
        means = _block_means(kf.reshape(b, t, D_QKV))
        means = jnp.pad(means, ((0, 0), (0, LANES - means.shape[1]), (0, 0)))
        gsub_col = jnp.broadcast_to(g_sub[l][:, None], (HEAD_DIM, LANES))
        o_moba = _moba_prompt(tab, qt, kb3, vt, means, bp, tq)
        o_fox = _fox_prompt(qt, kb3, vt, c, tq)
        o_sb = _sb_prompt(qt, kb3, vt, su_q, tq)
        o_dif = _diff_prompt(tab, qt, kb3, vt, bp, lam[l], gsub_col, lam_init, tq)
        outs = [a.reshape(rows, MIX_WIDTH) for a in (o_moba, o_fox, o_sb, o_dif)]
    else:
        cache_k, cache_v, cache_lft, page_table = past
        q4 = qb.astype(F32).reshape(b, t, D_QKV)
        q8 = jnp.concatenate([q4] * (8 // t), axis=1)
        padn = ((0, 0), (0, 16 - t), (0, 0))
        kn = jnp.pad(kb.reshape(b, t, D_QKV), padn)
        vn = jnp.pad(vb.reshape(b, t, D_QKV), padn)
        lfn = jnp.swapaxes(lf.reshape(b, t, HPM), 1, 2)
        lfn = jnp.pad(jnp.repeat(lfn, t, axis=1), ((0, 0), (0, 0), (0, 8 - t)))
        gd1, gd2 = gd
        z3 = jnp.zeros((3 * MIX_WIDTH,), F32)
        gmul = jnp.concatenate([z3, jnp.tile(g_sub[l], HPM) * (1.0 - lam_init)]).reshape(1, D_QKV)
        gadd = jnp.concatenate([z3 + 1.0, jnp.zeros((MIX_WIDTH,), F32)]).reshape(1, D_QKV)
        o = _sample_attention(l, page_table, q8, kn, vn, lfn, bs, (*sample_consts, sl_p, gd1, gd2),
                              lam[l], gmul, gadd, cache_k, cache_v, cache_lft, lam_init)
        o = o.reshape(rows, D_QKV)
        outs = [o[:, m * MIX_WIDTH:(m + 1) * MIX_WIDTH] for m in range(N_MIXERS)]

    y = _merge(x2, outs, gmix, wg, w_branch[l].astype(BF16), w_o[l].astype(BF16), tm=min(tm, rows))
    y = _mlp(y, g_mlp[l].reshape(1, D_MODEL), w_up[l].astype(BF16), w_down[l].astype(BF16), tm=min(tm, rows))
    return (y.reshape(b, t, D_MODEL), kf.reshape(b, t, N_HEADS, HEAD_DIM),
            vf.reshape(b, t, N_HEADS, HEAD_DIM), lf.reshape(b, t, HPM))


def _constants(rel_bias, tq, n_pages, page, dec_seq):
    idxp, idxs = _bias_index_tiles(tq, n_pages, page, dec_seq)
    tab = rel_bias.T
    bp, bs = _bias_tiles(tab, jnp.asarray(idxp), jnp.asarray(idxs), dec_seq)
    g1, g2 = _qk_group_mats()
    qmask, omask = _sample_masks(dec_seq)
    gd1 = np.zeros((D_QKV, LANES), np.float32)
    gd2 = np.zeros((LANES, D_QKV), np.float32)
    for c in range(3 * MIX_WIDTH, D_QKV):
        gd1[c, (c - 3 * MIX_WIDTH) // HEAD_DIM] = 1.0 / HEAD_DIM
        gd2[(c - 3 * MIX_WIDTH) // HEAD_DIM, c] = 1.0
    su_q = np.concatenate([_strict_lower(tq).T, np.ones((ONES_ROWS, tq), np.float32)], axis=0)
    bf = lambda a: jnp.asarray(a, BF16)
    return (tab, bp, bs, bf(g1), bf(g2), bf(su_q), bf(_upper_incl(tq)), bf(_strict_lower(page)),
            (jnp.asarray(qmask), jnp.asarray(omask)), (bf(gd1), bf(gd2)))


def kernel(x_prompt, x_sample, cache_k, cache_v, cache_logf, page_table, rel_bias, w_in, b_forget,
           g_q, g_k, lam, g_sub, w_branch, w_o, g_mix, g_mlp, w_up, w_down):
    depth = w_in.shape[0]
    tq = MOBA_BLOCK
    tm = 512
    n_pages = page_table.shape[1]
    page = cache_k.shape[2]
    dec_seq = x_sample.shape[1]
    n_phys = cache_k.shape[1]
    assert (n_pages * page) % MOBA_BLOCK == 0 and 8 % dec_seq == 0 and x_prompt.shape[1] % tq == 0

    consts = _constants(rel_bias, tq, n_pages, page, dec_seq)
    w = (w_in, b_forget, g_q, g_k, lam, g_sub, w_branch, w_o, g_mix, g_mlp, w_up, w_down)
    to_pages = lambda c: jnp.transpose(c, (0, 1, 3, 4, 2)).reshape(depth, n_phys, D_QKV, page)
    past = (to_pages(cache_k), to_pages(cache_v), jnp.swapaxes(cache_logf, 2, 3), page_table)

    hp, hs = x_prompt, x_sample
    outs = [[] for _ in range(6)]
    for l in range(depth):
        hp, kp, vp, fp = _layer(l, hp, None, w, consts, tq, tm)
        hs, ks, vs, fs = _layer(l, hs, past, w, consts, tq, tm)
        for lst, a in zip(outs, (kp, vp, fp, ks, vs, fs)):
            lst.append(a)
    return (hp, hs) + tuple(jnp.stack(a) for a in outs)
```

```python
import functools
import math

import numpy as np
import jax
import jax.numpy as jnp
from jax import lax
from jax.experimental import pallas as pl
from jax.experimental.pallas import tpu as pltpu

F32 = jnp.float32
BF16 = jnp.bfloat16

D_MODEL = 1024
HEAD_DIM = 64
N_MIXERS = 4
HPM = 4
N_HEADS = 16
MIX_WIDTH = HPM * HEAD_DIM
D_QKV = N_HEADS * HEAD_DIM
DIFF_HALF = HEAD_DIM // 2
D_FF = 4 * D_MODEL
MOBA_BLOCK = 256
MOBA_TOPK = 3
N_BUCKETS = 32
MAX_DISTANCE = 128
EPS = 1e-6
NEG = -1e30
LOG2E = 1.4426950408889634
LANES = 128
PAIR = 2 * HEAD_DIM
VMEM_LIMIT = 56 * 1024 * 1024


def _params(*sem):
    return pltpu.CompilerParams(dimension_semantics=sem, vmem_limit_bytes=VMEM_LIMIT)


def _nt(a, b):
    return lax.dot_general(a, b, (((1,), (1,)), ((), ())), preferred_element_type=F32)


def _nn(a, b):
    return jnp.dot(a, b, preferred_element_type=F32)


def _split2(x):
    hi = x.astype(BF16)
    lo = (x - hi.astype(F32)).astype(BF16)
    return hi, lo


def _split3(x):
    hi = x.astype(BF16)
    r = x - hi.astype(F32)
    mid = r.astype(BF16)
    lo = (r - mid.astype(F32)).astype(BF16)
    return hi, mid, lo


def _nn_split(x, w, parts):
    pieces = _split2(x) if parts == 2 else _split3(x)
    out = _nn(pieces[0], w)
    for p in pieces[1:]:
        out = out + _nn(p, w)
    return out


def _log_sigmoid(z):
    return jnp.minimum(z, 0.0) - jnp.log1p(jnp.exp(-jnp.abs(z)))


def _const_spec(shape):
    nd = len(shape)
    return pl.BlockSpec(shape, lambda *_: (0,) * nd)


def _t5_bucket_np(rel):
    n = np.maximum(rel, 0)
    max_exact = N_BUCKETS // 2
    nf = np.maximum(n, 1).astype(np.float32)
    large = max_exact + (np.log(nf / np.float32(max_exact)) / np.float32(math.log(MAX_DISTANCE / max_exact))
                         * np.float32(N_BUCKETS - max_exact)).astype(np.int32)
    return np.where(n < max_exact, n, np.minimum(large, N_BUCKETS - 1)).astype(np.int32)


def _qk_group_mats():
    g1 = np.zeros((D_QKV, LANES), np.float32)
    g2 = np.zeros((LANES, D_QKV), np.float32)
    for c in range(D_QKV):
        h = c // HEAD_DIM
        if h < 2 * HPM:
            gid, size = h, HEAD_DIM
        elif h < 3 * HPM:
            continue
        else:
            gid, size = 2 * HPM + (c - 3 * MIX_WIDTH) // DIFF_HALF, DIFF_HALF
        g1[c, gid] = 1.0 / size
        g2[gid, c] = 1.0
    return g1, g2


def _strict_lower(n):
    r = np.arange(n)
    return (r[:, None] > r[None, :]).astype(np.float32)


def _upper_incl(n):
    r = np.arange(n)
    return (r[:, None] <= r[None, :]).astype(np.float32)


def _inproj_kernel(x_ref, gmix_ref, wqkv_ref, wf_ref, bf_ref, g1_ref, g2_ref,
                   qmul_ref, qadd_ref, kmul_ref, kadd_ref,
                   qb_ref, kb_ref, vb_ref, kf_ref, vf_ref, lf_ref):
    x = x_ref[...]
    h = (x * lax.rsqrt(jnp.mean(x * x, axis=-1, keepdims=True) + EPS)) * gmix_ref[...]
    hb = h.astype(BF16)
    g1 = g1_ref[...]
    g2 = g2_ref[...]

    def group_norm(a, mul_ref, add_ref):
        ms = _nn_split(a * a, g1, 2)
        inv = lax.rsqrt(ms + EPS)
        bc = _nn_split(inv, g2, 3)
        return a * (bc * mul_ref[...] + add_ref[...])

    q = _nt(hb, wqkv_ref[0:D_QKV, :])
    qb_ref[...] = group_norm(q, qmul_ref, qadd_ref).astype(BF16)
    k = _nt(hb, wqkv_ref[D_QKV:2 * D_QKV, :])
    kn = group_norm(k, kmul_ref, kadd_ref)
    kf_ref[...] = kn
    kb_ref[...] = kn.astype(BF16)
    v = _nt(hb, wqkv_ref[2 * D_QKV:3 * D_QKV, :])
    vf_ref[...] = v
    vb_ref[...] = v.astype(BF16)
    z = _nt(hb, wf_ref[...]) + bf_ref[...]
    lf_ref[...] = _log_sigmoid(z)[:, 0:HPM]


def _inproj(x, gmix, wqkv, wf, bf, g1, g2, qmul, qadd, kmul, kadd, tm):
    rows = x.shape[0]
    row = lambda w: pl.BlockSpec((tm, w), lambda i: (i, 0))
    vec = _const_spec((1, D_QKV))
    return pl.pallas_call(
        _inproj_kernel,
        grid=(rows // tm,),
        in_specs=[row(D_MODEL), _const_spec((1, D_MODEL)), _const_spec((3 * D_QKV, D_MODEL)),
                  _const_spec((LANES, D_MODEL)), _const_spec((1, LANES)),
                  _const_spec((D_QKV, LANES)), _const_spec((LANES, D_QKV)), vec, vec, vec, vec],
        out_specs=[row(D_QKV), row(D_QKV), row(D_QKV), row(D_QKV), row(D_QKV), row(HPM)],
        out_shape=[jax.ShapeDtypeStruct((rows, D_QKV), BF16)] * 3
        + [jax.ShapeDtypeStruct((rows, D_QKV), F32)] * 2 + [jax.ShapeDtypeStruct((rows, HPM), F32)],
        compiler_params=_params("parallel"),
        name="inproj",
    )(x, gmix, wqkv, wf, bf, g1, g2, qmul, qadd, kmul, kadd)


def _inproj_t_kernel(x_ref, gmix_ref, wqkv_ref, wf_ref, bf_ref, qmul_ref, qadd_ref, kmul_ref, kadd_ref,
                     qt_ref, kb_ref, vt_ref, kf_ref, vf_ref, lf_ref):
    x = x_ref[...]
    tm = x.shape[0]
    h = (x * lax.rsqrt(jnp.mean(x * x, axis=-1, keepdims=True) + EPS)) * gmix_ref[...]
    hb = h.astype(BF16)
    wide = lambda ref: jnp.concatenate([ref[...]] * (tm // LANES), axis=1)
    n64, n32 = 2 * MIX_WIDTH, MIX_WIDTH

    def group_norm(a, mul_ref, add_ref):
        sq = a * a

        def inv_rms(rows0, rows1, width):
            g = sq[rows0:rows1].reshape((rows1 - rows0) // width, width, tm)
            inv = lax.rsqrt(jnp.mean(g, axis=1, keepdims=True) + EPS)
            return jnp.broadcast_to(inv, g.shape).reshape(rows1 - rows0, tm)

        scale = jnp.concatenate([inv_rms(0, n64, HEAD_DIM), jnp.zeros((D_QKV - n64 - n32, tm), F32),
                                 inv_rms(D_QKV - n32, D_QKV, DIFF_HALF)], axis=0)
        return a * (scale * wide(mul_ref) + wide(add_ref))

    q = _nt(wqkv_ref[0:D_QKV, :], hb)
    qt_ref[...] = group_norm(q, qmul_ref, qadd_ref).astype(BF16)
    k = _nt(wqkv_ref[D_QKV:2 * D_QKV, :], hb)
    kn = group_norm(k, kmul_ref, kadd_ref)
    kf_ref[...] = kn
    kb_ref[...] = kn.T.astype(BF16)
    v = _nt(wqkv_ref[2 * D_QKV:3 * D_QKV, :], hb)
    vf_ref[...] = v
    vb = v.astype(BF16)
    tq = vt_ref.shape[2]
    for i in range(tm // tq):
        vt_ref[i] = vb[:, i * tq:(i + 1) * tq]
    z = _nt(wf_ref[...], hb)[0:8, :] + wide(bf_ref)
    lf_ref[...] = _log_sigmoid(z)


def _inproj_t(x, gmix, wqkv, wf, bf_col, qmul, qadd, kmul, kadd, tm, tq):
    b, t, _ = x.shape
    col = _const_spec((D_QKV, LANES))
    feat = lambda rows: pl.BlockSpec((None, rows, tm), lambda bi, i: (bi, 0, i))
    return pl.pallas_call(
        _inproj_t_kernel,
        grid=(b, t // tm),
        in_specs=[pl.BlockSpec((None, tm, D_MODEL), lambda bi, i: (bi, i, 0)), _const_spec((1, D_MODEL)),
                  _const_spec((3 * D_QKV, D_MODEL)), _const_spec((LANES, D_MODEL)), _const_spec((8, LANES)),
                  col, col, col, col],
        out_specs=[feat(D_QKV), pl.BlockSpec((None, tm, D_QKV), lambda bi, i: (bi, i, 0)),
                   pl.BlockSpec((None, tm // tq, D_QKV, tq), lambda bi, i: (bi, i, 0, 0)),
                   feat(D_QKV), feat(D_QKV), feat(8)],
        out_shape=[jax.ShapeDtypeStruct((b, D_QKV, t), BF16), jax.ShapeDtypeStruct((b, t, D_QKV), BF16),
                   jax.ShapeDtypeStruct((b, t // tq, D_QKV, tq), BF16),
                   jax.ShapeDtypeStruct((b, D_QKV, t), F32), jax.ShapeDtypeStruct((b, D_QKV, t), F32),
                   jax.ShapeDtypeStruct((b, 8, t), F32)],
        compiler_params=_params("parallel", "parallel"),
        name="inproj_t",
    )(x, gmix, wqkv, wf, bf_col, qmul, qadd, kmul, kadd)


def _cumsum_kernel(lf_ref, tri_ref, c_ref, *, t, blk):
    tri = tri_ref[...]
    carry = jnp.zeros((8, 1), F32)
    for i in range(t // blk):
        cs = _nn_split(lf_ref[:, i * blk:(i + 1) * blk], tri, 3) + carry
        c_ref[:, i * blk:(i + 1) * blk] = cs
        carry = cs[:, blk - 1:blk]


def _cumsum(lf_t, tri):
    b, r, t = lf_t.shape
    blk = tri.shape[0]
    return pl.pallas_call(
        functools.partial(_cumsum_kernel, t=t, blk=blk),
        grid=(b,),
        in_specs=[pl.BlockSpec((None, r, t), lambda i: (i, 0, 0)), _const_spec((blk, blk))],
        out_specs=pl.BlockSpec((None, r, t), lambda i: (i, 0, 0)),
        out_shape=jax.ShapeDtypeStruct((b, r, t), F32),
        compiler_params=_params("parallel"),
        name="logf_cumsum",
    )(lf_t, tri)


def _means_kernel(k_ref, o_ref, *, nb):
    lane = lax.broadcasted_iota(jnp.int32, (1, LANES), 1)
    acc = jnp.zeros(o_ref.shape, F32)
    for n in range(nb):
        col = jnp.mean(k_ref[:, n * MOBA_BLOCK:(n + 1) * MOBA_BLOCK], axis=1, keepdims=True)
        acc = jnp.where(lane == n, col, acc)
    o_ref[...] = acc


def _block_means(kf_t):
    b, _, t = kf_t.shape
    nb = t // MOBA_BLOCK
    assert nb <= LANES
    return pl.pallas_call(
        functools.partial(_means_kernel, nb=nb),
        grid=(b,),
        in_specs=[pl.BlockSpec((None, MIX_WIDTH, t), lambda i: (i, 0, 0))],
        out_specs=pl.BlockSpec((None, MIX_WIDTH, LANES), lambda i: (i, 0, 0)),
        out_shape=jax.ShapeDtypeStruct((b, MIX_WIDTH, LANES), F32),
        compiler_params=_params("parallel"),
        name="moba_block_means",
    )(kf_t)


def _bias_kernel(tab_ref, idxp_ref, idxs_ref, bp_ref, bs_ref, *, dec_seq):
    idxp = idxp_ref[...]
    idxs = idxs_ref[...]
    row = lax.broadcasted_iota(jnp.int32, idxs.shape, 1)
    for h in range(2 * HPM):
        acc = jnp.zeros(idxp.shape, F32)
        for b in range(N_BUCKETS):
            acc = jnp.where(idxp == b, tab_ref[h, b] * LOG2E, acc)
        bp_ref[h] = acc
    for m in range(2):
        acc = jnp.zeros(idxs.shape, F32)
        for hl in range(HPM):
            for b in range(N_BUCKETS):
                acc = jnp.where((idxs == b) & (row // dec_seq == hl), tab_ref[m * HPM + hl, b], acc)
        bs_ref[m] = acc


def _bias_tiles(tab, idxp, idxs, dec_seq):
    return pl.pallas_call(
        functools.partial(_bias_kernel, dec_seq=dec_seq),
        in_specs=[pl.BlockSpec(memory_space=pltpu.SMEM), pl.BlockSpec(memory_space=pltpu.VMEM),
                  pl.BlockSpec(memory_space=pltpu.VMEM)],
        out_specs=[pl.BlockSpec(memory_space=pltpu.VMEM), pl.BlockSpec(memory_space=pltpu.VMEM)],
        out_shape=[jax.ShapeDtypeStruct((2 * HPM,) + idxp.shape, F32),
                   jax.ShapeDtypeStruct((2,) + idxs.shape, F32)],
        compiler_params=pltpu.CompilerParams(vmem_limit_bytes=VMEM_LIMIT),
        name="t5_bias_tiles",
    )(tab, idxp, idxs)


ONES_ROWS = 16
ACC_ROWS = HEAD_DIM + ONES_ROWS
EXP2_DEAD = -200.0


def _rows(n):
    return lax.broadcasted_iota(jnp.int32, (n, 1), 0)


def _key_before_query(tk, tq, strict):
    s = lax.broadcasted_iota(jnp.int32, (tk, tq), 0)
    t = lax.broadcasted_iota(jnp.int32, (tk, tq), 1)
    return (s < t) if strict else (s <= t)


def _row_range(qt, lo, hi):
    r = _rows(qt.shape[0])
    return jnp.where((r >= lo) & (r < hi), qt, jnp.zeros_like(qt))


def _v_aug(vt, x):
    return jnp.concatenate([vt[x * HEAD_DIM:(x + 1) * HEAD_DIM, :],
                            jnp.ones((ONES_ROWS, vt.shape[1]), BF16)], axis=0)


def _extra_rows(by_row, n, width):
    r = _rows(n)
    out = jnp.zeros((n, width), F32)
    for row, val in by_row.items():
        out = jnp.where(r == row, val, out)
    return out


def _pieces(by_row_f32, base):
    out = {}
    for i, val in enumerate(by_row_f32):
        for j, piece in enumerate(_split3(val)):
            out[base + 3 * i + j] = piece.astype(F32)
    return out


def _softmax_step(i, s, va, m_sc, acc_sc, sel=None):
    m_old = m_sc[i]
    bm = jnp.max(s, axis=0, keepdims=True)
    if sel is not None:
        bm = jnp.where(sel, bm, NEG)
    m_new = jnp.maximum(m_old, bm)
    m_sub = m_new if sel is None else jnp.where(sel, m_new, -NEG)
    p = jnp.exp2(s - m_sub)
    acc_sc[i] = acc_sc[i] * jnp.exp2(m_old - m_new) + _nn(va, p.astype(BF16))
    m_sc[i] = m_new


def _normalized(acc_sc, i):
    a = acc_sc[i]
    return a[0:HEAD_DIM, :] / a[HEAD_DIM:HEAD_DIM + 1, :]


def _init(m_sc, acc_sc):
    m_sc[...] = jnp.full(m_sc.shape, NEG, F32)
    acc_sc[...] = jnp.zeros(acc_sc.shape, F32)


def _sweep(qi, group, near, far, alive=None):
    @pl.when(qi == 0)
    def _():
        near(1)

    @pl.when(qi > 0)
    def _():
        near(2)

    nf = jnp.maximum(qi - 1, 0)
    if alive is None:
        def body(i, c):
            top = nf - 1 - group * i
            far([top - g for g in range(group)])
            return c

        lax.fori_loop(0, nf // group, body, 0)
        ok = None
    else:
        def body(c):
            i, _ = c
            top = nf - 1 - group * i
            far([top - g for g in range(group)])
            return i + 1, alive(top - group + 1)

        _, ok = lax.while_loop(lambda c: (c[0] < nf // group) & c[1], body, (0, alive(nf)))
    rem = nf % group
    size = group // 2
    while size >= 1:
        top = (rem & (2 * size - 1)) - 1
        cond = (rem & size) != 0
        if ok is not None:
            cond = cond & ok

        @pl.when(cond)
        def _(size=size, top=top):
            far([top - g for g in range(size)])

        if ok is not None and size > 1:
            ok = ok & alive(rem & (size - 1))
        size //= 2


def _softmax_blocks(items, scores, v_of, m_sc, acc_sc, tq):
    ss = [scores(j, kind, i) for (j, kind, i, x, sel) in items]
    for s, (j, kind, i, x, sel) in zip(ss, items):
        if kind == "diag":
            s = jnp.where(_key_before_query(tq, tq, False), s, NEG)
        _softmax_step(i, s, _v_aug(v_of(j), x), m_sc, acc_sc, sel(j, x) if sel is not None else None)


def _fox_kernel(qt_ref, k_ref, vt_ref, cq_ref, call_ref, cend_ref, o_ref, kx_sc, kn_sc, m_sc, acc_sc, *, tq):
    hp = pl.program_id(1)
    qi = pl.program_id(2)
    r8 = _rows(8)
    is_a = lax.broadcasted_iota(jnp.int32, (1, PAIR), 1) < HEAD_DIM

    def head_row(c, x):
        return jnp.sum(jnp.where(r8 == 2 * hp + x, c, 0.0), axis=0, keepdims=True)

    @pl.when(qi == 0)
    def _():
        c = call_ref[...] * LOG2E
        by_row = _pieces([head_row(c, 0), head_row(c, 1)], 0)
        by_row.update({6: 1.0, 7: 1.0, 8: 1.0})
        e = _extra_rows(by_row, 16, c.shape[1])
        e = jnp.concatenate([e, jnp.zeros((PAIR - 16, c.shape[1]), F32)], axis=0)
        kx_sc[...] = e.T.astype(BF16)
        ksq = jnp.square(k_ref[...].astype(F32))
        for x in range(2):
            n2 = jnp.sum(jnp.where(is_a == (x == 0), ksq, 0.0), axis=1, keepdims=True)
            kn_sc[x] = jnp.sqrt(jnp.max(n2, axis=0, keepdims=True))

    cq = cq_ref[...] * LOG2E
    qt = qt_ref[...]
    qps, cts, zbs = [], [], []
    for x in range(2):
        ct = head_row(cq, x)
        by_row = {3 * x: -1.0, 3 * x + 1: -1.0, 3 * x + 2: -1.0}
        by_row.update(_pieces([ct], 6))
        qx = _extra_rows(by_row, PAIR, tq).astype(BF16)
        qh = _row_range(qt, x * HEAD_DIM, (x + 1) * HEAD_DIM)
        qps.append(jnp.concatenate([qh, qx], axis=0))
        cts.append(ct)
        qn = jnp.sqrt(jnp.sum(jnp.square(qh.astype(F32)), axis=0, keepdims=True))
        zbs.append(qn * kn_sc[x] * 1.001 + 0.001)
    _init(m_sc, acc_sc)

    def alive(low):
        lane = lax.broadcasted_iota(jnp.int32, cend_ref.shape, 1)
        cl = jnp.sum(jnp.where(lane == low - 1, cend_ref[...] * LOG2E, 0.0), axis=1, keepdims=True)
        worst = [zbs[x] + cts[x] - head_row(cl, x) - m_sc[x] for x in range(2)]
        return jnp.max(jnp.maximum(worst[0], worst[1])) > EXP2_DEAD

    def scores(j, kind, x):
        r0 = pl.multiple_of(j * tq, tq)
        kp = jnp.concatenate([k_ref[pl.ds(r0, tq), :], kx_sc[pl.ds(r0, tq), :]], axis=1)
        return _nn(kp, qps[x])

    def run(blocks):
        items = [(j, kind, x, x, None) for j, kind in blocks for x in range(2)]
        _softmax_blocks(items, scores, lambda j: vt_ref[j], m_sc, acc_sc, tq)

    _sweep(qi, 4,
           lambda n: run([(qi, "diag")] + ([(qi - 1, "far")] if n == 2 else [])),
           lambda js: run([(j, "far") for j in js]), alive)
    o = jnp.concatenate([_normalized(acc_sc, 0), _normalized(acc_sc, 1)], axis=0)
    o_ref[...] = o.T.astype(o_ref.dtype)


def _moba_kernel(tab_ref, qt_ref, k_ref, vt_ref, mean_ref, bta_ref, btb_ref, o_ref,
                 m_sc, acc_sc, sel_sc, *, tq, nb):
    hp = pl.program_id(1)
    qi = pl.program_id(2)
    qt = qt_ref[...]
    bt = (bta_ref, btb_ref)
    ones_k = jnp.ones((tq, PAIR), BF16)
    nbp = sel_sc.shape[1]
    rb = _rows(nbp)
    mh, ml = _split2(mean_ref[...].T)
    qs, qps = [], []
    for x in range(2):
        qx = _row_range(qt, x * HEAD_DIM, (x + 1) * HEAD_DIM)
        qs.append(qx)
        far = jnp.full((1, tq), tab_ref[2 * hp + x, N_BUCKETS - 1] * LOG2E, F32)
        qps.append(jnp.concatenate([qx, _extra_rows(_pieces([far], 0), PAIR, tq).astype(BF16)], axis=0))
        gate = (_nn(mh, qx) + _nn(ml, qx))[0:nbp, :]
        cnt = jnp.zeros((nbp, tq), jnp.int32)
        for n in range(nb):
            gn = gate[n:n + 1, :]
            beats = (gn > gate) | ((gn == gate) & (n < rb))
            cnt = cnt + jnp.where(beats, jnp.where(n < qi, 1, 0), 0)
        sel_sc[x] = jnp.where((cnt < MOBA_TOPK) & (rb < qi), 1.0, 0.0)
    _init(m_sc, acc_sc)

    def scores(j, kind, x):
        k = k_ref[pl.ds(pl.multiple_of(j * tq, tq), tq), :]
        if kind == "far":
            return _nn(jnp.concatenate([k, ones_k], axis=1), qps[x])
        return _nn(k, qs[x]) + bt[x][0 if kind == "diag" else 1]

    def run(blocks):
        items = [(j, kind, x, x, None if kind == "diag" else (lambda j, x: sel_sc[x, pl.ds(j, 1), :] > 0.0))
                 for j, kind in blocks for x in range(2)]
        _softmax_blocks(items, scores, lambda j: vt_ref[j], m_sc, acc_sc, tq)

    _sweep(qi, 4,
           lambda n: run([(qi, "diag")] + ([(qi - 1, "sub")] if n == 2 else [])),
           lambda js: run([(j, "far") for j in js]))
    o = jnp.concatenate([_normalized(acc_sc, 0), _normalized(acc_sc, 1)], axis=0)
    o_ref[...] = o.T.astype(o_ref.dtype)


def _sb_kernel(qt_ref, k_ref, vt_ref, su_ref, o_ref, r_sc, acc_sc, *, tq):
    qi = pl.program_id(2)
    qt = qt_ref[...]
    qs = [_row_range(qt, x * HEAD_DIM, (x + 1) * HEAD_DIM) for x in range(2)]
    su = su_ref[...]
    r_sc[...] = jnp.zeros(r_sc.shape, F32)
    acc_sc[...] = jnp.zeros(acc_sc.shape, F32)

    def run(blocks):
        tiles = [(j, masked, x) for j, masked in blocks for x in range(2)]
        zs = [_nn(k_ref[pl.ds(pl.multiple_of(j * tq, tq), tq), :], qs[x]) for j, masked, x in tiles]
        lbs, lats = [], []
        for z, (j, masked, x) in zip(zs, tiles):
            log_beta = jnp.minimum(z, 0.0) - jnp.log2(1.0 + jnp.exp2(-jnp.abs(z)))
            log_keep = log_beta - z
            if masked:
                log_keep = jnp.where(_key_before_query(tq, tq, True), log_keep, 0.0)
            hi, lo = _split2(log_keep)
            lbs.append(log_beta)
            lats.append(_nn(su, hi) + _nn(su, lo))
        r = [r_sc[0], r_sc[1]]
        pvs = [jnp.zeros((HEAD_DIM, tq), F32)] * 2
        for log_beta, lat, (j, masked, x) in zip(lbs, lats, tiles):
            a = jnp.exp2(log_beta + lat[0:tq, :] + r[x])
            if masked:
                a = jnp.where(_key_before_query(tq, tq, True), a, 0.0)
            r[x] = r[x] + lat[tq:tq + 1, :]
            pvs[x] = pvs[x] + _nn(vt_ref[j][x * HEAD_DIM:(x + 1) * HEAD_DIM, :], a.astype(BF16))
        for x in range(2):
            r_sc[x] = r[x]
            acc_sc[x] = acc_sc[x] + pvs[x]

    def alive(low):
        return jnp.max(jnp.maximum(r_sc[0], r_sc[1])) > EXP2_DEAD

    _sweep(qi, 4,
           lambda n: run([(qi, True)] + ([(qi - 1, False)] if n == 2 else [])),
           lambda js: run([(j, False) for j in js]), alive)
    o = jnp.concatenate([acc_sc[0], acc_sc[1]], axis=0)
    o_ref[...] = o.T.astype(o_ref.dtype)


def _lam_value(lam_ref, lam_init):
    lm = lam_ref[...]
    a = jnp.sum(lm[0:1] * lm[1:2], axis=-1, keepdims=True)
    b = jnp.sum(lm[2:3] * lm[3:4], axis=-1, keepdims=True)
    return jnp.exp(a) - jnp.exp(b) + lam_init


def _diff_kernel(tab_ref, qt_ref, k_ref, vt_ref, bta_ref, btb_ref, lam_ref, gsub_ref, o_ref,
                 m_sc, acc_sc, *, tq, lam_init):
    hp = pl.program_id(1)
    qi = pl.program_id(2)
    qt = qt_ref[...]
    bt = (bta_ref, btb_ref)
    ones_k = jnp.ones((tq, PAIR), BF16)
    qs, qps = [], []
    for i in range(4):
        qx = _row_range(qt, i * DIFF_HALF, (i + 1) * DIFF_HALF)
        qs.append(qx)
        far = jnp.full((1, tq), tab_ref[HPM + 2 * hp + i // 2, N_BUCKETS - 1] * LOG2E, F32)
        qps.append(jnp.concatenate([qx, _extra_rows(_pieces([far], 0), PAIR, tq).astype(BF16)], axis=0))
    _init(m_sc, acc_sc)

    def scores(j, kind, i):
        k = k_ref[pl.ds(pl.multiple_of(j * tq, tq), tq), :]
        if kind == "far":
            return _nn(jnp.concatenate([k, ones_k], axis=1), qps[i])
        return _nn(k, qs[i]) + bt[i // 2][0 if kind == "diag" else 1]

    def run(blocks):
        items = [(j, kind, i, i // 2, None) for j, kind in blocks for i in range(4)]
        _softmax_blocks(items, scores, lambda j: vt_ref[j], m_sc, acc_sc, tq)

    _sweep(qi, 2,
           lambda n: run([(qi, "diag")] + ([(qi - 1, "sub")] if n == 2 else [])),
           lambda js: run([(j, "far") for j in js]))
    lam_val = _lam_value(lam_ref, lam_init)
    gcol = jnp.concatenate([gsub_ref[...]] * (tq // LANES), axis=1)
    outs = []
    for x in range(2):
        o = _normalized(acc_sc, 2 * x) - lam_val * _normalized(acc_sc, 2 * x + 1)
        inv = lax.rsqrt(jnp.mean(o * o, axis=0, keepdims=True) + EPS)
        outs.append(((o * inv) * gcol) * (1.0 - lam_init))
    o_ref[...] = jnp.concatenate(outs, axis=0).T.astype(o_ref.dtype)


def _pair_specs(t, tq, col0):
    nk = t // tq
    qspec = pl.BlockSpec((None, PAIR, tq), lambda b, hp, qi: (b, col0 + hp, qi))
    kspec = pl.BlockSpec((None, t, PAIR), lambda b, hp, qi: (b, 0, col0 + hp))
    vspec = pl.BlockSpec((None, nk, PAIR, tq), lambda b, hp, qi: (b, 0, col0 + hp, 0))
    ospec = pl.BlockSpec((None, tq, PAIR), lambda b, hp, qi: (b, qi, hp))
    return qspec, kspec, vspec, ospec


def _attn_out(b, t):
    return jax.ShapeDtypeStruct((b, t, MIX_WIDTH), BF16)


def _stat_scratch(n_maps, tq, rows):
    return [pltpu.VMEM((n_maps, 1, tq), F32), pltpu.VMEM((n_maps, rows, tq), F32)]


def _fox_prompt(qt, kb, vt, c, tq):
    b, t, _ = kb.shape
    qspec, kspec, vspec, ospec = _pair_specs(t, tq, 2)
    cq = pl.BlockSpec((None, 8, tq), lambda bi, hp, qi: (bi, 0, qi))
    call = pl.BlockSpec((None, 8, t), lambda bi, hp, qi: (bi, 0, 0))
    cend = c[:, :, tq - 1::tq]
    cends = pl.BlockSpec((None, 8, t // tq), lambda bi, hp, qi: (bi, 0, 0))
    return pl.pallas_call(
        functools.partial(_fox_kernel, tq=tq),
        grid=(b, 2, t // tq),
        in_specs=[qspec, kspec, vspec, cq, call, cends],
        out_specs=ospec,
        out_shape=_attn_out(b, t),
        scratch_shapes=[pltpu.VMEM((t, PAIR), BF16), pltpu.VMEM((2, 1, 1), F32)]
        + _stat_scratch(2, tq, ACC_ROWS),
        compiler_params=_params("parallel", "parallel", "arbitrary"),
        name="fox_prompt",
    )(qt, kb, vt, c, c, cend)


def _moba_prompt(tab, qt, kb, vt, means, bp, tq):
    b, t, _ = kb.shape
    nb = t // MOBA_BLOCK
    nbp = -(-nb // 8) * 8
    qspec, kspec, vspec, ospec = _pair_specs(t, tq, 0)
    mspec = pl.BlockSpec((None, PAIR, LANES), lambda bi, hp, qi: (bi, hp, 0))
    bt = lambda x: pl.BlockSpec((None, 2, tq, tq), lambda bi, hp, qi: (2 * hp + x, 0, 0, 0))
    return pl.pallas_call(
        functools.partial(_moba_kernel, tq=tq, nb=nb),
        grid=(b, 2, t // tq),
        in_specs=[pl.BlockSpec(memory_space=pltpu.SMEM), qspec, kspec, vspec, mspec, bt(0), bt(1)],
        out_specs=ospec,
        out_shape=_attn_out(b, t),
        scratch_shapes=_stat_scratch(2, tq, ACC_ROWS) + [pltpu.VMEM((2, nbp, tq), F32)],
        compiler_params=_params("parallel", "parallel", "arbitrary"),
        name="moba_prompt",
    )(tab, qt, kb, vt, means, bp, bp)


def _sb_prompt(qt, kb, vt, su, tq):
    b, t, _ = kb.shape
    qspec, kspec, vspec, ospec = _pair_specs(t, tq, 4)
    return pl.pallas_call(
        functools.partial(_sb_kernel, tq=tq),
        grid=(b, 2, t // tq),
        in_specs=[qspec, kspec, vspec, _const_spec(su.shape)],
        out_specs=ospec,
        out_shape=_attn_out(b, t),
        scratch_shapes=_stat_scratch(2, tq, HEAD_DIM),
        compiler_params=_params("parallel", "parallel", "arbitrary"),
        name="sb_prompt",
    )(qt, kb, vt, su)


def _diff_prompt(tab, qt, kb, vt, bp, lam, gsub_col, lam_init, tq):
    b, t, _ = kb.shape
    qspec, kspec, vspec, ospec = _pair_specs(t, tq, 6)
    bt = lambda x: pl.BlockSpec((None, 2, tq, tq), lambda bi, hp, qi: (HPM + 2 * hp + x, 0, 0, 0))
    return pl.pallas_call(
        functools.partial(_diff_kernel, tq=tq, lam_init=lam_init),
        grid=(b, 2, t // tq),
        in_specs=[pl.BlockSpec(memory_space=pltpu.SMEM), qspec, kspec, vspec, bt(0), bt(1),
                  _const_spec((4, DIFF_HALF)), _const_spec((HEAD_DIM, LANES))],
        out_specs=ospec,
        out_shape=_attn_out(b, t),
        scratch_shapes=_stat_scratch(4, tq, ACC_ROWS),
        compiler_params=_params("parallel", "parallel", "arbitrary"),
        name="diff_prompt",
    )(tab, qt, kb, vt, bp, bp, lam, gsub_col)


N_SLOTS = 20


def _sample_masks(dec_seq):
    rows = N_SLOTS * dec_seq
    qmask = np.zeros((rows, D_QKV), np.float32)
    for r in range(rows):
        slot = r // dec_seq
        if slot < 3 * HPM:
            qmask[r, slot * HEAD_DIM:(slot + 1) * HEAD_DIM] = 1.0
        elif slot < 4 * HPM:
            qmask[r, slot * HEAD_DIM:slot * HEAD_DIM + DIFF_HALF] = 1.0
        else:
            c0 = (slot - HPM) * HEAD_DIM + DIFF_HALF
            qmask[r, c0:c0 + DIFF_HALF] = 1.0
    omask = np.zeros((N_HEADS * dec_seq, D_QKV), np.float32)
    for r in range(N_HEADS * dec_seq):
        omask[r, (r // dec_seq) * HEAD_DIM:(r // dec_seq + 1) * HEAD_DIM] = 1.0
    return qmask, omask


def _expand_rows(a, dec_seq):
    rows = HPM * dec_seq
    r = lax.broadcasted_iota(jnp.int32, (rows, 1), 0)
    out = jnp.broadcast_to(a[HPM - 1:HPM, :], (rows, a.shape[1]))
    for h in range(HPM - 2, -1, -1):
        out = jnp.where(r < (h + 1) * dec_seq, a[h:h + 1, :], out)
    return out


def _softmax_pages(s_pages):
    m = s_pages[0].max(axis=-1, keepdims=True)
    for s in s_pages[1:]:
        m = jnp.maximum(m, s.max(axis=-1, keepdims=True))
    ps = [jnp.exp(s - m) for s in s_pages]
    l = ps[0].sum(axis=-1, keepdims=True)
    for p in ps[1:]:
        l = l + p.sum(axis=-1, keepdims=True)
    return [p / l for p in ps]


def _sample_kernel(*refs, n_pages, dec_seq, lam_init):
    pt_ref = refs[0]
    del pt_ref
    (q_ref, kn_ref, vn_ref, lfn_ref, bs_ref, qmask_ref, omask_ref, sl_ref, lam_ref,
     gd1_ref, gd2_ref, gmul_ref, gadd_ref) = refs[1:14]
    k_refs = refs[14:14 + n_pages]
    v_refs = refs[14 + n_pages:14 + 2 * n_pages]
    lf_refs = refs[14 + 2 * n_pages:14 + 3 * n_pages]
    o_ref = refs[14 + 3 * n_pages]
    s_sc, p_sc = refs[14 + 3 * n_pages + 1:]

    g = HPM * dec_seq
    npg = n_pages + 1
    q = q_ref[...]
    qbd = (jnp.concatenate([q] * (N_SLOTS * dec_seq // 8), axis=0) * qmask_ref[...]).astype(BF16)
    pad = jnp.zeros((LANES - kn_ref.shape[0], D_QKV), BF16)
    for p in range(n_pages):
        s_sc[p] = _nn(qbd, k_refs[p][...].astype(BF16))
    s_sc[n_pages] = _nt(qbd, jnp.concatenate([kn_ref[...], pad], axis=0))

    lane = lax.broadcasted_iota(jnp.int32, (1, LANES), 1)
    qrow = lax.broadcasted_iota(jnp.int32, (g, 1), 0) % dec_seq
    new_ok = lane <= qrow
    new_past = lane < qrow
    sl = sl_ref[...]

    def pages(r0):
        return [s_sc[p, r0:r0 + g, :] for p in range(npg)]

    sm = pages(0)
    ppb = MOBA_BLOCK // LANES
    nblk = n_pages // ppb
    gates = []
    for n in range(nblk):
        tot = sm[n * ppb]
        for i in range(1, ppb):
            tot = tot + sm[n * ppb + i]
        gates.append(jnp.sum(tot, axis=-1, keepdims=True))
    s_pages = []
    for n in range(nblk):
        cnt = jnp.zeros((g, 1), jnp.int32)
        for n2 in range(nblk):
            if n2 == n:
                continue
            beats = (gates[n2] > gates[n]) | ((gates[n2] == gates[n]) & (n2 < n))
            cnt = cnt + jnp.where(beats, 1, 0)
        for i in range(ppb):
            p = n * ppb + i
            s_pages.append(jnp.where(cnt < MOBA_TOPK, sm[p] + bs_ref[0, p], NEG))
    s_pages.append(jnp.where(new_ok, sm[n_pages] + bs_ref[0, n_pages], NEG))
    for p, pr in enumerate(_softmax_pages(s_pages)):
        p_sc[p, 0:g, :] = pr.astype(BF16)

    lfn = lfn_ref[...]
    cn_row = jnp.zeros((g, 1), F32)
    bias_new = jnp.zeros((g, LANES), F32)
    for n in range(dec_seq):
        col = lfn[:, n:n + 1]
        cn_row = cn_row + jnp.where(n <= qrow, col, 0.0)
        bias_new = bias_new + jnp.where((lane < n) & (n <= qrow), col, 0.0)
    lfe = [_expand_rows(lf_refs[p][...], dec_seq) for p in range(n_pages)]
    suffix = _nn_split(jnp.concatenate(lfe, axis=0), sl, 3)
    sf = pages(g)
    s_pages = [None] * npg
    s_pages[n_pages] = jnp.where(new_ok, sf[n_pages] + bias_new, NEG)
    after = cn_row
    for p in range(n_pages - 1, -1, -1):
        s_pages[p] = sf[p] + (suffix[p * g:(p + 1) * g, :] + after)
        after = after + jnp.sum(lfe[p], axis=-1, keepdims=True)
    for p, pr in enumerate(_softmax_pages(s_pages)):
        p_sc[p, g:2 * g, :] = pr.astype(BF16)

    ss = pages(2 * g)
    lbs, lks = [], []
    for p in range(npg):
        lb = _log_sigmoid(ss[p])
        lk = lb - ss[p]
        if p == n_pages:
            lk = jnp.where(new_past, lk, 0.0)
        lbs.append(lb)
        lks.append(lk)
    later_in = _nn_split(jnp.concatenate(lks, axis=0), sl, 2)
    after = jnp.zeros((g, 1), F32)
    for p in range(npg - 1, -1, -1):
        a = jnp.exp(lbs[p] + later_in[p * g:(p + 1) * g, :] + after)
        if p == n_pages:
            a = jnp.where(new_past, a, 0.0)
        p_sc[p, 2 * g:3 * g, :] = a.astype(BF16)
        after = after + jnp.sum(lks[p], axis=-1, keepdims=True)

    lam_val = _lam_value(lam_ref, lam_init)
    maps = []
    for mp in range(2):
        sd = pages((3 + mp) * g)
        s_pages = [sd[p] + bs_ref[1, p] for p in range(n_pages)]
        s_pages.append(jnp.where(new_ok, sd[n_pages] + bs_ref[1, n_pages], NEG))
        maps.append(_softmax_pages(s_pages))
    for p in range(npg):
        p_sc[p, 3 * g:4 * g, :] = (maps[0][p] - lam_val * maps[1][p]).astype(BF16)

    padv = jnp.zeros((LANES - vn_ref.shape[0], D_QKV), BF16)
    acc = _nn(p_sc[n_pages], jnp.concatenate([vn_ref[...], padv], axis=0))
    for p in range(n_pages):
        acc = acc + _nt(p_sc[p], v_refs[p][...].astype(BF16))
    rm = acc * omask_ref[...]
    y = rm[0:8, :]
    for i in range(1, N_HEADS * dec_seq // 8):
        y = y + rm[8 * i:8 * (i + 1), :]
    o8 = y + pltpu.roll(y, dec_seq, 0)
    ms = _nn_split(o8 * o8, gd1_ref[...], 2)
    bc = _nn_split(lax.rsqrt(ms + EPS), gd2_ref[...], 3)
    o8 = o8 * (bc * gmul_ref[...] + gadd_ref[...])
    o_ref[...] = o8[0:dec_seq, :]


def _sample_attention(layer, page_table, q8, kn, vn, lfn, bs, consts, lam, gmul, gadd,
                      cache_k, cache_v, cache_lft, lam_init):
    db, n_pages = page_table.shape
    dec_seq = lfn.shape[1] // HPM
    page = cache_k.shape[3]
    qmask, omask, sl, gd1, gd2 = consts
    rows = N_SLOTS * dec_seq

    def batch_spec(shape):
        nd = len(shape)
        return pl.BlockSpec((None,) + shape, lambda b, pt: (b,) + (0,) * nd)

    def const(shape):
        nd = len(shape)
        return pl.BlockSpec(shape, lambda b, pt: (0,) * nd)

    def paged(width_rows, width_cols):
        return [pl.BlockSpec((None, None, width_rows, width_cols),
                             functools.partial(lambda b, pt, j: (layer, pt[b, j], 0, 0), j=j))
                for j in range(n_pages)]

    in_specs = ([batch_spec((8, D_QKV)), batch_spec(kn.shape[1:]), batch_spec(vn.shape[1:]),
                 batch_spec(lfn.shape[1:]), const(bs.shape), const(qmask.shape), const(omask.shape),
                 const(sl.shape), const(lam.shape), const(gd1.shape), const(gd2.shape),
                 const(gmul.shape), const(gadd.shape)]
                + paged(D_QKV, page) + paged(D_QKV, page) + paged(HPM, page))
    grid_spec = pltpu.PrefetchScalarGridSpec(
        num_scalar_prefetch=1,
        grid=(db,),
        in_specs=in_specs,
        out_specs=pl.BlockSpec((None, dec_seq, D_QKV), lambda b, pt: (b, 0, 0)),
        scratch_shapes=[pltpu.VMEM((n_pages + 1, rows, LANES), F32),
                        pltpu.VMEM((n_pages + 1, N_HEADS * dec_seq, LANES), BF16)],
    )
    return pl.pallas_call(
        functools.partial(_sample_kernel, n_pages=n_pages, dec_seq=dec_seq, lam_init=lam_init),
        grid_spec=grid_spec,
        out_shape=jax.ShapeDtypeStruct((db, dec_seq, D_QKV), F32),
        compiler_params=_params("arbitrary"),
        name="sample_attention",
    )(page_table, q8, kn, vn, lfn, bs, qmask, omask, sl, lam, gd1, gd2, gmul, gadd,
      *([cache_k] * n_pages), *([cache_v] * n_pages), *([cache_lft] * n_pages))


def _merge_kernel(x_ref, o0_ref, o1_ref, o2_ref, o3_ref, gmix_ref, wg_ref, wbr_ref, wo_ref, y_ref):
    x = x_ref[...]
    h = (x * lax.rsqrt(jnp.mean(x * x, axis=-1, keepdims=True) + EPS)) * gmix_ref[...]
    hb = h.astype(BF16)
    acc = jnp.zeros(x.shape, F32)
    for m, o_ref in enumerate((o0_ref, o1_ref, o2_ref, o3_ref)):
        gate = 1.0 / (1.0 + jnp.exp(-_nt(hb, wg_ref[m * D_MODEL:(m + 1) * D_MODEL, :])))
        acc = acc + gate * _nn(o_ref[...].astype(BF16), wbr_ref[m])
    y_ref[...] = x + _nn(acc.astype(BF16), wo_ref[...])


def _merge(x, os_, gmix, wg, wbr, wo, tm):
    rows = x.shape[0]
    row = lambda w: pl.BlockSpec((tm, w), lambda i: (i, 0))
    return pl.pallas_call(
        _merge_kernel,
        grid=(rows // tm,),
        in_specs=[row(D_MODEL)] + [row(MIX_WIDTH)] * 4
        + [_const_spec((1, D_MODEL)), _const_spec((N_MIXERS * D_MODEL, D_MODEL)),
           _const_spec((N_MIXERS, MIX_WIDTH, D_MODEL)), _const_spec((D_MODEL, D_MODEL))],
        out_specs=row(D_MODEL),
        out_shape=jax.ShapeDtypeStruct((rows, D_MODEL), F32),
        compiler_params=_params("parallel"),
        name="merge",
    )(x, *os_, gmix, wg, wbr, wo)


def _mlp_kernel(x_ref, g_ref, wup_ref, wdn_ref, y_ref, *, chunk):
    x = x_ref[...]
    h = ((x * lax.rsqrt(jnp.mean(x * x, axis=-1, keepdims=True) + EPS)) * g_ref[...]).astype(BF16)
    acc = x
    for c in range(D_FF // chunk):
        u = jnp.maximum(_nn(h, wup_ref[:, c * chunk:(c + 1) * chunk]), 0.0)
        acc = acc + _nn((u * u).astype(BF16), wdn_ref[c * chunk:(c + 1) * chunk, :])
    y_ref[...] = acc


def _mlp(x, g, wup, wdn, tm):
    rows = x.shape[0]
    row = pl.BlockSpec((tm, D_MODEL), lambda i: (i, 0))
    return pl.pallas_call(
        functools.partial(_mlp_kernel, chunk=1024),
        grid=(rows // tm,),
        in_specs=[row, _const_spec((1, D_MODEL)), _const_spec((D_MODEL, D_FF)), _const_spec((D_FF, D_MODEL))],
        out_specs=row,
        out_shape=jax.ShapeDtypeStruct((rows, D_MODEL), F32),
        compiler_params=_params("parallel"),
        name="mlp",
    )(x, g, wup, wdn)


def _qk_vectors(gq, gk, qscale):
    z = jnp.zeros((MIX_WIDTH,), F32)
    one = jnp.ones((MIX_WIDTH,), F32)
    t4 = lambda a: jnp.tile(a, HPM)
    sc = HEAD_DIM ** -0.5 * qscale
    qmul = jnp.concatenate([t4(gq[0]) * sc, t4(gq[1]) * sc, z, t4(gq[2]) * (DIFF_HALF ** -0.5 * qscale)])
    qadd = jnp.concatenate([z, z, one * sc, z])
    kmul = jnp.concatenate([t4(gk[0]), t4(gk[1]), z, t4(gk[2])])
    kadd = jnp.concatenate([z, z, one, z])
    return [a.reshape(1, D_QKV) for a in (qmul, qadd, kmul, kadd)]


def _bias_index_tiles(tq, n_pages, page, dec_seq):
    a = np.arange(tq)
    diag = _t5_bucket_np(a[None, :] - a[:, None])
    sub = _t5_bucket_np(tq + a[None, :] - a[:, None])
    idxp = np.stack([diag, sub]).astype(np.int32)
    past_len = n_pages * page
    qpos = past_len + (np.arange(HPM * dec_seq) % dec_seq)
    kpos = np.concatenate([np.arange(past_len), past_len + np.arange(page)])
    idxs = _t5_bucket_np(qpos[None, :, None] - kpos.reshape(n_pages + 1, 1, page))
    return idxp, idxs.astype(np.int32)


def _layer(l, x, past, w, consts, tq, tm):
    (tab, bp, bs, g1, g2, su_q, tri_c, sl_p, sample_consts, gd) = consts
    (w_in, b_forget, g_q, g_k, lam, g_sub, w_branch, w_o, g_mix, g_mlp, w_up, w_down) = w
    b, t, _ = x.shape
    rows = b * t
    lam_init = 0.8 - 0.6 * math.exp(-0.3 * l)
    wt = w_in[l].T.astype(BF16)
    wqkv = wt[:3 * D_QKV]
    wf = jnp.pad(wt[3 * D_QKV:3 * D_QKV + HPM], ((0, LANES - HPM), (0, 0)))
    wg = wt[3 * D_QKV + HPM:]
    gmix = g_mix[l].reshape(1, D_MODEL)
    qk_vecs = _qk_vectors(g_q[l], g_k[l], LOG2E if past is None else 1.0)
    x2 = x.reshape(rows, D_MODEL)

    if past is None:
        cols = [jnp.broadcast_to(a.reshape(D_QKV, 1), (D_QKV, LANES)) for a in qk_vecs]
        bf_col = jnp.broadcast_to(jnp.pad(b_forget[l], (0, 8 - HPM))[:, None], (8, LANES))
        qt, kb3, vt, kf_t, vf_t, lf_t = _inproj_t(x, gmix, wqkv, wf, bf_col, *cols, tm=tm, tq=tq)
        to_heads = lambda a: jnp.transpose(a.reshape(b, N_HEADS, HEAD_DIM, t), (0, 3, 1, 2))
        k_new, v_new, lf_new = to_heads(kf_t), to_heads(vf_t), jnp.swapaxes(lf_t[:, :HPM], 1, 2)
        c = _cumsum(lf_t, tri_c)
        means = _block_means(kf_t)
        gsub_col = jnp.broadcast_to(g_sub[l][:, None], (HEAD_DIM, LANES))
        o_moba = _moba_prompt(tab, qt, kb3, vt, means, bp, tq)
        o_fox = _fox_prompt(qt, kb3, vt, c, tq)
        o_sb = _sb_prompt(qt, kb3, vt, su_q, tq)
        o_dif = _diff_prompt(tab, qt, kb3, vt, bp, lam[l], gsub_col, lam_init, tq)
        outs = [a.reshape(rows, MIX_WIDTH) for a in (o_moba, o_fox, o_sb, o_dif)]
    else:
        cache_k, cache_v, cache_lft, page_table = past
        bf = jnp.pad(b_forget[l], (0, LANES - HPM)).reshape(1, LANES)
        qb, kb, vb, kf, vf, lf = _inproj(x2, gmix, wqkv, wf, bf, g1, g2, *qk_vecs, tm=min(tm, rows))
        k_new, v_new = kf.reshape(b, t, N_HEADS, HEAD_DIM), vf.reshape(b, t, N_HEADS, HEAD_DIM)
        lf_new = lf.reshape(b, t, HPM)
        q4 = qb.astype(F32).reshape(b, t, D_QKV)
        q8 = jnp.concatenate([q4] * (8 // t), axis=1)
        padn = ((0, 0), (0, 16 - t), (0, 0))
        kn = jnp.pad(kb.reshape(b, t, D_QKV), padn)
        vn = jnp.pad(vb.reshape(b, t, D_QKV), padn)
        lfn = jnp.swapaxes(lf.reshape(b, t, HPM), 1, 2)
        lfn = jnp.pad(jnp.repeat(lfn, t, axis=1), ((0, 0), (0, 0), (0, 8 - t)))
        gd1, gd2 = gd
        z3 = jnp.zeros((3 * MIX_WIDTH,), F32)
        gmul = jnp.concatenate([z3, jnp.tile(g_sub[l], HPM) * (1.0 - lam_init)]).reshape(1, D_QKV)
        gadd = jnp.concatenate([z3 + 1.0, jnp.zeros((MIX_WIDTH,), F32)]).reshape(1, D_QKV)
        o = _sample_attention(l, page_table, q8, kn, vn, lfn, bs, (*sample_consts, sl_p, gd1, gd2),
                              lam[l], gmul, gadd, cache_k, cache_v, cache_lft, lam_init)
        o = o.reshape(rows, D_QKV)
        outs = [o[:, m * MIX_WIDTH:(m + 1) * MIX_WIDTH] for m in range(N_MIXERS)]

    y = _merge(x2, outs, gmix, wg, w_branch[l].astype(BF16), w_o[l].astype(BF16), tm=min(tm, rows))
    y = _mlp(y, g_mlp[l].reshape(1, D_MODEL), w_up[l].astype(BF16), w_down[l].astype(BF16), tm=min(tm, rows))
    return y.reshape(b, t, D_MODEL), k_new, v_new, lf_new


def _constants(rel_bias, tq, n_pages, page, dec_seq):
    idxp, idxs = _bias_index_tiles(tq, n_pages, page, dec_seq)
    tab = rel_bias.T
    bp, bs = _bias_tiles(tab, jnp.asarray(idxp), jnp.asarray(idxs), dec_seq)
    g1, g2 = _qk_group_mats()
    qmask, omask = _sample_masks(dec_seq)
    gd1 = np.zeros((D_QKV, LANES), np.float32)
    gd2 = np.zeros((LANES, D_QKV), np.float32)
    for c in range(3 * MIX_WIDTH, D_QKV):
        gd1[c, (c - 3 * MIX_WIDTH) // HEAD_DIM] = 1.0 / HEAD_DIM
        gd2[(c - 3 * MIX_WIDTH) // HEAD_DIM, c] = 1.0
    su_q = np.concatenate([_strict_lower(tq).T, np.ones((ONES_ROWS, tq), np.float32)], axis=0)
    bf = lambda a: jnp.asarray(a, BF16)
    return (tab, bp, bs, bf(g1), bf(g2), bf(su_q), bf(_upper_incl(tq)), bf(_strict_lower(page)),
            (jnp.asarray(qmask), jnp.asarray(omask)), (bf(gd1), bf(gd2)))


def kernel(x_prompt, x_sample, cache_k, cache_v, cache_logf, page_table, rel_bias, w_in, b_forget,
           g_q, g_k, lam, g_sub, w_branch, w_o, g_mix, g_mlp, w_up, w_down):
    depth = w_in.shape[0]
    tq = MOBA_BLOCK
    tm = 512
    n_pages = page_table.shape[1]
    page = cache_k.shape[2]
    dec_seq = x_sample.shape[1]
    n_phys = cache_k.shape[1]
    assert (n_pages * page) % MOBA_BLOCK == 0 and 8 % dec_seq == 0 and x_prompt.shape[1] % tq == 0

    consts = _constants(rel_bias, tq, n_pages, page, dec_seq)
    w = (w_in, b_forget, g_q, g_k, lam, g_sub, w_branch, w_o, g_mix, g_mlp, w_up, w_down)
    to_pages = lambda c: jnp.transpose(c, (0, 1, 3, 4, 2)).reshape(depth, n_phys, D_QKV, page)
    past = (to_pages(cache_k), to_pages(cache_v), jnp.swapaxes(cache_logf, 2, 3), page_table)

    hp, hs = x_prompt, x_sample
    outs = [[] for _ in range(6)]
    for l in range(depth):
        hp, kp, vp, fp = _layer(l, hp, None, w, consts, tq, tm)
        hs, ks, vs, fs = _layer(l, hs, past, w, consts, tq, tm)
        for lst, a in zip(outs, (kp, vp, fp, ks, vs, fs)):
            lst.append(a)
    return (hp, hs) + tuple(jnp.stack(a) for a in outs)
```

```python
import functools
import math

import numpy as np
import jax
import jax.numpy as jnp
from jax import lax
from jax.experimental import pallas as pl
from jax.experimental.pallas import tpu as pltpu

F32 = jnp.float32
BF16 = jnp.bfloat16

D_MODEL = 1024
HEAD_DIM = 64
N_MIXERS = 4
HPM = 4
N_HEADS = 16
MIX_WIDTH = HPM * HEAD_DIM
D_QKV = N_HEADS * HEAD_DIM
DIFF_HALF = HEAD_DIM // 2
D_FF = 4 * D_MODEL
MOBA_BLOCK = 256
MOBA_TOPK = 3
N_BUCKETS = 32
MAX_DISTANCE = 128
EPS = 1e-6
NEG = -1e30
LOG2E = 1.4426950408889634
LANES = 128
PAIR = 2 * HEAD_DIM
VMEM_LIMIT = 56 * 1024 * 1024


def _params(*sem):
    return pltpu.CompilerParams(dimension_semantics=sem, vmem_limit_bytes=VMEM_LIMIT)


def _nt(a, b):
    return lax.dot_general(a, b, (((1,), (1,)), ((), ())), preferred_element_type=F32)


def _nn(a, b):
    return jnp.dot(a, b, preferred_element_type=F32)


def _split2(x):
    hi = x.astype(BF16)
    lo = (x - hi.astype(F32)).astype(BF16)
    return hi, lo


def _split3(x):
    hi = x.astype(BF16)
    r = x - hi.astype(F32)
    mid = r.astype(BF16)
    lo = (r - mid.astype(F32)).astype(BF16)
    return hi, mid, lo


def _nn_split(x, w, parts):
    pieces = _split2(x) if parts == 2 else _split3(x)
    out = _nn(pieces[0], w)
    for p in pieces[1:]:
        out = out + _nn(p, w)
    return out


def _log_sigmoid(z):
    return jnp.minimum(z, 0.0) - jnp.log1p(jnp.exp(-jnp.abs(z)))


def _const_spec(shape):
    nd = len(shape)
    return pl.BlockSpec(shape, lambda *_: (0,) * nd)


def _t5_bucket_np(rel):
    n = np.maximum(rel, 0)
    max_exact = N_BUCKETS // 2
    nf = np.maximum(n, 1).astype(np.float32)
    large = max_exact + (np.log(nf / np.float32(max_exact)) / np.float32(math.log(MAX_DISTANCE / max_exact))
                         * np.float32(N_BUCKETS - max_exact)).astype(np.int32)
    return np.where(n < max_exact, n, np.minimum(large, N_BUCKETS - 1)).astype(np.int32)


def _qk_group_mats():
    g1 = np.zeros((D_QKV, LANES), np.float32)
    g2 = np.zeros((LANES, D_QKV), np.float32)
    for c in range(D_QKV):
        h = c // HEAD_DIM
        if h < 2 * HPM:
            gid, size = h, HEAD_DIM
        elif h < 3 * HPM:
            continue
        else:
            gid, size = 2 * HPM + (c - 3 * MIX_WIDTH) // DIFF_HALF, DIFF_HALF
        g1[c, gid] = 1.0 / size
        g2[gid, c] = 1.0
    return g1, g2


def _strict_lower(n):
    r = np.arange(n)
    return (r[:, None] > r[None, :]).astype(np.float32)


def _upper_incl(n):
    r = np.arange(n)
    return (r[:, None] <= r[None, :]).astype(np.float32)


def _inproj_kernel(x_ref, gmix_ref, wqkv_ref, wf_ref, bf_ref, g1_ref, g2_ref,
                   qmul_ref, qadd_ref, kmul_ref, kadd_ref,
                   qb_ref, kb_ref, vb_ref, kf_ref, vf_ref, lf_ref):
    x = x_ref[...]
    h = (x * lax.rsqrt(jnp.mean(x * x, axis=-1, keepdims=True) + EPS)) * gmix_ref[...]
    hb = h.astype(BF16)
    g1 = g1_ref[...]
    g2 = g2_ref[...]

    def group_norm(a, mul_ref, add_ref):
        ms = _nn_split(a * a, g1, 2)
        inv = lax.rsqrt(ms + EPS)
        bc = _nn_split(inv, g2, 3)
        return a * (bc * mul_ref[...] + add_ref[...])

    q = _nt(hb, wqkv_ref[0:D_QKV, :])
    qb_ref[...] = group_norm(q, qmul_ref, qadd_ref).astype(BF16)
    k = _nt(hb, wqkv_ref[D_QKV:2 * D_QKV, :])
    kn = group_norm(k, kmul_ref, kadd_ref)
    kf_ref[...] = kn
    kb_ref[...] = kn.astype(BF16)
    v = _nt(hb, wqkv_ref[2 * D_QKV:3 * D_QKV, :])
    vf_ref[...] = v
    vb_ref[...] = v.astype(BF16)
    z = _nt(hb, wf_ref[...]) + bf_ref[...]
    lf_ref[...] = _log_sigmoid(z)[:, 0:HPM]


def _inproj(x, gmix, wqkv, wf, bf, g1, g2, qmul, qadd, kmul, kadd, tm):
    rows = x.shape[0]
    row = lambda w: pl.BlockSpec((tm, w), lambda i: (i, 0))
    vec = _const_spec((1, D_QKV))
    return pl.pallas_call(
        _inproj_kernel,
        grid=(rows // tm,),
        in_specs=[row(D_MODEL), _const_spec((1, D_MODEL)), _const_spec((3 * D_QKV, D_MODEL)),
                  _const_spec((LANES, D_MODEL)), _const_spec((1, LANES)),
                  _const_spec((D_QKV, LANES)), _const_spec((LANES, D_QKV)), vec, vec, vec, vec],
        out_specs=[row(D_QKV), row(D_QKV), row(D_QKV), row(D_QKV), row(D_QKV), row(HPM)],
        out_shape=[jax.ShapeDtypeStruct((rows, D_QKV), BF16)] * 3
        + [jax.ShapeDtypeStruct((rows, D_QKV), F32)] * 2 + [jax.ShapeDtypeStruct((rows, HPM), F32)],
        compiler_params=_params("parallel"),
        name="inproj",
    )(x, gmix, wqkv, wf, bf, g1, g2, qmul, qadd, kmul, kadd)


def _inproj_t_kernel(x_ref, gmix_ref, wqkv_ref, wf_ref, bf_ref, qmul_ref, qadd_ref, kmul_ref, kadd_ref, *refs):
    qt_ref, kb_ref, vt_ref, kf_ref, vf_ref, lf_ref = refs[-6:]
    x = x_ref[...]
    tm = x.shape[0]
    h = (x * lax.rsqrt(jnp.mean(x * x, axis=-1, keepdims=True) + EPS)) * gmix_ref[...]
    hb = h.astype(BF16)
    wide = lambda ref: jnp.concatenate([ref[...]] * (tm // LANES), axis=1)
    n64, n32 = 2 * MIX_WIDTH, MIX_WIDTH

    def group_norm(a, mul_ref, add_ref):
        sq = a * a

        def inv_rms(rows0, rows1, width):
            g = sq[rows0:rows1].reshape((rows1 - rows0) // width, width, tm)
            inv = lax.rsqrt(jnp.mean(g, axis=1, keepdims=True) + EPS)
            return jnp.broadcast_to(inv, g.shape).reshape(rows1 - rows0, tm)

        scale = jnp.concatenate([inv_rms(0, n64, HEAD_DIM), jnp.zeros((D_QKV - n64 - n32, tm), F32),
                                 inv_rms(D_QKV - n32, D_QKV, DIFF_HALF)], axis=0)
        return a * (scale * wide(mul_ref) + wide(add_ref))

    q = _nt(wqkv_ref[0:D_QKV, :], hb)
    qt_ref[...] = group_norm(q, qmul_ref, qadd_ref).astype(BF16)
    k = _nt(wqkv_ref[D_QKV:2 * D_QKV, :], hb)
    kn = group_norm(k, kmul_ref, kadd_ref)
    kf_ref[...] = kn
    kb_ref[...] = kn.T.astype(BF16)
    v = _nt(wqkv_ref[2 * D_QKV:3 * D_QKV, :], hb)
    vf_ref[...] = v
    vb = v.astype(BF16)
    tq = vt_ref.shape[2]
    for i in range(tm // tq):
        vt_ref[i] = vb[:, i * tq:(i + 1) * tq]
    z = _nt(wf_ref[...], hb)[0:8, :] + wide(bf_ref)
    lf_ref[...] = _log_sigmoid(z)


def _inproj_t(x, gmix, wqkv, wf, bf_col, qmul, qadd, kmul, kadd, tm, tq, layer, depth, kv_all):
    b, t, _ = x.shape
    col = _const_spec((D_QKV, LANES))
    feat = lambda rows: pl.BlockSpec((None, rows, tm), lambda bi, i: (bi, 0, i))
    layered = pl.BlockSpec((None, None, D_QKV, tm), lambda bi, i: (layer, bi, 0, i))
    operands = [x, gmix, wqkv, wf, bf_col, qmul, qadd, kmul, kadd]
    in_specs = [pl.BlockSpec((None, tm, D_MODEL), lambda bi, i: (bi, i, 0)), _const_spec((1, D_MODEL)),
                _const_spec((3 * D_QKV, D_MODEL)), _const_spec((LANES, D_MODEL)), _const_spec((8, LANES)),
                col, col, col, col]
    aliases = {}
    if kv_all is not None:
        aliases = {len(operands): 3, len(operands) + 1: 4}
        operands += list(kv_all)
        in_specs += [pl.BlockSpec(memory_space=pl.ANY)] * 2
    return pl.pallas_call(
        _inproj_t_kernel,
        grid=(b, t // tm),
        in_specs=in_specs,
        out_specs=[feat(D_QKV), pl.BlockSpec((None, tm, D_QKV), lambda bi, i: (bi, i, 0)),
                   pl.BlockSpec((None, tm // tq, D_QKV, tq), lambda bi, i: (bi, i, 0, 0)),
                   layered, layered, feat(8)],
        out_shape=[jax.ShapeDtypeStruct((b, D_QKV, t), BF16), jax.ShapeDtypeStruct((b, t, D_QKV), BF16),
                   jax.ShapeDtypeStruct((b, t // tq, D_QKV, tq), BF16),
                   jax.ShapeDtypeStruct((depth, b, D_QKV, t), F32), jax.ShapeDtypeStruct((depth, b, D_QKV, t), F32),
                   jax.ShapeDtypeStruct((b, 8, t), F32)],
        input_output_aliases=aliases,
        compiler_params=_params("parallel", "parallel"),
        name="inproj_t",
    )(*operands)


def _cumsum_kernel(lf_ref, tri_ref, c_ref, *, t, blk):
    tri = tri_ref[...]
    carry = jnp.zeros((8, 1), F32)
    for i in range(t // blk):
        cs = _nn_split(lf_ref[:, i * blk:(i + 1) * blk], tri, 3) + carry
        c_ref[:, i * blk:(i + 1) * blk] = cs
        carry = cs[:, blk - 1:blk]


def _cumsum(lf_t, tri):
    b, r, t = lf_t.shape
    blk = tri.shape[0]
    return pl.pallas_call(
        functools.partial(_cumsum_kernel, t=t, blk=blk),
        grid=(b,),
        in_specs=[pl.BlockSpec((None, r, t), lambda i: (i, 0, 0)), _const_spec((blk, blk))],
        out_specs=pl.BlockSpec((None, r, t), lambda i: (i, 0, 0)),
        out_shape=jax.ShapeDtypeStruct((b, r, t), F32),
        compiler_params=_params("parallel"),
        name="logf_cumsum",
    )(lf_t, tri)


def _means_kernel(k_ref, o_ref, *, nb):
    lane = lax.broadcasted_iota(jnp.int32, (1, LANES), 1)
    acc = jnp.zeros(o_ref.shape, F32)
    for n in range(nb):
        col = jnp.mean(k_ref[:, n * MOBA_BLOCK:(n + 1) * MOBA_BLOCK], axis=1, keepdims=True)
        acc = jnp.where(lane == n, col, acc)
    o_ref[...] = acc


def _block_means(kf_all, layer):
    _, b, _, t = kf_all.shape
    nb = t // MOBA_BLOCK
    assert nb <= LANES
    return pl.pallas_call(
        functools.partial(_means_kernel, nb=nb),
        grid=(b,),
        in_specs=[pl.BlockSpec((None, None, MIX_WIDTH, t), lambda i: (layer, i, 0, 0))],
        out_specs=pl.BlockSpec((None, MIX_WIDTH, LANES), lambda i: (i, 0, 0)),
        out_shape=jax.ShapeDtypeStruct((b, MIX_WIDTH, LANES), F32),
        compiler_params=_params("parallel"),
        name="moba_block_means",
    )(kf_all)


def _bias_kernel(tab_ref, idxp_ref, idxs_ref, bp_ref, bs_ref, *, dec_seq):
    idxp = idxp_ref[...]
    idxs = idxs_ref[...]
    row = lax.broadcasted_iota(jnp.int32, idxs.shape, 1)
    for h in range(2 * HPM):
        acc = jnp.zeros(idxp.shape, F32)
        for b in range(N_BUCKETS):
            acc = jnp.where(idxp == b, tab_ref[h, b] * LOG2E, acc)
        bp_ref[h] = acc
    for m in range(2):
        acc = jnp.zeros(idxs.shape, F32)
        for hl in range(HPM):
            for b in range(N_BUCKETS):
                acc = jnp.where((idxs == b) & (row // dec_seq == hl), tab_ref[m * HPM + hl, b], acc)
        bs_ref[m] = acc


def _bias_tiles(tab, idxp, idxs, dec_seq):
    return pl.pallas_call(
        functools.partial(_bias_kernel, dec_seq=dec_seq),
        in_specs=[pl.BlockSpec(memory_space=pltpu.SMEM), pl.BlockSpec(memory_space=pltpu.VMEM),
                  pl.BlockSpec(memory_space=pltpu.VMEM)],
        out_specs=[pl.BlockSpec(memory_space=pltpu.VMEM), pl.BlockSpec(memory_space=pltpu.VMEM)],
        out_shape=[jax.ShapeDtypeStruct((2 * HPM,) + idxp.shape, F32),
                   jax.ShapeDtypeStruct((2,) + idxs.shape, F32)],
        compiler_params=pltpu.CompilerParams(vmem_limit_bytes=VMEM_LIMIT),
        name="t5_bias_tiles",
    )(tab, idxp, idxs)


ONES_ROWS = 16
ACC_ROWS = HEAD_DIM + ONES_ROWS
EXP2_DEAD = -200.0


def _rows(n):
    return lax.broadcasted_iota(jnp.int32, (n, 1), 0)


def _key_before_query(tk, tq, strict):
    s = lax.broadcasted_iota(jnp.int32, (tk, tq), 0)
    t = lax.broadcasted_iota(jnp.int32, (tk, tq), 1)
    return (s < t) if strict else (s <= t)


def _row_range(qt, lo, hi):
    r = _rows(qt.shape[0])
    return jnp.where((r >= lo) & (r < hi), qt, jnp.zeros_like(qt))


def _v_aug(vt, x):
    return jnp.concatenate([vt[x * HEAD_DIM:(x + 1) * HEAD_DIM, :],
                            jnp.ones((ONES_ROWS, vt.shape[1]), BF16)], axis=0)


def _extra_rows(by_row, n, width):
    r = _rows(n)
    out = jnp.zeros((n, width), F32)
    for row, val in by_row.items():
        out = jnp.where(r == row, val, out)
    return out


def _pieces(by_row_f32, base):
    out = {}
    for i, val in enumerate(by_row_f32):
        for j, piece in enumerate(_split3(val)):
            out[base + 3 * i + j] = piece.astype(F32)
    return out


def _softmax_step(i, s, va, m_sc, acc_sc, sel=None):
    m_old = m_sc[i]
    bm = jnp.max(s, axis=0, keepdims=True)
    if sel is not None:
        bm = jnp.where(sel, bm, NEG)
    m_new = jnp.maximum(m_old, bm)
    m_sub = m_new if sel is None else jnp.where(sel, m_new, -NEG)
    p = jnp.exp2(s - m_sub)
    acc_sc[i] = acc_sc[i] * jnp.exp2(m_old - m_new) + _nn(va, p.astype(BF16))
    m_sc[i] = m_new


def _normalized(acc_sc, i):
    a = acc_sc[i]
    return a[0:HEAD_DIM, :] / a[HEAD_DIM:HEAD_DIM + 1, :]


def _init(m_sc, acc_sc):
    m_sc[...] = jnp.full(m_sc.shape, NEG, F32)
    acc_sc[...] = jnp.zeros(acc_sc.shape, F32)


def _sweep(qi, group, near, far, alive=None):
    @pl.when(qi == 0)
    def _():
        near(1)

    @pl.when(qi > 0)
    def _():
        near(2)

    nf = jnp.maximum(qi - 1, 0)
    if alive is None:
        def body(i, c):
            top = nf - 1 - group * i
            far([top - g for g in range(group)])
            return c

        lax.fori_loop(0, nf // group, body, 0)
        ok = None
    else:
        def body(c):
            i, _ = c
            top = nf - 1 - group * i
            far([top - g for g in range(group)])
            return i + 1, alive(top - group + 1)

        _, ok = lax.while_loop(lambda c: (c[0] < nf // group) & c[1], body, (0, alive(nf)))
    rem = nf % group
    size = group // 2
    while size >= 1:
        top = (rem & (2 * size - 1)) - 1
        cond = (rem & size) != 0
        if ok is not None:
            cond = cond & ok

        @pl.when(cond)
        def _(size=size, top=top):
            far([top - g for g in range(size)])

        if ok is not None and size > 1:
            ok = ok & alive(rem & (size - 1))
        size //= 2


def _softmax_blocks(items, scores, v_of, m_sc, acc_sc, tq):
    ss = [scores(j, kind, i) for (j, kind, i, x, sel) in items]
    for s, (j, kind, i, x, sel) in zip(ss, items):
        if kind == "diag":
            s = jnp.where(_key_before_query(tq, tq, False), s, NEG)
        _softmax_step(i, s, _v_aug(v_of(j), x), m_sc, acc_sc, sel(j, x) if sel is not None else None)


def _fox_kernel(qt_ref, k_ref, vt_ref, cq_ref, call_ref, cend_ref, o_ref, kx_sc, kn_sc, m_sc, acc_sc, *, tq):
    qi = pl.program_id(1)

    def head_row(c, h):
        return c[h:h + 1, :]

    @pl.when(qi == 0)
    def _():
        c = call_ref[...] * LOG2E
        for p in range(HPM // 2):
            by_row = _pieces([head_row(c, 2 * p), head_row(c, 2 * p + 1)], 0)
            by_row.update({6: 1.0, 7: 1.0, 8: 1.0})
            e = _extra_rows(by_row, 16, c.shape[1])
            e = jnp.concatenate([e, jnp.zeros((PAIR - 16, c.shape[1]), F32)], axis=0)
            kx_sc[p] = e.T.astype(BF16)
        ksq = jnp.square(k_ref[...].astype(F32))
        lane = lax.broadcasted_iota(jnp.int32, (1, MIX_WIDTH), 1)
        for h in range(HPM):
            n2 = jnp.sum(jnp.where(lane // HEAD_DIM == h, ksq, 0.0), axis=1, keepdims=True)
            kn_sc[h] = jnp.sqrt(jnp.max(n2, axis=0, keepdims=True))

    cq = cq_ref[...] * LOG2E
    qps, cts, zbs = [], [], []
    for h in range(HPM):
        p, x = divmod(h, 2)
        ct = head_row(cq, h)
        by_row = {3 * x: -1.0, 3 * x + 1: -1.0, 3 * x + 2: -1.0}
        by_row.update(_pieces([ct], 6))
        qx = _extra_rows(by_row, PAIR, tq).astype(BF16)
        qh = _row_range(qt_ref[p * PAIR:(p + 1) * PAIR, :], x * HEAD_DIM, (x + 1) * HEAD_DIM)
        qps.append(jnp.concatenate([qh, qx], axis=0))
        cts.append(ct)
        qn = jnp.sqrt(jnp.sum(jnp.square(qh.astype(F32)), axis=0, keepdims=True))
        zbs.append(qn * kn_sc[h] * 1.001 + 0.001)
    _init(m_sc, acc_sc)

    def alive(low):
        lane = lax.broadcasted_iota(jnp.int32, cend_ref.shape, 1)
        cl = jnp.sum(jnp.where(lane == low - 1, cend_ref[...] * LOG2E, 0.0), axis=1, keepdims=True)
        worst = zbs[0] + cts[0] - head_row(cl, 0) - m_sc[0]
        for h in range(1, HPM):
            worst = jnp.maximum(worst, zbs[h] + cts[h] - head_row(cl, h) - m_sc[h])
        return jnp.max(worst) > EXP2_DEAD

    def scores(j, kind, h):
        r0 = pl.multiple_of(j * tq, tq)
        p = h // 2
        kp = jnp.concatenate([k_ref[pl.ds(r0, tq), p * PAIR:(p + 1) * PAIR], kx_sc[p, pl.ds(r0, tq), :]], axis=1)
        return _nn(kp, qps[h])

    def run(blocks):
        items = [(j, kind, h, h, None) for j, kind in blocks for h in range(HPM)]
        _softmax_blocks(items, scores, lambda j: vt_ref[j], m_sc, acc_sc, tq)

    _sweep(qi, 4,
           lambda n: run([(qi, "diag")] + ([(qi - 1, "far")] if n == 2 else [])),
           lambda js: run([(j, "far") for j in js]), alive)
    o = jnp.concatenate([_normalized(acc_sc, h) for h in range(HPM)], axis=0)
    o_ref[...] = o.T.astype(o_ref.dtype)


def _moba_kernel(tab_ref, qt_ref, k_ref, vt_ref, mean_ref, bt_ref, o_ref, m_sc, acc_sc, sel_sc, *, tq, nb):
    qi = pl.program_id(1)
    ones_k = jnp.ones((tq, PAIR), BF16)
    nbp = sel_sc.shape[1]
    rb = _rows(nbp)
    qs, qps = [], []
    for h in range(HPM):
        p, x = divmod(h, 2)
        mh, ml = _split2(mean_ref[p * PAIR:(p + 1) * PAIR, :].T)
        qx = _row_range(qt_ref[p * PAIR:(p + 1) * PAIR, :], x * HEAD_DIM, (x + 1) * HEAD_DIM)
        qs.append(qx)
        far = jnp.full((1, tq), tab_ref[h, N_BUCKETS - 1] * LOG2E, F32)
        qps.append(jnp.concatenate([qx, _extra_rows(_pieces([far], 0), PAIR, tq).astype(BF16)], axis=0))
        gate = (_nn(mh, qx) + _nn(ml, qx))[0:nbp, :]
        cnt = jnp.zeros((nbp, tq), jnp.int32)
        for n in range(nb):
            gn = gate[n:n + 1, :]
            beats = (gn > gate) | ((gn == gate) & (n < rb))
            cnt = cnt + jnp.where(beats, jnp.where(n < qi, 1, 0), 0)
        sel_sc[h] = jnp.where((cnt < MOBA_TOPK) & (rb < qi), 1.0, 0.0)
    _init(m_sc, acc_sc)

    def scores(j, kind, h):
        p = h // 2
        k = k_ref[pl.ds(pl.multiple_of(j * tq, tq), tq), p * PAIR:(p + 1) * PAIR]
        if kind == "far":
            return _nn(jnp.concatenate([k, ones_k], axis=1), qps[h])
        return _nn(k, qs[h]) + bt_ref[h, 0 if kind == "diag" else 1]

    def run(blocks):
        items = [(j, kind, h, h, None if kind == "diag" else (lambda j, h: sel_sc[h, pl.ds(j, 1), :] > 0.0))
                 for j, kind in blocks for h in range(HPM)]
        _softmax_blocks(items, scores, lambda j: vt_ref[j], m_sc, acc_sc, tq)

    _sweep(qi, 4,
           lambda n: run([(qi, "diag")] + ([(qi - 1, "sub")] if n == 2 else [])),
           lambda js: run([(j, "far") for j in js]))
    o = jnp.concatenate([_normalized(acc_sc, h) for h in range(HPM)], axis=0)
    o_ref[...] = o.T.astype(o_ref.dtype)


def _sb_kernel(qt_ref, k_ref, vt_ref, su_ref, o_ref, r_sc, acc_sc, *, tq):
    qi = pl.program_id(1)
    qs = [_row_range(qt_ref[(h // 2) * PAIR:(h // 2 + 1) * PAIR, :], (h % 2) * HEAD_DIM, (h % 2 + 1) * HEAD_DIM)
          for h in range(HPM)]
    su = su_ref[...]
    r_sc[...] = jnp.zeros(r_sc.shape, F32)
    acc_sc[...] = jnp.zeros(acc_sc.shape, F32)

    def run(blocks):
        tiles = [(j, masked, h) for j, masked in blocks for h in range(HPM)]
        zs = [_nn(k_ref[pl.ds(pl.multiple_of(j * tq, tq), tq), (h // 2) * PAIR:(h // 2 + 1) * PAIR], qs[h])
              for j, masked, h in tiles]
        lbs, lats = [], []
        for z, (j, masked, h) in zip(zs, tiles):
            log_beta = jnp.minimum(z, 0.0) - jnp.log2(1.0 + jnp.exp2(-jnp.abs(z)))
            log_keep = log_beta - z
            if masked:
                log_keep = jnp.where(_key_before_query(tq, tq, True), log_keep, 0.0)
            hi, lo = _split2(log_keep)
            lbs.append(log_beta)
            lats.append(_nn(su, hi) + _nn(su, lo))
        r = [r_sc[h] for h in range(HPM)]
        pvs = [jnp.zeros((HEAD_DIM, tq), F32)] * HPM
        for log_beta, lat, (j, masked, h) in zip(lbs, lats, tiles):
            a = jnp.exp2(log_beta + lat[0:tq, :] + r[h])
            if masked:
                a = jnp.where(_key_before_query(tq, tq, True), a, 0.0)
            r[h] = r[h] + lat[tq:tq + 1, :]
            pvs[h] = pvs[h] + _nn(vt_ref[j][h * HEAD_DIM:(h + 1) * HEAD_DIM, :], a.astype(BF16))
        for h in range(HPM):
            r_sc[h] = r[h]
            acc_sc[h] = acc_sc[h] + pvs[h]

    def alive(low):
        return jnp.max(r_sc[...]) > EXP2_DEAD

    _sweep(qi, 2,
           lambda n: run([(qi, True)] + ([(qi - 1, False)] if n == 2 else [])),
           lambda js: run([(j, False) for j in js]), alive)
    o = jnp.concatenate([acc_sc[h] for h in range(HPM)], axis=0)
    o_ref[...] = o.T.astype(o_ref.dtype)


def _lam_value(lam_ref, lam_init):
    lm = lam_ref[...]
    a = jnp.sum(lm[0:1] * lm[1:2], axis=-1, keepdims=True)
    b = jnp.sum(lm[2:3] * lm[3:4], axis=-1, keepdims=True)
    return jnp.exp(a) - jnp.exp(b) + lam_init


def _diff_kernel(tab_ref, qt_ref, k_ref, vt_ref, bt_ref, lam_ref, gsub_ref, o_ref, m_sc, acc_sc, *, tq, lam_init):
    qi = pl.program_id(1)
    ones_k = jnp.ones((tq, PAIR), BF16)
    qs, qps = [], []
    for i in range(2 * HPM):
        p, r = divmod(i * DIFF_HALF, PAIR)
        qx = _row_range(qt_ref[p * PAIR:(p + 1) * PAIR, :], r, r + DIFF_HALF)
        qs.append(qx)
        far = jnp.full((1, tq), tab_ref[HPM + i // 2, N_BUCKETS - 1] * LOG2E, F32)
        qps.append(jnp.concatenate([qx, _extra_rows(_pieces([far], 0), PAIR, tq).astype(BF16)], axis=0))
    _init(m_sc, acc_sc)

    def scores(j, kind, i):
        p = i // 4
        k = k_ref[pl.ds(pl.multiple_of(j * tq, tq), tq), p * PAIR:(p + 1) * PAIR]
        if kind == "far":
            return _nn(jnp.concatenate([k, ones_k], axis=1), qps[i])
        return _nn(k, qs[i]) + bt_ref[i // 2, 0 if kind == "diag" else 1]

    def run(blocks):
        items = [(j, kind, i, i // 2, None) for j, kind in blocks for i in range(2 * HPM)]
        _softmax_blocks(items, scores, lambda j: vt_ref[j], m_sc, acc_sc, tq)

    _sweep(qi, 2,
           lambda n: run([(qi, "diag")] + ([(qi - 1, "sub")] if n == 2 else [])),
           lambda js: run([(j, "far") for j in js]))
    lam_val = _lam_value(lam_ref, lam_init)
    gcol = jnp.concatenate([gsub_ref[...]] * (tq // LANES), axis=1)
    outs = []
    for x in range(HPM):
        o = _normalized(acc_sc, 2 * x) - lam_val * _normalized(acc_sc, 2 * x + 1)
        inv = lax.rsqrt(jnp.mean(o * o, axis=0, keepdims=True) + EPS)
        outs.append(((o * inv) * gcol) * (1.0 - lam_init))
    o_ref[...] = jnp.concatenate(outs, axis=0).T.astype(o_ref.dtype)


def _mixer_specs(t, tq, mixer):
    nk = t // tq
    qspec = pl.BlockSpec((None, MIX_WIDTH, tq), lambda b, qi: (b, mixer, qi))
    kspec = pl.BlockSpec((None, t, MIX_WIDTH), lambda b, qi: (b, 0, mixer))
    vspec = pl.BlockSpec((None, nk, MIX_WIDTH, tq), lambda b, qi: (b, 0, mixer, 0))
    ospec = pl.BlockSpec((None, tq, MIX_WIDTH), lambda b, qi: (b, qi, 0))
    return qspec, kspec, vspec, ospec


def _attn_out(b, t):
    return jax.ShapeDtypeStruct((b, t, MIX_WIDTH), BF16)


def _stat_scratch(n_maps, tq, rows):
    return [pltpu.VMEM((n_maps, 1, tq), F32), pltpu.VMEM((n_maps, rows, tq), F32)]


def _fox_prompt(qt, kb, vt, c, tq):
    b, t, _ = kb.shape
    qspec, kspec, vspec, ospec = _mixer_specs(t, tq, 1)
    cq = pl.BlockSpec((None, 8, tq), lambda bi, qi: (bi, 0, qi))
    call = pl.BlockSpec((None, 8, t), lambda bi, qi: (bi, 0, 0))
    cend = c[:, :, tq - 1::tq]
    cends = pl.BlockSpec((None, 8, t // tq), lambda bi, qi: (bi, 0, 0))
    return pl.pallas_call(
        functools.partial(_fox_kernel, tq=tq),
        grid=(b, t // tq),
        in_specs=[qspec, kspec, vspec, cq, call, cends],
        out_specs=ospec,
        out_shape=_attn_out(b, t),
        scratch_shapes=[pltpu.VMEM((HPM // 2, t, PAIR), BF16), pltpu.VMEM((HPM, 1, 1), F32)]
        + _stat_scratch(HPM, tq, ACC_ROWS),
        compiler_params=_params("parallel", "arbitrary"),
        name="fox_prompt",
    )(qt, kb, vt, c, c, cend)


def _moba_prompt(tab, qt, kb, vt, means, bp, tq):
    b, t, _ = kb.shape
    nb = t // MOBA_BLOCK
    nbp = -(-nb // 8) * 8
    qspec, kspec, vspec, ospec = _mixer_specs(t, tq, 0)
    mspec = pl.BlockSpec((None, MIX_WIDTH, LANES), lambda bi, qi: (bi, 0, 0))
    bt = pl.BlockSpec((HPM, 2, tq, tq), lambda bi, qi: (0, 0, 0, 0))
    return pl.pallas_call(
        functools.partial(_moba_kernel, tq=tq, nb=nb),
        grid=(b, t // tq),
        in_specs=[pl.BlockSpec(memory_space=pltpu.SMEM), qspec, kspec, vspec, mspec, bt],
        out_specs=ospec,
        out_shape=_attn_out(b, t),
        scratch_shapes=_stat_scratch(HPM, tq, ACC_ROWS) + [pltpu.VMEM((HPM, nbp, tq), F32)],
        compiler_params=_params("parallel", "arbitrary"),
        name="moba_prompt",
    )(tab, qt, kb, vt, means, bp)


def _sb_prompt(qt, kb, vt, su, tq):
    b, t, _ = kb.shape
    qspec, kspec, vspec, ospec = _mixer_specs(t, tq, 2)
    return pl.pallas_call(
        functools.partial(_sb_kernel, tq=tq),
        grid=(b, t // tq),
        in_specs=[qspec, kspec, vspec, _const_spec(su.shape)],
        out_specs=ospec,
        out_shape=_attn_out(b, t),
        scratch_shapes=_stat_scratch(HPM, tq, HEAD_DIM),
        compiler_params=_params("parallel", "arbitrary"),
        name="sb_prompt",
    )(qt, kb, vt, su)


def _diff_prompt(tab, qt, kb, vt, bp, lam, gsub_col, lam_init, tq):
    b, t, _ = kb.shape
    qspec, kspec, vspec, ospec = _mixer_specs(t, tq, 3)
    bt = pl.BlockSpec((HPM, 2, tq, tq), lambda bi, qi: (1, 0, 0, 0))
    return pl.pallas_call(
        functools.partial(_diff_kernel, tq=tq, lam_init=lam_init),
        grid=(b, t // tq),
        in_specs=[pl.BlockSpec(memory_space=pltpu.SMEM), qspec, kspec, vspec, bt,
                  _const_spec((4, DIFF_HALF)), _const_spec((HEAD_DIM, LANES))],
        out_specs=ospec,
        out_shape=_attn_out(b, t),
        scratch_shapes=_stat_scratch(2 * HPM, tq, ACC_ROWS),
        compiler_params=_params("parallel", "arbitrary"),
        name="diff_prompt",
    )(tab, qt, kb, vt, bp, lam, gsub_col)


N_SLOTS = 20


def _sample_masks(dec_seq):
    rows = N_SLOTS * dec_seq
    qmask = np.zeros((rows, D_QKV), np.float32)
    for r in range(rows):
        slot = r // dec_seq
        if slot < 3 * HPM:
            qmask[r, slot * HEAD_DIM:(slot + 1) * HEAD_DIM] = 1.0
        elif slot < 4 * HPM:
            qmask[r, slot * HEAD_DIM:slot * HEAD_DIM + DIFF_HALF] = 1.0
        else:
            c0 = (slot - HPM) * HEAD_DIM + DIFF_HALF
            qmask[r, c0:c0 + DIFF_HALF] = 1.0
    omask = np.zeros((N_HEADS * dec_seq, D_QKV), np.float32)
    for r in range(N_HEADS * dec_seq):
        omask[r, (r // dec_seq) * HEAD_DIM:(r // dec_seq + 1) * HEAD_DIM] = 1.0
    return qmask, omask


def _expand_rows(a, dec_seq):
    rows = HPM * dec_seq
    r = lax.broadcasted_iota(jnp.int32, (rows, 1), 0)
    out = jnp.broadcast_to(a[HPM - 1:HPM, :], (rows, a.shape[1]))
    for h in range(HPM - 2, -1, -1):
        out = jnp.where(r < (h + 1) * dec_seq, a[h:h + 1, :], out)
    return out


def _softmax_pages(s_pages):
    m = s_pages[0].max(axis=-1, keepdims=True)
    for s in s_pages[1:]:
        m = jnp.maximum(m, s.max(axis=-1, keepdims=True))
    ps = [jnp.exp(s - m) for s in s_pages]
    l = ps[0].sum(axis=-1, keepdims=True)
    for p in ps[1:]:
        l = l + p.sum(axis=-1, keepdims=True)
    return [p / l for p in ps]


def _sample_kernel(*refs, n_pages, dec_seq, lam_init):
    pt_ref = refs[0]
    del pt_ref
    (q_ref, kn_ref, vn_ref, lfn_ref, bs_ref, qmask_ref, omask_ref, sl_ref, lam_ref,
     gd1_ref, gd2_ref, gmul_ref, gadd_ref) = refs[1:14]
    k_refs = refs[14:14 + n_pages]
    v_refs = refs[14 + n_pages:14 + 2 * n_pages]
    lf_refs = refs[14 + 2 * n_pages:14 + 3 * n_pages]
    o_ref = refs[14 + 3 * n_pages]
    s_sc, p_sc = refs[14 + 3 * n_pages + 1:]

    g = HPM * dec_seq
    npg = n_pages + 1
    q = q_ref[...]
    qbd = (jnp.concatenate([q] * (N_SLOTS * dec_seq // 8), axis=0) * qmask_ref[...]).astype(BF16)
    pad = jnp.zeros((LANES - kn_ref.shape[0], D_QKV), BF16)
    for p in range(n_pages):
        s_sc[p] = _nn(qbd, k_refs[p][...].astype(BF16))
    s_sc[n_pages] = _nt(qbd, jnp.concatenate([kn_ref[...], pad], axis=0))

    lane = lax.broadcasted_iota(jnp.int32, (1, LANES), 1)
    qrow = lax.broadcasted_iota(jnp.int32, (g, 1), 0) % dec_seq
    new_ok = lane <= qrow
    new_past = lane < qrow
    sl = sl_ref[...]

    def pages(r0):
        return [s_sc[p, r0:r0 + g, :] for p in range(npg)]

    sm = pages(0)
    ppb = MOBA_BLOCK // LANES
    nblk = n_pages // ppb
    gates = []
    for n in range(nblk):
        tot = sm[n * ppb]
        for i in range(1, ppb):
            tot = tot + sm[n * ppb + i]
        gates.append(jnp.sum(tot, axis=-1, keepdims=True))
    s_pages = []
    for n in range(nblk):
        cnt = jnp.zeros((g, 1), jnp.int32)
        for n2 in range(nblk):
            if n2 == n:
                continue
            beats = (gates[n2] > gates[n]) | ((gates[n2] == gates[n]) & (n2 < n))
            cnt = cnt + jnp.where(beats, 1, 0)
        for i in range(ppb):
            p = n * ppb + i
            s_pages.append(jnp.where(cnt < MOBA_TOPK, sm[p] + bs_ref[0, p], NEG))
    s_pages.append(jnp.where(new_ok, sm[n_pages] + bs_ref[0, n_pages], NEG))
    for p, pr in enumerate(_softmax_pages(s_pages)):
        p_sc[p, 0:g, :] = pr.astype(BF16)

    lfn = lfn_ref[...]
    cn_row = jnp.zeros((g, 1), F32)
    bias_new = jnp.zeros((g, LANES), F32)
    for n in range(dec_seq):
        col = lfn[:, n:n + 1]
        cn_row = cn_row + jnp.where(n <= qrow, col, 0.0)
        bias_new = bias_new + jnp.where((lane < n) & (n <= qrow), col, 0.0)
    lfe = [_expand_rows(lf_refs[p][...], dec_seq) for p in range(n_pages)]
    suffix = _nn_split(jnp.concatenate(lfe, axis=0), sl, 3)
    sf = pages(g)
    s_pages = [None] * npg
    s_pages[n_pages] = jnp.where(new_ok, sf[n_pages] + bias_new, NEG)
    after = cn_row
    for p in range(n_pages - 1, -1, -1):
        s_pages[p] = sf[p] + (suffix[p * g:(p + 1) * g, :] + after)
        after = after + jnp.sum(lfe[p], axis=-1, keepdims=True)
    for p, pr in enumerate(_softmax_pages(s_pages)):
        p_sc[p, g:2 * g, :] = pr.astype(BF16)

    ss = pages(2 * g)
    lbs, lks = [], []
    for p in range(npg):
        lb = _log_sigmoid(ss[p])
        lk = lb - ss[p]
        if p == n_pages:
            lk = jnp.where(new_past, lk, 0.0)
        lbs.append(lb)
        lks.append(lk)
    later_in = _nn_split(jnp.concatenate(lks, axis=0), sl, 2)
    after = jnp.zeros((g, 1), F32)
    for p in range(npg - 1, -1, -1):
        a = jnp.exp(lbs[p] + later_in[p * g:(p + 1) * g, :] + after)
        if p == n_pages:
            a = jnp.where(new_past, a, 0.0)
        p_sc[p, 2 * g:3 * g, :] = a.astype(BF16)
        after = after + jnp.sum(lks[p], axis=-1, keepdims=True)

    lam_val = _lam_value(lam_ref, lam_init)
    maps = []
    for mp in range(2):
        sd = pages((3 + mp) * g)
        s_pages = [sd[p] + bs_ref[1, p] for p in range(n_pages)]
        s_pages.append(jnp.where(new_ok, sd[n_pages] + bs_ref[1, n_pages], NEG))
        maps.append(_softmax_pages(s_pages))
    for p in range(npg):
        p_sc[p, 3 * g:4 * g, :] = (maps[0][p] - lam_val * maps[1][p]).astype(BF16)

    padv = jnp.zeros((LANES - vn_ref.shape[0], D_QKV), BF16)
    acc = _nn(p_sc[n_pages], jnp.concatenate([vn_ref[...], padv], axis=0))
    for p in range(n_pages):
        acc = acc + _nt(p_sc[p], v_refs[p][...].astype(BF16))
    rm = acc * omask_ref[...]
    y = rm[0:8, :]
    for i in range(1, N_HEADS * dec_seq // 8):
        y = y + rm[8 * i:8 * (i + 1), :]
    o8 = y + pltpu.roll(y, dec_seq, 0)
    ms = _nn_split(o8 * o8, gd1_ref[...], 2)
    bc = _nn_split(lax.rsqrt(ms + EPS), gd2_ref[...], 3)
    o8 = o8 * (bc * gmul_ref[...] + gadd_ref[...])
    o_ref[...] = o8[0:dec_seq, :]


def _sample_attention(layer, page_table, q8, kn, vn, lfn, bs, consts, lam, gmul, gadd,
                      cache_k, cache_v, cache_lft, lam_init):
    db, n_pages = page_table.shape
    dec_seq = lfn.shape[1] // HPM
    page = cache_k.shape[3]
    qmask, omask, sl, gd1, gd2 = consts
    rows = N_SLOTS * dec_seq

    def batch_spec(shape):
        nd = len(shape)
        return pl.BlockSpec((None,) + shape, lambda b, pt: (b,) + (0,) * nd)

    def const(shape):
        nd = len(shape)
        return pl.BlockSpec(shape, lambda b, pt: (0,) * nd)

    def paged(width_rows, width_cols):
        return [pl.BlockSpec((None, None, width_rows, width_cols),
                             functools.partial(lambda b, pt, j: (layer, pt[b, j], 0, 0), j=j))
                for j in range(n_pages)]

    in_specs = ([batch_spec((8, D_QKV)), batch_spec(kn.shape[1:]), batch_spec(vn.shape[1:]),
                 batch_spec(lfn.shape[1:]), const(bs.shape), const(qmask.shape), const(omask.shape),
                 const(sl.shape), const(lam.shape), const(gd1.shape), const(gd2.shape),
                 const(gmul.shape), const(gadd.shape)]
                + paged(D_QKV, page) + paged(D_QKV, page) + paged(HPM, page))
    grid_spec = pltpu.PrefetchScalarGridSpec(
        num_scalar_prefetch=1,
        grid=(db,),
        in_specs=in_specs,
        out_specs=pl.BlockSpec((None, dec_seq, D_QKV), lambda b, pt: (b, 0, 0)),
        scratch_shapes=[pltpu.VMEM((n_pages + 1, rows, LANES), F32),
                        pltpu.VMEM((n_pages + 1, N_HEADS * dec_seq, LANES), BF16)],
    )
    return pl.pallas_call(
        functools.partial(_sample_kernel, n_pages=n_pages, dec_seq=dec_seq, lam_init=lam_init),
        grid_spec=grid_spec,
        out_shape=jax.ShapeDtypeStruct((db, dec_seq, D_QKV), F32),
        compiler_params=_params("arbitrary"),
        name="sample_attention",
    )(page_table, q8, kn, vn, lfn, bs, qmask, omask, sl, lam, gd1, gd2, gmul, gadd,
      *([cache_k] * n_pages), *([cache_v] * n_pages), *([cache_lft] * n_pages))


def _merge_kernel(x_ref, o0_ref, o1_ref, o2_ref, o3_ref, gmix_ref, wg_ref, wbr_ref, wo_ref, y_ref):
    x = x_ref[...]
    h = (x * lax.rsqrt(jnp.mean(x * x, axis=-1, keepdims=True) + EPS)) * gmix_ref[...]
    hb = h.astype(BF16)
    acc = jnp.zeros(x.shape, F32)
    for m, o_ref in enumerate((o0_ref, o1_ref, o2_ref, o3_ref)):
        gate = 1.0 / (1.0 + jnp.exp(-_nt(hb, wg_ref[m * D_MODEL:(m + 1) * D_MODEL, :])))
        acc = acc + gate * _nn(o_ref[...].astype(BF16), wbr_ref[m])
    y_ref[...] = x + _nn(acc.astype(BF16), wo_ref[...])


def _merge(x, os_, gmix, wg, wbr, wo, tm):
    rows = x.shape[0]
    row = lambda w: pl.BlockSpec((tm, w), lambda i: (i, 0))
    return pl.pallas_call(
        _merge_kernel,
        grid=(rows // tm,),
        in_specs=[row(D_MODEL)] + [row(MIX_WIDTH)] * 4
        + [_const_spec((1, D_MODEL)), _const_spec((N_MIXERS * D_MODEL, D_MODEL)),
           _const_spec((N_MIXERS, MIX_WIDTH, D_MODEL)), _const_spec((D_MODEL, D_MODEL))],
        out_specs=row(D_MODEL),
        out_shape=jax.ShapeDtypeStruct((rows, D_MODEL), F32),
        compiler_params=_params("parallel"),
        name="merge",
    )(x, *os_, gmix, wg, wbr, wo)


def _mlp_kernel(x_ref, g_ref, wup_ref, wdn_ref, y_ref, *, chunk):
    x = x_ref[...]
    h = ((x * lax.rsqrt(jnp.mean(x * x, axis=-1, keepdims=True) + EPS)) * g_ref[...]).astype(BF16)
    acc = x
    for c in range(D_FF // chunk):
        u = jnp.maximum(_nn(h, wup_ref[:, c * chunk:(c + 1) * chunk]), 0.0)
        acc = acc + _nn((u * u).astype(BF16), wdn_ref[c * chunk:(c + 1) * chunk, :])
    y_ref[...] = acc


def _mlp(x, g, wup, wdn, tm):
    rows = x.shape[0]
    row = pl.BlockSpec((tm, D_MODEL), lambda i: (i, 0))
    return pl.pallas_call(
        functools.partial(_mlp_kernel, chunk=1024),
        grid=(rows // tm,),
        in_specs=[row, _const_spec((1, D_MODEL)), _const_spec((D_MODEL, D_FF)), _const_spec((D_FF, D_MODEL))],
        out_specs=row,
        out_shape=jax.ShapeDtypeStruct((rows, D_MODEL), F32),
        compiler_params=_params("parallel"),
        name="mlp",
    )(x, g, wup, wdn)


def _qk_vectors(gq, gk, qscale):
    z = jnp.zeros((MIX_WIDTH,), F32)
    one = jnp.ones((MIX_WIDTH,), F32)
    t4 = lambda a: jnp.tile(a, HPM)
    sc = HEAD_DIM ** -0.5 * qscale
    qmul = jnp.concatenate([t4(gq[0]) * sc, t4(gq[1]) * sc, z, t4(gq[2]) * (DIFF_HALF ** -0.5 * qscale)])
    qadd = jnp.concatenate([z, z, one * sc, z])
    kmul = jnp.concatenate([t4(gk[0]), t4(gk[1]), z, t4(gk[2])])
    kadd = jnp.concatenate([z, z, one, z])
    return [a.reshape(1, D_QKV) for a in (qmul, qadd, kmul, kadd)]


def _bias_index_tiles(tq, n_pages, page, dec_seq):
    a = np.arange(tq)
    diag = _t5_bucket_np(a[None, :] - a[:, None])
    sub = _t5_bucket_np(tq + a[None, :] - a[:, None])
    idxp = np.stack([diag, sub]).astype(np.int32)
    past_len = n_pages * page
    qpos = past_len + (np.arange(HPM * dec_seq) % dec_seq)
    kpos = np.concatenate([np.arange(past_len), past_len + np.arange(page)])
    idxs = _t5_bucket_np(qpos[None, :, None] - kpos.reshape(n_pages + 1, 1, page))
    return idxp, idxs.astype(np.int32)


def _layer(l, x, past, w, consts, tq, tm, kv_all=None):
    (tab, bp, bs, g1, g2, su_q, tri_c, sl_p, sample_consts, gd) = consts
    (w_in, b_forget, g_q, g_k, lam, g_sub, w_branch, w_o, g_mix, g_mlp, w_up, w_down) = w
    b, t, _ = x.shape
    rows = b * t
    lam_init = 0.8 - 0.6 * math.exp(-0.3 * l)
    wt = w_in[l].T.astype(BF16)
    wqkv = wt[:3 * D_QKV]
    wf = jnp.pad(wt[3 * D_QKV:3 * D_QKV + HPM], ((0, LANES - HPM), (0, 0)))
    wg = wt[3 * D_QKV + HPM:]
    gmix = g_mix[l].reshape(1, D_MODEL)
    qk_vecs = _qk_vectors(g_q[l], g_k[l], LOG2E if past is None else 1.0)
    x2 = x.reshape(rows, D_MODEL)

    if past is None:
        cols = [jnp.broadcast_to(a.reshape(D_QKV, 1), (D_QKV, LANES)) for a in qk_vecs]
        bf_col = jnp.broadcast_to(jnp.pad(b_forget[l], (0, 8 - HPM))[:, None], (8, LANES))
        qt, kb3, vt, k_new, v_new, lf_t = _inproj_t(x, gmix, wqkv, wf, bf_col, *cols, tm=tm, tq=tq,
                                                    layer=l, depth=w_in.shape[0], kv_all=kv_all)
        lf_new = jnp.swapaxes(lf_t[:, :HPM], 1, 2)
        c = _cumsum(lf_t, tri_c)
        means = _block_means(k_new, l)
        gsub_col = jnp.broadcast_to(g_sub[l][:, None], (HEAD_DIM, LANES))
        o_moba = _moba_prompt(tab, qt, kb3, vt, means, bp, tq)
        o_fox = _fox_prompt(qt, kb3, vt, c, tq)
        o_sb = _sb_prompt(qt, kb3, vt, su_q, tq)
        o_dif = _diff_prompt(tab, qt, kb3, vt, bp, lam[l], gsub_col, lam_init, tq)
        outs = [a.reshape(rows, MIX_WIDTH) for a in (o_moba, o_fox, o_sb, o_dif)]
    else:
        cache_k, cache_v, cache_lft, page_table = past
        bf = jnp.pad(b_forget[l], (0, LANES - HPM)).reshape(1, LANES)
        qb, kb, vb, kf, vf, lf = _inproj(x2, gmix, wqkv, wf, bf, g1, g2, *qk_vecs, tm=min(tm, rows))
        k_new, v_new = kf.reshape(b, t, N_HEADS, HEAD_DIM), vf.reshape(b, t, N_HEADS, HEAD_DIM)
        lf_new = lf.reshape(b, t, HPM)
        q4 = qb.astype(F32).reshape(b, t, D_QKV)
        q8 = jnp.concatenate([q4] * (8 // t), axis=1)
        padn = ((0, 0), (0, 16 - t), (0, 0))
        kn = jnp.pad(kb.reshape(b, t, D_QKV), padn)
        vn = jnp.pad(vb.reshape(b, t, D_QKV), padn)
        lfn = jnp.swapaxes(lf.reshape(b, t, HPM), 1, 2)
        lfn = jnp.pad(jnp.repeat(lfn, t, axis=1), ((0, 0), (0, 0), (0, 8 - t)))
        gd1, gd2 = gd
        z3 = jnp.zeros((3 * MIX_WIDTH,), F32)
        gmul = jnp.concatenate([z3, jnp.tile(g_sub[l], HPM) * (1.0 - lam_init)]).reshape(1, D_QKV)
        gadd = jnp.concatenate([z3 + 1.0, jnp.zeros((MIX_WIDTH,), F32)]).reshape(1, D_QKV)
        o = _sample_attention(l, page_table, q8, kn, vn, lfn, bs, (*sample_consts, sl_p, gd1, gd2),
                              lam[l], gmul, gadd, cache_k, cache_v, cache_lft, lam_init)
        o = o.reshape(rows, D_QKV)
        outs = [o[:, m * MIX_WIDTH:(m + 1) * MIX_WIDTH] for m in range(N_MIXERS)]

    y = _merge(x2, outs, gmix, wg, w_branch[l].astype(BF16), w_o[l].astype(BF16), tm=min(tm, rows))
    y = _mlp(y, g_mlp[l].reshape(1, D_MODEL), w_up[l].astype(BF16), w_down[l].astype(BF16), tm=min(tm, rows))
    return y.reshape(b, t, D_MODEL), k_new, v_new, lf_new


def _constants(rel_bias, tq, n_pages, page, dec_seq):
    idxp, idxs = _bias_index_tiles(tq, n_pages, page, dec_seq)
    tab = rel_bias.T
    bp, bs = _bias_tiles(tab, jnp.asarray(idxp), jnp.asarray(idxs), dec_seq)
    g1, g2 = _qk_group_mats()
    qmask, omask = _sample_masks(dec_seq)
    gd1 = np.zeros((D_QKV, LANES), np.float32)
    gd2 = np.zeros((LANES, D_QKV), np.float32)
    for c in range(3 * MIX_WIDTH, D_QKV):
        gd1[c, (c - 3 * MIX_WIDTH) // HEAD_DIM] = 1.0 / HEAD_DIM
        gd2[(c - 3 * MIX_WIDTH) // HEAD_DIM, c] = 1.0
    su_q = np.concatenate([_strict_lower(tq).T, np.ones((ONES_ROWS, tq), np.float32)], axis=0)
    bf = lambda a: jnp.asarray(a, BF16)
    return (tab, bp, bs, bf(g1), bf(g2), bf(su_q), bf(_upper_incl(tq)), bf(_strict_lower(page)),
            (jnp.asarray(qmask), jnp.asarray(omask)), (bf(gd1), bf(gd2)))


def kernel(x_prompt, x_sample, cache_k, cache_v, cache_logf, page_table, rel_bias, w_in, b_forget,
           g_q, g_k, lam, g_sub, w_branch, w_o, g_mix, g_mlp, w_up, w_down):
    depth = w_in.shape[0]
    tq = MOBA_BLOCK
    tm = 512
    n_pages = page_table.shape[1]
    page = cache_k.shape[2]
    dec_seq = x_sample.shape[1]
    n_phys = cache_k.shape[1]
    assert (n_pages * page) % MOBA_BLOCK == 0 and 8 % dec_seq == 0 and x_prompt.shape[1] % tq == 0

    consts = _constants(rel_bias, tq, n_pages, page, dec_seq)
    w = (w_in, b_forget, g_q, g_k, lam, g_sub, w_branch, w_o, g_mix, g_mlp, w_up, w_down)
    to_pages = lambda c: jnp.transpose(c, (0, 1, 3, 4, 2)).reshape(depth, n_phys, D_QKV, page)
    past = (to_pages(cache_k), to_pages(cache_v), jnp.swapaxes(cache_logf, 2, 3), page_table)

    hp, hs = x_prompt, x_sample
    outs = [[] for _ in range(4)]
    kv_all = None
    for l in range(depth):
        hp, kp, vp, fp = _layer(l, hp, None, w, consts, tq, tm, kv_all)
        kv_all = (kp, vp)
        hs, ks, vs, fs = _layer(l, hs, past, w, consts, tq, tm)
        for lst, a in zip(outs, (fp, ks, vs, fs)):
            lst.append(a)
    b, t = x_prompt.shape[:2]
    to_heads = lambda a: jnp.transpose(a.reshape(depth, b, N_HEADS, HEAD_DIM, t), (0, 1, 4, 2, 3))
    fp, ks, vs, fs = (jnp.stack(a) for a in outs)
    return hp, hs, to_heads(kv_all[0]), to_heads(kv_all[1]), fp, ks, vs, fs
```

```python
import functools
import math

import numpy as np
import jax
import jax.numpy as jnp
from jax import lax
from jax.experimental import pallas as pl
from jax.experimental.pallas import tpu as pltpu

F32 = jnp.float32
BF16 = jnp.bfloat16

D_MODEL = 1024
HEAD_DIM = 64
N_MIXERS = 4
HPM = 4
N_HEADS = 16
MIX_WIDTH = HPM * HEAD_DIM
D_QKV = N_HEADS * HEAD_DIM
DIFF_HALF = HEAD_DIM // 2
D_FF = 4 * D_MODEL
MOBA_BLOCK = 256
MOBA_TOPK = 3
N_BUCKETS = 32
MAX_DISTANCE = 128
EPS = 1e-6
NEG = -1e30
LOG2E = 1.4426950408889634
LANES = 128
PAIR = 2 * HEAD_DIM
VMEM_LIMIT = 56 * 1024 * 1024


def _params(*sem):
    return pltpu.CompilerParams(dimension_semantics=sem, vmem_limit_bytes=VMEM_LIMIT)


def _nt(a, b):
    return lax.dot_general(a, b, (((1,), (1,)), ((), ())), preferred_element_type=F32)


def _nn(a, b):
    return jnp.dot(a, b, preferred_element_type=F32)


def _split2(x):
    hi = x.astype(BF16)
    lo = (x - hi.astype(F32)).astype(BF16)
    return hi, lo


def _split3(x):
    hi = x.astype(BF16)
    r = x - hi.astype(F32)
    mid = r.astype(BF16)
    lo = (r - mid.astype(F32)).astype(BF16)
    return hi, mid, lo


def _nn_split(x, w, parts):
    pieces = _split2(x) if parts == 2 else _split3(x)
    out = _nn(pieces[0], w)
    for p in pieces[1:]:
        out = out + _nn(p, w)
    return out


def _log_sigmoid(z):
    return jnp.minimum(z, 0.0) - jnp.log1p(jnp.exp(-jnp.abs(z)))


def _const_spec(shape):
    nd = len(shape)
    return pl.BlockSpec(shape, lambda *_: (0,) * nd)


def _t5_bucket_np(rel):
    n = np.maximum(rel, 0)
    max_exact = N_BUCKETS // 2
    nf = np.maximum(n, 1).astype(np.float32)
    large = max_exact + (np.log(nf / np.float32(max_exact)) / np.float32(math.log(MAX_DISTANCE / max_exact))
                         * np.float32(N_BUCKETS - max_exact)).astype(np.int32)
    return np.where(n < max_exact, n, np.minimum(large, N_BUCKETS - 1)).astype(np.int32)


def _qk_group_mats():
    g1 = np.zeros((D_QKV, LANES), np.float32)
    g2 = np.zeros((LANES, D_QKV), np.float32)
    for c in range(D_QKV):
        h = c // HEAD_DIM
        if h < 2 * HPM:
            gid, size = h, HEAD_DIM
        elif h < 3 * HPM:
            continue
        else:
            gid, size = 2 * HPM + (c - 3 * MIX_WIDTH) // DIFF_HALF, DIFF_HALF
        g1[c, gid] = 1.0 / size
        g2[gid, c] = 1.0
    return g1, g2


def _strict_lower(n):
    r = np.arange(n)
    return (r[:, None] > r[None, :]).astype(np.float32)


def _upper_incl(n):
    r = np.arange(n)
    return (r[:, None] <= r[None, :]).astype(np.float32)


def _inproj_kernel(x_ref, gmix_ref, wqkv_ref, wf_ref, bf_ref, g1_ref, g2_ref,
                   qmul_ref, qadd_ref, kmul_ref, kadd_ref,
                   qb_ref, kb_ref, vb_ref, kf_ref, vf_ref, lf_ref):
    x = x_ref[...]
    h = (x * lax.rsqrt(jnp.mean(x * x, axis=-1, keepdims=True) + EPS)) * gmix_ref[...]
    hb = h.astype(BF16)
    g1 = g1_ref[...]
    g2 = g2_ref[...]

    def group_norm(a, mul_ref, add_ref):
        ms = _nn_split(a * a, g1, 2)
        inv = lax.rsqrt(ms + EPS)
        bc = _nn_split(inv, g2, 3)
        return a * (bc * mul_ref[...] + add_ref[...])

    q = _nt(hb, wqkv_ref[0:D_QKV, :])
    qb_ref[...] = group_norm(q, qmul_ref, qadd_ref).astype(BF16)
    k = _nt(hb, wqkv_ref[D_QKV:2 * D_QKV, :])
    kn = group_norm(k, kmul_ref, kadd_ref)
    kf_ref[...] = kn
    kb_ref[...] = kn.astype(BF16)
    v = _nt(hb, wqkv_ref[2 * D_QKV:3 * D_QKV, :])
    vf_ref[...] = v
    vb_ref[...] = v.astype(BF16)
    z = _nt(hb, wf_ref[...]) + bf_ref[...]
    lf_ref[...] = _log_sigmoid(z)[:, 0:HPM]


def _inproj(x, gmix, wqkv, wf, bf, g1, g2, qmul, qadd, kmul, kadd, tm):
    rows = x.shape[0]
    row = lambda w: pl.BlockSpec((tm, w), lambda i: (i, 0))
    vec = _const_spec((1, D_QKV))
    return pl.pallas_call(
        _inproj_kernel,
        grid=(rows // tm,),
        in_specs=[row(D_MODEL), _const_spec((1, D_MODEL)), _const_spec((3 * D_QKV, D_MODEL)),
                  _const_spec((LANES, D_MODEL)), _const_spec((1, LANES)),
                  _const_spec((D_QKV, LANES)), _const_spec((LANES, D_QKV)), vec, vec, vec, vec],
        out_specs=[row(D_QKV), row(D_QKV), row(D_QKV), row(D_QKV), row(D_QKV), row(HPM)],
        out_shape=[jax.ShapeDtypeStruct((rows, D_QKV), BF16)] * 3
        + [jax.ShapeDtypeStruct((rows, D_QKV), F32)] * 2 + [jax.ShapeDtypeStruct((rows, HPM), F32)],
        compiler_params=_params("parallel"),
        name="inproj",
    )(x, gmix, wqkv, wf, bf, g1, g2, qmul, qadd, kmul, kadd)


def _inproj_t_kernel(x_ref, gmix_ref, wqkv_ref, wf_ref, bf_ref, qmul_ref, qadd_ref, kmul_ref, kadd_ref, *refs):
    qt_ref, kb_ref, vt_ref, kf_ref, vf_ref, lf_ref = refs[-6:]
    x = x_ref[...]
    tm = x.shape[0]
    h = (x * lax.rsqrt(jnp.mean(x * x, axis=-1, keepdims=True) + EPS)) * gmix_ref[...]
    hb = h.astype(BF16)
    wide = lambda ref: jnp.concatenate([ref[...]] * (tm // LANES), axis=1)
    n64, n32 = 2 * MIX_WIDTH, MIX_WIDTH

    def group_norm(a, mul_ref, add_ref):
        sq = a * a

        def inv_rms(rows0, rows1, width):
            g = sq[rows0:rows1].reshape((rows1 - rows0) // width, width, tm)
            inv = lax.rsqrt(jnp.mean(g, axis=1, keepdims=True) + EPS)
            return jnp.broadcast_to(inv, g.shape).reshape(rows1 - rows0, tm)

        scale = jnp.concatenate([inv_rms(0, n64, HEAD_DIM), jnp.zeros((D_QKV - n64 - n32, tm), F32),
                                 inv_rms(D_QKV - n32, D_QKV, DIFF_HALF)], axis=0)
        return a * (scale * wide(mul_ref) + wide(add_ref))

    q = _nt(wqkv_ref[0:D_QKV, :], hb)
    qt_ref[...] = group_norm(q, qmul_ref, qadd_ref).astype(BF16)
    k = _nt(wqkv_ref[D_QKV:2 * D_QKV, :], hb)
    kn = group_norm(k, kmul_ref, kadd_ref)
    kf_ref[...] = kn
    kb_ref[...] = kn.T.astype(BF16)
    v = _nt(wqkv_ref[2 * D_QKV:3 * D_QKV, :], hb)
    vf_ref[...] = v
    vb = v.astype(BF16)
    tq = vt_ref.shape[2]
    for i in range(tm // tq):
        vt_ref[i] = vb[:, i * tq:(i + 1) * tq]
    z = _nt(wf_ref[...], hb)[0:8, :] + wide(bf_ref)
    lf_ref[...] = _log_sigmoid(z)


def _inproj_t(x, gmix, wqkv, wf, bf_col, qmul, qadd, kmul, kadd, tm, tq, layer, depth, kv_all):
    b, t, _ = x.shape
    col = _const_spec((D_QKV, LANES))
    feat = lambda rows: pl.BlockSpec((None, rows, tm), lambda bi, i: (bi, 0, i))
    layered = pl.BlockSpec((None, None, D_QKV, tm), lambda bi, i: (layer, bi, 0, i))
    operands = [x, gmix, wqkv, wf, bf_col, qmul, qadd, kmul, kadd]
    in_specs = [pl.BlockSpec((None, tm, D_MODEL), lambda bi, i: (bi, i, 0)), _const_spec((1, D_MODEL)),
                _const_spec((3 * D_QKV, D_MODEL)), _const_spec((LANES, D_MODEL)), _const_spec((8, LANES)),
                col, col, col, col]
    aliases = {}
    if kv_all is not None:
        aliases = {len(operands): 3, len(operands) + 1: 4}
        operands += list(kv_all)
        in_specs += [pl.BlockSpec(memory_space=pl.ANY)] * 2
    return pl.pallas_call(
        _inproj_t_kernel,
        grid=(b, t // tm),
        in_specs=in_specs,
        out_specs=[feat(D_QKV), pl.BlockSpec((None, tm, D_QKV), lambda bi, i: (bi, i, 0)),
                   pl.BlockSpec((None, tm // tq, D_QKV, tq), lambda bi, i: (bi, i, 0, 0)),
                   layered, layered, feat(8)],
        out_shape=[jax.ShapeDtypeStruct((b, D_QKV, t), BF16), jax.ShapeDtypeStruct((b, t, D_QKV), BF16),
                   jax.ShapeDtypeStruct((b, t // tq, D_QKV, tq), BF16),
                   jax.ShapeDtypeStruct((depth, b, D_QKV, t), F32), jax.ShapeDtypeStruct((depth, b, D_QKV, t), F32),
                   jax.ShapeDtypeStruct((b, 8, t), F32)],
        input_output_aliases=aliases,
        compiler_params=_params("parallel", "parallel"),
        name="inproj_t",
    )(*operands)


def _cumsum_kernel(lf_ref, tri_ref, c_ref, *, t, blk):
    tri = tri_ref[...]
    carry = jnp.zeros((8, 1), F32)
    for i in range(t // blk):
        cs = _nn_split(lf_ref[:, i * blk:(i + 1) * blk], tri, 3) + carry
        c_ref[:, i * blk:(i + 1) * blk] = cs
        carry = cs[:, blk - 1:blk]


def _cumsum(lf_t, tri):
    b, r, t = lf_t.shape
    blk = tri.shape[0]
    return pl.pallas_call(
        functools.partial(_cumsum_kernel, t=t, blk=blk),
        grid=(b,),
        in_specs=[pl.BlockSpec((None, r, t), lambda i: (i, 0, 0)), _const_spec((blk, blk))],
        out_specs=pl.BlockSpec((None, r, t), lambda i: (i, 0, 0)),
        out_shape=jax.ShapeDtypeStruct((b, r, t), F32),
        compiler_params=_params("parallel"),
        name="logf_cumsum",
    )(lf_t, tri)


def _means_kernel(k_ref, o_ref, *, nb):
    lane = lax.broadcasted_iota(jnp.int32, (1, LANES), 1)
    acc = jnp.zeros(o_ref.shape, F32)
    for n in range(nb):
        col = jnp.mean(k_ref[:, n * MOBA_BLOCK:(n + 1) * MOBA_BLOCK], axis=1, keepdims=True)
        acc = jnp.where(lane == n, col, acc)
    o_ref[...] = acc


def _block_means(kf_all, layer):
    _, b, _, t = kf_all.shape
    nb = t // MOBA_BLOCK
    assert nb <= LANES
    return pl.pallas_call(
        functools.partial(_means_kernel, nb=nb),
        grid=(b,),
        in_specs=[pl.BlockSpec((None, None, MIX_WIDTH, t), lambda i: (layer, i, 0, 0))],
        out_specs=pl.BlockSpec((None, MIX_WIDTH, LANES), lambda i: (i, 0, 0)),
        out_shape=jax.ShapeDtypeStruct((b, MIX_WIDTH, LANES), F32),
        compiler_params=_params("parallel"),
        name="moba_block_means",
    )(kf_all)


def _bias_kernel(tab_ref, idxp_ref, idxs_ref, bp_ref, bs_ref, *, dec_seq):
    idxp = idxp_ref[...]
    idxs = idxs_ref[...]
    row = lax.broadcasted_iota(jnp.int32, idxs.shape, 1)
    for h in range(2 * HPM):
        acc = jnp.zeros(idxp.shape, F32)
        for b in range(N_BUCKETS):
            acc = jnp.where(idxp == b, tab_ref[h, b] * LOG2E, acc)
        bp_ref[h] = acc
    for m in range(2):
        acc = jnp.zeros(idxs.shape, F32)
        for hl in range(HPM):
            for b in range(N_BUCKETS):
                acc = jnp.where((idxs == b) & (row // dec_seq == hl), tab_ref[m * HPM + hl, b], acc)
        bs_ref[m] = acc


def _bias_tiles(tab, idxp, idxs, dec_seq):
    return pl.pallas_call(
        functools.partial(_bias_kernel, dec_seq=dec_seq),
        in_specs=[pl.BlockSpec(memory_space=pltpu.SMEM), pl.BlockSpec(memory_space=pltpu.VMEM),
                  pl.BlockSpec(memory_space=pltpu.VMEM)],
        out_specs=[pl.BlockSpec(memory_space=pltpu.VMEM), pl.BlockSpec(memory_space=pltpu.VMEM)],
        out_shape=[jax.ShapeDtypeStruct((2 * HPM,) + idxp.shape, F32),
                   jax.ShapeDtypeStruct((2,) + idxs.shape, F32)],
        compiler_params=pltpu.CompilerParams(vmem_limit_bytes=VMEM_LIMIT),
        name="t5_bias_tiles",
    )(tab, idxp, idxs)


ONES_ROWS = 16
ACC_ROWS = HEAD_DIM + ONES_ROWS
EXP2_DEAD = -200.0


def _rows(n):
    return lax.broadcasted_iota(jnp.int32, (n, 1), 0)


def _key_before_query(tk, tq, strict):
    s = lax.broadcasted_iota(jnp.int32, (tk, tq), 0)
    t = lax.broadcasted_iota(jnp.int32, (tk, tq), 1)
    return (s < t) if strict else (s <= t)


def _row_range(qt, lo, hi):
    r = _rows(qt.shape[0])
    return jnp.where((r >= lo) & (r < hi), qt, jnp.zeros_like(qt))


def _v_aug(vt, x):
    return jnp.concatenate([vt[x * HEAD_DIM:(x + 1) * HEAD_DIM, :],
                            jnp.ones((ONES_ROWS, vt.shape[1]), BF16)], axis=0)


def _extra_rows(by_row, n, width):
    r = _rows(n)
    out = jnp.zeros((n, width), F32)
    for row, val in by_row.items():
        out = jnp.where(r == row, val, out)
    return out


def _pieces(by_row_f32, base):
    out = {}
    for i, val in enumerate(by_row_f32):
        for j, piece in enumerate(_split3(val)):
            out[base + 3 * i + j] = piece.astype(F32)
    return out


def _softmax_step(i, s, va, m_sc, acc_sc, sel=None):
    m_old = m_sc[i]
    bm = jnp.max(s, axis=0, keepdims=True)
    if sel is not None:
        bm = jnp.where(sel, bm, NEG)
    m_new = jnp.maximum(m_old, bm)
    m_sub = m_new if sel is None else jnp.where(sel, m_new, -NEG)
    p = jnp.exp2(s - m_sub)
    acc_sc[i] = acc_sc[i] * jnp.exp2(m_old - m_new) + _nn(va, p.astype(BF16))
    m_sc[i] = m_new


def _normalized(acc_sc, i):
    a = acc_sc[i]
    return a[0:HEAD_DIM, :] / a[HEAD_DIM:HEAD_DIM + 1, :]


def _init(m_sc, acc_sc):
    m_sc[...] = jnp.full(m_sc.shape, NEG, F32)
    acc_sc[...] = jnp.zeros(acc_sc.shape, F32)


def _sweep(qi, group, near, far, alive=None, pin=None):
    @pl.when(qi == 0)
    def _():
        near(1)

    @pl.when(qi > 0)
    def _():
        near(2)

    nf = jnp.maximum(qi - 1, 0)

    def rest(pinned):
        if alive is None:
            def body(i, c):
                top = nf - 1 - group * i
                far([top - g for g in range(group)], pinned)
                return c

            lax.fori_loop(0, nf // group, body, 0)
            ok = None
        else:
            def body(c):
                i, _ = c
                top = nf - 1 - group * i
                far([top - g for g in range(group)], pinned)
                return i + 1, alive(top - group + 1)

            _, ok = lax.while_loop(lambda c: (c[0] < nf // group) & c[1], body, (0, alive(nf)))
        rem = nf % group
        size = group // 2
        while size >= 1:
            top = (rem & (2 * size - 1)) - 1
            cond = (rem & size) != 0
            if ok is not None:
                cond = cond & ok

            @pl.when(cond)
            def _(size=size, top=top):
                far([top - g for g in range(size)], pinned)

            if ok is not None and size > 1:
                ok = ok & alive(rem & (size - 1))
            size //= 2

    if pin is None:
        rest(False)
    else:
        pinned = pin()

        @pl.when(pinned)
        def _():
            rest(True)

        @pl.when(jnp.logical_not(pinned))
        def _():
            rest(False)


PIN_HEADROOM = 100.0
PIN_MAX_GAP = 220.0


def _pin_reference(bounds, m_sc, acc_sc):
    gap = bounds[0] - m_sc[0]
    for i in range(1, len(bounds)):
        gap = jnp.maximum(gap, bounds[i] - m_sc[i])
    pinned = jnp.max(gap) <= PIN_MAX_GAP

    @pl.when(pinned)
    def _():
        for i, bound in enumerate(bounds):
            m = m_sc[i]
            ref = jnp.maximum(m, bound - PIN_HEADROOM)
            acc_sc[i] = acc_sc[i] * jnp.exp2(m - ref)
            m_sc[i] = ref

    return pinned


def _softmax_blocks(items, scores, v_of, m_sc, acc_sc, tq, pinned=False):
    ss = [scores(j, kind, i) for (j, kind, i, x, sel) in items]
    if not pinned:
        for s, (j, kind, i, x, sel) in zip(ss, items):
            if kind == "diag":
                s = jnp.where(_key_before_query(tq, tq, False), s, NEG)
            _softmax_step(i, s, _v_aug(v_of(j), x), m_sc, acc_sc, sel(j, x) if sel is not None else None)
        return
    pvs = {}
    for s, (j, kind, i, x, sel) in zip(ss, items):
        ref = m_sc[i]
        if sel is not None:
            ref = jnp.where(sel(j, x), ref, -NEG)
        pv = _nn(_v_aug(v_of(j), x), jnp.exp2(s - ref).astype(BF16))
        pvs[i] = pv if i not in pvs else pvs[i] + pv
    for i, pv in pvs.items():
        acc_sc[i] = acc_sc[i] + pv


def _max_key_norms(k_ref, kn_sc):
    ksq = jnp.square(k_ref[...].astype(F32))
    lane = lax.broadcasted_iota(jnp.int32, (1, MIX_WIDTH), 1)
    for h in range(HPM):
        n2 = jnp.sum(jnp.where(lane // HEAD_DIM == h, ksq, 0.0), axis=1, keepdims=True)
        kn_sc[h] = jnp.sqrt(jnp.max(n2, axis=0, keepdims=True))


def _score_bound(q, kn, bias_max):
    qn = jnp.sqrt(jnp.sum(jnp.square(q.astype(F32)), axis=0, keepdims=True))
    return qn * kn * 1.001 + (bias_max + 0.01)


def _table_max(tab_ref, h):
    m = tab_ref[h, 0]
    for b in range(1, N_BUCKETS):
        m = jnp.maximum(m, tab_ref[h, b])
    return m * LOG2E


def _fox_kernel(qt_ref, k_ref, vt_ref, cq_ref, call_ref, cend_ref, o_ref, kx_sc, kn_sc, m_sc, acc_sc, *, tq):
    qi = pl.program_id(1)

    def head_row(c, h):
        return c[h:h + 1, :]

    @pl.when(qi == 0)
    def _():
        c = call_ref[...] * LOG2E
        for p in range(HPM // 2):
            by_row = _pieces([head_row(c, 2 * p), head_row(c, 2 * p + 1)], 0)
            by_row.update({6: 1.0, 7: 1.0, 8: 1.0})
            e = _extra_rows(by_row, 16, c.shape[1])
            e = jnp.concatenate([e, jnp.zeros((PAIR - 16, c.shape[1]), F32)], axis=0)
            kx_sc[p] = e.T.astype(BF16)
        _max_key_norms(k_ref, kn_sc)

    cq = cq_ref[...] * LOG2E
    qps, cts, zbs = [], [], []
    for h in range(HPM):
        p, x = divmod(h, 2)
        ct = head_row(cq, h)
        by_row = {3 * x: -1.0, 3 * x + 1: -1.0, 3 * x + 2: -1.0}
        by_row.update(_pieces([ct], 6))
        qx = _extra_rows(by_row, PAIR, tq).astype(BF16)
        qh = _row_range(qt_ref[p * PAIR:(p + 1) * PAIR, :], x * HEAD_DIM, (x + 1) * HEAD_DIM)
        qps.append(jnp.concatenate([qh, qx], axis=0))
        cts.append(ct)
        qn = jnp.sqrt(jnp.sum(jnp.square(qh.astype(F32)), axis=0, keepdims=True))
        zbs.append(qn * kn_sc[h] * 1.001 + 0.001)
    _init(m_sc, acc_sc)

    def alive(low):
        lane = lax.broadcasted_iota(jnp.int32, cend_ref.shape, 1)
        cl = jnp.sum(jnp.where(lane == low - 1, cend_ref[...] * LOG2E, 0.0), axis=1, keepdims=True)
        worst = zbs[0] + cts[0] - head_row(cl, 0) - m_sc[0]
        for h in range(1, HPM):
            worst = jnp.maximum(worst, zbs[h] + cts[h] - head_row(cl, h) - m_sc[h])
        return jnp.max(worst) > EXP2_DEAD

    def scores(j, kind, h):
        r0 = pl.multiple_of(j * tq, tq)
        p = h // 2
        kp = jnp.concatenate([k_ref[pl.ds(r0, tq), p * PAIR:(p + 1) * PAIR], kx_sc[p, pl.ds(r0, tq), :]], axis=1)
        return _nn(kp, qps[h])

    def run(blocks, pinned=False):
        items = [(j, kind, h, h, None) for j, kind in blocks for h in range(HPM)]
        _softmax_blocks(items, scores, lambda j: vt_ref[j], m_sc, acc_sc, tq, pinned)

    _sweep(qi, 4,
           lambda n: run([(qi, "diag")] + ([(qi - 1, "far")] if n == 2 else [])),
           lambda js, pinned: run([(j, "far") for j in js], pinned), alive,
           lambda: _pin_reference([zb + 0.01 for zb in zbs], m_sc, acc_sc))
    o = jnp.concatenate([_normalized(acc_sc, h) for h in range(HPM)], axis=0)
    o_ref[...] = o.T.astype(o_ref.dtype)


def _moba_kernel(tab_ref, qt_ref, k_ref, vt_ref, mean_ref, bt_ref, o_ref, m_sc, acc_sc, sel_sc, kn_sc, *, tq, nb):
    qi = pl.program_id(1)

    @pl.when(qi == 0)
    def _():
        _max_key_norms(k_ref, kn_sc)

    ones_k = jnp.ones((tq, PAIR), BF16)
    nbp = sel_sc.shape[1]
    rb = _rows(nbp)
    qs, qps, bounds = [], [], []
    for h in range(HPM):
        p, x = divmod(h, 2)
        mh, ml = _split2(mean_ref[p * PAIR:(p + 1) * PAIR, :].T)
        qx = _row_range(qt_ref[p * PAIR:(p + 1) * PAIR, :], x * HEAD_DIM, (x + 1) * HEAD_DIM)
        qs.append(qx)
        bounds.append(_score_bound(qx, kn_sc[h], _table_max(tab_ref, h)))
        far = jnp.full((1, tq), tab_ref[h, N_BUCKETS - 1] * LOG2E, F32)
        qps.append(jnp.concatenate([qx, _extra_rows(_pieces([far], 0), PAIR, tq).astype(BF16)], axis=0))
        gate = (_nn(mh, qx) + _nn(ml, qx))[0:nbp, :]
        cnt = jnp.zeros((nbp, tq), jnp.int32)
        for n in range(nb):
            gn = gate[n:n + 1, :]
            beats = (gn > gate) | ((gn == gate) & (n < rb))
            cnt = cnt + jnp.where(beats, jnp.where(n < qi, 1, 0), 0)
        sel_sc[h] = jnp.where((cnt < MOBA_TOPK) & (rb < qi), 1.0, 0.0)
    _init(m_sc, acc_sc)

    def scores(j, kind, h):
        p = h // 2
        k = k_ref[pl.ds(pl.multiple_of(j * tq, tq), tq), p * PAIR:(p + 1) * PAIR]
        if kind == "far":
            return _nn(jnp.concatenate([k, ones_k], axis=1), qps[h])
        return _nn(k, qs[h]) + bt_ref[h, 0 if kind == "diag" else 1]

    def run(blocks, pinned=False):
        items = [(j, kind, h, h, None if kind == "diag" else (lambda j, h: sel_sc[h, pl.ds(j, 1), :] > 0.0))
                 for j, kind in blocks for h in range(HPM)]
        _softmax_blocks(items, scores, lambda j: vt_ref[j], m_sc, acc_sc, tq, pinned)

    _sweep(qi, 4,
           lambda n: run([(qi, "diag")] + ([(qi - 1, "sub")] if n == 2 else [])),
           lambda js, pinned: run([(j, "far") for j in js], pinned),
           pin=lambda: _pin_reference(bounds, m_sc, acc_sc))
    o = jnp.concatenate([_normalized(acc_sc, h) for h in range(HPM)], axis=0)
    o_ref[...] = o.T.astype(o_ref.dtype)


def _sb_kernel(qt_ref, k_ref, vt_ref, su_ref, o_ref, r_sc, acc_sc, *, tq):
    qi = pl.program_id(1)
    qs = [_row_range(qt_ref[(h // 2) * PAIR:(h // 2 + 1) * PAIR, :], (h % 2) * HEAD_DIM, (h % 2 + 1) * HEAD_DIM)
          for h in range(HPM)]
    su = su_ref[...]
    r_sc[...] = jnp.zeros(r_sc.shape, F32)
    acc_sc[...] = jnp.zeros(acc_sc.shape, F32)

    def run(blocks):
        tiles = [(j, masked, h) for j, masked in blocks for h in range(HPM)]
        zs = [_nn(k_ref[pl.ds(pl.multiple_of(j * tq, tq), tq), (h // 2) * PAIR:(h // 2 + 1) * PAIR], qs[h])
              for j, masked, h in tiles]
        lbs, lats = [], []
        for z, (j, masked, h) in zip(zs, tiles):
            log_beta = jnp.minimum(z, 0.0) - jnp.log2(1.0 + jnp.exp2(-jnp.abs(z)))
            log_keep = log_beta - z
            if masked:
                log_keep = jnp.where(_key_before_query(tq, tq, True), log_keep, 0.0)
            hi, lo = _split2(log_keep)
            lbs.append(log_beta)
            lats.append(_nn(su, hi) + _nn(su, lo))
        r = [r_sc[h] for h in range(HPM)]
        pvs = [jnp.zeros((HEAD_DIM, tq), F32)] * HPM
        for log_beta, lat, (j, masked, h) in zip(lbs, lats, tiles):
            a = jnp.exp2(log_beta + lat[0:tq, :] + r[h])
            if masked:
                a = jnp.where(_key_before_query(tq, tq, True), a, 0.0)
            r[h] = r[h] + lat[tq:tq + 1, :]
            pvs[h] = pvs[h] + _nn(vt_ref[j][h * HEAD_DIM:(h + 1) * HEAD_DIM, :], a.astype(BF16))
        for h in range(HPM):
            r_sc[h] = r[h]
            acc_sc[h] = acc_sc[h] + pvs[h]

    def alive(low):
        return jnp.max(r_sc[...]) > EXP2_DEAD

    _sweep(qi, 2,
           lambda n: run([(qi, True)] + ([(qi - 1, False)] if n == 2 else [])),
           lambda js, pinned: run([(j, False) for j in js]), alive)
    o = jnp.concatenate([acc_sc[h] for h in range(HPM)], axis=0)
    o_ref[...] = o.T.astype(o_ref.dtype)


def _lam_value(lam_ref, lam_init):
    lm = lam_ref[...]
    a = jnp.sum(lm[0:1] * lm[1:2], axis=-1, keepdims=True)
    b = jnp.sum(lm[2:3] * lm[3:4], axis=-1, keepdims=True)
    return jnp.exp(a) - jnp.exp(b) + lam_init


def _diff_kernel(tab_ref, qt_ref, k_ref, vt_ref, bt_ref, lam_ref, gsub_ref, o_ref, m_sc, acc_sc, kn_sc,
                 *, tq, lam_init):
    qi = pl.program_id(1)

    @pl.when(qi == 0)
    def _():
        _max_key_norms(k_ref, kn_sc)

    ones_k = jnp.ones((tq, PAIR), BF16)
    qs, qps, bounds = [], [], []
    for i in range(2 * HPM):
        p, r = divmod(i * DIFF_HALF, PAIR)
        qx = _row_range(qt_ref[p * PAIR:(p + 1) * PAIR, :], r, r + DIFF_HALF)
        qs.append(qx)
        bounds.append(_score_bound(qx, kn_sc[i // 2], _table_max(tab_ref, HPM + i // 2)))
        far = jnp.full((1, tq), tab_ref[HPM + i // 2, N_BUCKETS - 1] * LOG2E, F32)
        qps.append(jnp.concatenate([qx, _extra_rows(_pieces([far], 0), PAIR, tq).astype(BF16)], axis=0))
    _init(m_sc, acc_sc)

    def scores(j, kind, i):
        p = i // 4
        k = k_ref[pl.ds(pl.multiple_of(j * tq, tq), tq), p * PAIR:(p + 1) * PAIR]
        if kind == "far":
            return _nn(jnp.concatenate([k, ones_k], axis=1), qps[i])
        return _nn(k, qs[i]) + bt_ref[i // 2, 0 if kind == "diag" else 1]

    def run(blocks, pinned=False):
        items = [(j, kind, i, i // 2, None) for j, kind in blocks for i in range(2 * HPM)]
        _softmax_blocks(items, scores, lambda j: vt_ref[j], m_sc, acc_sc, tq, pinned)

    _sweep(qi, 2,
           lambda n: run([(qi, "diag")] + ([(qi - 1, "sub")] if n == 2 else [])),
           lambda js, pinned: run([(j, "far") for j in js], pinned),
           pin=lambda: _pin_reference(bounds, m_sc, acc_sc))
    lam_val = _lam_value(lam_ref, lam_init)
    gcol = jnp.concatenate([gsub_ref[...]] * (tq // LANES), axis=1)
    outs = []
    for x in range(HPM):
        o = _normalized(acc_sc, 2 * x) - lam_val * _normalized(acc_sc, 2 * x + 1)
        inv = lax.rsqrt(jnp.mean(o * o, axis=0, keepdims=True) + EPS)
        outs.append(((o * inv) * gcol) * (1.0 - lam_init))
    o_ref[...] = jnp.concatenate(outs, axis=0).T.astype(o_ref.dtype)


def _mixer_specs(t, tq, mixer):
    nk = t // tq
    qspec = pl.BlockSpec((None, MIX_WIDTH, tq), lambda b, qi: (b, mixer, qi))
    kspec = pl.BlockSpec((None, t, MIX_WIDTH), lambda b, qi: (b, 0, mixer))
    vspec = pl.BlockSpec((None, nk, MIX_WIDTH, tq), lambda b, qi: (b, 0, mixer, 0))
    ospec = pl.BlockSpec((None, tq, MIX_WIDTH), lambda b, qi: (b, qi, 0))
    return qspec, kspec, vspec, ospec


def _attn_out(b, t):
    return jax.ShapeDtypeStruct((b, t, MIX_WIDTH), BF16)


def _stat_scratch(n_maps, tq, rows):
    return [pltpu.VMEM((n_maps, 1, tq), F32), pltpu.VMEM((n_maps, rows, tq), F32)]


def _fox_prompt(qt, kb, vt, c, tq):
    b, t, _ = kb.shape
    qspec, kspec, vspec, ospec = _mixer_specs(t, tq, 1)
    cq = pl.BlockSpec((None, 8, tq), lambda bi, qi: (bi, 0, qi))
    call = pl.BlockSpec((None, 8, t), lambda bi, qi: (bi, 0, 0))
    cend = c[:, :, tq - 1::tq]
    cends = pl.BlockSpec((None, 8, t // tq), lambda bi, qi: (bi, 0, 0))
    return pl.pallas_call(
        functools.partial(_fox_kernel, tq=tq),
        grid=(b, t // tq),
        in_specs=[qspec, kspec, vspec, cq, call, cends],
        out_specs=ospec,
        out_shape=_attn_out(b, t),
        scratch_shapes=[pltpu.VMEM((HPM // 2, t, PAIR), BF16), pltpu.VMEM((HPM, 1, 1), F32)]
        + _stat_scratch(HPM, tq, ACC_ROWS),
        compiler_params=_params("parallel", "arbitrary"),
        name="fox_prompt",
    )(qt, kb, vt, c, c, cend)


def _moba_prompt(tab, qt, kb, vt, means, bp, tq):
    b, t, _ = kb.shape
    nb = t // MOBA_BLOCK
    nbp = -(-nb // 8) * 8
    qspec, kspec, vspec, ospec = _mixer_specs(t, tq, 0)
    mspec = pl.BlockSpec((None, MIX_WIDTH, LANES), lambda bi, qi: (bi, 0, 0))
    bt = pl.BlockSpec((HPM, 2, tq, tq), lambda bi, qi: (0, 0, 0, 0))
    return pl.pallas_call(
        functools.partial(_moba_kernel, tq=tq, nb=nb),
        grid=(b, t // tq),
        in_specs=[pl.BlockSpec(memory_space=pltpu.SMEM), qspec, kspec, vspec, mspec, bt],
        out_specs=ospec,
        out_shape=_attn_out(b, t),
        scratch_shapes=_stat_scratch(HPM, tq, ACC_ROWS) + [pltpu.VMEM((HPM, nbp, tq), F32),
                                                          pltpu.VMEM((HPM, 1, 1), F32)],
        compiler_params=_params("parallel", "arbitrary"),
        name="moba_prompt",
    )(tab, qt, kb, vt, means, bp)


def _sb_prompt(qt, kb, vt, su, tq):
    b, t, _ = kb.shape
    qspec, kspec, vspec, ospec = _mixer_specs(t, tq, 2)
    return pl.pallas_call(
        functools.partial(_sb_kernel, tq=tq),
        grid=(b, t // tq),
        in_specs=[qspec, kspec, vspec, _const_spec(su.shape)],
        out_specs=ospec,
        out_shape=_attn_out(b, t),
        scratch_shapes=_stat_scratch(HPM, tq, HEAD_DIM),
        compiler_params=_params("parallel", "arbitrary"),
        name="sb_prompt",
    )(qt, kb, vt, su)


def _diff_prompt(tab, qt, kb, vt, bp, lam, gsub_col, lam_init, tq):
    b, t, _ = kb.shape
    qspec, kspec, vspec, ospec = _mixer_specs(t, tq, 3)
    bt = pl.BlockSpec((HPM, 2, tq, tq), lambda bi, qi: (1, 0, 0, 0))
    return pl.pallas_call(
        functools.partial(_diff_kernel, tq=tq, lam_init=lam_init),
        grid=(b, t // tq),
        in_specs=[pl.BlockSpec(memory_space=pltpu.SMEM), qspec, kspec, vspec, bt,
                  _const_spec((4, DIFF_HALF)), _const_spec((HEAD_DIM, LANES))],
        out_specs=ospec,
        out_shape=_attn_out(b, t),
        scratch_shapes=_stat_scratch(2 * HPM, tq, ACC_ROWS) + [pltpu.VMEM((HPM, 1, 1), F32)],
        compiler_params=_params("parallel", "arbitrary"),
        name="diff_prompt",
    )(tab, qt, kb, vt, bp, lam, gsub_col)


N_SLOTS = 20


def _sample_masks(dec_seq):
    rows = N_SLOTS * dec_seq
    qmask = np.zeros((rows, D_QKV), np.float32)
    for r in range(rows):
        slot = r // dec_seq
        if slot < 3 * HPM:
            qmask[r, slot * HEAD_DIM:(slot + 1) * HEAD_DIM] = 1.0
        elif slot < 4 * HPM:
            qmask[r, slot * HEAD_DIM:slot * HEAD_DIM + DIFF_HALF] = 1.0
        else:
            c0 = (slot - HPM) * HEAD_DIM + DIFF_HALF
            qmask[r, c0:c0 + DIFF_HALF] = 1.0
    omask = np.zeros((N_HEADS * dec_seq, D_QKV), np.float32)
    for r in range(N_HEADS * dec_seq):
        omask[r, (r // dec_seq) * HEAD_DIM:(r // dec_seq + 1) * HEAD_DIM] = 1.0
    return qmask, omask


def _expand_rows(a, dec_seq):
    rows = HPM * dec_seq
    r = lax.broadcasted_iota(jnp.int32, (rows, 1), 0)
    out = jnp.broadcast_to(a[HPM - 1:HPM, :], (rows, a.shape[1]))
    for h in range(HPM - 2, -1, -1):
        out = jnp.where(r < (h + 1) * dec_seq, a[h:h + 1, :], out)
    return out


def _softmax_pages(s_pages):
    m = s_pages[0].max(axis=-1, keepdims=True)
    for s in s_pages[1:]:
        m = jnp.maximum(m, s.max(axis=-1, keepdims=True))
    ps = [jnp.exp(s - m) for s in s_pages]
    l = ps[0].sum(axis=-1, keepdims=True)
    for p in ps[1:]:
        l = l + p.sum(axis=-1, keepdims=True)
    return [p / l for p in ps]


def _sample_kernel(*refs, n_pages, dec_seq, lam_init):
    pt_ref = refs[0]
    del pt_ref
    (q_ref, kn_ref, vn_ref, lfn_ref, bs_ref, qmask_ref, omask_ref, sl_ref, lam_ref,
     gd1_ref, gd2_ref, gmul_ref, gadd_ref) = refs[1:14]
    k_refs = refs[14:14 + n_pages]
    v_refs = refs[14 + n_pages:14 + 2 * n_pages]
    lf_refs = refs[14 + 2 * n_pages:14 + 3 * n_pages]
    o_ref = refs[14 + 3 * n_pages]
    s_sc, p_sc = refs[14 + 3 * n_pages + 1:]

    g = HPM * dec_seq
    npg = n_pages + 1
    q = q_ref[...]
    qbd = (jnp.concatenate([q] * (N_SLOTS * dec_seq // 8), axis=0) * qmask_ref[...]).astype(BF16)
    pad = jnp.zeros((LANES - kn_ref.shape[0], D_QKV), BF16)
    for p in range(n_pages):
        s_sc[p] = _nn(qbd, k_refs[p][...].astype(BF16))
    s_sc[n_pages] = _nt(qbd, jnp.concatenate([kn_ref[...], pad], axis=0))

    lane = lax.broadcasted_iota(jnp.int32, (1, LANES), 1)
    qrow = lax.broadcasted_iota(jnp.int32, (g, 1), 0) % dec_seq
    new_ok = lane <= qrow
    new_past = lane < qrow
    sl = sl_ref[...]

    def pages(r0):
        return [s_sc[p, r0:r0 + g, :] for p in range(npg)]

    sm = pages(0)
    ppb = MOBA_BLOCK // LANES
    nblk = n_pages // ppb
    gates = []
    for n in range(nblk):
        tot = sm[n * ppb]
        for i in range(1, ppb):
            tot = tot + sm[n * ppb + i]
        gates.append(jnp.sum(tot, axis=-1, keepdims=True))
    s_pages = []
    for n in range(nblk):
        cnt = jnp.zeros((g, 1), jnp.int32)
        for n2 in range(nblk):
            if n2 == n:
                continue
            beats = (gates[n2] > gates[n]) | ((gates[n2] == gates[n]) & (n2 < n))
            cnt = cnt + jnp.where(beats, 1, 0)
        for i in range(ppb):
            p = n * ppb + i
            s_pages.append(jnp.where(cnt < MOBA_TOPK, sm[p] + bs_ref[0, p], NEG))
    s_pages.append(jnp.where(new_ok, sm[n_pages] + bs_ref[0, n_pages], NEG))
    for p, pr in enumerate(_softmax_pages(s_pages)):
        p_sc[p, 0:g, :] = pr.astype(BF16)

    lfn = lfn_ref[...]
    cn_row = jnp.zeros((g, 1), F32)
    bias_new = jnp.zeros((g, LANES), F32)
    for n in range(dec_seq):
        col = lfn[:, n:n + 1]
        cn_row = cn_row + jnp.where(n <= qrow, col, 0.0)
        bias_new = bias_new + jnp.where((lane < n) & (n <= qrow), col, 0.0)
    lfe = [_expand_rows(lf_refs[p][...], dec_seq) for p in range(n_pages)]
    suffix = _nn_split(jnp.concatenate(lfe, axis=0), sl, 3)
    sf = pages(g)
    s_pages = [None] * npg
    s_pages[n_pages] = jnp.where(new_ok, sf[n_pages] + bias_new, NEG)
    after = cn_row
    for p in range(n_pages - 1, -1, -1):
        s_pages[p] = sf[p] + (suffix[p * g:(p + 1) * g, :] + after)
        after = after + jnp.sum(lfe[p], axis=-1, keepdims=True)
    for p, pr in enumerate(_softmax_pages(s_pages)):
        p_sc[p, g:2 * g, :] = pr.astype(BF16)

    ss = pages(2 * g)
    lbs, lks = [], []
    for p in range(npg):
        lb = _log_sigmoid(ss[p])
        lk = lb - ss[p]
        if p == n_pages:
            lk = jnp.where(new_past, lk, 0.0)
        lbs.append(lb)
        lks.append(lk)
    later_in = _nn_split(jnp.concatenate(lks, axis=0), sl, 2)
    after = jnp.zeros((g, 1), F32)
    for p in range(npg - 1, -1, -1):
        a = jnp.exp(lbs[p] + later_in[p * g:(p + 1) * g, :] + after)
        if p == n_pages:
            a = jnp.where(new_past, a, 0.0)
        p_sc[p, 2 * g:3 * g, :] = a.astype(BF16)
        after = after + jnp.sum(lks[p], axis=-1, keepdims=True)

    lam_val = _lam_value(lam_ref, lam_init)
    maps = []
    for mp in range(2):
        sd = pages((3 + mp) * g)
        s_pages = [sd[p] + bs_ref[1, p] for p in range(n_pages)]
        s_pages.append(jnp.where(new_ok, sd[n_pages] + bs_ref[1, n_pages], NEG))
        maps.append(_softmax_pages(s_pages))
    for p in range(npg):
        p_sc[p, 3 * g:4 * g, :] = (maps[0][p] - lam_val * maps[1][p]).astype(BF16)

    padv = jnp.zeros((LANES - vn_ref.shape[0], D_QKV), BF16)
    acc = _nn(p_sc[n_pages], jnp.concatenate([vn_ref[...], padv], axis=0))
    for p in range(n_pages):
        acc = acc + _nt(p_sc[p], v_refs[p][...].astype(BF16))
    rm = acc * omask_ref[...]
    y = rm[0:8, :]
    for i in range(1, N_HEADS * dec_seq // 8):
        y = y + rm[8 * i:8 * (i + 1), :]
    o8 = y + pltpu.roll(y, dec_seq, 0)
    ms = _nn_split(o8 * o8, gd1_ref[...], 2)
    bc = _nn_split(lax.rsqrt(ms + EPS), gd2_ref[...], 3)
    o8 = o8 * (bc * gmul_ref[...] + gadd_ref[...])
    o_ref[...] = o8[0:dec_seq, :]


def _sample_attention(layer, page_table, q8, kn, vn, lfn, bs, consts, lam, gmul, gadd,
                      cache_k, cache_v, cache_lft, lam_init):
    db, n_pages = page_table.shape
    dec_seq = lfn.shape[1] // HPM
    page = cache_k.shape[3]
    qmask, omask, sl, gd1, gd2 = consts
    rows = N_SLOTS * dec_seq

    def batch_spec(shape):
        nd = len(shape)
        return pl.BlockSpec((None,) + shape, lambda b, pt: (b,) + (0,) * nd)

    def const(shape):
        nd = len(shape)
        return pl.BlockSpec(shape, lambda b, pt: (0,) * nd)

    def paged(width_rows, width_cols):
        return [pl.BlockSpec((None, None, width_rows, width_cols),
                             functools.partial(lambda b, pt, j: (layer, pt[b, j], 0, 0), j=j))
                for j in range(n_pages)]

    in_specs = ([batch_spec((8, D_QKV)), batch_spec(kn.shape[1:]), batch_spec(vn.shape[1:]),
                 batch_spec(lfn.shape[1:]), const(bs.shape), const(qmask.shape), const(omask.shape),
                 const(sl.shape), const(lam.shape), const(gd1.shape), const(gd2.shape),
                 const(gmul.shape), const(gadd.shape)]
                + paged(D_QKV, page) + paged(D_QKV, page) + paged(HPM, page))
    grid_spec = pltpu.PrefetchScalarGridSpec(
        num_scalar_prefetch=1,
        grid=(db,),
        in_specs=in_specs,
        out_specs=pl.BlockSpec((None, dec_seq, D_QKV), lambda b, pt: (b, 0, 0)),
        scratch_shapes=[pltpu.VMEM((n_pages + 1, rows, LANES), F32),
                        pltpu.VMEM((n_pages + 1, N_HEADS * dec_seq, LANES), BF16)],
    )
    return pl.pallas_call(
        functools.partial(_sample_kernel, n_pages=n_pages, dec_seq=dec_seq, lam_init=lam_init),
        grid_spec=grid_spec,
        out_shape=jax.ShapeDtypeStruct((db, dec_seq, D_QKV), F32),
        compiler_params=_params("arbitrary"),
        name="sample_attention",
    )(page_table, q8, kn, vn, lfn, bs, qmask, omask, sl, lam, gd1, gd2, gmul, gadd,
      *([cache_k] * n_pages), *([cache_v] * n_pages), *([cache_lft] * n_pages))


def _merge_kernel(x_ref, o0_ref, o1_ref, o2_ref, o3_ref, gmix_ref, wg_ref, wbr_ref, wo_ref, y_ref):
    x = x_ref[...]
    h = (x * lax.rsqrt(jnp.mean(x * x, axis=-1, keepdims=True) + EPS)) * gmix_ref[...]
    hb = h.astype(BF16)
    acc = jnp.zeros(x.shape, F32)
    for m, o_ref in enumerate((o0_ref, o1_ref, o2_ref, o3_ref)):
        gate = 1.0 / (1.0 + jnp.exp(-_nt(hb, wg_ref[m * D_MODEL:(m + 1) * D_MODEL, :])))
        acc = acc + gate * _nn(o_ref[...].astype(BF16), wbr_ref[m])
    y_ref[...] = x + _nn(acc.astype(BF16), wo_ref[...])


def _merge(x, os_, gmix, wg, wbr, wo, tm):
    rows = x.shape[0]
    row = lambda w: pl.BlockSpec((tm, w), lambda i: (i, 0))
    return pl.pallas_call(
        _merge_kernel,
        grid=(rows // tm,),
        in_specs=[row(D_MODEL)] + [row(MIX_WIDTH)] * 4
        + [_const_spec((1, D_MODEL)), _const_spec((N_MIXERS * D_MODEL, D_MODEL)),
           _const_spec((N_MIXERS, MIX_WIDTH, D_MODEL)), _const_spec((D_MODEL, D_MODEL))],
        out_specs=row(D_MODEL),
        out_shape=jax.ShapeDtypeStruct((rows, D_MODEL), F32),
        compiler_params=_params("parallel"),
        name="merge",
    )(x, *os_, gmix, wg, wbr, wo)


def _mlp_kernel(x_ref, g_ref, wup_ref, wdn_ref, y_ref, *, chunk):
    x = x_ref[...]
    h = ((x * lax.rsqrt(jnp.mean(x * x, axis=-1, keepdims=True) + EPS)) * g_ref[...]).astype(BF16)
    acc = x
    for c in range(D_FF // chunk):
        u = jnp.maximum(_nn(h, wup_ref[:, c * chunk:(c + 1) * chunk]), 0.0)
        acc = acc + _nn((u * u).astype(BF16), wdn_ref[c * chunk:(c + 1) * chunk, :])
    y_ref[...] = acc


def _mlp(x, g, wup, wdn, tm):
    rows = x.shape[0]
    row = pl.BlockSpec((tm, D_MODEL), lambda i: (i, 0))
    return pl.pallas_call(
        functools.partial(_mlp_kernel, chunk=1024),
        grid=(rows // tm,),
        in_specs=[row, _const_spec((1, D_MODEL)), _const_spec((D_MODEL, D_FF)), _const_spec((D_FF, D_MODEL))],
        out_specs=row,
        out_shape=jax.ShapeDtypeStruct((rows, D_MODEL), F32),
        compiler_params=_params("parallel"),
        name="mlp",
    )(x, g, wup, wdn)


def _qk_vectors(gq, gk, qscale):
    z = jnp.zeros((MIX_WIDTH,), F32)
    one = jnp.ones((MIX_WIDTH,), F32)
    t4 = lambda a: jnp.tile(a, HPM)
    sc = HEAD_DIM ** -0.5 * qscale
    qmul = jnp.concatenate([t4(gq[0]) * sc, t4(gq[1]) * sc, z, t4(gq[2]) * (DIFF_HALF ** -0.5 * qscale)])
    qadd = jnp.concatenate([z, z, one * sc, z])
    kmul = jnp.concatenate([t4(gk[0]), t4(gk[1]), z, t4(gk[2])])
    kadd = jnp.concatenate([z, z, one, z])
    return [a.reshape(1, D_QKV) for a in (qmul, qadd, kmul, kadd)]


def _bias_index_tiles(tq, n_pages, page, dec_seq):
    a = np.arange(tq)
    diag = _t5_bucket_np(a[None, :] - a[:, None])
    sub = _t5_bucket_np(tq + a[None, :] - a[:, None])
    idxp = np.stack([diag, sub]).astype(np.int32)
    past_len = n_pages * page
    qpos = past_len + (np.arange(HPM * dec_seq) % dec_seq)
    kpos = np.concatenate([np.arange(past_len), past_len + np.arange(page)])
    idxs = _t5_bucket_np(qpos[None, :, None] - kpos.reshape(n_pages + 1, 1, page))
    return idxp, idxs.astype(np.int32)


def _layer(l, x, past, w, consts, tq, tm, kv_all=None):
    (tab, bp, bs, g1, g2, su_q, tri_c, sl_p, sample_consts, gd) = consts
    (w_in, b_forget, g_q, g_k, lam, g_sub, w_branch, w_o, g_mix, g_mlp, w_up, w_down) = w
    b, t, _ = x.shape
    rows = b * t
    lam_init = 0.8 - 0.6 * math.exp(-0.3 * l)
    wt = w_in[l].T.astype(BF16)
    wqkv = wt[:3 * D_QKV]
    wf = jnp.pad(wt[3 * D_QKV:3 * D_QKV + HPM], ((0, LANES - HPM), (0, 0)))
    wg = wt[3 * D_QKV + HPM:]
    gmix = g_mix[l].reshape(1, D_MODEL)
    qk_vecs = _qk_vectors(g_q[l], g_k[l], LOG2E if past is None else 1.0)
    x2 = x.reshape(rows, D_MODEL)

    if past is None:
        cols = [jnp.broadcast_to(a.reshape(D_QKV, 1), (D_QKV, LANES)) for a in qk_vecs]
        bf_col = jnp.broadcast_to(jnp.pad(b_forget[l], (0, 8 - HPM))[:, None], (8, LANES))
        qt, kb3, vt, k_new, v_new, lf_t = _inproj_t(x, gmix, wqkv, wf, bf_col, *cols, tm=tm, tq=tq,
                                                    layer=l, depth=w_in.shape[0], kv_all=kv_all)
        lf_new = jnp.swapaxes(lf_t[:, :HPM], 1, 2)
        c = _cumsum(lf_t, tri_c)
        means = _block_means(k_new, l)
        gsub_col = jnp.broadcast_to(g_sub[l][:, None], (HEAD_DIM, LANES))
        o_moba = _moba_prompt(tab, qt, kb3, vt, means, bp, tq)
        o_fox = _fox_prompt(qt, kb3, vt, c, tq)
        o_sb = _sb_prompt(qt, kb3, vt, su_q, tq)
        o_dif = _diff_prompt(tab, qt, kb3, vt, bp, lam[l], gsub_col, lam_init, tq)
        outs = [a.reshape(rows, MIX_WIDTH) for a in (o_moba, o_fox, o_sb, o_dif)]
    else:
        cache_k, cache_v, cache_lft, page_table = past
        bf = jnp.pad(b_forget[l], (0, LANES - HPM)).reshape(1, LANES)
        qb, kb, vb, kf, vf, lf = _inproj(x2, gmix, wqkv, wf, bf, g1, g2, *qk_vecs, tm=min(tm, rows))
        k_new, v_new = kf.reshape(b, t, N_HEADS, HEAD_DIM), vf.reshape(b, t, N_HEADS, HEAD_DIM)
        lf_new = lf.reshape(b, t, HPM)
        q4 = qb.astype(F32).reshape(b, t, D_QKV)
        q8 = jnp.concatenate([q4] * (8 // t), axis=1)
        padn = ((0, 0), (0, 16 - t), (0, 0))
        kn = jnp.pad(kb.reshape(b, t, D_QKV), padn)
        vn = jnp.pad(vb.reshape(b, t, D_QKV), padn)
        lfn = jnp.swapaxes(lf.reshape(b, t, HPM), 1, 2)
        lfn = jnp.pad(jnp.repeat(lfn, t, axis=1), ((0, 0), (0, 0), (0, 8 - t)))
        gd1, gd2 = gd
        z3 = jnp.zeros((3 * MIX_WIDTH,), F32)
        gmul = jnp.concatenate([z3, jnp.tile(g_sub[l], HPM) * (1.0 - lam_init)]).reshape(1, D_QKV)
        gadd = jnp.concatenate([z3 + 1.0, jnp.zeros((MIX_WIDTH,), F32)]).reshape(1, D_QKV)
        o = _sample_attention(l, page_table, q8, kn, vn, lfn, bs, (*sample_consts, sl_p, gd1, gd2),
                              lam[l], gmul, gadd, cache_k, cache_v, cache_lft, lam_init)
        o = o.reshape(rows, D_QKV)
        outs = [o[:, m * MIX_WIDTH:(m + 1) * MIX_WIDTH] for m in range(N_MIXERS)]

    y = _merge(x2, outs, gmix, wg, w_branch[l].astype(BF16), w_o[l].astype(BF16), tm=min(tm, rows))
    y = _mlp(y, g_mlp[l].reshape(1, D_MODEL), w_up[l].astype(BF16), w_down[l].astype(BF16), tm=min(tm, rows))
    return y.reshape(b, t, D_MODEL), k_new, v_new, lf_new


def _constants(rel_bias, tq, n_pages, page, dec_seq):
    idxp, idxs = _bias_index_tiles(tq, n_pages, page, dec_seq)
    tab = rel_bias.T
    bp, bs = _bias_tiles(tab, jnp.asarray(idxp), jnp.asarray(idxs), dec_seq)
    g1, g2 = _qk_group_mats()
    qmask, omask = _sample_masks(dec_seq)
    gd1 = np.zeros((D_QKV, LANES), np.float32)
    gd2 = np.zeros((LANES, D_QKV), np.float32)
    for c in range(3 * MIX_WIDTH, D_QKV):
        gd1[c, (c - 3 * MIX_WIDTH) // HEAD_DIM] = 1.0 / HEAD_DIM
        gd2[(c - 3 * MIX_WIDTH) // HEAD_DIM, c] = 1.0
    su_q = np.concatenate([_strict_lower(tq).T, np.ones((ONES_ROWS, tq), np.float32)], axis=0)
    bf = lambda a: jnp.asarray(a, BF16)
    return (tab, bp, bs, bf(g1), bf(g2), bf(su_q), bf(_upper_incl(tq)), bf(_strict_lower(page)),
            (jnp.asarray(qmask), jnp.asarray(omask)), (bf(gd1), bf(gd2)))


def kernel(x_prompt, x_sample, cache_k, cache_v, cache_logf, page_table, rel_bias, w_in, b_forget,
           g_q, g_k, lam, g_sub, w_branch, w_o, g_mix, g_mlp, w_up, w_down):
    depth = w_in.shape[0]
    tq = MOBA_BLOCK
    tm = 512
    n_pages = page_table.shape[1]
    page = cache_k.shape[2]
    dec_seq = x_sample.shape[1]
    n_phys = cache_k.shape[1]
    assert (n_pages * page) % MOBA_BLOCK == 0 and 8 % dec_seq == 0 and x_prompt.shape[1] % tq == 0

    consts = _constants(rel_bias, tq, n_pages, page, dec_seq)
    w = (w_in, b_forget, g_q, g_k, lam, g_sub, w_branch, w_o, g_mix, g_mlp, w_up, w_down)
    to_pages = lambda c: jnp.transpose(c, (0, 1, 3, 4, 2)).reshape(depth, n_phys, D_QKV, page)
    past = (to_pages(cache_k), to_pages(cache_v), jnp.swapaxes(cache_logf, 2, 3), page_table)

    hp, hs = x_prompt, x_sample
    outs = [[] for _ in range(4)]
    kv_all = None
    for l in range(depth):
        hp, kp, vp, fp = _layer(l, hp, None, w, consts, tq, tm, kv_all)
        kv_all = (kp, vp)
        hs, ks, vs, fs = _layer(l, hs, past, w, consts, tq, tm)
        for lst, a in zip(outs, (fp, ks, vs, fs)):
            lst.append(a)
    b, t = x_prompt.shape[:2]
    to_heads = lambda a: jnp.transpose(a.reshape(depth, b, N_HEADS, HEAD_DIM, t), (0, 1, 4, 2, 3))
    fp, ks, vs, fs = (jnp.stack(a) for a in outs)
    return hp, hs, to_heads(kv_all[0]), to_heads(kv_all[1]), fp, ks, vs, fs
```

```python
import functools
import math

import numpy as np
import jax
import jax.numpy as jnp
from jax import lax
from jax.experimental import pallas as pl
from jax.experimental.pallas import tpu as pltpu

F32 = jnp.float32
BF16 = jnp.bfloat16

D_MODEL = 1024
HEAD_DIM = 64
N_MIXERS = 4
HPM = 4
N_HEADS = 16
MIX_WIDTH = HPM * HEAD_DIM
D_QKV = N_HEADS * HEAD_DIM
DIFF_HALF = HEAD_DIM // 2
D_FF = 4 * D_MODEL
MOBA_BLOCK = 256
MOBA_TOPK = 3
N_BUCKETS = 32
MAX_DISTANCE = 128
EPS = 1e-6
NEG = -1e30
LOG2E = 1.4426950408889634
LANES = 128
PAIR = 2 * HEAD_DIM
VMEM_LIMIT = 56 * 1024 * 1024


def _params(*sem):
    return pltpu.CompilerParams(dimension_semantics=sem, vmem_limit_bytes=VMEM_LIMIT)


def _nt(a, b):
    return lax.dot_general(a, b, (((1,), (1,)), ((), ())), preferred_element_type=F32)


def _nn(a, b):
    return jnp.dot(a, b, preferred_element_type=F32)


def _split2(x):
    hi = x.astype(BF16)
    lo = (x - hi.astype(F32)).astype(BF16)
    return hi, lo


def _split3(x):
    hi = x.astype(BF16)
    r = x - hi.astype(F32)
    mid = r.astype(BF16)
    lo = (r - mid.astype(F32)).astype(BF16)
    return hi, mid, lo


def _nn_split(x, w, parts):
    pieces = _split2(x) if parts == 2 else _split3(x)
    out = _nn(pieces[0], w)
    for p in pieces[1:]:
        out = out + _nn(p, w)
    return out


def _log_sigmoid(z):
    return jnp.minimum(z, 0.0) - jnp.log1p(jnp.exp(-jnp.abs(z)))


def _const_spec(shape):
    nd = len(shape)
    return pl.BlockSpec(shape, lambda *_: (0,) * nd)


def _t5_bucket_np(rel):
    n = np.maximum(rel, 0)
    max_exact = N_BUCKETS // 2
    nf = np.maximum(n, 1).astype(np.float32)
    large = max_exact + (np.log(nf / np.float32(max_exact)) / np.float32(math.log(MAX_DISTANCE / max_exact))
                         * np.float32(N_BUCKETS - max_exact)).astype(np.int32)
    return np.where(n < max_exact, n, np.minimum(large, N_BUCKETS - 1)).astype(np.int32)


def _qk_group_mats():
    g1 = np.zeros((D_QKV, LANES), np.float32)
    g2 = np.zeros((LANES, D_QKV), np.float32)
    for c in range(D_QKV):
        h = c // HEAD_DIM
        if h < 2 * HPM:
            gid, size = h, HEAD_DIM
        elif h < 3 * HPM:
            continue
        else:
            gid, size = 2 * HPM + (c - 3 * MIX_WIDTH) // DIFF_HALF, DIFF_HALF
        g1[c, gid] = 1.0 / size
        g2[gid, c] = 1.0
    return g1, g2


def _strict_lower(n):
    r = np.arange(n)
    return (r[:, None] > r[None, :]).astype(np.float32)


def _upper_incl(n):
    r = np.arange(n)
    return (r[:, None] <= r[None, :]).astype(np.float32)


_WF_SPEC = pl.BlockSpec((LANES, D_MODEL), lambda *_: (3 * D_QKV // LANES, 0))


def _inproj_kernel(x_ref, gmix_ref, wqkv_ref, wf_ref, bf_ref, g1_ref, g2_ref,
                   qmul_ref, qadd_ref, kmul_ref, kadd_ref,
                   qb_ref, kb_ref, vb_ref, kf_ref, vf_ref, lf_ref):
    x = x_ref[...]
    h = (x * lax.rsqrt(jnp.mean(x * x, axis=-1, keepdims=True) + EPS)) * gmix_ref[...]
    hb = h.astype(BF16)
    g1 = g1_ref[...]
    g2 = g2_ref[...]

    def group_norm(a, mul_ref, add_ref):
        ms = _nn_split(a * a, g1, 2)
        inv = lax.rsqrt(ms + EPS)
        bc = _nn_split(inv, g2, 3)
        return a * (bc * mul_ref[...] + add_ref[...])

    q = _nt(hb, wqkv_ref[0:D_QKV, :])
    qb_ref[...] = group_norm(q, qmul_ref, qadd_ref).astype(BF16)
    k = _nt(hb, wqkv_ref[D_QKV:2 * D_QKV, :])
    kn = group_norm(k, kmul_ref, kadd_ref)
    kf_ref[...] = kn
    kb_ref[...] = kn.astype(BF16)
    v = _nt(hb, wqkv_ref[2 * D_QKV:3 * D_QKV, :])
    vf_ref[...] = v
    vb_ref[...] = v.astype(BF16)
    z = _nt(hb, wf_ref[...]) + bf_ref[...]
    lf_ref[...] = _log_sigmoid(z)[:, 0:HPM]


def _inproj(x, gmix, wqkv, wf, bf, g1, g2, qmul, qadd, kmul, kadd, tm):
    rows = x.shape[0]
    row = lambda w: pl.BlockSpec((tm, w), lambda i: (i, 0))
    vec = _const_spec((1, D_QKV))
    return pl.pallas_call(
        _inproj_kernel,
        grid=(rows // tm,),
        in_specs=[row(D_MODEL), _const_spec((1, D_MODEL)), _const_spec((3 * D_QKV, D_MODEL)),
                  _WF_SPEC, _const_spec((1, LANES)),
                  _const_spec((D_QKV, LANES)), _const_spec((LANES, D_QKV)), vec, vec, vec, vec],
        out_specs=[row(D_QKV), row(D_QKV), row(D_QKV), row(D_QKV), row(D_QKV), row(HPM)],
        out_shape=[jax.ShapeDtypeStruct((rows, D_QKV), BF16)] * 3
        + [jax.ShapeDtypeStruct((rows, D_QKV), F32)] * 2 + [jax.ShapeDtypeStruct((rows, HPM), F32)],
        compiler_params=_params("parallel"),
        name="inproj",
    )(x, gmix, wqkv, wf, bf, g1, g2, qmul, qadd, kmul, kadd)


def _inproj_t_kernel(x_ref, gmix_ref, wqkv_ref, wf_ref, bf_ref, qmul_ref, qadd_ref, kmul_ref, kadd_ref, *refs):
    qt_ref, kb_ref, vt_ref, kf_ref, vf_ref, lf_ref = refs[-6:]
    x = x_ref[...]
    tm = x.shape[0]
    h = (x * lax.rsqrt(jnp.mean(x * x, axis=-1, keepdims=True) + EPS)) * gmix_ref[...]
    hb = h.astype(BF16)
    wide = lambda ref: jnp.concatenate([ref[...]] * (tm // LANES), axis=1)
    n64, n32 = 2 * MIX_WIDTH, MIX_WIDTH

    def group_norm(a, mul_ref, add_ref):
        sq = a * a

        def inv_rms(rows0, rows1, width):
            g = sq[rows0:rows1].reshape((rows1 - rows0) // width, width, tm)
            inv = lax.rsqrt(jnp.mean(g, axis=1, keepdims=True) + EPS)
            return jnp.broadcast_to(inv, g.shape).reshape(rows1 - rows0, tm)

        scale = jnp.concatenate([inv_rms(0, n64, HEAD_DIM), jnp.zeros((D_QKV - n64 - n32, tm), F32),
                                 inv_rms(D_QKV - n32, D_QKV, DIFF_HALF)], axis=0)
        return a * (scale * wide(mul_ref) + wide(add_ref))

    q = _nt(wqkv_ref[0:D_QKV, :], hb)
    qt_ref[...] = group_norm(q, qmul_ref, qadd_ref).astype(BF16)
    k = _nt(wqkv_ref[D_QKV:2 * D_QKV, :], hb)
    kn = group_norm(k, kmul_ref, kadd_ref)
    kf_ref[...] = kn
    kb_ref[...] = kn.T.astype(BF16)
    v = _nt(wqkv_ref[2 * D_QKV:3 * D_QKV, :], hb)
    vf_ref[...] = v
    vb = v.astype(BF16)
    tq = vt_ref.shape[2]
    for i in range(tm // tq):
        vt_ref[i] = vb[:, i * tq:(i + 1) * tq]
    z = _nt(wf_ref[...], hb)[0:8, :] + wide(bf_ref)
    lf_ref[...] = _log_sigmoid(z)


def _inproj_t(x, gmix, wqkv, wf, bf_col, qmul, qadd, kmul, kadd, tm, tq, layer, depth, kv_all):
    b, t, _ = x.shape
    col = _const_spec((D_QKV, LANES))
    feat = lambda rows: pl.BlockSpec((None, rows, tm), lambda bi, i: (bi, 0, i))
    layered = pl.BlockSpec((None, None, D_QKV, tm), lambda bi, i: (layer, bi, 0, i))
    operands = [x, gmix, wqkv, wf, bf_col, qmul, qadd, kmul, kadd]
    in_specs = [pl.BlockSpec((None, tm, D_MODEL), lambda bi, i: (bi, i, 0)), _const_spec((1, D_MODEL)),
                _const_spec((3 * D_QKV, D_MODEL)), _WF_SPEC, _const_spec((8, LANES)),
                col, col, col, col]
    aliases = {}
    if kv_all is not None:
        aliases = {len(operands): 3, len(operands) + 1: 4}
        operands += list(kv_all)
        in_specs += [pl.BlockSpec(memory_space=pl.ANY)] * 2
    return pl.pallas_call(
        _inproj_t_kernel,
        grid=(b, t // tm),
        in_specs=in_specs,
        out_specs=[feat(D_QKV), pl.BlockSpec((None, tm, D_QKV), lambda bi, i: (bi, i, 0)),
                   pl.BlockSpec((None, tm // tq, D_QKV, tq), lambda bi, i: (bi, i, 0, 0)),
                   layered, layered, feat(8)],
        out_shape=[jax.ShapeDtypeStruct((b, D_QKV, t), BF16), jax.ShapeDtypeStruct((b, t, D_QKV), BF16),
                   jax.ShapeDtypeStruct((b, t // tq, D_QKV, tq), BF16),
                   jax.ShapeDtypeStruct((depth, b, D_QKV, t), F32), jax.ShapeDtypeStruct((depth, b, D_QKV, t), F32),
                   jax.ShapeDtypeStruct((b, 8, t), F32)],
        input_output_aliases=aliases,
        compiler_params=_params("parallel", "parallel"),
        name="inproj_t",
    )(*operands)


def _cumsum_kernel(lf_ref, tri_ref, c_ref, *, t, blk):
    tri = tri_ref[...]
    carry = jnp.zeros((8, 1), F32)
    for i in range(t // blk):
        cs = _nn_split(lf_ref[:, i * blk:(i + 1) * blk], tri, 3) + carry
        c_ref[:, i * blk:(i + 1) * blk] = cs
        carry = cs[:, blk - 1:blk]


def _cumsum(lf_t, tri):
    b, r, t = lf_t.shape
    blk = tri.shape[0]
    return pl.pallas_call(
        functools.partial(_cumsum_kernel, t=t, blk=blk),
        grid=(b,),
        in_specs=[pl.BlockSpec((None, r, t), lambda i: (i, 0, 0)), _const_spec((blk, blk))],
        out_specs=pl.BlockSpec((None, r, t), lambda i: (i, 0, 0)),
        out_shape=jax.ShapeDtypeStruct((b, r, t), F32),
        compiler_params=_params("parallel"),
        name="logf_cumsum",
    )(lf_t, tri)


def _means_kernel(k_ref, o_ref, *, nb):
    lane = lax.broadcasted_iota(jnp.int32, (1, LANES), 1)
    acc = jnp.zeros(o_ref.shape, F32)
    for n in range(nb):
        col = jnp.mean(k_ref[:, n * MOBA_BLOCK:(n + 1) * MOBA_BLOCK], axis=1, keepdims=True)
        acc = jnp.where(lane == n, col, acc)
    o_ref[...] = acc


def _block_means(kf_all, layer):
    _, b, _, t = kf_all.shape
    nb = t // MOBA_BLOCK
    assert nb <= LANES
    return pl.pallas_call(
        functools.partial(_means_kernel, nb=nb),
        grid=(b,),
        in_specs=[pl.BlockSpec((None, None, MIX_WIDTH, t), lambda i: (layer, i, 0, 0))],
        out_specs=pl.BlockSpec((None, MIX_WIDTH, LANES), lambda i: (i, 0, 0)),
        out_shape=jax.ShapeDtypeStruct((b, MIX_WIDTH, LANES), F32),
        compiler_params=_params("parallel"),
        name="moba_block_means",
    )(kf_all)


def _bias_kernel(tab_ref, idxp_ref, idxs_ref, bp_ref, bs_ref, *, dec_seq):
    idxp = idxp_ref[...]
    idxs = idxs_ref[...]
    row = lax.broadcasted_iota(jnp.int32, idxs.shape, 1)
    for h in range(2 * HPM):
        acc = jnp.zeros(idxp.shape, F32)
        for b in range(N_BUCKETS):
            acc = jnp.where(idxp == b, tab_ref[h, b] * LOG2E, acc)
        bp_ref[h] = acc
    for m in range(2):
        acc = jnp.zeros(idxs.shape, F32)
        for hl in range(HPM):
            for b in range(N_BUCKETS):
                acc = jnp.where((idxs == b) & (row // dec_seq == hl), tab_ref[m * HPM + hl, b], acc)
        bs_ref[m] = acc


def _bias_tiles(tab, idxp, idxs, dec_seq):
    return pl.pallas_call(
        functools.partial(_bias_kernel, dec_seq=dec_seq),
        in_specs=[pl.BlockSpec(memory_space=pltpu.SMEM), pl.BlockSpec(memory_space=pltpu.VMEM),
                  pl.BlockSpec(memory_space=pltpu.VMEM)],
        out_specs=[pl.BlockSpec(memory_space=pltpu.VMEM), pl.BlockSpec(memory_space=pltpu.VMEM)],
        out_shape=[jax.ShapeDtypeStruct((2 * HPM,) + idxp.shape, F32),
                   jax.ShapeDtypeStruct((2,) + idxs.shape, F32)],
        compiler_params=pltpu.CompilerParams(vmem_limit_bytes=VMEM_LIMIT),
        name="t5_bias_tiles",
    )(tab, idxp, idxs)


ONES_ROWS = 16
ACC_ROWS = HEAD_DIM + ONES_ROWS
EXP2_DEAD = -200.0


def _rows(n):
    return lax.broadcasted_iota(jnp.int32, (n, 1), 0)


def _key_before_query(tk, tq, strict):
    s = lax.broadcasted_iota(jnp.int32, (tk, tq), 0)
    t = lax.broadcasted_iota(jnp.int32, (tk, tq), 1)
    return (s < t) if strict else (s <= t)


def _row_range(qt, lo, hi):
    r = _rows(qt.shape[0])
    return jnp.where((r >= lo) & (r < hi), qt, jnp.zeros_like(qt))


def _v_aug(vt, x):
    return jnp.concatenate([vt[x * HEAD_DIM:(x + 1) * HEAD_DIM, :],
                            jnp.ones((ONES_ROWS, vt.shape[1]), BF16)], axis=0)


def _extra_rows(by_row, n, width):
    r = _rows(n)
    out = jnp.zeros((n, width), F32)
    for row, val in by_row.items():
        out = jnp.where(r == row, val, out)
    return out


def _pieces(by_row_f32, base):
    out = {}
    for i, val in enumerate(by_row_f32):
        for j, piece in enumerate(_split3(val)):
            out[base + 3 * i + j] = piece.astype(F32)
    return out


def _softmax_step(i, s, va, m_sc, acc_sc, sel=None):
    m_old = m_sc[i]
    bm = jnp.max(s, axis=0, keepdims=True)
    if sel is not None:
        bm = jnp.where(sel, bm, NEG)
    m_new = jnp.maximum(m_old, bm)
    m_sub = m_new if sel is None else jnp.where(sel, m_new, -NEG)
    p = jnp.exp2(s - m_sub)
    acc_sc[i] = acc_sc[i] * jnp.exp2(m_old - m_new) + _nn(va, p.astype(BF16))
    m_sc[i] = m_new


def _normalized(acc_sc, i):
    a = acc_sc[i]
    return a[0:HEAD_DIM, :] / a[HEAD_DIM:HEAD_DIM + 1, :]


def _init(m_sc, acc_sc):
    m_sc[...] = jnp.full(m_sc.shape, NEG, F32)
    acc_sc[...] = jnp.zeros(acc_sc.shape, F32)


def _sweep(qi, group, near, far, alive=None, pin=None):
    nf = jnp.maximum(qi - 1, 0)

    def rest(pinned):
        @pl.when(qi == 0)
        def _():
            near(1, pinned)

        @pl.when(qi > 0)
        def _():
            near(2, pinned)

        if alive is None:
            def body(i, c):
                top = nf - 1 - group * i
                far([top - g for g in range(group)], pinned)
                return c

            lax.fori_loop(0, nf // group, body, 0)
            ok = None
        else:
            def body(c):
                i, _ = c
                top = nf - 1 - group * i
                far([top - g for g in range(group)], pinned)
                return i + 1, alive(top - group + 1)

            _, ok = lax.while_loop(lambda c: (c[0] < nf // group) & c[1], body, (0, alive(nf)))
        rem = nf % group
        size = group // 2
        while size >= 1:
            top = (rem & (2 * size - 1)) - 1
            cond = (rem & size) != 0
            if ok is not None:
                cond = cond & ok

            @pl.when(cond)
            def _(size=size, top=top):
                far([top - g for g in range(size)], pinned)

            if ok is not None and size > 1:
                ok = ok & alive(rem & (size - 1))
            size //= 2

    if pin is None:
        rest(False)
    else:
        pinned = pin()

        @pl.when(pinned)
        def _():
            rest(True)

        @pl.when(jnp.logical_not(pinned))
        def _():
            rest(False)


PIN_HEADROOM = 100.0
PIN_MAX_GAP = 220.0


def _pin_reference(bounds, lows, m_sc):
    gap = bounds[0] - lows[0]
    for i in range(1, len(bounds)):
        gap = jnp.maximum(gap, bounds[i] - lows[i])
    pinned = jnp.max(gap) <= PIN_MAX_GAP

    @pl.when(pinned)
    def _():
        for i, (bound, low) in enumerate(zip(bounds, lows)):
            m_sc[i] = jnp.maximum(low, bound - PIN_HEADROOM)

    return pinned


def _own_key_scores(qt_ref, kd_ref, rows_per_map):
    prod = qt_ref[...].astype(F32) * kd_ref[...].astype(BF16).astype(F32)
    tq = prod.shape[1]
    z = jnp.sum(prod.reshape(MIX_WIDTH // rows_per_map, rows_per_map, tq), axis=1, keepdims=True)
    return z - 0.01


def _softmax_blocks(items, scores, v_of, m_sc, acc_sc, tq, pinned=False):
    ss = [scores(j, kind, i) for (j, kind, i, x, sel) in items]
    if not pinned:
        for s, (j, kind, i, x, sel) in zip(ss, items):
            if kind == "diag":
                s = jnp.where(_key_before_query(tq, tq, False), s, NEG)
            _softmax_step(i, s, _v_aug(v_of(j), x), m_sc, acc_sc, sel(j, x) if sel is not None else None)
        return
    pvs = {}
    for s, (j, kind, i, x, sel) in zip(ss, items):
        if kind == "diag":
            s = jnp.where(_key_before_query(tq, tq, False), s, NEG)
        ref = m_sc[i]
        if sel is not None:
            ref = jnp.where(sel(j, x), ref, -NEG)
        pv = _nn(_v_aug(v_of(j), x), jnp.exp2(s - ref).astype(BF16))
        pvs[i] = pv if i not in pvs else pvs[i] + pv
    for i, pv in pvs.items():
        acc_sc[i] = acc_sc[i] + pv


def _max_key_norms(k_ref, kn_sc):
    ksq = jnp.square(k_ref[...].astype(F32))
    lane = lax.broadcasted_iota(jnp.int32, (1, MIX_WIDTH), 1)
    for h in range(HPM):
        n2 = jnp.sum(jnp.where(lane // HEAD_DIM == h, ksq, 0.0), axis=1, keepdims=True)
        kn_sc[h] = jnp.sqrt(jnp.max(n2, axis=0, keepdims=True))


def _score_bound(q, kn, bias_max):
    qn = jnp.sqrt(jnp.sum(jnp.square(q.astype(F32)), axis=0, keepdims=True))
    return qn * kn * 1.001 + (bias_max + 0.01)


def _table_max(tab_ref, h):
    m = tab_ref[h, 0]
    for b in range(1, N_BUCKETS):
        m = jnp.maximum(m, tab_ref[h, b])
    return m * LOG2E


def _fox_kernel(qt_ref, k_ref, vt_ref, kd_ref, cq_ref, call_ref, cend_ref, o_ref, kx_sc, kn_sc, m_sc, acc_sc,
                *, tq):
    qi = pl.program_id(1)

    def head_row(c, h):
        return c[h:h + 1, :]

    @pl.when(qi == 0)
    def _():
        c = call_ref[...] * LOG2E
        for p in range(HPM // 2):
            by_row = _pieces([head_row(c, 2 * p), head_row(c, 2 * p + 1)], 0)
            by_row.update({6: 1.0, 7: 1.0, 8: 1.0})
            e = _extra_rows(by_row, 16, c.shape[1])
            e = jnp.concatenate([e, jnp.zeros((PAIR - 16, c.shape[1]), F32)], axis=0)
            kx_sc[p] = e.T.astype(BF16)
        _max_key_norms(k_ref, kn_sc)

    cq = cq_ref[...] * LOG2E
    qps, cts, zbs = [], [], []
    for h in range(HPM):
        p, x = divmod(h, 2)
        ct = head_row(cq, h)
        by_row = {3 * x: -1.0, 3 * x + 1: -1.0, 3 * x + 2: -1.0}
        by_row.update(_pieces([ct], 6))
        qx = _extra_rows(by_row, PAIR, tq).astype(BF16)
        qh = _row_range(qt_ref[p * PAIR:(p + 1) * PAIR, :], x * HEAD_DIM, (x + 1) * HEAD_DIM)
        qps.append(jnp.concatenate([qh, qx], axis=0))
        cts.append(ct)
        qn = jnp.sqrt(jnp.sum(jnp.square(qh.astype(F32)), axis=0, keepdims=True))
        zbs.append(qn * kn_sc[h] * 1.001 + 0.001)
    _init(m_sc, acc_sc)

    def alive(low):
        lane = lax.broadcasted_iota(jnp.int32, cend_ref.shape, 1)
        cl = jnp.sum(jnp.where(lane == low - 1, cend_ref[...] * LOG2E, 0.0), axis=1, keepdims=True)
        worst = zbs[0] + cts[0] - head_row(cl, 0) - m_sc[0]
        for h in range(1, HPM):
            worst = jnp.maximum(worst, zbs[h] + cts[h] - head_row(cl, h) - m_sc[h])
        return jnp.max(worst) > EXP2_DEAD

    def scores(j, kind, h):
        r0 = pl.multiple_of(j * tq, tq)
        p = h // 2
        kp = jnp.concatenate([k_ref[pl.ds(r0, tq), p * PAIR:(p + 1) * PAIR], kx_sc[p, pl.ds(r0, tq), :]], axis=1)
        return _nn(kp, qps[h])

    def run(blocks, pinned=False):
        items = [(j, kind, h, h, None) for j, kind in blocks for h in range(HPM)]
        _softmax_blocks(items, scores, lambda j: vt_ref[j], m_sc, acc_sc, tq, pinned)

    lows = _own_key_scores(qt_ref, kd_ref, HEAD_DIM)
    _sweep(qi, 4,
           lambda n, pinned: run([(qi, "diag")] + ([(qi - 1, "far")] if n == 2 else []), pinned),
           lambda js, pinned: run([(j, "far") for j in js], pinned), alive,
           lambda: _pin_reference([zb + 0.01 for zb in zbs], [lows[h] for h in range(HPM)], m_sc))
    o = jnp.concatenate([_normalized(acc_sc, h) for h in range(HPM)], axis=0)
    o_ref[...] = o.T.astype(o_ref.dtype)


def _moba_kernel(tab_ref, qt_ref, k_ref, vt_ref, kd_ref, mean_ref, bt_ref, o_ref, m_sc, acc_sc, sel_sc, kn_sc,
                 *, tq, nb):
    qi = pl.program_id(1)

    @pl.when(qi == 0)
    def _():
        _max_key_norms(k_ref, kn_sc)

    ones_k = jnp.ones((tq, PAIR), BF16)
    nbp = sel_sc.shape[1]
    rb = _rows(nbp)
    qs, qps, bounds = [], [], []
    means = [_split2(mean_ref[p * PAIR:(p + 1) * PAIR, :].T) for p in range(HPM // 2)]
    for h in range(HPM):
        p, x = divmod(h, 2)
        qx = _row_range(qt_ref[p * PAIR:(p + 1) * PAIR, :], x * HEAD_DIM, (x + 1) * HEAD_DIM)
        qs.append(qx)
        bounds.append(_score_bound(qx, kn_sc[h], _table_max(tab_ref, h)))
        far = jnp.full((1, tq), tab_ref[h, N_BUCKETS - 1] * LOG2E, F32)
        qps.append(jnp.concatenate([qx, _extra_rows(_pieces([far], 0), PAIR, tq).astype(BF16)], axis=0))
    gates = [(_nn(means[h // 2][0], qs[h]) + _nn(means[h // 2][1], qs[h]))[0:nbp, :] for h in range(HPM)]
    for h, gate in enumerate(gates):
        cnt = jnp.zeros((nbp, tq), jnp.int32)
        for n in range(nb):
            gn = gate[n:n + 1, :]
            beats = (gn > gate) | ((gn == gate) & (n < rb))
            cnt = cnt + jnp.where(beats, jnp.where(n < qi, 1, 0), 0)
        sel_sc[h] = jnp.where((cnt < MOBA_TOPK) & (rb < qi), 1.0, 0.0)
    _init(m_sc, acc_sc)

    def scores(j, kind, h):
        p = h // 2
        k = k_ref[pl.ds(pl.multiple_of(j * tq, tq), tq), p * PAIR:(p + 1) * PAIR]
        if kind == "far":
            return _nn(jnp.concatenate([k, ones_k], axis=1), qps[h])
        return _nn(k, qs[h]) + bt_ref[h, 0 if kind == "diag" else 1]

    def run(blocks, pinned=False):
        items = [(j, kind, h, h, None if kind == "diag" else (lambda j, h: sel_sc[h, pl.ds(j, 1), :] > 0.0))
                 for j, kind in blocks for h in range(HPM)]
        _softmax_blocks(items, scores, lambda j: vt_ref[j], m_sc, acc_sc, tq, pinned)

    own = _own_key_scores(qt_ref, kd_ref, HEAD_DIM)
    lows = [own[h] + tab_ref[h, 0] * LOG2E for h in range(HPM)]
    _sweep(qi, 4,
           lambda n, pinned: run([(qi, "diag")] + ([(qi - 1, "sub")] if n == 2 else []), pinned),
           lambda js, pinned: run([(j, "far") for j in js], pinned),
           pin=lambda: _pin_reference(bounds, lows, m_sc))
    o = jnp.concatenate([_normalized(acc_sc, h) for h in range(HPM)], axis=0)
    o_ref[...] = o.T.astype(o_ref.dtype)


def _sb_kernel(qt_ref, k_ref, vt_ref, su_ref, o_ref, r_sc, acc_sc, *, tq):
    qi = pl.program_id(1)
    qs = [_row_range(qt_ref[(h // 2) * PAIR:(h // 2 + 1) * PAIR, :], (h % 2) * HEAD_DIM, (h % 2 + 1) * HEAD_DIM)
          for h in range(HPM)]
    su = su_ref[...]
    r_sc[...] = jnp.zeros(r_sc.shape, F32)
    acc_sc[...] = jnp.zeros(acc_sc.shape, F32)

    def run(blocks):
        tiles = [(j, masked, h) for j, masked in blocks for h in range(HPM)]
        zs = [_nn(k_ref[pl.ds(pl.multiple_of(j * tq, tq), tq), (h // 2) * PAIR:(h // 2 + 1) * PAIR], qs[h])
              for j, masked, h in tiles]
        lbs, lats = [], []
        for z, (j, masked, h) in zip(zs, tiles):
            log_beta = jnp.minimum(z, 0.0) - jnp.log2(1.0 + jnp.exp2(-jnp.abs(z)))
            log_keep = log_beta - z
            if masked:
                log_keep = jnp.where(_key_before_query(tq, tq, True), log_keep, 0.0)
            lbs.append(log_beta)
            lats.append(_nn(su, log_keep.astype(BF16)))
        r = [r_sc[h] for h in range(HPM)]
        pvs = [jnp.zeros((HEAD_DIM, tq), F32)] * HPM
        for log_beta, lat, (j, masked, h) in zip(lbs, lats, tiles):
            a = jnp.exp2(log_beta + lat[0:tq, :] + r[h])
            if masked:
                a = jnp.where(_key_before_query(tq, tq, True), a, 0.0)
            r[h] = r[h] + lat[tq:tq + 1, :]
            pvs[h] = pvs[h] + _nn(vt_ref[j][h * HEAD_DIM:(h + 1) * HEAD_DIM, :], a.astype(BF16))
        for h in range(HPM):
            r_sc[h] = r[h]
            acc_sc[h] = acc_sc[h] + pvs[h]

    def alive(low):
        return jnp.max(r_sc[...]) > EXP2_DEAD

    _sweep(qi, 2,
           lambda n, pinned: run([(qi, True)] + ([(qi - 1, False)] if n == 2 else [])),
           lambda js, pinned: run([(j, False) for j in js]), alive)
    o = jnp.concatenate([acc_sc[h] for h in range(HPM)], axis=0)
    o_ref[...] = o.T.astype(o_ref.dtype)


def _lam_value(lam_ref, lam_init):
    lm = lam_ref[...]
    a = jnp.sum(lm[0:1] * lm[1:2], axis=-1, keepdims=True)
    b = jnp.sum(lm[2:3] * lm[3:4], axis=-1, keepdims=True)
    return jnp.exp(a) - jnp.exp(b) + lam_init


def _diff_kernel(tab_ref, qt_ref, k_ref, vt_ref, kd_ref, bt_ref, lam_ref, gsub_ref, o_ref, m_sc, acc_sc, kn_sc,
                 *, tq, lam_init):
    qi = pl.program_id(1)

    @pl.when(qi == 0)
    def _():
        _max_key_norms(k_ref, kn_sc)

    ones_k = jnp.ones((tq, PAIR), BF16)
    qs, qps, bounds = [], [], []
    for i in range(2 * HPM):
        p, r = divmod(i * DIFF_HALF, PAIR)
        qx = _row_range(qt_ref[p * PAIR:(p + 1) * PAIR, :], r, r + DIFF_HALF)
        qs.append(qx)
        bounds.append(_score_bound(qx, kn_sc[i // 2], _table_max(tab_ref, HPM + i // 2)))
        far = jnp.full((1, tq), tab_ref[HPM + i // 2, N_BUCKETS - 1] * LOG2E, F32)
        qps.append(jnp.concatenate([qx, _extra_rows(_pieces([far], 0), PAIR, tq).astype(BF16)], axis=0))
    _init(m_sc, acc_sc)

    def scores(j, kind, i):
        p = i // 4
        k = k_ref[pl.ds(pl.multiple_of(j * tq, tq), tq), p * PAIR:(p + 1) * PAIR]
        if kind == "far":
            return _nn(jnp.concatenate([k, ones_k], axis=1), qps[i])
        return _nn(k, qs[i]) + bt_ref[i // 2, 0 if kind == "diag" else 1]

    def run(blocks, pinned=False):
        items = [(j, kind, i, i // 2, None) for j, kind in blocks for i in range(2 * HPM)]
        _softmax_blocks(items, scores, lambda j: vt_ref[j], m_sc, acc_sc, tq, pinned)

    own = _own_key_scores(qt_ref, kd_ref, DIFF_HALF)
    lows = [own[i] + tab_ref[HPM + i // 2, 0] * LOG2E for i in range(2 * HPM)]
    _sweep(qi, 2,
           lambda n, pinned: run([(qi, "diag")] + ([(qi - 1, "sub")] if n == 2 else []), pinned),
           lambda js, pinned: run([(j, "far") for j in js], pinned),
           pin=lambda: _pin_reference(bounds, lows, m_sc))
    lam_val = _lam_value(lam_ref, lam_init)
    gcol = jnp.concatenate([gsub_ref[...]] * (tq // LANES), axis=1)
    outs = []
    for x in range(HPM):
        o = _normalized(acc_sc, 2 * x) - lam_val * _normalized(acc_sc, 2 * x + 1)
        inv = lax.rsqrt(jnp.mean(o * o, axis=0, keepdims=True) + EPS)
        outs.append(((o * inv) * gcol) * (1.0 - lam_init))
    o_ref[...] = jnp.concatenate(outs, axis=0).T.astype(o_ref.dtype)


def _mixer_specs(t, tq, mixer, layer):
    nk = t // tq
    qspec = pl.BlockSpec((None, MIX_WIDTH, tq), lambda b, qi: (b, mixer, qi))
    kspec = pl.BlockSpec((None, t, MIX_WIDTH), lambda b, qi: (b, 0, mixer))
    vspec = pl.BlockSpec((None, nk, MIX_WIDTH, tq), lambda b, qi: (b, 0, mixer, 0))
    kdspec = pl.BlockSpec((None, None, MIX_WIDTH, tq), lambda b, qi: (layer, b, mixer, qi))
    ospec = pl.BlockSpec((None, tq, MIX_WIDTH), lambda b, qi: (b, qi, 0))
    return [qspec, kspec, vspec, kdspec], ospec


def _attn_out(b, t):
    return jax.ShapeDtypeStruct((b, t, MIX_WIDTH), BF16)


def _stat_scratch(n_maps, tq, rows):
    return [pltpu.VMEM((n_maps, 1, tq), F32), pltpu.VMEM((n_maps, rows, tq), F32)]


def _fox_prompt(qkv, layer, c, tq):
    b, t, _ = qkv[1].shape
    specs, ospec = _mixer_specs(t, tq, 1, layer)
    cq = pl.BlockSpec((None, 8, tq), lambda bi, qi: (bi, 0, qi))
    call = pl.BlockSpec((None, 8, t), lambda bi, qi: (bi, 0, 0))
    cend = c[:, :, tq - 1::tq]
    cends = pl.BlockSpec((None, 8, t // tq), lambda bi, qi: (bi, 0, 0))
    return pl.pallas_call(
        functools.partial(_fox_kernel, tq=tq),
        grid=(b, t // tq),
        in_specs=specs + [cq, call, cends],
        out_specs=ospec,
        out_shape=_attn_out(b, t),
        scratch_shapes=[pltpu.VMEM((HPM // 2, t, PAIR), BF16), pltpu.VMEM((HPM, 1, 1), F32)]
        + _stat_scratch(HPM, tq, ACC_ROWS),
        compiler_params=_params("parallel", "arbitrary"),
        name="fox_prompt",
    )(*qkv, c, c, cend)


def _moba_prompt(tab, qkv, layer, means, bp, tq):
    b, t, _ = qkv[1].shape
    nb = t // MOBA_BLOCK
    nbp = -(-nb // 8) * 8
    specs, ospec = _mixer_specs(t, tq, 0, layer)
    mspec = pl.BlockSpec((None, MIX_WIDTH, LANES), lambda bi, qi: (bi, 0, 0))
    bt = pl.BlockSpec((HPM, 2, tq, tq), lambda bi, qi: (0, 0, 0, 0))
    return pl.pallas_call(
        functools.partial(_moba_kernel, tq=tq, nb=nb),
        grid=(b, t // tq),
        in_specs=[pl.BlockSpec(memory_space=pltpu.SMEM)] + specs + [mspec, bt],
        out_specs=ospec,
        out_shape=_attn_out(b, t),
        scratch_shapes=_stat_scratch(HPM, tq, ACC_ROWS) + [pltpu.VMEM((HPM, nbp, tq), F32),
                                                          pltpu.VMEM((HPM, 1, 1), F32)],
        compiler_params=_params("parallel", "arbitrary"),
        name="moba_prompt",
    )(tab, *qkv, means, bp)


def _sb_prompt(qkv, layer, su, tq):
    b, t, _ = qkv[1].shape
    specs, ospec = _mixer_specs(t, tq, 2, layer)
    return pl.pallas_call(
        functools.partial(_sb_kernel, tq=tq),
        grid=(b, t // tq),
        in_specs=specs[:3] + [_const_spec(su.shape)],
        out_specs=ospec,
        out_shape=_attn_out(b, t),
        scratch_shapes=_stat_scratch(HPM, tq, HEAD_DIM),
        compiler_params=_params("parallel", "arbitrary"),
        name="sb_prompt",
    )(*qkv[:3], su)


def _diff_prompt(tab, qkv, layer, bp, lam, gsub_col, lam_init, tq):
    b, t, _ = qkv[1].shape
    specs, ospec = _mixer_specs(t, tq, 3, layer)
    bt = pl.BlockSpec((HPM, 2, tq, tq), lambda bi, qi: (1, 0, 0, 0))
    return pl.pallas_call(
        functools.partial(_diff_kernel, tq=tq, lam_init=lam_init),
        grid=(b, t // tq),
        in_specs=[pl.BlockSpec(memory_space=pltpu.SMEM)] + specs
        + [bt, _const_spec((4, DIFF_HALF)), _const_spec((HEAD_DIM, LANES))],
        out_specs=ospec,
        out_shape=_attn_out(b, t),
        scratch_shapes=_stat_scratch(2 * HPM, tq, ACC_ROWS) + [pltpu.VMEM((HPM, 1, 1), F32)],
        compiler_params=_params("parallel", "arbitrary"),
        name="diff_prompt",
    )(tab, *qkv, bp, lam, gsub_col)


N_SLOTS = 20


def _sample_masks(dec_seq):
    rows = N_SLOTS * dec_seq
    qmask = np.zeros((rows, D_QKV), np.float32)
    for r in range(rows):
        slot = r // dec_seq
        if slot < 3 * HPM:
            qmask[r, slot * HEAD_DIM:(slot + 1) * HEAD_DIM] = 1.0
        elif slot < 4 * HPM:
            qmask[r, slot * HEAD_DIM:slot * HEAD_DIM + DIFF_HALF] = 1.0
        else:
            c0 = (slot - HPM) * HEAD_DIM + DIFF_HALF
            qmask[r, c0:c0 + DIFF_HALF] = 1.0
    omask = np.zeros((N_HEADS * dec_seq, D_QKV), np.float32)
    for r in range(N_HEADS * dec_seq):
        omask[r, (r // dec_seq) * HEAD_DIM:(r // dec_seq + 1) * HEAD_DIM] = 1.0
    return qmask, omask


def _expand_rows(a, dec_seq):
    rows = HPM * dec_seq
    r = lax.broadcasted_iota(jnp.int32, (rows, 1), 0)
    out = jnp.broadcast_to(a[HPM - 1:HPM, :], (rows, a.shape[1]))
    for h in range(HPM - 2, -1, -1):
        out = jnp.where(r < (h + 1) * dec_seq, a[h:h + 1, :], out)
    return out


def _softmax_pages(s_pages):
    m = s_pages[0].max(axis=-1, keepdims=True)
    for s in s_pages[1:]:
        m = jnp.maximum(m, s.max(axis=-1, keepdims=True))
    ps = [jnp.exp(s - m) for s in s_pages]
    l = ps[0].sum(axis=-1, keepdims=True)
    for p in ps[1:]:
        l = l + p.sum(axis=-1, keepdims=True)
    return [p / l for p in ps]


def _sample_kernel(*refs, n_pages, dec_seq, lam_init):
    pt_ref = refs[0]
    del pt_ref
    (q_ref, kn_ref, vn_ref, lfn_ref, bs_ref, qmask_ref, omask_ref, sl_ref, lam_ref,
     gd1_ref, gd2_ref, gmul_ref, gadd_ref) = refs[1:14]
    k_refs = refs[14:14 + n_pages]
    v_refs = refs[14 + n_pages:14 + 2 * n_pages]
    lf_refs = refs[14 + 2 * n_pages:14 + 3 * n_pages]
    o_ref = refs[14 + 3 * n_pages]
    s_sc, p_sc = refs[14 + 3 * n_pages + 1:]

    g = HPM * dec_seq
    npg = n_pages + 1
    q = q_ref[...]
    qbd = (jnp.concatenate([q] * (N_SLOTS * dec_seq // 8), axis=0) * qmask_ref[...]).astype(BF16)
    pad = jnp.zeros((LANES - kn_ref.shape[0], D_QKV), BF16)
    for p in range(n_pages):
        s_sc[p] = _nn(qbd, k_refs[p][...].astype(BF16))
    s_sc[n_pages] = _nt(qbd, jnp.concatenate([kn_ref[...], pad], axis=0))

    lane = lax.broadcasted_iota(jnp.int32, (1, LANES), 1)
    qrow = lax.broadcasted_iota(jnp.int32, (g, 1), 0) % dec_seq
    new_ok = lane <= qrow
    new_past = lane < qrow
    sl = sl_ref[...]

    def pages(r0):
        return [s_sc[p, r0:r0 + g, :] for p in range(npg)]

    sm = pages(0)
    ppb = MOBA_BLOCK // LANES
    nblk = n_pages // ppb
    gates = []
    for n in range(nblk):
        tot = sm[n * ppb]
        for i in range(1, ppb):
            tot = tot + sm[n * ppb + i]
        gates.append(jnp.sum(tot, axis=-1, keepdims=True))
    s_pages = []
    for n in range(nblk):
        cnt = jnp.zeros((g, 1), jnp.int32)
        for n2 in range(nblk):
            if n2 == n:
                continue
            beats = (gates[n2] > gates[n]) | ((gates[n2] == gates[n]) & (n2 < n))
            cnt = cnt + jnp.where(beats, 1, 0)
        for i in range(ppb):
            p = n * ppb + i
            s_pages.append(jnp.where(cnt < MOBA_TOPK, sm[p] + bs_ref[0, p], NEG))
    s_pages.append(jnp.where(new_ok, sm[n_pages] + bs_ref[0, n_pages], NEG))
    for p, pr in enumerate(_softmax_pages(s_pages)):
        p_sc[p, 0:g, :] = pr.astype(BF16)

    lfn = lfn_ref[...]
    cn_row = jnp.zeros((g, 1), F32)
    bias_new = jnp.zeros((g, LANES), F32)
    for n in range(dec_seq):
        col = lfn[:, n:n + 1]
        cn_row = cn_row + jnp.where(n <= qrow, col, 0.0)
        bias_new = bias_new + jnp.where((lane < n) & (n <= qrow), col, 0.0)
    lfe = [_expand_rows(lf_refs[p][...], dec_seq) for p in range(n_pages)]
    suffix = _nn_split(jnp.concatenate(lfe, axis=0), sl, 3)
    sf = pages(g)
    s_pages = [None] * npg
    s_pages[n_pages] = jnp.where(new_ok, sf[n_pages] + bias_new, NEG)
    after = cn_row
    for p in range(n_pages - 1, -1, -1):
        s_pages[p] = sf[p] + (suffix[p * g:(p + 1) * g, :] + after)
        after = after + jnp.sum(lfe[p], axis=-1, keepdims=True)
    for p, pr in enumerate(_softmax_pages(s_pages)):
        p_sc[p, g:2 * g, :] = pr.astype(BF16)

    ss = pages(2 * g)
    lbs, lks = [], []
    for p in range(npg):
        lb = _log_sigmoid(ss[p])
        lk = lb - ss[p]
        if p == n_pages:
            lk = jnp.where(new_past, lk, 0.0)
        lbs.append(lb)
        lks.append(lk)
    later_in = _nn_split(jnp.concatenate(lks, axis=0), sl, 2)
    after = jnp.zeros((g, 1), F32)
    for p in range(npg - 1, -1, -1):
        a = jnp.exp(lbs[p] + later_in[p * g:(p + 1) * g, :] + after)
        if p == n_pages:
            a = jnp.where(new_past, a, 0.0)
        p_sc[p, 2 * g:3 * g, :] = a.astype(BF16)
        after = after + jnp.sum(lks[p], axis=-1, keepdims=True)

    lam_val = _lam_value(lam_ref, lam_init)
    maps = []
    for mp in range(2):
        sd = pages((3 + mp) * g)
        s_pages = [sd[p] + bs_ref[1, p] for p in range(n_pages)]
        s_pages.append(jnp.where(new_ok, sd[n_pages] + bs_ref[1, n_pages], NEG))
        maps.append(_softmax_pages(s_pages))
    for p in range(npg):
        p_sc[p, 3 * g:4 * g, :] = (maps[0][p] - lam_val * maps[1][p]).astype(BF16)

    padv = jnp.zeros((LANES - vn_ref.shape[0], D_QKV), BF16)
    acc = _nn(p_sc[n_pages], jnp.concatenate([vn_ref[...], padv], axis=0))
    for p in range(n_pages):
        acc = acc + _nt(p_sc[p], v_refs[p][...].astype(BF16))
    rm = acc * omask_ref[...]
    y = rm[0:8, :]
    for i in range(1, N_HEADS * dec_seq // 8):
        y = y + rm[8 * i:8 * (i + 1), :]
    o8 = y + pltpu.roll(y, dec_seq, 0)
    ms = _nn_split(o8 * o8, gd1_ref[...], 2)
    bc = _nn_split(lax.rsqrt(ms + EPS), gd2_ref[...], 3)
    o8 = o8 * (bc * gmul_ref[...] + gadd_ref[...])
    o_ref[...] = o8[0:dec_seq, :]


def _sample_attention(layer, page_table, q8, kn, vn, lfn, bs, consts, lam, gmul, gadd,
                      cache_k, cache_v, cache_lft, lam_init):
    db, n_pages = page_table.shape
    dec_seq = lfn.shape[1] // HPM
    page = cache_k.shape[3]
    qmask, omask, sl, gd1, gd2 = consts
    rows = N_SLOTS * dec_seq

    def batch_spec(shape):
        nd = len(shape)
        return pl.BlockSpec((None,) + shape, lambda b, pt: (b,) + (0,) * nd)

    def const(shape):
        nd = len(shape)
        return pl.BlockSpec(shape, lambda b, pt: (0,) * nd)

    def paged(width_rows, width_cols):
        return [pl.BlockSpec((None, None, width_rows, width_cols),
                             functools.partial(lambda b, pt, j: (layer, pt[b, j], 0, 0), j=j))
                for j in range(n_pages)]

    in_specs = ([batch_spec((8, D_QKV)), batch_spec(kn.shape[1:]), batch_spec(vn.shape[1:]),
                 batch_spec(lfn.shape[1:]), const(bs.shape), const(qmask.shape), const(omask.shape),
                 const(sl.shape), const(lam.shape), const(gd1.shape), const(gd2.shape),
                 const(gmul.shape), const(gadd.shape)]
                + paged(D_QKV, page) + paged(D_QKV, page) + paged(HPM, page))
    grid_spec = pltpu.PrefetchScalarGridSpec(
        num_scalar_prefetch=1,
        grid=(db,),
        in_specs=in_specs,
        out_specs=pl.BlockSpec((None, dec_seq, D_QKV), lambda b, pt: (b, 0, 0)),
        scratch_shapes=[pltpu.VMEM((n_pages + 1, rows, LANES), F32),
                        pltpu.VMEM((n_pages + 1, N_HEADS * dec_seq, LANES), BF16)],
    )
    return pl.pallas_call(
        functools.partial(_sample_kernel, n_pages=n_pages, dec_seq=dec_seq, lam_init=lam_init),
        grid_spec=grid_spec,
        out_shape=jax.ShapeDtypeStruct((db, dec_seq, D_QKV), F32),
        compiler_params=_params("arbitrary"),
        name="sample_attention",
    )(page_table, q8, kn, vn, lfn, bs, qmask, omask, sl, lam, gd1, gd2, gmul, gadd,
      *([cache_k] * n_pages), *([cache_v] * n_pages), *([cache_lft] * n_pages))


def _merge_kernel(x_ref, o0_ref, o1_ref, o2_ref, o3_ref, gmix_ref, wg_ref, wbr_ref, wo_ref, y_ref):
    x = x_ref[...]
    h = (x * lax.rsqrt(jnp.mean(x * x, axis=-1, keepdims=True) + EPS)) * gmix_ref[...]
    hb = h.astype(BF16)
    acc = jnp.zeros(x.shape, F32)
    for m, o_ref in enumerate((o0_ref, o1_ref, o2_ref, o3_ref)):
        gate = 1.0 / (1.0 + jnp.exp(-_nt(hb, wg_ref[m * D_MODEL:(m + 1) * D_MODEL, :])))
        acc = acc + gate * _nn(o_ref[...].astype(BF16), wbr_ref[m])
    y_ref[...] = x + _nn(acc.astype(BF16), wo_ref[...])


def _merge(x, os_, gmix, wg, wbr, wo, tm):
    rows = x.shape[0]
    row = lambda w: pl.BlockSpec((tm, w), lambda i: (i, 0))
    return pl.pallas_call(
        _merge_kernel,
        grid=(rows // tm,),
        in_specs=[row(D_MODEL)] + [row(MIX_WIDTH)] * 4
        + [_const_spec((1, D_MODEL)), _const_spec((N_MIXERS * D_MODEL, D_MODEL)),
           _const_spec((N_MIXERS, MIX_WIDTH, D_MODEL)), _const_spec((D_MODEL, D_MODEL))],
        out_specs=row(D_MODEL),
        out_shape=jax.ShapeDtypeStruct((rows, D_MODEL), F32),
        compiler_params=_params("parallel"),
        name="merge",
    )(x, *os_, gmix, wg, wbr, wo)


def _mlp_kernel(x_ref, g_ref, wup_ref, wdn_ref, y_ref, *, chunk):
    x = x_ref[...]
    h = ((x * lax.rsqrt(jnp.mean(x * x, axis=-1, keepdims=True) + EPS)) * g_ref[...]).astype(BF16)
    acc = x
    for c in range(D_FF // chunk):
        u = jnp.maximum(_nn(h, wup_ref[:, c * chunk:(c + 1) * chunk]), 0.0)
        acc = acc + _nn((u * u).astype(BF16), wdn_ref[c * chunk:(c + 1) * chunk, :])
    y_ref[...] = acc


def _mlp(x, g, wup, wdn, tm):
    rows = x.shape[0]
    row = pl.BlockSpec((tm, D_MODEL), lambda i: (i, 0))
    return pl.pallas_call(
        functools.partial(_mlp_kernel, chunk=1024),
        grid=(rows // tm,),
        in_specs=[row, _const_spec((1, D_MODEL)), _const_spec((D_MODEL, D_FF)), _const_spec((D_FF, D_MODEL))],
        out_specs=row,
        out_shape=jax.ShapeDtypeStruct((rows, D_MODEL), F32),
        compiler_params=_params("parallel"),
        name="mlp",
    )(x, g, wup, wdn)


def _qk_vectors(gq, gk, qscale):
    z = jnp.zeros((MIX_WIDTH,), F32)
    one = jnp.ones((MIX_WIDTH,), F32)
    t4 = lambda a: jnp.tile(a, HPM)
    sc = HEAD_DIM ** -0.5 * qscale
    qmul = jnp.concatenate([t4(gq[0]) * sc, t4(gq[1]) * sc, z, t4(gq[2]) * (DIFF_HALF ** -0.5 * qscale)])
    qadd = jnp.concatenate([z, z, one * sc, z])
    kmul = jnp.concatenate([t4(gk[0]), t4(gk[1]), z, t4(gk[2])])
    kadd = jnp.concatenate([z, z, one, z])
    return [a.reshape(1, D_QKV) for a in (qmul, qadd, kmul, kadd)]


def _bias_index_tiles(tq, n_pages, page, dec_seq):
    a = np.arange(tq)
    diag = _t5_bucket_np(a[None, :] - a[:, None])
    sub = _t5_bucket_np(tq + a[None, :] - a[:, None])
    idxp = np.stack([diag, sub]).astype(np.int32)
    past_len = n_pages * page
    qpos = past_len + (np.arange(HPM * dec_seq) % dec_seq)
    kpos = np.concatenate([np.arange(past_len), past_len + np.arange(page)])
    idxs = _t5_bucket_np(qpos[None, :, None] - kpos.reshape(n_pages + 1, 1, page))
    return idxp, idxs.astype(np.int32)


def _layer(l, x, past, w, consts, tq, tm, kv_all=None):
    (tab, bp, bs, g1, g2, su_q, tri_c, sl_p, sample_consts, gd) = consts
    (w_in, b_forget, g_q, g_k, lam, g_sub, w_branch, w_o, g_mix, g_mlp, w_up, w_down) = w
    b, t, _ = x.shape
    rows = b * t
    lam_init = 0.8 - 0.6 * math.exp(-0.3 * l)
    wt = w_in[l].T.astype(BF16)
    wqkv = wf = wt
    wg = wt[3 * D_QKV + HPM:]
    gmix = g_mix[l].reshape(1, D_MODEL)
    qk_vecs = _qk_vectors(g_q[l], g_k[l], LOG2E if past is None else 1.0)
    x2 = x.reshape(rows, D_MODEL)

    if past is None:
        cols = [jnp.broadcast_to(a.reshape(D_QKV, 1), (D_QKV, LANES)) for a in qk_vecs]
        bf_col = jnp.broadcast_to(jnp.pad(b_forget[l], (0, 8 - HPM))[:, None], (8, LANES))
        qt, kb3, vt, k_new, v_new, lf_t = _inproj_t(x, gmix, wqkv, wf, bf_col, *cols, tm=tm, tq=tq,
                                                    layer=l, depth=w_in.shape[0], kv_all=kv_all)
        lf_new = jnp.swapaxes(lf_t[:, :HPM], 1, 2)
        c = _cumsum(lf_t, tri_c)
        means = _block_means(k_new, l)
        gsub_col = jnp.broadcast_to(g_sub[l][:, None], (HEAD_DIM, LANES))
        qkv = (qt, kb3, vt, k_new)
        o_moba = _moba_prompt(tab, qkv, l, means, bp, tq)
        o_fox = _fox_prompt(qkv, l, c, tq)
        o_sb = _sb_prompt(qkv, l, su_q, tq)
        o_dif = _diff_prompt(tab, qkv, l, bp, lam[l], gsub_col, lam_init, tq)
        outs = [a.reshape(rows, MIX_WIDTH) for a in (o_moba, o_fox, o_sb, o_dif)]
    else:
        cache_k, cache_v, cache_lft, page_table = past
        bf = jnp.pad(b_forget[l], (0, LANES - HPM)).reshape(1, LANES)
        qb, kb, vb, kf, vf, lf = _inproj(x2, gmix, wqkv, wf, bf, g1, g2, *qk_vecs, tm=min(tm, rows))
        k_new, v_new = kf.reshape(b, t, N_HEADS, HEAD_DIM), vf.reshape(b, t, N_HEADS, HEAD_DIM)
        lf_new = lf.reshape(b, t, HPM)
        q4 = qb.astype(F32).reshape(b, t, D_QKV)
        q8 = jnp.concatenate([q4] * (8 // t), axis=1)
        padn = ((0, 0), (0, 16 - t), (0, 0))
        kn = jnp.pad(kb.reshape(b, t, D_QKV), padn)
        vn = jnp.pad(vb.reshape(b, t, D_QKV), padn)
        lfn = jnp.swapaxes(lf.reshape(b, t, HPM), 1, 2)
        lfn = jnp.pad(jnp.repeat(lfn, t, axis=1), ((0, 0), (0, 0), (0, 8 - t)))
        gd1, gd2 = gd
        z3 = jnp.zeros((3 * MIX_WIDTH,), F32)
        gmul = jnp.concatenate([z3, jnp.tile(g_sub[l], HPM) * (1.0 - lam_init)]).reshape(1, D_QKV)
        gadd = jnp.concatenate([z3 + 1.0, jnp.zeros((MIX_WIDTH,), F32)]).reshape(1, D_QKV)
        o = _sample_attention(l, page_table, q8, kn, vn, lfn, bs, (*sample_consts, sl_p, gd1, gd2),
                              lam[l], gmul, gadd, cache_k, cache_v, cache_lft, lam_init)
        o = o.reshape(rows, D_QKV)
        outs = [o[:, m * MIX_WIDTH:(m + 1) * MIX_WIDTH] for m in range(N_MIXERS)]

    y = _merge(x2, outs, gmix, wg, w_branch[l].astype(BF16), w_o[l].astype(BF16), tm=min(tm, rows))
    y = _mlp(y, g_mlp[l].reshape(1, D_MODEL), w_up[l].astype(BF16), w_down[l].astype(BF16), tm=min(tm, rows))
    return y.reshape(b, t, D_MODEL), k_new, v_new, lf_new


def _constants(rel_bias, tq, n_pages, page, dec_seq):
    idxp, idxs = _bias_index_tiles(tq, n_pages, page, dec_seq)
    tab = rel_bias.T
    bp, bs = _bias_tiles(tab, jnp.asarray(idxp), jnp.asarray(idxs), dec_seq)
    g1, g2 = _qk_group_mats()
    qmask, omask = _sample_masks(dec_seq)
    gd1 = np.zeros((D_QKV, LANES), np.float32)
    gd2 = np.zeros((LANES, D_QKV), np.float32)
    for c in range(3 * MIX_WIDTH, D_QKV):
        gd1[c, (c - 3 * MIX_WIDTH) // HEAD_DIM] = 1.0 / HEAD_DIM
        gd2[(c - 3 * MIX_WIDTH) // HEAD_DIM, c] = 1.0
    su_q = np.concatenate([_strict_lower(tq).T, np.ones((ONES_ROWS, tq), np.float32)], axis=0)
    bf = lambda a: jnp.asarray(a, BF16)
    return (tab, bp, bs, bf(g1), bf(g2), bf(su_q), bf(_upper_incl(tq)), bf(_strict_lower(page)),
            (jnp.asarray(qmask), jnp.asarray(omask)), (bf(gd1), bf(gd2)))


def kernel(x_prompt, x_sample, cache_k, cache_v, cache_logf, page_table, rel_bias, w_in, b_forget,
           g_q, g_k, lam, g_sub, w_branch, w_o, g_mix, g_mlp, w_up, w_down):
    depth = w_in.shape[0]
    tq = MOBA_BLOCK
    tm = 512
    n_pages = page_table.shape[1]
    page = cache_k.shape[2]
    dec_seq = x_sample.shape[1]
    n_phys = cache_k.shape[1]
    assert (n_pages * page) % MOBA_BLOCK == 0 and 8 % dec_seq == 0 and x_prompt.shape[1] % tq == 0

    consts = _constants(rel_bias, tq, n_pages, page, dec_seq)
    w = (w_in, b_forget, g_q, g_k, lam, g_sub, w_branch, w_o, g_mix, g_mlp, w_up, w_down)
    to_pages = lambda c: jnp.transpose(c, (0, 1, 3, 4, 2)).reshape(depth, n_phys, D_QKV, page)
    past = (to_pages(cache_k), to_pages(cache_v), jnp.swapaxes(cache_logf, 2, 3), page_table)

    hp, hs = x_prompt, x_sample
    outs = [[] for _ in range(4)]
    kv_all = None
    for l in range(depth):
        hp, kp, vp, fp = _layer(l, hp, None, w, consts, tq, tm, kv_all)
        kv_all = (kp, vp)
        hs, ks, vs, fs = _layer(l, hs, past, w, consts, tq, tm)
        for lst, a in zip(outs, (fp, ks, vs, fs)):
            lst.append(a)
    b, t = x_prompt.shape[:2]
    to_heads = lambda a: jnp.transpose(a.reshape(depth, b, N_HEADS, HEAD_DIM, t), (0, 1, 4, 2, 3))
    fp, ks, vs, fs = (jnp.stack(a) for a in outs)
    return hp, hs, to_heads(kv_all[0]), to_heads(kv_all[1]), fp, ks, vs, fs
```

```python
import functools
import math

import numpy as np
import jax
import jax.numpy as jnp
from jax import lax
from jax.experimental import pallas as pl
from jax.experimental.pallas import tpu as pltpu

F32 = jnp.float32
BF16 = jnp.bfloat16

D_MODEL = 1024
HEAD_DIM = 64
N_MIXERS = 4
HPM = 4
N_HEADS = 16
MIX_WIDTH = HPM * HEAD_DIM
D_QKV = N_HEADS * HEAD_DIM
DIFF_HALF = HEAD_DIM // 2
D_FF = 4 * D_MODEL
MOBA_BLOCK = 256
MOBA_TOPK = 3
N_BUCKETS = 32
MAX_DISTANCE = 128
EPS = 1e-6
NEG = -1e30
LOG2E = 1.4426950408889634
LANES = 128
PAIR = 2 * HEAD_DIM
VMEM_LIMIT = 56 * 1024 * 1024


def _params(*sem):
    return pltpu.CompilerParams(dimension_semantics=sem, vmem_limit_bytes=VMEM_LIMIT)


def _nt(a, b):
    return lax.dot_general(a, b, (((1,), (1,)), ((), ())), preferred_element_type=F32)


def _nn(a, b):
    return jnp.dot(a, b, preferred_element_type=F32)


def _split2(x):
    hi = x.astype(BF16)
    lo = (x - hi.astype(F32)).astype(BF16)
    return hi, lo


def _split3(x):
    hi = x.astype(BF16)
    r = x - hi.astype(F32)
    mid = r.astype(BF16)
    lo = (r - mid.astype(F32)).astype(BF16)
    return hi, mid, lo


def _nn_split(x, w, parts):
    pieces = _split2(x) if parts == 2 else _split3(x)
    out = _nn(pieces[0], w)
    for p in pieces[1:]:
        out = out + _nn(p, w)
    return out


def _log_sigmoid(z):
    return jnp.minimum(z, 0.0) - jnp.log1p(jnp.exp(-jnp.abs(z)))


def _const_spec(shape):
    nd = len(shape)
    return pl.BlockSpec(shape, lambda *_: (0,) * nd)


def _t5_bucket_np(rel):
    n = np.maximum(rel, 0)
    max_exact = N_BUCKETS // 2
    nf = np.maximum(n, 1).astype(np.float32)
    large = max_exact + (np.log(nf / np.float32(max_exact)) / np.float32(math.log(MAX_DISTANCE / max_exact))
                         * np.float32(N_BUCKETS - max_exact)).astype(np.int32)
    return np.where(n < max_exact, n, np.minimum(large, N_BUCKETS - 1)).astype(np.int32)


def _qk_group_mats():
    g1 = np.zeros((D_QKV, LANES), np.float32)
    g2 = np.zeros((LANES, D_QKV), np.float32)
    for c in range(D_QKV):
        h = c // HEAD_DIM
        if h < 2 * HPM:
            gid, size = h, HEAD_DIM
        elif h < 3 * HPM:
            continue
        else:
            gid, size = 2 * HPM + (c - 3 * MIX_WIDTH) // DIFF_HALF, DIFF_HALF
        g1[c, gid] = 1.0 / size
        g2[gid, c] = 1.0
    return g1, g2


def _strict_lower(n):
    r = np.arange(n)
    return (r[:, None] > r[None, :]).astype(np.float32)


def _upper_incl(n):
    r = np.arange(n)
    return (r[:, None] <= r[None, :]).astype(np.float32)


_WF_SPEC = pl.BlockSpec((LANES, D_MODEL), lambda *_: (3 * D_QKV // LANES, 0))


def _inproj_kernel(x_ref, gmix_ref, wqkv_ref, wf_ref, bf_ref, g1_ref, g2_ref,
                   qmul_ref, qadd_ref, kmul_ref, kadd_ref,
                   qb_ref, kb_ref, vb_ref, kf_ref, vf_ref, lf_ref):
    x = x_ref[...]
    h = (x * lax.rsqrt(jnp.mean(x * x, axis=-1, keepdims=True) + EPS)) * gmix_ref[...]
    hb = h.astype(BF16)
    g1 = g1_ref[...]
    g2 = g2_ref[...]

    def group_norm(a, mul_ref, add_ref):
        ms = _nn_split(a * a, g1, 2)
        inv = lax.rsqrt(ms + EPS)
        bc = _nn_split(inv, g2, 3)
        return a * (bc * mul_ref[...] + add_ref[...])

    q = _nt(hb, wqkv_ref[0:D_QKV, :])
    qb_ref[...] = group_norm(q, qmul_ref, qadd_ref).astype(BF16)
    k = _nt(hb, wqkv_ref[D_QKV:2 * D_QKV, :])
    kn = group_norm(k, kmul_ref, kadd_ref)
    kf_ref[...] = kn
    kb_ref[...] = kn.astype(BF16)
    v = _nt(hb, wqkv_ref[2 * D_QKV:3 * D_QKV, :])
    vf_ref[...] = v
    vb_ref[...] = v.astype(BF16)
    z = _nt(hb, wf_ref[...]) + bf_ref[...]
    lf_ref[...] = _log_sigmoid(z)[:, 0:HPM]


def _inproj(x, gmix, wqkv, wf, bf, g1, g2, qmul, qadd, kmul, kadd, tm):
    rows = x.shape[0]
    row = lambda w: pl.BlockSpec((tm, w), lambda i: (i, 0))
    vec = _const_spec((1, D_QKV))
    return pl.pallas_call(
        _inproj_kernel,
        grid=(rows // tm,),
        in_specs=[row(D_MODEL), _const_spec((1, D_MODEL)), _const_spec((3 * D_QKV, D_MODEL)),
                  _WF_SPEC, _const_spec((1, LANES)),
                  _const_spec((D_QKV, LANES)), _const_spec((LANES, D_QKV)), vec, vec, vec, vec],
        out_specs=[row(D_QKV), row(D_QKV), row(D_QKV), row(D_QKV), row(D_QKV), row(HPM)],
        out_shape=[jax.ShapeDtypeStruct((rows, D_QKV), BF16)] * 3
        + [jax.ShapeDtypeStruct((rows, D_QKV), F32)] * 2 + [jax.ShapeDtypeStruct((rows, HPM), F32)],
        compiler_params=_params("parallel"),
        name="inproj",
    )(x, gmix, wqkv, wf, bf, g1, g2, qmul, qadd, kmul, kadd)


def _inproj_t_kernel(x_ref, gmix_ref, wqkv_ref, wf_ref, bf_ref, qmul_ref, qadd_ref, kmul_ref, kadd_ref, *refs):
    qt_ref, kb_ref, vt_ref, kf_ref, vf_ref, lf_ref = refs[-6:]
    x = x_ref[...]
    tm = x.shape[0]
    h = (x * lax.rsqrt(jnp.mean(x * x, axis=-1, keepdims=True) + EPS)) * gmix_ref[...]
    hb = h.astype(BF16)
    wide = lambda ref: jnp.concatenate([ref[...]] * (tm // LANES), axis=1)
    n64, n32 = 2 * MIX_WIDTH, MIX_WIDTH

    def group_norm(a, mul_ref, add_ref):
        sq = a * a

        def inv_rms(rows0, rows1, width):
            g = sq[rows0:rows1].reshape((rows1 - rows0) // width, width, tm)
            inv = lax.rsqrt(jnp.mean(g, axis=1, keepdims=True) + EPS)
            return jnp.broadcast_to(inv, g.shape).reshape(rows1 - rows0, tm)

        scale = jnp.concatenate([inv_rms(0, n64, HEAD_DIM), jnp.zeros((D_QKV - n64 - n32, tm), F32),
                                 inv_rms(D_QKV - n32, D_QKV, DIFF_HALF)], axis=0)
        return a * (scale * wide(mul_ref) + wide(add_ref))

    q = _nt(wqkv_ref[0:D_QKV, :], hb)
    qt_ref[...] = group_norm(q, qmul_ref, qadd_ref).astype(BF16)
    k = _nt(wqkv_ref[D_QKV:2 * D_QKV, :], hb)
    kn = group_norm(k, kmul_ref, kadd_ref)
    kf_ref[...] = kn
    kb_ref[...] = kn.T.astype(BF16)
    v = _nt(wqkv_ref[2 * D_QKV:3 * D_QKV, :], hb)
    vf_ref[...] = v
    vb = v.astype(BF16)
    tq = vt_ref.shape[2]
    for i in range(tm // tq):
        vt_ref[i] = vb[:, i * tq:(i + 1) * tq]
    z = _nt(wf_ref[...], hb)[0:8, :] + wide(bf_ref)
    lf_ref[...] = _log_sigmoid(z)


def _inproj_t(x, gmix, wqkv, wf, bf_col, qmul, qadd, kmul, kadd, tm, tq, layer, depth, kv_all):
    b, t, _ = x.shape
    col = _const_spec((D_QKV, LANES))
    feat = lambda rows: pl.BlockSpec((None, rows, tm), lambda bi, i: (bi, 0, i))
    layered = pl.BlockSpec((None, None, D_QKV, tm), lambda bi, i: (layer, bi, 0, i))
    operands = [x, gmix, wqkv, wf, bf_col, qmul, qadd, kmul, kadd]
    in_specs = [pl.BlockSpec((None, tm, D_MODEL), lambda bi, i: (bi, i, 0)), _const_spec((1, D_MODEL)),
                _const_spec((3 * D_QKV, D_MODEL)), _WF_SPEC, _const_spec((8, LANES)),
                col, col, col, col]
    aliases = {}
    if kv_all is not None:
        aliases = {len(operands): 3, len(operands) + 1: 4}
        operands += list(kv_all)
        in_specs += [pl.BlockSpec(memory_space=pl.ANY)] * 2
    return pl.pallas_call(
        _inproj_t_kernel,
        grid=(b, t // tm),
        in_specs=in_specs,
        out_specs=[feat(D_QKV), pl.BlockSpec((None, tm, D_QKV), lambda bi, i: (bi, i, 0)),
                   pl.BlockSpec((None, tm // tq, D_QKV, tq), lambda bi, i: (bi, i, 0, 0)),
                   layered, layered, feat(8)],
        out_shape=[jax.ShapeDtypeStruct((b, D_QKV, t), BF16), jax.ShapeDtypeStruct((b, t, D_QKV), BF16),
                   jax.ShapeDtypeStruct((b, t // tq, D_QKV, tq), BF16),
                   jax.ShapeDtypeStruct((depth, b, D_QKV, t), F32), jax.ShapeDtypeStruct((depth, b, D_QKV, t), F32),
                   jax.ShapeDtypeStruct((b, 8, t), F32)],
        input_output_aliases=aliases,
        compiler_params=_params("parallel", "parallel"),
        name="inproj_t",
    )(*operands)


def _cumsum_kernel(lf_ref, tri_ref, c_ref, *, t, blk):
    tri = tri_ref[...]
    carry = jnp.zeros((8, 1), F32)
    for i in range(t // blk):
        cs = _nn_split(lf_ref[:, i * blk:(i + 1) * blk], tri, 3) + carry
        c_ref[:, i * blk:(i + 1) * blk] = cs
        carry = cs[:, blk - 1:blk]


def _cumsum(lf_t, tri):
    b, r, t = lf_t.shape
    blk = tri.shape[0]
    return pl.pallas_call(
        functools.partial(_cumsum_kernel, t=t, blk=blk),
        grid=(b,),
        in_specs=[pl.BlockSpec((None, r, t), lambda i: (i, 0, 0)), _const_spec((blk, blk))],
        out_specs=pl.BlockSpec((None, r, t), lambda i: (i, 0, 0)),
        out_shape=jax.ShapeDtypeStruct((b, r, t), F32),
        compiler_params=_params("parallel"),
        name="logf_cumsum",
    )(lf_t, tri)


def _means_kernel(k_ref, o_ref, *, nb):
    lane = lax.broadcasted_iota(jnp.int32, (1, LANES), 1)
    acc = jnp.zeros(o_ref.shape, F32)
    for n in range(nb):
        col = jnp.mean(k_ref[:, n * MOBA_BLOCK:(n + 1) * MOBA_BLOCK], axis=1, keepdims=True)
        acc = jnp.where(lane == n, col, acc)
    o_ref[...] = acc


def _block_means(kf_all, layer):
    _, b, _, t = kf_all.shape
    nb = t // MOBA_BLOCK
    assert nb <= LANES
    return pl.pallas_call(
        functools.partial(_means_kernel, nb=nb),
        grid=(b,),
        in_specs=[pl.BlockSpec((None, None, MIX_WIDTH, t), lambda i: (layer, i, 0, 0))],
        out_specs=pl.BlockSpec((None, MIX_WIDTH, LANES), lambda i: (i, 0, 0)),
        out_shape=jax.ShapeDtypeStruct((b, MIX_WIDTH, LANES), F32),
        compiler_params=_params("parallel"),
        name="moba_block_means",
    )(kf_all)


def _bias_kernel(tab_ref, idxp_ref, idxs_ref, bp_ref, bs_ref, *, dec_seq):
    idxp = idxp_ref[...]
    idxs = idxs_ref[...]
    row = lax.broadcasted_iota(jnp.int32, idxs.shape, 1)
    for h in range(2 * HPM):
        acc = jnp.zeros(idxp.shape, F32)
        for b in range(N_BUCKETS):
            acc = jnp.where(idxp == b, tab_ref[h, b] * LOG2E, acc)
        bp_ref[h] = acc
    for m in range(2):
        acc = jnp.zeros(idxs.shape, F32)
        for hl in range(HPM):
            for b in range(N_BUCKETS):
                acc = jnp.where((idxs == b) & (row // dec_seq == hl), tab_ref[m * HPM + hl, b], acc)
        bs_ref[m] = acc


def _bias_tiles(tab, idxp, idxs, dec_seq):
    return pl.pallas_call(
        functools.partial(_bias_kernel, dec_seq=dec_seq),
        in_specs=[pl.BlockSpec(memory_space=pltpu.SMEM), pl.BlockSpec(memory_space=pltpu.VMEM),
                  pl.BlockSpec(memory_space=pltpu.VMEM)],
        out_specs=[pl.BlockSpec(memory_space=pltpu.VMEM), pl.BlockSpec(memory_space=pltpu.VMEM)],
        out_shape=[jax.ShapeDtypeStruct((2 * HPM,) + idxp.shape, F32),
                   jax.ShapeDtypeStruct((2,) + idxs.shape, F32)],
        compiler_params=pltpu.CompilerParams(vmem_limit_bytes=VMEM_LIMIT),
        name="t5_bias_tiles",
    )(tab, idxp, idxs)


ONES_ROWS = 16
ACC_ROWS = HEAD_DIM + ONES_ROWS
EXP2_DEAD = -200.0


def _rows(n):
    return lax.broadcasted_iota(jnp.int32, (n, 1), 0)


def _key_before_query(tk, tq, strict):
    s = lax.broadcasted_iota(jnp.int32, (tk, tq), 0)
    t = lax.broadcasted_iota(jnp.int32, (tk, tq), 1)
    return (s < t) if strict else (s <= t)


def _row_range(qt, lo, hi):
    r = _rows(qt.shape[0])
    return jnp.where((r >= lo) & (r < hi), qt, jnp.zeros_like(qt))


def _v_aug(vt, x):
    return jnp.concatenate([vt[x * HEAD_DIM:(x + 1) * HEAD_DIM, :],
                            jnp.ones((ONES_ROWS, vt.shape[1]), BF16)], axis=0)


def _extra_rows(by_row, n, width):
    r = _rows(n)
    out = jnp.zeros((n, width), F32)
    for row, val in by_row.items():
        out = jnp.where(r == row, val, out)
    return out


def _pieces(by_row_f32, base):
    out = {}
    for i, val in enumerate(by_row_f32):
        for j, piece in enumerate(_split3(val)):
            out[base + 3 * i + j] = piece.astype(F32)
    return out


def _softmax_step(i, s, va, m_sc, acc_sc, sel=None):
    m_old = m_sc[i]
    bm = jnp.max(s, axis=0, keepdims=True)
    if sel is not None:
        bm = jnp.where(sel, bm, NEG)
    m_new = jnp.maximum(m_old, bm)
    m_sub = m_new if sel is None else jnp.where(sel, m_new, -NEG)
    p = jnp.exp2(s - m_sub)
    acc_sc[i] = acc_sc[i] * jnp.exp2(m_old - m_new) + _nn(va, p.astype(BF16))
    m_sc[i] = m_new


def _normalized(acc_sc, i):
    a = acc_sc[i]
    return a[0:HEAD_DIM, :] / a[HEAD_DIM:HEAD_DIM + 1, :]


def _init(m_sc, acc_sc):
    m_sc[...] = jnp.full(m_sc.shape, NEG, F32)
    acc_sc[...] = jnp.zeros(acc_sc.shape, F32)


def _sweep(qi, group, near, far, alive=None, pin=None):
    nf = jnp.maximum(qi - 1, 0)

    def rest(pinned):
        @pl.when(qi == 0)
        def _():
            near(1, pinned)

        @pl.when(qi > 0)
        def _():
            near(2, pinned)

        if alive is None:
            def body(i, c):
                top = nf - 1 - group * i
                far([top - g for g in range(group)], pinned)
                return c

            lax.fori_loop(0, nf // group, body, 0)
            ok = None
        else:
            def body(c):
                i, _ = c
                top = nf - 1 - group * i
                far([top - g for g in range(group)], pinned)
                return i + 1, alive(top - group + 1)

            _, ok = lax.while_loop(lambda c: (c[0] < nf // group) & c[1], body, (0, alive(nf)))
        rem = nf % group
        size = group // 2
        while size >= 1:
            top = (rem & (2 * size - 1)) - 1
            cond = (rem & size) != 0
            if ok is not None:
                cond = cond & ok

            @pl.when(cond)
            def _(size=size, top=top):
                far([top - g for g in range(size)], pinned)

            if ok is not None and size > 1:
                ok = ok & alive(rem & (size - 1))
            size //= 2

    if pin is None:
        rest(False)
    else:
        pinned = pin()

        @pl.when(pinned)
        def _():
            rest(True)

        @pl.when(jnp.logical_not(pinned))
        def _():
            rest(False)


PIN_HEADROOM = 100.0
PIN_MAX_GAP = 220.0


def _pin_reference(bounds, lows, m_sc):
    gap = bounds[0] - lows[0]
    for i in range(1, len(bounds)):
        gap = jnp.maximum(gap, bounds[i] - lows[i])
    pinned = jnp.max(gap) <= PIN_MAX_GAP

    @pl.when(pinned)
    def _():
        for i, (bound, low) in enumerate(zip(bounds, lows)):
            m_sc[i] = jnp.maximum(low, bound - PIN_HEADROOM)

    return pinned


def _own_key_scores(qt_ref, kd_ref, rows_per_map):
    prod = qt_ref[...].astype(F32) * kd_ref[...].astype(BF16).astype(F32)
    tq = prod.shape[1]
    z = jnp.sum(prod.reshape(MIX_WIDTH // rows_per_map, rows_per_map, tq), axis=1, keepdims=True)
    return z - 0.01


def _softmax_blocks(items, scores, v_of, m_sc, acc_sc, tq, pinned=False):
    ss = [scores(j, kind, i) for (j, kind, i, x, sel) in items]
    if not pinned:
        for s, (j, kind, i, x, sel) in zip(ss, items):
            if kind == "diag":
                s = jnp.where(_key_before_query(tq, tq, False), s, NEG)
            _softmax_step(i, s, _v_aug(v_of(j), x), m_sc, acc_sc, sel(j, x) if sel is not None else None)
        return
    pvs = {}
    for s, (j, kind, i, x, sel) in zip(ss, items):
        if kind == "diag":
            s = jnp.where(_key_before_query(tq, tq, False), s, NEG)
        ref = m_sc[i]
        if sel is not None:
            ref = jnp.where(sel(j, x), ref, -NEG)
        pv = _nn(_v_aug(v_of(j), x), jnp.exp2(s - ref).astype(BF16))
        pvs[i] = pv if i not in pvs else pvs[i] + pv
    for i, pv in pvs.items():
        acc_sc[i] = acc_sc[i] + pv


def _max_key_norms(k_ref, kn_sc):
    ksq = jnp.square(k_ref[...].astype(F32))
    lane = lax.broadcasted_iota(jnp.int32, (1, MIX_WIDTH), 1)
    for h in range(HPM):
        n2 = jnp.sum(jnp.where(lane // HEAD_DIM == h, ksq, 0.0), axis=1, keepdims=True)
        kn_sc[h] = jnp.sqrt(jnp.max(n2, axis=0, keepdims=True))


def _score_bound(q, kn, bias_max):
    qn = jnp.sqrt(jnp.sum(jnp.square(q.astype(F32)), axis=0, keepdims=True))
    return qn * kn * 1.001 + (bias_max + 0.01)


def _table_max(tab_ref, h):
    m = tab_ref[h, 0]
    for b in range(1, N_BUCKETS):
        m = jnp.maximum(m, tab_ref[h, b])
    return m * LOG2E


def _fox_kernel(qt_ref, k_ref, vt_ref, kd_ref, cq_ref, call_ref, cend_ref, o_ref, kx_sc, kn_sc, m_sc, acc_sc,
                *, tq):
    qi = pl.program_id(1)

    def head_row(c, h):
        return c[h:h + 1, :]

    @pl.when(qi == 0)
    def _():
        c = call_ref[...] * LOG2E
        for p in range(HPM // 2):
            by_row = _pieces([head_row(c, 2 * p), head_row(c, 2 * p + 1)], 0)
            by_row.update({6: 1.0, 7: 1.0, 8: 1.0})
            e = _extra_rows(by_row, 16, c.shape[1])
            e = jnp.concatenate([e, jnp.zeros((PAIR - 16, c.shape[1]), F32)], axis=0)
            kx_sc[p] = e.T.astype(BF16)
        _max_key_norms(k_ref, kn_sc)

    cq = cq_ref[...] * LOG2E
    qps, cts, zbs = [], [], []
    for h in range(HPM):
        p, x = divmod(h, 2)
        ct = head_row(cq, h)
        by_row = {3 * x: -1.0, 3 * x + 1: -1.0, 3 * x + 2: -1.0}
        by_row.update(_pieces([ct], 6))
        qx = _extra_rows(by_row, PAIR, tq).astype(BF16)
        qh = _row_range(qt_ref[p * PAIR:(p + 1) * PAIR, :], x * HEAD_DIM, (x + 1) * HEAD_DIM)
        qps.append(jnp.concatenate([qh, qx], axis=0))
        cts.append(ct)
        qn = jnp.sqrt(jnp.sum(jnp.square(qh.astype(F32)), axis=0, keepdims=True))
        zbs.append(qn * kn_sc[h] * 1.001 + 0.001)
    _init(m_sc, acc_sc)

    def alive(low):
        lane = lax.broadcasted_iota(jnp.int32, cend_ref.shape, 1)
        cl = jnp.sum(jnp.where(lane == low - 1, cend_ref[...] * LOG2E, 0.0), axis=1, keepdims=True)
        worst = zbs[0] + cts[0] - head_row(cl, 0) - m_sc[0]
        for h in range(1, HPM):
            worst = jnp.maximum(worst, zbs[h] + cts[h] - head_row(cl, h) - m_sc[h])
        return jnp.max(worst) > EXP2_DEAD

    def scores(j, kind, h):
        r0 = pl.multiple_of(j * tq, tq)
        p = h // 2
        kp = jnp.concatenate([k_ref[pl.ds(r0, tq), p * PAIR:(p + 1) * PAIR], kx_sc[p, pl.ds(r0, tq), :]], axis=1)
        return _nn(kp, qps[h])

    def run(blocks, pinned=False):
        items = [(j, kind, h, h, None) for j, kind in blocks for h in range(HPM)]
        _softmax_blocks(items, scores, lambda j: vt_ref[j], m_sc, acc_sc, tq, pinned)

    lows = _own_key_scores(qt_ref, kd_ref, HEAD_DIM)
    _sweep(qi, 4,
           lambda n, pinned: run([(qi, "diag")] + ([(qi - 1, "far")] if n == 2 else []), pinned),
           lambda js, pinned: run([(j, "far") for j in js], pinned), alive,
           lambda: _pin_reference([zb + 0.01 for zb in zbs], [lows[h] for h in range(HPM)], m_sc))
    o = jnp.concatenate([_normalized(acc_sc, h) for h in range(HPM)], axis=0)
    o_ref[...] = o.T.astype(o_ref.dtype)


def _moba_kernel(tab_ref, qt_ref, k_ref, vt_ref, kd_ref, mean_ref, bt_ref, o_ref, m_sc, acc_sc, sel_sc, kn_sc,
                 *, tq, nb):
    qi = pl.program_id(1)

    @pl.when(qi == 0)
    def _():
        _max_key_norms(k_ref, kn_sc)

    ones_k = jnp.ones((tq, PAIR), BF16)
    nbp = sel_sc.shape[1]
    rb = _rows(nbp)
    qs, qps, bounds = [], [], []
    means = [_split2(mean_ref[p * PAIR:(p + 1) * PAIR, :].T) for p in range(HPM // 2)]
    for h in range(HPM):
        p, x = divmod(h, 2)
        qx = _row_range(qt_ref[p * PAIR:(p + 1) * PAIR, :], x * HEAD_DIM, (x + 1) * HEAD_DIM)
        qs.append(qx)
        bounds.append(_score_bound(qx, kn_sc[h], _table_max(tab_ref, h)))
        far = jnp.full((1, tq), tab_ref[h, N_BUCKETS - 1] * LOG2E, F32)
        qps.append(jnp.concatenate([qx, _extra_rows(_pieces([far], 0), PAIR, tq).astype(BF16)], axis=0))
    gates = [(_nn(means[h // 2][0], qs[h]) + _nn(means[h // 2][1], qs[h]))[0:nbp, :] for h in range(HPM)]
    for h, gate in enumerate(gates):
        cnt = jnp.zeros((nbp, tq), jnp.int32)
        for n in range(nb):
            gn = gate[n:n + 1, :]
            beats = (gn > gate) | ((gn == gate) & (n < rb))
            cnt = cnt + jnp.where(beats, jnp.where(n < qi, 1, 0), 0)
        sel_sc[h] = jnp.where((cnt < MOBA_TOPK) & (rb < qi), 1.0, 0.0)
    _init(m_sc, acc_sc)

    def scores(j, kind, h):
        p = h // 2
        k = k_ref[pl.ds(pl.multiple_of(j * tq, tq), tq), p * PAIR:(p + 1) * PAIR]
        if kind == "far":
            return _nn(jnp.concatenate([k, ones_k], axis=1), qps[h])
        return _nn(k, qs[h]) + bt_ref[h, 0 if kind == "diag" else 1]

    def run(blocks, pinned=False):
        items = [(j, kind, h, h, None if kind == "diag" else (lambda j, h: sel_sc[h, pl.ds(j, 1), :] > 0.0))
                 for j, kind in blocks for h in range(HPM)]
        _softmax_blocks(items, scores, lambda j: vt_ref[j], m_sc, acc_sc, tq, pinned)

    own = _own_key_scores(qt_ref, kd_ref, HEAD_DIM)
    lows = [own[h] + tab_ref[h, 0] * LOG2E for h in range(HPM)]
    _sweep(qi, 4,
           lambda n, pinned: run([(qi, "diag")] + ([(qi - 1, "sub")] if n == 2 else []), pinned),
           lambda js, pinned: run([(j, "far") for j in js], pinned),
           pin=lambda: _pin_reference(bounds, lows, m_sc))
    o = jnp.concatenate([_normalized(acc_sc, h) for h in range(HPM)], axis=0)
    o_ref[...] = o.T.astype(o_ref.dtype)


def _sb_kernel(qt_ref, k_ref, vt_ref, su_ref, o_ref, r_sc, acc_sc, *, tq):
    qi = pl.program_id(1)
    qs = [_row_range(qt_ref[(h // 2) * PAIR:(h // 2 + 1) * PAIR, :], (h % 2) * HEAD_DIM, (h % 2 + 1) * HEAD_DIM)
          for h in range(HPM)]
    su = su_ref[...]
    r_sc[...] = jnp.zeros(r_sc.shape, F32)
    acc_sc[...] = jnp.zeros(acc_sc.shape, F32)

    def run(blocks):
        tiles = [(j, masked, h) for j, masked in blocks for h in range(HPM)]
        zs = [_nn(k_ref[pl.ds(pl.multiple_of(j * tq, tq), tq), (h // 2) * PAIR:(h // 2 + 1) * PAIR], qs[h])
              for j, masked, h in tiles]
        lbs, lats = [], []
        for z, (j, masked, h) in zip(zs, tiles):
            log_beta = jnp.minimum(z, 0.0) - jnp.log2(1.0 + jnp.exp2(-jnp.abs(z)))
            log_keep = log_beta - z
            if masked:
                log_keep = jnp.where(_key_before_query(tq, tq, True), log_keep, 0.0)
            lbs.append(log_beta)
            lats.append(_nn(su, log_keep.astype(BF16)))
        r = [r_sc[h] for h in range(HPM)]
        pvs = [jnp.zeros((HEAD_DIM, tq), F32)] * HPM
        for log_beta, lat, (j, masked, h) in zip(lbs, lats, tiles):
            a = jnp.exp2(log_beta + lat[0:tq, :] + r[h])
            if masked:
                a = jnp.where(_key_before_query(tq, tq, True), a, 0.0)
            r[h] = r[h] + lat[tq:tq + 1, :]
            pvs[h] = pvs[h] + _nn(vt_ref[j][h * HEAD_DIM:(h + 1) * HEAD_DIM, :], a.astype(BF16))
        for h in range(HPM):
            r_sc[h] = r[h]
            acc_sc[h] = acc_sc[h] + pvs[h]

    def alive(low):
        return jnp.max(r_sc[...]) > EXP2_DEAD

    _sweep(qi, 2,
           lambda n, pinned: run([(qi, True)] + ([(qi - 1, False)] if n == 2 else [])),
           lambda js, pinned: run([(j, False) for j in js]), alive)
    o = jnp.concatenate([acc_sc[h] for h in range(HPM)], axis=0)
    o_ref[...] = o.T.astype(o_ref.dtype)


def _lam_value(lam_ref, lam_init):
    lm = lam_ref[...]
    a = jnp.sum(lm[0:1] * lm[1:2], axis=-1, keepdims=True)
    b = jnp.sum(lm[2:3] * lm[3:4], axis=-1, keepdims=True)
    return jnp.exp(a) - jnp.exp(b) + lam_init


def _diff_kernel(tab_ref, qt_ref, k_ref, vt_ref, kd_ref, bt_ref, lam_ref, gsub_ref, o_ref, m_sc, acc_sc, kn_sc,
                 *, tq, lam_init):
    qi = pl.program_id(1)

    @pl.when(qi == 0)
    def _():
        _max_key_norms(k_ref, kn_sc)

    ones_k = jnp.ones((tq, PAIR), BF16)
    qs, qps, bounds = [], [], []
    for i in range(2 * HPM):
        p, r = divmod(i * DIFF_HALF, PAIR)
        qx = _row_range(qt_ref[p * PAIR:(p + 1) * PAIR, :], r, r + DIFF_HALF)
        qs.append(qx)
        bounds.append(_score_bound(qx, kn_sc[i // 2], _table_max(tab_ref, HPM + i // 2)))
        far = jnp.full((1, tq), tab_ref[HPM + i // 2, N_BUCKETS - 1] * LOG2E, F32)
        qps.append(jnp.concatenate([qx, _extra_rows(_pieces([far], 0), PAIR, tq).astype(BF16)], axis=0))
    _init(m_sc, acc_sc)

    def scores(j, kind, i):
        p = i // 4
        k = k_ref[pl.ds(pl.multiple_of(j * tq, tq), tq), p * PAIR:(p + 1) * PAIR]
        if kind == "far":
            return _nn(jnp.concatenate([k, ones_k], axis=1), qps[i])
        return _nn(k, qs[i]) + bt_ref[i // 2, 0 if kind == "diag" else 1]

    def run(blocks, pinned=False):
        items = [(j, kind, i, i // 2, None) for j, kind in blocks for i in range(2 * HPM)]
        _softmax_blocks(items, scores, lambda j: vt_ref[j], m_sc, acc_sc, tq, pinned)

    own = _own_key_scores(qt_ref, kd_ref, DIFF_HALF)
    lows = [own[i] + tab_ref[HPM + i // 2, 0] * LOG2E for i in range(2 * HPM)]
    _sweep(qi, 2,
           lambda n, pinned: run([(qi, "diag")] + ([(qi - 1, "sub")] if n == 2 else []), pinned),
           lambda js, pinned: run([(j, "far") for j in js], pinned),
           pin=lambda: _pin_reference(bounds, lows, m_sc))
    lam_val = _lam_value(lam_ref, lam_init)
    gcol = jnp.concatenate([gsub_ref[...]] * (tq // LANES), axis=1)
    outs = []
    for x in range(HPM):
        o = _normalized(acc_sc, 2 * x) - lam_val * _normalized(acc_sc, 2 * x + 1)
        inv = lax.rsqrt(jnp.mean(o * o, axis=0, keepdims=True) + EPS)
        outs.append(((o * inv) * gcol) * (1.0 - lam_init))
    o_ref[...] = jnp.concatenate(outs, axis=0).T.astype(o_ref.dtype)


def _mixer_specs(t, tq, mixer, layer):
    nk = t // tq
    qspec = pl.BlockSpec((None, MIX_WIDTH, tq), lambda b, qi: (b, mixer, qi))
    kspec = pl.BlockSpec((None, t, MIX_WIDTH), lambda b, qi: (b, 0, mixer))
    vspec = pl.BlockSpec((None, nk, MIX_WIDTH, tq), lambda b, qi: (b, 0, mixer, 0))
    kdspec = pl.BlockSpec((None, None, MIX_WIDTH, tq), lambda b, qi: (layer, b, mixer, qi))
    ospec = pl.BlockSpec((None, tq, MIX_WIDTH), lambda b, qi: (b, qi, 0))
    return [qspec, kspec, vspec, kdspec], ospec


def _attn_out(b, t):
    return jax.ShapeDtypeStruct((b, t, MIX_WIDTH), BF16)


def _stat_scratch(n_maps, tq, rows):
    return [pltpu.VMEM((n_maps, 1, tq), F32), pltpu.VMEM((n_maps, rows, tq), F32)]


def _fox_prompt(qkv, layer, c, tq):
    b, t, _ = qkv[1].shape
    specs, ospec = _mixer_specs(t, tq, 1, layer)
    cq = pl.BlockSpec((None, 8, tq), lambda bi, qi: (bi, 0, qi))
    call = pl.BlockSpec((None, 8, t), lambda bi, qi: (bi, 0, 0))
    cend = c[:, :, tq - 1::tq]
    cends = pl.BlockSpec((None, 8, t // tq), lambda bi, qi: (bi, 0, 0))
    return pl.pallas_call(
        functools.partial(_fox_kernel, tq=tq),
        grid=(b, t // tq),
        in_specs=specs + [cq, call, cends],
        out_specs=ospec,
        out_shape=_attn_out(b, t),
        scratch_shapes=[pltpu.VMEM((HPM // 2, t, PAIR), BF16), pltpu.VMEM((HPM, 1, 1), F32)]
        + _stat_scratch(HPM, tq, ACC_ROWS),
        compiler_params=_params("parallel", "arbitrary"),
        name="fox_prompt",
    )(*qkv, c, c, cend)


def _moba_prompt(tab, qkv, layer, means, bp, tq):
    b, t, _ = qkv[1].shape
    nb = t // MOBA_BLOCK
    nbp = -(-nb // 8) * 8
    specs, ospec = _mixer_specs(t, tq, 0, layer)
    mspec = pl.BlockSpec((None, MIX_WIDTH, LANES), lambda bi, qi: (bi, 0, 0))
    bt = pl.BlockSpec((HPM, 2, tq, tq), lambda bi, qi: (0, 0, 0, 0))
    return pl.pallas_call(
        functools.partial(_moba_kernel, tq=tq, nb=nb),
        grid=(b, t // tq),
        in_specs=[pl.BlockSpec(memory_space=pltpu.SMEM)] + specs + [mspec, bt],
        out_specs=ospec,
        out_shape=_attn_out(b, t),
        scratch_shapes=_stat_scratch(HPM, tq, ACC_ROWS) + [pltpu.VMEM((HPM, nbp, tq), F32),
                                                          pltpu.VMEM((HPM, 1, 1), F32)],
        compiler_params=_params("parallel", "arbitrary"),
        name="moba_prompt",
    )(tab, *qkv, means, bp)


def _sb_prompt(qkv, layer, su, tq):
    b, t, _ = qkv[1].shape
    specs, ospec = _mixer_specs(t, tq, 2, layer)
    return pl.pallas_call(
        functools.partial(_sb_kernel, tq=tq),
        grid=(b, t // tq),
        in_specs=specs[:3] + [_const_spec(su.shape)],
        out_specs=ospec,
        out_shape=_attn_out(b, t),
        scratch_shapes=_stat_scratch(HPM, tq, HEAD_DIM),
        compiler_params=_params("parallel", "arbitrary"),
        name="sb_prompt",
    )(*qkv[:3], su)


def _diff_prompt(tab, qkv, layer, bp, lam, gsub_col, lam_init, tq):
    b, t, _ = qkv[1].shape
    specs, ospec = _mixer_specs(t, tq, 3, layer)
    bt = pl.BlockSpec((HPM, 2, tq, tq), lambda bi, qi: (1, 0, 0, 0))
    return pl.pallas_call(
        functools.partial(_diff_kernel, tq=tq, lam_init=lam_init),
        grid=(b, t // tq),
        in_specs=[pl.BlockSpec(memory_space=pltpu.SMEM)] + specs
        + [bt, _const_spec((4, DIFF_HALF)), _const_spec((HEAD_DIM, LANES))],
        out_specs=ospec,
        out_shape=_attn_out(b, t),
        scratch_shapes=_stat_scratch(2 * HPM, tq, ACC_ROWS) + [pltpu.VMEM((HPM, 1, 1), F32)],
        compiler_params=_params("parallel", "arbitrary"),
        name="diff_prompt",
    )(tab, *qkv, bp, lam, gsub_col)


N_SLOTS = 20


def _sample_masks(dec_seq):
    rows = N_SLOTS * dec_seq
    qmask = np.zeros((rows, D_QKV), np.float32)
    for r in range(rows):
        slot = r // dec_seq
        if slot < 3 * HPM:
            qmask[r, slot * HEAD_DIM:(slot + 1) * HEAD_DIM] = 1.0
        elif slot < 4 * HPM:
            qmask[r, slot * HEAD_DIM:slot * HEAD_DIM + DIFF_HALF] = 1.0
        else:
            c0 = (slot - HPM) * HEAD_DIM + DIFF_HALF
            qmask[r, c0:c0 + DIFF_HALF] = 1.0
    omask = np.zeros((N_HEADS * dec_seq, D_QKV), np.float32)
    for r in range(N_HEADS * dec_seq):
        omask[r, (r // dec_seq) * HEAD_DIM:(r // dec_seq + 1) * HEAD_DIM] = 1.0
    return qmask, omask


def _expand_rows(a, dec_seq):
    rows = HPM * dec_seq
    r = lax.broadcasted_iota(jnp.int32, (rows, 1), 0)
    out = jnp.broadcast_to(a[HPM - 1:HPM, :], (rows, a.shape[1]))
    for h in range(HPM - 2, -1, -1):
        out = jnp.where(r < (h + 1) * dec_seq, a[h:h + 1, :], out)
    return out


def _softmax_pages(s_pages):
    m = s_pages[0].max(axis=-1, keepdims=True)
    for s in s_pages[1:]:
        m = jnp.maximum(m, s.max(axis=-1, keepdims=True))
    ps = [jnp.exp(s - m) for s in s_pages]
    l = ps[0].sum(axis=-1, keepdims=True)
    for p in ps[1:]:
        l = l + p.sum(axis=-1, keepdims=True)
    return [p / l for p in ps]


def _sample_kernel(*refs, n_pages, dec_seq, lam_init):
    pt_ref = refs[0]
    del pt_ref
    (q_ref, kn_ref, vn_ref, lfn_ref, bs_ref, qmask_ref, omask_ref, sl_ref, lam_ref,
     gd1_ref, gd2_ref, gmul_ref, gadd_ref) = refs[1:14]
    k_refs = refs[14:14 + n_pages]
    v_refs = refs[14 + n_pages:14 + 2 * n_pages]
    lf_refs = refs[14 + 2 * n_pages:14 + 3 * n_pages]
    o_ref = refs[14 + 3 * n_pages]
    s_sc, p_sc = refs[14 + 3 * n_pages + 1:]

    g = HPM * dec_seq
    npg = n_pages + 1
    q = q_ref[...]
    qbd = (jnp.concatenate([q] * (N_SLOTS * dec_seq // 8), axis=0) * qmask_ref[...]).astype(BF16)
    pad = jnp.zeros((LANES - kn_ref.shape[0], D_QKV), BF16)
    for p in range(n_pages):
        s_sc[p] = _nn(qbd, k_refs[p][...].astype(BF16))
    s_sc[n_pages] = _nt(qbd, jnp.concatenate([kn_ref[...], pad], axis=0))

    lane = lax.broadcasted_iota(jnp.int32, (1, LANES), 1)
    qrow = lax.broadcasted_iota(jnp.int32, (g, 1), 0) % dec_seq
    new_ok = lane <= qrow
    new_past = lane < qrow
    sl = sl_ref[...]

    def pages(r0):
        return [s_sc[p, r0:r0 + g, :] for p in range(npg)]

    sm = pages(0)
    ppb = MOBA_BLOCK // LANES
    nblk = n_pages // ppb
    gates = []
    for n in range(nblk):
        tot = sm[n * ppb]
        for i in range(1, ppb):
            tot = tot + sm[n * ppb + i]
        gates.append(jnp.sum(tot, axis=-1, keepdims=True))
    s_pages = []
    for n in range(nblk):
        cnt = jnp.zeros((g, 1), jnp.int32)
        for n2 in range(nblk):
            if n2 == n:
                continue
            beats = (gates[n2] > gates[n]) | ((gates[n2] == gates[n]) & (n2 < n))
            cnt = cnt + jnp.where(beats, 1, 0)
        for i in range(ppb):
            p = n * ppb + i
            s_pages.append(jnp.where(cnt < MOBA_TOPK, sm[p] + bs_ref[0, p], NEG))
    s_pages.append(jnp.where(new_ok, sm[n_pages] + bs_ref[0, n_pages], NEG))
    for p, pr in enumerate(_softmax_pages(s_pages)):
        p_sc[p, 0:g, :] = pr.astype(BF16)

    lfn = lfn_ref[...]
    cn_row = jnp.zeros((g, 1), F32)
    bias_new = jnp.zeros((g, LANES), F32)
    for n in range(dec_seq):
        col = lfn[:, n:n + 1]
        cn_row = cn_row + jnp.where(n <= qrow, col, 0.0)
        bias_new = bias_new + jnp.where((lane < n) & (n <= qrow), col, 0.0)
    lfe = [_expand_rows(lf_refs[p][...], dec_seq) for p in range(n_pages)]
    suffix = _nn_split(jnp.concatenate(lfe, axis=0), sl, 3)
    sf = pages(g)
    s_pages = [None] * npg
    s_pages[n_pages] = jnp.where(new_ok, sf[n_pages] + bias_new, NEG)
    after = cn_row
    for p in range(n_pages - 1, -1, -1):
        s_pages[p] = sf[p] + (suffix[p * g:(p + 1) * g, :] + after)
        after = after + jnp.sum(lfe[p], axis=-1, keepdims=True)
    for p, pr in enumerate(_softmax_pages(s_pages)):
        p_sc[p, g:2 * g, :] = pr.astype(BF16)

    ss = pages(2 * g)
    lbs, lks = [], []
    for p in range(npg):
        lb = _log_sigmoid(ss[p])
        lk = lb - ss[p]
        if p == n_pages:
            lk = jnp.where(new_past, lk, 0.0)
        lbs.append(lb)
        lks.append(lk)
    later_in = _nn_split(jnp.concatenate(lks, axis=0), sl, 2)
    after = jnp.zeros((g, 1), F32)
    for p in range(npg - 1, -1, -1):
        a = jnp.exp(lbs[p] + later_in[p * g:(p + 1) * g, :] + after)
        if p == n_pages:
            a = jnp.where(new_past, a, 0.0)
        p_sc[p, 2 * g:3 * g, :] = a.astype(BF16)
        after = after + jnp.sum(lks[p], axis=-1, keepdims=True)

    lam_val = _lam_value(lam_ref, lam_init)
    maps = []
    for mp in range(2):
        sd = pages((3 + mp) * g)
        s_pages = [sd[p] + bs_ref[1, p] for p in range(n_pages)]
        s_pages.append(jnp.where(new_ok, sd[n_pages] + bs_ref[1, n_pages], NEG))
        maps.append(_softmax_pages(s_pages))
    for p in range(npg):
        p_sc[p, 3 * g:4 * g, :] = (maps[0][p] - lam_val * maps[1][p]).astype(BF16)

    padv = jnp.zeros((LANES - vn_ref.shape[0], D_QKV), BF16)
    acc = _nn(p_sc[n_pages], jnp.concatenate([vn_ref[...], padv], axis=0))
    for p in range(n_pages):
        acc = acc + _nt(p_sc[p], v_refs[p][...].astype(BF16))
    rm = acc * omask_ref[...]
    y = rm[0:8, :]
    for i in range(1, N_HEADS * dec_seq // 8):
        y = y + rm[8 * i:8 * (i + 1), :]
    o8 = y + pltpu.roll(y, dec_seq, 0)
    ms = _nn_split(o8 * o8, gd1_ref[...], 2)
    bc = _nn_split(lax.rsqrt(ms + EPS), gd2_ref[...], 3)
    o8 = o8 * (bc * gmul_ref[...] + gadd_ref[...])
    o_ref[...] = o8[0:dec_seq, :]


def _sample_attention(layer, page_table, q8, kn, vn, lfn, bs, consts, lam, gmul, gadd,
                      cache_k, cache_v, cache_lft, lam_init):
    db, n_pages = page_table.shape
    dec_seq = lfn.shape[1] // HPM
    page = cache_k.shape[3]
    qmask, omask, sl, gd1, gd2 = consts
    rows = N_SLOTS * dec_seq

    def batch_spec(shape):
        nd = len(shape)
        return pl.BlockSpec((None,) + shape, lambda b, pt: (b,) + (0,) * nd)

    def const(shape):
        nd = len(shape)
        return pl.BlockSpec(shape, lambda b, pt: (0,) * nd)

    def paged(width_rows, width_cols):
        return [pl.BlockSpec((None, None, width_rows, width_cols),
                             functools.partial(lambda b, pt, j: (layer, pt[b, j], 0, 0), j=j))
                for j in range(n_pages)]

    in_specs = ([batch_spec((8, D_QKV)), batch_spec(kn.shape[1:]), batch_spec(vn.shape[1:]),
                 batch_spec(lfn.shape[1:]), const(bs.shape), const(qmask.shape), const(omask.shape),
                 const(sl.shape), const(lam.shape), const(gd1.shape), const(gd2.shape),
                 const(gmul.shape), const(gadd.shape)]
                + paged(D_QKV, page) + paged(D_QKV, page) + paged(HPM, page))
    grid_spec = pltpu.PrefetchScalarGridSpec(
        num_scalar_prefetch=1,
        grid=(db,),
        in_specs=in_specs,
        out_specs=pl.BlockSpec((None, dec_seq, D_QKV), lambda b, pt: (b, 0, 0)),
        scratch_shapes=[pltpu.VMEM((n_pages + 1, rows, LANES), F32),
                        pltpu.VMEM((n_pages + 1, N_HEADS * dec_seq, LANES), BF16)],
    )
    return pl.pallas_call(
        functools.partial(_sample_kernel, n_pages=n_pages, dec_seq=dec_seq, lam_init=lam_init),
        grid_spec=grid_spec,
        out_shape=jax.ShapeDtypeStruct((db, dec_seq, D_QKV), F32),
        compiler_params=_params("arbitrary"),
        name="sample_attention",
    )(page_table, q8, kn, vn, lfn, bs, qmask, omask, sl, lam, gd1, gd2, gmul, gadd,
      *([cache_k] * n_pages), *([cache_v] * n_pages), *([cache_lft] * n_pages))


MLP_CHUNK = 1024


def _merge_mlp_kernel(x_ref, o0_ref, o1_ref, o2_ref, o3_ref, gmix_ref, wg_ref, wbr_ref, wo_ref,
                      gmlp_ref, wup_ref, wdn_ref, y_ref):
    x = x_ref[...]
    h = (x * lax.rsqrt(jnp.mean(x * x, axis=-1, keepdims=True) + EPS)) * gmix_ref[...]
    hb = h.astype(BF16)
    acc = jnp.zeros(x.shape, F32)
    for m, o_ref in enumerate((o0_ref, o1_ref, o2_ref, o3_ref)):
        gate = 1.0 / (1.0 + jnp.exp(-_nt(hb, wg_ref[m * D_MODEL:(m + 1) * D_MODEL, :])))
        acc = acc + gate * _nn(o_ref[...].astype(BF16), wbr_ref[m])
    y = x + _nn(acc.astype(BF16), wo_ref[...])
    hm = ((y * lax.rsqrt(jnp.mean(y * y, axis=-1, keepdims=True) + EPS)) * gmlp_ref[...]).astype(BF16)
    for c in range(D_FF // MLP_CHUNK):
        u = jnp.maximum(_nn(hm, wup_ref[:, c * MLP_CHUNK:(c + 1) * MLP_CHUNK]), 0.0)
        y = y + _nn((u * u).astype(BF16), wdn_ref[c * MLP_CHUNK:(c + 1) * MLP_CHUNK, :])
    y_ref[...] = y


def _resident_spec(shape):
    nd = len(shape)
    return pl.BlockSpec(shape, lambda *_: (0,) * nd, pipeline_mode=pl.Buffered(1))


def _merge_mlp(x, os_, gmix, wg, wbr, wo, gmlp, wup, wdn, tm):
    rows = x.shape[0]
    row = lambda w: pl.BlockSpec((tm, w), lambda i: (i, 0))
    return pl.pallas_call(
        _merge_mlp_kernel,
        grid=(rows // tm,),
        in_specs=[row(D_MODEL)] + [row(MIX_WIDTH)] * 4
        + [_const_spec((1, D_MODEL)), _resident_spec((N_MIXERS * D_MODEL, D_MODEL)),
           _resident_spec((N_MIXERS, MIX_WIDTH, D_MODEL)), _resident_spec((D_MODEL, D_MODEL)),
           _const_spec((1, D_MODEL)), _resident_spec((D_MODEL, D_FF)), _resident_spec((D_FF, D_MODEL))],
        out_specs=row(D_MODEL),
        out_shape=jax.ShapeDtypeStruct((rows, D_MODEL), F32),
        compiler_params=_params("parallel"),
        name="merge_mlp",
    )(x, *os_, gmix, wg, wbr, wo, gmlp, wup, wdn)


def _qk_vectors(gq, gk, qscale):
    z = jnp.zeros((MIX_WIDTH,), F32)
    one = jnp.ones((MIX_WIDTH,), F32)
    t4 = lambda a: jnp.tile(a, HPM)
    sc = HEAD_DIM ** -0.5 * qscale
    qmul = jnp.concatenate([t4(gq[0]) * sc, t4(gq[1]) * sc, z, t4(gq[2]) * (DIFF_HALF ** -0.5 * qscale)])
    qadd = jnp.concatenate([z, z, one * sc, z])
    kmul = jnp.concatenate([t4(gk[0]), t4(gk[1]), z, t4(gk[2])])
    kadd = jnp.concatenate([z, z, one, z])
    return [a.reshape(1, D_QKV) for a in (qmul, qadd, kmul, kadd)]


def _bias_index_tiles(tq, n_pages, page, dec_seq):
    a = np.arange(tq)
    diag = _t5_bucket_np(a[None, :] - a[:, None])
    sub = _t5_bucket_np(tq + a[None, :] - a[:, None])
    idxp = np.stack([diag, sub]).astype(np.int32)
    past_len = n_pages * page
    qpos = past_len + (np.arange(HPM * dec_seq) % dec_seq)
    kpos = np.concatenate([np.arange(past_len), past_len + np.arange(page)])
    idxs = _t5_bucket_np(qpos[None, :, None] - kpos.reshape(n_pages + 1, 1, page))
    return idxp, idxs.astype(np.int32)


def _layer(l, x, past, w, consts, tq, tm, kv_all=None):
    (tab, bp, bs, g1, g2, su_q, tri_c, sl_p, sample_consts, gd) = consts
    (w_in, b_forget, g_q, g_k, lam, g_sub, w_branch, w_o, g_mix, g_mlp, w_up, w_down) = w
    b, t, _ = x.shape
    rows = b * t
    lam_init = 0.8 - 0.6 * math.exp(-0.3 * l)
    wt = w_in[l].T.astype(BF16)
    wqkv = wf = wt
    wg = wt[3 * D_QKV + HPM:]
    gmix = g_mix[l].reshape(1, D_MODEL)
    qk_vecs = _qk_vectors(g_q[l], g_k[l], LOG2E if past is None else 1.0)
    x2 = x.reshape(rows, D_MODEL)

    if past is None:
        cols = [jnp.broadcast_to(a.reshape(D_QKV, 1), (D_QKV, LANES)) for a in qk_vecs]
        bf_col = jnp.broadcast_to(jnp.pad(b_forget[l], (0, 8 - HPM))[:, None], (8, LANES))
        qt, kb3, vt, k_new, v_new, lf_t = _inproj_t(x, gmix, wqkv, wf, bf_col, *cols, tm=tm, tq=tq,
                                                    layer=l, depth=w_in.shape[0], kv_all=kv_all)
        lf_new = jnp.swapaxes(lf_t[:, :HPM], 1, 2)
        c = _cumsum(lf_t, tri_c)
        means = _block_means(k_new, l)
        gsub_col = jnp.broadcast_to(g_sub[l][:, None], (HEAD_DIM, LANES))
        qkv = (qt, kb3, vt, k_new)
        o_moba = _moba_prompt(tab, qkv, l, means, bp, tq)
        o_fox = _fox_prompt(qkv, l, c, tq)
        o_sb = _sb_prompt(qkv, l, su_q, tq)
        o_dif = _diff_prompt(tab, qkv, l, bp, lam[l], gsub_col, lam_init, tq)
        outs = [a.reshape(rows, MIX_WIDTH) for a in (o_moba, o_fox, o_sb, o_dif)]
    else:
        cache_k, cache_v, cache_lft, page_table = past
        bf = jnp.pad(b_forget[l], (0, LANES - HPM)).reshape(1, LANES)
        qb, kb, vb, kf, vf, lf = _inproj(x2, gmix, wqkv, wf, bf, g1, g2, *qk_vecs, tm=min(tm, rows))
        k_new, v_new = kf.reshape(b, t, N_HEADS, HEAD_DIM), vf.reshape(b, t, N_HEADS, HEAD_DIM)
        lf_new = lf.reshape(b, t, HPM)
        q4 = qb.astype(F32).reshape(b, t, D_QKV)
        q8 = jnp.concatenate([q4] * (8 // t), axis=1)
        padn = ((0, 0), (0, 16 - t), (0, 0))
        kn = jnp.pad(kb.reshape(b, t, D_QKV), padn)
        vn = jnp.pad(vb.reshape(b, t, D_QKV), padn)
        lfn = jnp.swapaxes(lf.reshape(b, t, HPM), 1, 2)
        lfn = jnp.pad(jnp.repeat(lfn, t, axis=1), ((0, 0), (0, 0), (0, 8 - t)))
        gd1, gd2 = gd
        z3 = jnp.zeros((3 * MIX_WIDTH,), F32)
        gmul = jnp.concatenate([z3, jnp.tile(g_sub[l], HPM) * (1.0 - lam_init)]).reshape(1, D_QKV)
        gadd = jnp.concatenate([z3 + 1.0, jnp.zeros((MIX_WIDTH,), F32)]).reshape(1, D_QKV)
        o = _sample_attention(l, page_table, q8, kn, vn, lfn, bs, (*sample_consts, sl_p, gd1, gd2),
                              lam[l], gmul, gadd, cache_k, cache_v, cache_lft, lam_init)
        o = o.reshape(rows, D_QKV)
        outs = [o[:, m * MIX_WIDTH:(m + 1) * MIX_WIDTH] for m in range(N_MIXERS)]

    y = _merge_mlp(x2, outs, gmix, wg, w_branch[l].astype(BF16), w_o[l].astype(BF16),
                   g_mlp[l].reshape(1, D_MODEL), w_up[l].astype(BF16), w_down[l].astype(BF16), tm=min(tm, rows))
    return y.reshape(b, t, D_MODEL), k_new, v_new, lf_new


def _constants(rel_bias, tq, n_pages, page, dec_seq):
    idxp, idxs = _bias_index_tiles(tq, n_pages, page, dec_seq)
    tab = rel_bias.T
    bp, bs = _bias_tiles(tab, jnp.asarray(idxp), jnp.asarray(idxs), dec_seq)
    g1, g2 = _qk_group_mats()
    qmask, omask = _sample_masks(dec_seq)
    gd1 = np.zeros((D_QKV, LANES), np.float32)
    gd2 = np.zeros((LANES, D_QKV), np.float32)
    for c in range(3 * MIX_WIDTH, D_QKV):
        gd1[c, (c - 3 * MIX_WIDTH) // HEAD_DIM] = 1.0 / HEAD_DIM
        gd2[(c - 3 * MIX_WIDTH) // HEAD_DIM, c] = 1.0
    su_q = np.concatenate([_strict_lower(tq).T, np.ones((ONES_ROWS, tq), np.float32)], axis=0)
    bf = lambda a: jnp.asarray(a, BF16)
    return (tab, bp, bs, bf(g1), bf(g2), bf(su_q), bf(_upper_incl(tq)), bf(_strict_lower(page)),
            (jnp.asarray(qmask), jnp.asarray(omask)), (bf(gd1), bf(gd2)))


def kernel(x_prompt, x_sample, cache_k, cache_v, cache_logf, page_table, rel_bias, w_in, b_forget,
           g_q, g_k, lam, g_sub, w_branch, w_o, g_mix, g_mlp, w_up, w_down):
    depth = w_in.shape[0]
    tq = MOBA_BLOCK
    tm = 512
    n_pages = page_table.shape[1]
    page = cache_k.shape[2]
    dec_seq = x_sample.shape[1]
    n_phys = cache_k.shape[1]
    assert (n_pages * page) % MOBA_BLOCK == 0 and 8 % dec_seq == 0 and x_prompt.shape[1] % tq == 0

    consts = _constants(rel_bias, tq, n_pages, page, dec_seq)
    w = (w_in, b_forget, g_q, g_k, lam, g_sub, w_branch, w_o, g_mix, g_mlp, w_up, w_down)
    to_pages = lambda c: jnp.transpose(c, (0, 1, 3, 4, 2)).reshape(depth, n_phys, D_QKV, page)
    past = (to_pages(cache_k), to_pages(cache_v), jnp.swapaxes(cache_logf, 2, 3), page_table)

    hp, hs = x_prompt, x_sample
    outs = [[] for _ in range(4)]
    kv_all = None
    for l in range(depth):
        hp, kp, vp, fp = _layer(l, hp, None, w, consts, tq, tm, kv_all)
        kv_all = (kp, vp)
        hs, ks, vs, fs = _layer(l, hs, past, w, consts, tq, tm)
        for lst, a in zip(outs, (fp, ks, vs, fs)):
            lst.append(a)
    b, t = x_prompt.shape[:2]
    to_heads = lambda a: jnp.transpose(a.reshape(depth, b, N_HEADS, HEAD_DIM, t), (0, 1, 4, 2, 3))
    fp, ks, vs, fs = (jnp.stack(a) for a in outs)
    return hp, hs, to_heads(kv_all[0]), to_heads(kv_all[1]), fp, ks, vs, fs
```

```python
import functools
import math

import numpy as np
import jax
import jax.numpy as jnp
from jax import lax
from jax.experimental import pallas as pl
from jax.experimental.pallas import tpu as pltpu

F32 = jnp.float32
BF16 = jnp.bfloat16

D_MODEL = 1024
HEAD_DIM = 64
N_MIXERS = 4
HPM = 4
N_HEADS = 16
MIX_WIDTH = HPM * HEAD_DIM
D_QKV = N_HEADS * HEAD_DIM
DIFF_HALF = HEAD_DIM // 2
D_FF = 4 * D_MODEL
MOBA_BLOCK = 256
MOBA_TOPK = 3
N_BUCKETS = 32
MAX_DISTANCE = 128
EPS = 1e-6
NEG = -1e30
LOG2E = 1.4426950408889634
LANES = 128
PAIR = 2 * HEAD_DIM
VMEM_LIMIT = 56 * 1024 * 1024


def _params(*sem):
    return pltpu.CompilerParams(dimension_semantics=sem, vmem_limit_bytes=VMEM_LIMIT)


def _nt(a, b):
    return lax.dot_general(a, b, (((1,), (1,)), ((), ())), preferred_element_type=F32)


def _nn(a, b):
    return jnp.dot(a, b, preferred_element_type=F32)


def _split2(x):
    hi = x.astype(BF16)
    lo = (x - hi.astype(F32)).astype(BF16)
    return hi, lo


def _split3(x):
    hi = x.astype(BF16)
    r = x - hi.astype(F32)
    mid = r.astype(BF16)
    lo = (r - mid.astype(F32)).astype(BF16)
    return hi, mid, lo


def _nn_split(x, w, parts):
    pieces = _split2(x) if parts == 2 else _split3(x)
    out = _nn(pieces[0], w)
    for p in pieces[1:]:
        out = out + _nn(p, w)
    return out


def _log_sigmoid(z):
    return jnp.minimum(z, 0.0) - jnp.log1p(jnp.exp(-jnp.abs(z)))


def _const_spec(shape):
    nd = len(shape)
    return pl.BlockSpec(shape, lambda *_: (0,) * nd)


def _t5_bucket_np(rel):
    n = np.maximum(rel, 0)
    max_exact = N_BUCKETS // 2
    nf = np.maximum(n, 1).astype(np.float32)
    large = max_exact + (np.log(nf / np.float32(max_exact)) / np.float32(math.log(MAX_DISTANCE / max_exact))
                         * np.float32(N_BUCKETS - max_exact)).astype(np.int32)
    return np.where(n < max_exact, n, np.minimum(large, N_BUCKETS - 1)).astype(np.int32)


def _qk_group_mats():
    g1 = np.zeros((D_QKV, LANES), np.float32)
    g2 = np.zeros((LANES, D_QKV), np.float32)
    for c in range(D_QKV):
        h = c // HEAD_DIM
        if h < 2 * HPM:
            gid, size = h, HEAD_DIM
        elif h < 3 * HPM:
            continue
        else:
            gid, size = 2 * HPM + (c - 3 * MIX_WIDTH) // DIFF_HALF, DIFF_HALF
        g1[c, gid] = 1.0 / size
        g2[gid, c] = 1.0
    return g1, g2


def _strict_lower(n):
    r = np.arange(n)
    return (r[:, None] > r[None, :]).astype(np.float32)


def _upper_incl(n):
    r = np.arange(n)
    return (r[:, None] <= r[None, :]).astype(np.float32)


_WF_SPEC = pl.BlockSpec((LANES, D_MODEL), lambda *_: (3 * D_QKV // LANES, 0))


def _inproj_kernel(x_ref, gmix_ref, wqkv_ref, wf_ref, bf_ref, g1_ref, g2_ref,
                   qmul_ref, qadd_ref, kmul_ref, kadd_ref,
                   qb_ref, kb_ref, vb_ref, kf_ref, vf_ref, lf_ref):
    x = x_ref[...]
    h = (x * lax.rsqrt(jnp.mean(x * x, axis=-1, keepdims=True) + EPS)) * gmix_ref[...]
    hb = h.astype(BF16)
    g1 = g1_ref[...]
    g2 = g2_ref[...]

    def group_norm(a, mul_ref, add_ref):
        ms = _nn_split(a * a, g1, 2)
        inv = lax.rsqrt(ms + EPS)
        bc = _nn_split(inv, g2, 3)
        return a * (bc * mul_ref[...] + add_ref[...])

    q = _nt(hb, wqkv_ref[0:D_QKV, :])
    qb_ref[...] = group_norm(q, qmul_ref, qadd_ref).astype(BF16)
    k = _nt(hb, wqkv_ref[D_QKV:2 * D_QKV, :])
    kn = group_norm(k, kmul_ref, kadd_ref)
    kf_ref[...] = kn
    kb_ref[...] = kn.astype(BF16)
    v = _nt(hb, wqkv_ref[2 * D_QKV:3 * D_QKV, :])
    vf_ref[...] = v
    vb_ref[...] = v.astype(BF16)
    z = _nt(hb, wf_ref[...]) + bf_ref[...]
    lf_ref[...] = _log_sigmoid(z)[:, 0:HPM]


def _inproj(x, gmix, wqkv, wf, bf, g1, g2, qmul, qadd, kmul, kadd, tm):
    rows = x.shape[0]
    row = lambda w: pl.BlockSpec((tm, w), lambda i: (i, 0))
    vec = _const_spec((1, D_QKV))
    return pl.pallas_call(
        _inproj_kernel,
        grid=(rows // tm,),
        in_specs=[row(D_MODEL), _const_spec((1, D_MODEL)), _const_spec((3 * D_QKV, D_MODEL)),
                  _WF_SPEC, _const_spec((1, LANES)),
                  _const_spec((D_QKV, LANES)), _const_spec((LANES, D_QKV)), vec, vec, vec, vec],
        out_specs=[row(D_QKV), row(D_QKV), row(D_QKV), row(D_QKV), row(D_QKV), row(HPM)],
        out_shape=[jax.ShapeDtypeStruct((rows, D_QKV), BF16)] * 3
        + [jax.ShapeDtypeStruct((rows, D_QKV), F32)] * 2 + [jax.ShapeDtypeStruct((rows, HPM), F32)],
        compiler_params=_params("parallel"),
        name="inproj",
    )(x, gmix, wqkv, wf, bf, g1, g2, qmul, qadd, kmul, kadd)


def _inproj_t_kernel(x_ref, gmix_ref, wqkv_ref, wf_ref, bf_ref, qmul_ref, qadd_ref, kmul_ref, kadd_ref, *refs):
    qt_ref, kb_ref, vt_ref, kf_ref, vf_ref, lf_ref, kn2_ref = refs[-7:]
    x = x_ref[...]
    tm = x.shape[0]
    h = (x * lax.rsqrt(jnp.mean(x * x, axis=-1, keepdims=True) + EPS)) * gmix_ref[...]
    hb = h.astype(BF16)
    wide = lambda ref: jnp.concatenate([ref[...]] * (tm // LANES), axis=1)
    n64, n32 = 2 * MIX_WIDTH, MIX_WIDTH

    def group_norm(a, mul_ref, add_ref):
        sq = a * a

        def inv_rms(rows0, rows1, width):
            g = sq[rows0:rows1].reshape((rows1 - rows0) // width, width, tm)
            inv = lax.rsqrt(jnp.mean(g, axis=1, keepdims=True) + EPS)
            return jnp.broadcast_to(inv, g.shape).reshape(rows1 - rows0, tm)

        scale = jnp.concatenate([inv_rms(0, n64, HEAD_DIM), jnp.zeros((D_QKV - n64 - n32, tm), F32),
                                 inv_rms(D_QKV - n32, D_QKV, DIFF_HALF)], axis=0)
        return a * (scale * wide(mul_ref) + wide(add_ref))

    q = _nt(wqkv_ref[0:D_QKV, :], hb)
    qt_ref[...] = group_norm(q, qmul_ref, qadd_ref).astype(BF16)
    k = _nt(wqkv_ref[D_QKV:2 * D_QKV, :], hb)
    kn = group_norm(k, kmul_ref, kadd_ref)
    kf_ref[...] = kn
    kb_ref[...] = kn.T.astype(BF16)
    kq = kn.astype(BF16).astype(F32)
    n2 = jnp.sum((kq * kq).reshape(N_HEADS, HEAD_DIM, tm), axis=1)
    kn2_ref[...] = jnp.broadcast_to(jnp.max(n2, axis=1, keepdims=True), (N_HEADS, LANES))
    v = _nt(wqkv_ref[2 * D_QKV:3 * D_QKV, :], hb)
    vf_ref[...] = v
    vb = v.astype(BF16)
    tq = vt_ref.shape[2]
    for i in range(tm // tq):
        vt_ref[i] = vb[:, i * tq:(i + 1) * tq]
    z = _nt(wf_ref[...], hb)[0:8, :] + wide(bf_ref)
    lf_ref[...] = _log_sigmoid(z)


def _inproj_t(x, gmix, wqkv, wf, bf_col, qmul, qadd, kmul, kadd, tm, tq, layer, depth, kv_all):
    b, t, _ = x.shape
    col = _const_spec((D_QKV, LANES))
    feat = lambda rows: pl.BlockSpec((None, rows, tm), lambda bi, i: (bi, 0, i))
    layered = pl.BlockSpec((None, None, D_QKV, tm), lambda bi, i: (layer, bi, 0, i))
    operands = [x, gmix, wqkv, wf, bf_col, qmul, qadd, kmul, kadd]
    in_specs = [pl.BlockSpec((None, tm, D_MODEL), lambda bi, i: (bi, i, 0)), _const_spec((1, D_MODEL)),
                _const_spec((3 * D_QKV, D_MODEL)), _WF_SPEC, _const_spec((8, LANES)),
                col, col, col, col]
    aliases = {}
    if kv_all is not None:
        aliases = {len(operands): 3, len(operands) + 1: 4}
        operands += list(kv_all)
        in_specs += [pl.BlockSpec(memory_space=pl.ANY)] * 2
    return pl.pallas_call(
        _inproj_t_kernel,
        grid=(b, t // tm),
        in_specs=in_specs,
        out_specs=[feat(D_QKV), pl.BlockSpec((None, tm, D_QKV), lambda bi, i: (bi, i, 0)),
                   pl.BlockSpec((None, tm // tq, D_QKV, tq), lambda bi, i: (bi, i, 0, 0)),
                   layered, layered, feat(8),
                   pl.BlockSpec((None, None, N_HEADS, LANES), lambda bi, i: (bi, i, 0, 0))],
        out_shape=[jax.ShapeDtypeStruct((b, D_QKV, t), BF16), jax.ShapeDtypeStruct((b, t, D_QKV), BF16),
                   jax.ShapeDtypeStruct((b, t // tq, D_QKV, tq), BF16),
                   jax.ShapeDtypeStruct((depth, b, D_QKV, t), F32), jax.ShapeDtypeStruct((depth, b, D_QKV, t), F32),
                   jax.ShapeDtypeStruct((b, 8, t), F32), jax.ShapeDtypeStruct((b, t // tm, N_HEADS, LANES), F32)],
        input_output_aliases=aliases,
        compiler_params=_params("parallel", "parallel"),
        name="inproj_t",
    )(*operands)


def _cumsum_kernel(lf_ref, tri_ref, c_ref, *, t, blk):
    tri = tri_ref[...]
    carry = jnp.zeros((8, 1), F32)
    for i in range(t // blk):
        cs = _nn_split(lf_ref[:, i * blk:(i + 1) * blk], tri, 3) + carry
        c_ref[:, i * blk:(i + 1) * blk] = cs
        carry = cs[:, blk - 1:blk]


def _cumsum(lf_t, tri):
    b, r, t = lf_t.shape
    blk = tri.shape[0]
    return pl.pallas_call(
        functools.partial(_cumsum_kernel, t=t, blk=blk),
        grid=(b,),
        in_specs=[pl.BlockSpec((None, r, t), lambda i: (i, 0, 0)), _const_spec((blk, blk))],
        out_specs=pl.BlockSpec((None, r, t), lambda i: (i, 0, 0)),
        out_shape=jax.ShapeDtypeStruct((b, r, t), F32),
        compiler_params=_params("parallel"),
        name="logf_cumsum",
    )(lf_t, tri)


def _means_kernel(k_ref, o_ref, *, nb):
    lane = lax.broadcasted_iota(jnp.int32, (1, LANES), 1)
    acc = jnp.zeros(o_ref.shape, F32)
    for n in range(nb):
        col = jnp.mean(k_ref[:, n * MOBA_BLOCK:(n + 1) * MOBA_BLOCK], axis=1, keepdims=True)
        acc = jnp.where(lane == n, col, acc)
    o_ref[...] = acc


def _block_means(kf_all, layer):
    _, b, _, t = kf_all.shape
    nb = t // MOBA_BLOCK
    assert nb <= LANES
    return pl.pallas_call(
        functools.partial(_means_kernel, nb=nb),
        grid=(b,),
        in_specs=[pl.BlockSpec((None, None, MIX_WIDTH, t), lambda i: (layer, i, 0, 0))],
        out_specs=pl.BlockSpec((None, MIX_WIDTH, LANES), lambda i: (i, 0, 0)),
        out_shape=jax.ShapeDtypeStruct((b, MIX_WIDTH, LANES), F32),
        compiler_params=_params("parallel"),
        name="moba_block_means",
    )(kf_all)


def _bias_kernel(tab_ref, idxp_ref, idxs_ref, bp_ref, bs_ref, *, dec_seq):
    idxp = idxp_ref[...]
    idxs = idxs_ref[...]
    row = lax.broadcasted_iota(jnp.int32, idxs.shape, 1)
    for h in range(2 * HPM):
        acc = jnp.zeros(idxp.shape, F32)
        for b in range(N_BUCKETS):
            acc = jnp.where(idxp == b, tab_ref[h, b] * LOG2E, acc)
        bp_ref[h] = acc
    for m in range(2):
        acc = jnp.zeros(idxs.shape, F32)
        for hl in range(HPM):
            for b in range(N_BUCKETS):
                acc = jnp.where((idxs == b) & (row // dec_seq == hl), tab_ref[m * HPM + hl, b], acc)
        bs_ref[m] = acc


def _bias_tiles(tab, idxp, idxs, dec_seq):
    return pl.pallas_call(
        functools.partial(_bias_kernel, dec_seq=dec_seq),
        in_specs=[pl.BlockSpec(memory_space=pltpu.SMEM), pl.BlockSpec(memory_space=pltpu.VMEM),
                  pl.BlockSpec(memory_space=pltpu.VMEM)],
        out_specs=[pl.BlockSpec(memory_space=pltpu.VMEM), pl.BlockSpec(memory_space=pltpu.VMEM)],
        out_shape=[jax.ShapeDtypeStruct((2 * HPM,) + idxp.shape, F32),
                   jax.ShapeDtypeStruct((2,) + idxs.shape, F32)],
        compiler_params=pltpu.CompilerParams(vmem_limit_bytes=VMEM_LIMIT),
        name="t5_bias_tiles",
    )(tab, idxp, idxs)


ONES_ROWS = 16
ACC_ROWS = HEAD_DIM + ONES_ROWS
EXP2_DEAD = -200.0


def _rows(n):
    return lax.broadcasted_iota(jnp.int32, (n, 1), 0)


def _key_before_query(tk, tq, strict):
    s = lax.broadcasted_iota(jnp.int32, (tk, tq), 0)
    t = lax.broadcasted_iota(jnp.int32, (tk, tq), 1)
    return (s < t) if strict else (s <= t)


def _row_range(qt, lo, hi):
    r = _rows(qt.shape[0])
    return jnp.where((r >= lo) & (r < hi), qt, jnp.zeros_like(qt))


def _v_aug(vt, x):
    return jnp.concatenate([vt[x * HEAD_DIM:(x + 1) * HEAD_DIM, :],
                            jnp.ones((ONES_ROWS, vt.shape[1]), BF16)], axis=0)


def _extra_rows(by_row, n, width):
    r = _rows(n)
    out = jnp.zeros((n, width), F32)
    for row, val in by_row.items():
        out = jnp.where(r == row, val, out)
    return out


def _pieces(by_row_f32, base):
    out = {}
    for i, val in enumerate(by_row_f32):
        for j, piece in enumerate(_split3(val)):
            out[base + 3 * i + j] = piece.astype(F32)
    return out


def _softmax_step(i, s, va, m_sc, acc_sc, sel=None):
    m_old = m_sc[i]
    bm = jnp.max(s, axis=0, keepdims=True)
    if sel is not None:
        bm = jnp.where(sel, bm, NEG)
    m_new = jnp.maximum(m_old, bm)
    m_sub = m_new if sel is None else jnp.where(sel, m_new, -NEG)
    p = jnp.exp2(s - m_sub)
    acc_sc[i] = acc_sc[i] * jnp.exp2(m_old - m_new) + _nn(va, p.astype(BF16))
    m_sc[i] = m_new


def _normalized(acc_sc, i):
    a = acc_sc[i]
    return a[0:HEAD_DIM, :] / a[HEAD_DIM:HEAD_DIM + 1, :]


def _init(m_sc, acc_sc):
    m_sc[...] = jnp.full(m_sc.shape, NEG, F32)
    acc_sc[...] = jnp.zeros(acc_sc.shape, F32)


def _sweep(qi, group, near, far, alive=None, pin=None):
    nf = jnp.maximum(qi - 1, 0)

    def rest(pinned):
        @pl.when(qi == 0)
        def _():
            near(1, pinned)

        @pl.when(qi > 0)
        def _():
            near(2, pinned)

        if alive is None:
            def body(i, c):
                top = nf - 1 - group * i
                far([top - g for g in range(group)], pinned)
                return c

            lax.fori_loop(0, nf // group, body, 0)
            ok = None
        else:
            def body(c):
                i, _ = c
                top = nf - 1 - group * i
                far([top - g for g in range(group)], pinned)
                return i + 1, alive(top - group + 1)

            _, ok = lax.while_loop(lambda c: (c[0] < nf // group) & c[1], body, (0, alive(nf)))
        rem = nf % group
        size = group // 2
        while size >= 1:
            top = (rem & (2 * size - 1)) - 1
            cond = (rem & size) != 0
            if ok is not None:
                cond = cond & ok

            @pl.when(cond)
            def _(size=size, top=top):
                far([top - g for g in range(size)], pinned)

            if ok is not None and size > 1:
                ok = ok & alive(rem & (size - 1))
            size //= 2

    if pin is None:
        rest(False)
    else:
        pinned = pin()

        @pl.when(pinned)
        def _():
            rest(True)

        @pl.when(jnp.logical_not(pinned))
        def _():
            rest(False)


PIN_HEADROOM = 100.0
PIN_MAX_GAP = 220.0


def _pin_reference(bounds, lows, m_sc):
    gap = bounds[0] - lows[0]
    for i in range(1, len(bounds)):
        gap = jnp.maximum(gap, bounds[i] - lows[i])
    pinned = jnp.max(gap) <= PIN_MAX_GAP

    @pl.when(pinned)
    def _():
        for i, (bound, low) in enumerate(zip(bounds, lows)):
            m_sc[i] = jnp.maximum(low, bound - PIN_HEADROOM)

    return pinned


def _own_key_scores(qt_ref, kd_ref, rows_per_map):
    prod = qt_ref[...].astype(F32) * kd_ref[...].astype(BF16).astype(F32)
    tq = prod.shape[1]
    z = jnp.sum(prod.reshape(MIX_WIDTH // rows_per_map, rows_per_map, tq), axis=1, keepdims=True)
    return z - 0.01


def _softmax_blocks(items, scores, v_of, m_sc, acc_sc, tq, pinned=False):
    ss = [scores(j, kind, i) for (j, kind, i, x, sel) in items]
    if not pinned:
        for s, (j, kind, i, x, sel) in zip(ss, items):
            if kind == "diag":
                s = jnp.where(_key_before_query(tq, tq, False), s, NEG)
            _softmax_step(i, s, _v_aug(v_of(j), x), m_sc, acc_sc, sel(j, x) if sel is not None else None)
        return
    pvs = {}
    for s, (j, kind, i, x, sel) in zip(ss, items):
        if kind == "diag":
            s = jnp.where(_key_before_query(tq, tq, False), s, NEG)
        ref = m_sc[i]
        if sel is not None:
            ref = jnp.where(sel(j, x), ref, -NEG)
        pv = _nn(_v_aug(v_of(j), x), jnp.exp2(s - ref).astype(BF16))
        pvs[i] = pv if i not in pvs else pvs[i] + pv
    for i, pv in pvs.items():
        acc_sc[i] = acc_sc[i] + pv


def _max_key_norm(kn2_ref, h):
    return jnp.sqrt(jnp.max(kn2_ref[:, h, :], axis=0, keepdims=True)[:, 0:1])


def _score_bound(q, kn, bias_max):
    qn = jnp.sqrt(jnp.sum(jnp.square(q.astype(F32)), axis=0, keepdims=True))
    return qn * kn * 1.001 + (bias_max + 0.01)


def _table_max(tab_ref, h):
    m = tab_ref[h, 0]
    for b in range(1, N_BUCKETS):
        m = jnp.maximum(m, tab_ref[h, b])
    return m * LOG2E


def _fox_kernel(qt_ref, k_ref, vt_ref, kd_ref, kn2_ref, cq_ref, call_ref, cend_ref, o_ref, kx_sc, m_sc, acc_sc,
                *, tq):
    qi = pl.program_id(1)

    def head_row(c, h):
        return c[h:h + 1, :]

    @pl.when(qi == 0)
    def _():
        c = call_ref[...] * LOG2E
        for p in range(HPM // 2):
            by_row = _pieces([head_row(c, 2 * p), head_row(c, 2 * p + 1)], 0)
            by_row.update({6: 1.0, 7: 1.0, 8: 1.0})
            e = _extra_rows(by_row, 16, c.shape[1])
            e = jnp.concatenate([e, jnp.zeros((PAIR - 16, c.shape[1]), F32)], axis=0)
            kx_sc[p] = e.T.astype(BF16)

    cq = cq_ref[...] * LOG2E
    qps, cts, zbs = [], [], []
    for h in range(HPM):
        p, x = divmod(h, 2)
        ct = head_row(cq, h)
        by_row = {3 * x: -1.0, 3 * x + 1: -1.0, 3 * x + 2: -1.0}
        by_row.update(_pieces([ct], 6))
        qx = _extra_rows(by_row, PAIR, tq).astype(BF16)
        qh = _row_range(qt_ref[p * PAIR:(p + 1) * PAIR, :], x * HEAD_DIM, (x + 1) * HEAD_DIM)
        qps.append(jnp.concatenate([qh, qx], axis=0))
        cts.append(ct)
        qn = jnp.sqrt(jnp.sum(jnp.square(qh.astype(F32)), axis=0, keepdims=True))
        zbs.append(qn * _max_key_norm(kn2_ref, h) * 1.001 + 0.001)
    _init(m_sc, acc_sc)

    def alive(low):
        lane = lax.broadcasted_iota(jnp.int32, cend_ref.shape, 1)
        cl = jnp.sum(jnp.where(lane == low - 1, cend_ref[...] * LOG2E, 0.0), axis=1, keepdims=True)
        worst = zbs[0] + cts[0] - head_row(cl, 0) - m_sc[0]
        for h in range(1, HPM):
            worst = jnp.maximum(worst, zbs[h] + cts[h] - head_row(cl, h) - m_sc[h])
        return jnp.max(worst) > EXP2_DEAD

    def scores(j, kind, h):
        r0 = pl.multiple_of(j * tq, tq)
        p = h // 2
        kp = jnp.concatenate([k_ref[pl.ds(r0, tq), p * PAIR:(p + 1) * PAIR], kx_sc[p, pl.ds(r0, tq), :]], axis=1)
        return _nn(kp, qps[h])

    def run(blocks, pinned=False):
        items = [(j, kind, h, h, None) for j, kind in blocks for h in range(HPM)]
        _softmax_blocks(items, scores, lambda j: vt_ref[j], m_sc, acc_sc, tq, pinned)

    lows = _own_key_scores(qt_ref, kd_ref, HEAD_DIM)
    _sweep(qi, 4,
           lambda n, pinned: run([(qi, "diag")] + ([(qi - 1, "far")] if n == 2 else []), pinned),
           lambda js, pinned: run([(j, "far") for j in js], pinned), alive,
           lambda: _pin_reference([zb + 0.01 for zb in zbs], [lows[h] for h in range(HPM)], m_sc))
    o = jnp.concatenate([_normalized(acc_sc, h) for h in range(HPM)], axis=0)
    o_ref[...] = o.T.astype(o_ref.dtype)


def _moba_kernel(tab_ref, qt_ref, k_ref, vt_ref, kd_ref, kn2_ref, mean_ref, bt_ref, o_ref, m_sc, acc_sc, sel_sc,
                 *, tq, nb):
    qi = pl.program_id(1)
    ones_k = jnp.ones((tq, PAIR), BF16)
    nbp = sel_sc.shape[1]
    rb = _rows(nbp)
    qs, qps, bounds = [], [], []
    means = [_split2(mean_ref[p * PAIR:(p + 1) * PAIR, :].T) for p in range(HPM // 2)]
    for h in range(HPM):
        p, x = divmod(h, 2)
        qx = _row_range(qt_ref[p * PAIR:(p + 1) * PAIR, :], x * HEAD_DIM, (x + 1) * HEAD_DIM)
        qs.append(qx)
        bounds.append(_score_bound(qx, _max_key_norm(kn2_ref, h), _table_max(tab_ref, h)))
        far = jnp.full((1, tq), tab_ref[h, N_BUCKETS - 1] * LOG2E, F32)
        qps.append(jnp.concatenate([qx, _extra_rows(_pieces([far], 0), PAIR, tq).astype(BF16)], axis=0))
    gates = [(_nn(means[h // 2][0], qs[h]) + _nn(means[h // 2][1], qs[h]))[0:nbp, :] for h in range(HPM)]
    for h, gate in enumerate(gates):
        cnt = jnp.zeros((nbp, tq), jnp.int32)
        for n in range(nb):
            gn = gate[n:n + 1, :]
            beats = (gn > gate) | ((gn == gate) & (n < rb))
            cnt = cnt + jnp.where(beats, jnp.where(n < qi, 1, 0), 0)
        sel_sc[h] = jnp.where((cnt < MOBA_TOPK) & (rb < qi), 1.0, 0.0)
    _init(m_sc, acc_sc)

    def scores(j, kind, h):
        p = h // 2
        k = k_ref[pl.ds(pl.multiple_of(j * tq, tq), tq), p * PAIR:(p + 1) * PAIR]
        if kind == "far":
            return _nn(jnp.concatenate([k, ones_k], axis=1), qps[h])
        return _nn(k, qs[h]) + bt_ref[h, 0 if kind == "diag" else 1]

    def run(blocks, pinned=False):
        items = [(j, kind, h, h, None if kind == "diag" else (lambda j, h: sel_sc[h, pl.ds(j, 1), :] > 0.0))
                 for j, kind in blocks for h in range(HPM)]
        _softmax_blocks(items, scores, lambda j: vt_ref[j], m_sc, acc_sc, tq, pinned)

    own = _own_key_scores(qt_ref, kd_ref, HEAD_DIM)
    lows = [own[h] + tab_ref[h, 0] * LOG2E for h in range(HPM)]
    _sweep(qi, 4,
           lambda n, pinned: run([(qi, "diag")] + ([(qi - 1, "sub")] if n == 2 else []), pinned),
           lambda js, pinned: run([(j, "far") for j in js], pinned),
           pin=lambda: _pin_reference(bounds, lows, m_sc))
    o = jnp.concatenate([_normalized(acc_sc, h) for h in range(HPM)], axis=0)
    o_ref[...] = o.T.astype(o_ref.dtype)


def _sb_kernel(qt_ref, k_ref, vt_ref, su_ref, o_ref, r_sc, acc_sc, *, tq):
    qi = pl.program_id(1)
    qs = [_row_range(qt_ref[(h // 2) * PAIR:(h // 2 + 1) * PAIR, :], (h % 2) * HEAD_DIM, (h % 2 + 1) * HEAD_DIM)
          for h in range(HPM)]
    su = su_ref[...]
    r_sc[...] = jnp.zeros(r_sc.shape, F32)
    acc_sc[...] = jnp.zeros(acc_sc.shape, F32)

    def run(blocks):
        tiles = [(j, masked, h) for j, masked in blocks for h in range(HPM)]
        zs = [_nn(k_ref[pl.ds(pl.multiple_of(j * tq, tq), tq), (h // 2) * PAIR:(h // 2 + 1) * PAIR], qs[h])
              for j, masked, h in tiles]
        lbs, lats = [], []
        for z, (j, masked, h) in zip(zs, tiles):
            log_beta = jnp.minimum(z, 0.0) - jnp.log2(1.0 + jnp.exp2(-jnp.abs(z)))
            log_keep = log_beta - z
            if masked:
                log_keep = jnp.where(_key_before_query(tq, tq, True), log_keep, 0.0)
            lbs.append(log_beta)
            lats.append(_nn(su, log_keep.astype(BF16)))
        r = [r_sc[h] for h in range(HPM)]
        pvs = [jnp.zeros((HEAD_DIM, tq), F32)] * HPM
        for log_beta, lat, (j, masked, h) in zip(lbs, lats, tiles):
            a = jnp.exp2(log_beta + lat[0:tq, :] + r[h])
            if masked:
                a = jnp.where(_key_before_query(tq, tq, True), a, 0.0)
            r[h] = r[h] + lat[tq:tq + 1, :]
            pvs[h] = pvs[h] + _nn(vt_ref[j][h * HEAD_DIM:(h + 1) * HEAD_DIM, :], a.astype(BF16))
        for h in range(HPM):
            r_sc[h] = r[h]
            acc_sc[h] = acc_sc[h] + pvs[h]

    def alive(low):
        return jnp.max(r_sc[...]) > EXP2_DEAD

    _sweep(qi, 2,
           lambda n, pinned: run([(qi, True)] + ([(qi - 1, False)] if n == 2 else [])),
           lambda js, pinned: run([(j, False) for j in js]), alive)
    o = jnp.concatenate([acc_sc[h] for h in range(HPM)], axis=0)
    o_ref[...] = o.T.astype(o_ref.dtype)


def _lam_value(lam_ref, lam_init):
    lm = lam_ref[...]
    a = jnp.sum(lm[0:1] * lm[1:2], axis=-1, keepdims=True)
    b = jnp.sum(lm[2:3] * lm[3:4], axis=-1, keepdims=True)
    return jnp.exp(a) - jnp.exp(b) + lam_init


def _diff_kernel(tab_ref, qt_ref, k_ref, vt_ref, kd_ref, kn2_ref, bt_ref, lam_ref, gsub_ref, o_ref, m_sc, acc_sc,
                 *, tq, lam_init):
    qi = pl.program_id(1)
    ones_k = jnp.ones((tq, PAIR), BF16)
    qs, qps, bounds = [], [], []
    for i in range(2 * HPM):
        p, r = divmod(i * DIFF_HALF, PAIR)
        qx = _row_range(qt_ref[p * PAIR:(p + 1) * PAIR, :], r, r + DIFF_HALF)
        qs.append(qx)
        bounds.append(_score_bound(qx, _max_key_norm(kn2_ref, i // 2), _table_max(tab_ref, HPM + i // 2)))
        far = jnp.full((1, tq), tab_ref[HPM + i // 2, N_BUCKETS - 1] * LOG2E, F32)
        qps.append(jnp.concatenate([qx, _extra_rows(_pieces([far], 0), PAIR, tq).astype(BF16)], axis=0))
    _init(m_sc, acc_sc)

    def scores(j, kind, i):
        p = i // 4
        k = k_ref[pl.ds(pl.multiple_of(j * tq, tq), tq), p * PAIR:(p + 1) * PAIR]
        if kind == "far":
            return _nn(jnp.concatenate([k, ones_k], axis=1), qps[i])
        return _nn(k, qs[i]) + bt_ref[i // 2, 0 if kind == "diag" else 1]

    def run(blocks, pinned=False):
        items = [(j, kind, i, i // 2, None) for j, kind in blocks for i in range(2 * HPM)]
        _softmax_blocks(items, scores, lambda j: vt_ref[j], m_sc, acc_sc, tq, pinned)

    own = _own_key_scores(qt_ref, kd_ref, DIFF_HALF)
    lows = [own[i] + tab_ref[HPM + i // 2, 0] * LOG2E for i in range(2 * HPM)]
    _sweep(qi, 2,
           lambda n, pinned: run([(qi, "diag")] + ([(qi - 1, "sub")] if n == 2 else []), pinned),
           lambda js, pinned: run([(j, "far") for j in js], pinned),
           pin=lambda: _pin_reference(bounds, lows, m_sc))
    lam_val = _lam_value(lam_ref, lam_init)
    gcol = jnp.concatenate([gsub_ref[...]] * (tq // LANES), axis=1)
    outs = []
    for x in range(HPM):
        o = _normalized(acc_sc, 2 * x) - lam_val * _normalized(acc_sc, 2 * x + 1)
        inv = lax.rsqrt(jnp.mean(o * o, axis=0, keepdims=True) + EPS)
        outs.append(((o * inv) * gcol) * (1.0 - lam_init))
    o_ref[...] = jnp.concatenate(outs, axis=0).T.astype(o_ref.dtype)


def _mixer_specs(t, tq, mixer, layer, kn2_tiles):
    nk = t // tq
    qspec = pl.BlockSpec((None, MIX_WIDTH, tq), lambda b, qi: (b, mixer, qi))
    kspec = pl.BlockSpec((None, t, MIX_WIDTH), lambda b, qi: (b, 0, mixer))
    vspec = pl.BlockSpec((None, nk, MIX_WIDTH, tq), lambda b, qi: (b, 0, mixer, 0))
    kdspec = pl.BlockSpec((None, None, MIX_WIDTH, tq), lambda b, qi: (layer, b, mixer, qi))
    knspec = pl.BlockSpec((None, kn2_tiles, None, HPM, LANES), lambda b, qi: (b, 0, mixer, 0, 0))
    ospec = pl.BlockSpec((None, tq, MIX_WIDTH), lambda b, qi: (b, qi, 0))
    return [qspec, kspec, vspec, kdspec, knspec], ospec


def _attn_out(b, t):
    return jax.ShapeDtypeStruct((b, t, MIX_WIDTH), BF16)


def _stat_scratch(n_maps, tq, rows):
    return [pltpu.VMEM((n_maps, 1, tq), F32), pltpu.VMEM((n_maps, rows, tq), F32)]


def _fox_prompt(qkv, layer, c, tq):
    b, t, _ = qkv[1].shape
    specs, ospec = _mixer_specs(t, tq, 1, layer, qkv[4].shape[1])
    cq = pl.BlockSpec((None, 8, tq), lambda bi, qi: (bi, 0, qi))
    call = pl.BlockSpec((None, 8, t), lambda bi, qi: (bi, 0, 0))
    cend = c[:, :, tq - 1::tq]
    cends = pl.BlockSpec((None, 8, t // tq), lambda bi, qi: (bi, 0, 0))
    return pl.pallas_call(
        functools.partial(_fox_kernel, tq=tq),
        grid=(b, t // tq),
        in_specs=specs + [cq, call, cends],
        out_specs=ospec,
        out_shape=_attn_out(b, t),
        scratch_shapes=[pltpu.VMEM((HPM // 2, t, PAIR), BF16)] + _stat_scratch(HPM, tq, ACC_ROWS),
        compiler_params=_params("parallel", "arbitrary"),
        name="fox_prompt",
    )(*qkv, c, c, cend)


def _moba_prompt(tab, qkv, layer, means, bp, tq):
    b, t, _ = qkv[1].shape
    nb = t // MOBA_BLOCK
    nbp = -(-nb // 8) * 8
    specs, ospec = _mixer_specs(t, tq, 0, layer, qkv[4].shape[1])
    mspec = pl.BlockSpec((None, MIX_WIDTH, LANES), lambda bi, qi: (bi, 0, 0))
    bt = pl.BlockSpec((HPM, 2, tq, tq), lambda bi, qi: (0, 0, 0, 0))
    return pl.pallas_call(
        functools.partial(_moba_kernel, tq=tq, nb=nb),
        grid=(b, t // tq),
        in_specs=[pl.BlockSpec(memory_space=pltpu.SMEM)] + specs + [mspec, bt],
        out_specs=ospec,
        out_shape=_attn_out(b, t),
        scratch_shapes=_stat_scratch(HPM, tq, ACC_ROWS) + [pltpu.VMEM((HPM, nbp, tq), F32)],
        compiler_params=_params("parallel", "arbitrary"),
        name="moba_prompt",
    )(tab, *qkv, means, bp)


def _sb_prompt(qkv, layer, su, tq):
    b, t, _ = qkv[1].shape
    specs, ospec = _mixer_specs(t, tq, 2, layer, qkv[4].shape[1])
    return pl.pallas_call(
        functools.partial(_sb_kernel, tq=tq),
        grid=(b, t // tq),
        in_specs=specs[:3] + [_const_spec(su.shape)],
        out_specs=ospec,
        out_shape=_attn_out(b, t),
        scratch_shapes=_stat_scratch(HPM, tq, HEAD_DIM),
        compiler_params=_params("parallel", "arbitrary"),
        name="sb_prompt",
    )(*qkv[:3], su)


def _diff_prompt(tab, qkv, layer, bp, lam, gsub_col, lam_init, tq):
    b, t, _ = qkv[1].shape
    specs, ospec = _mixer_specs(t, tq, 3, layer, qkv[4].shape[1])
    bt = pl.BlockSpec((HPM, 2, tq, tq), lambda bi, qi: (1, 0, 0, 0))
    return pl.pallas_call(
        functools.partial(_diff_kernel, tq=tq, lam_init=lam_init),
        grid=(b, t // tq),
        in_specs=[pl.BlockSpec(memory_space=pltpu.SMEM)] + specs
        + [bt, _const_spec((4, DIFF_HALF)), _const_spec((HEAD_DIM, LANES))],
        out_specs=ospec,
        out_shape=_attn_out(b, t),
        scratch_shapes=_stat_scratch(2 * HPM, tq, ACC_ROWS),
        compiler_params=_params("parallel", "arbitrary"),
        name="diff_prompt",
    )(tab, *qkv, bp, lam, gsub_col)


N_SLOTS = 20


def _sample_masks(dec_seq):
    rows = N_SLOTS * dec_seq
    qmask = np.zeros((rows, D_QKV), np.float32)
    for r in range(rows):
        slot = r // dec_seq
        if slot < 3 * HPM:
            qmask[r, slot * HEAD_DIM:(slot + 1) * HEAD_DIM] = 1.0
        elif slot < 4 * HPM:
            qmask[r, slot * HEAD_DIM:slot * HEAD_DIM + DIFF_HALF] = 1.0
        else:
            c0 = (slot - HPM) * HEAD_DIM + DIFF_HALF
            qmask[r, c0:c0 + DIFF_HALF] = 1.0
    omask = np.zeros((N_HEADS * dec_seq, D_QKV), np.float32)
    for r in range(N_HEADS * dec_seq):
        omask[r, (r // dec_seq) * HEAD_DIM:(r // dec_seq + 1) * HEAD_DIM] = 1.0
    return qmask, omask


def _expand_rows(a, dec_seq):
    rows = HPM * dec_seq
    r = lax.broadcasted_iota(jnp.int32, (rows, 1), 0)
    out = jnp.broadcast_to(a[HPM - 1:HPM, :], (rows, a.shape[1]))
    for h in range(HPM - 2, -1, -1):
        out = jnp.where(r < (h + 1) * dec_seq, a[h:h + 1, :], out)
    return out


def _softmax_pages(s_pages):
    m = s_pages[0].max(axis=-1, keepdims=True)
    for s in s_pages[1:]:
        m = jnp.maximum(m, s.max(axis=-1, keepdims=True))
    ps = [jnp.exp(s - m) for s in s_pages]
    l = ps[0].sum(axis=-1, keepdims=True)
    for p in ps[1:]:
        l = l + p.sum(axis=-1, keepdims=True)
    return [p / l for p in ps]


def _sample_kernel(*refs, n_pages, dec_seq, lam_init):
    pt_ref = refs[0]
    del pt_ref
    (q_ref, kn_ref, vn_ref, lfn_ref, bs_ref, qmask_ref, omask_ref, sl_ref, lam_ref,
     gd1_ref, gd2_ref, gmul_ref, gadd_ref) = refs[1:14]
    k_refs = refs[14:14 + n_pages]
    v_refs = refs[14 + n_pages:14 + 2 * n_pages]
    lf_refs = refs[14 + 2 * n_pages:14 + 3 * n_pages]
    o_ref = refs[14 + 3 * n_pages]
    s_sc, p_sc = refs[14 + 3 * n_pages + 1:]

    g = HPM * dec_seq
    npg = n_pages + 1
    q = q_ref[...]
    qbd = (jnp.concatenate([q] * (N_SLOTS * dec_seq // 8), axis=0) * qmask_ref[...]).astype(BF16)
    pad = jnp.zeros((LANES - kn_ref.shape[0], D_QKV), BF16)
    for p in range(n_pages):
        s_sc[p] = _nn(qbd, k_refs[p][...].astype(BF16))
    s_sc[n_pages] = _nt(qbd, jnp.concatenate([kn_ref[...], pad], axis=0))

    lane = lax.broadcasted_iota(jnp.int32, (1, LANES), 1)
    qrow = lax.broadcasted_iota(jnp.int32, (g, 1), 0) % dec_seq
    new_ok = lane <= qrow
    new_past = lane < qrow
    sl = sl_ref[...]

    def pages(r0):
        return [s_sc[p, r0:r0 + g, :] for p in range(npg)]

    sm = pages(0)
    ppb = MOBA_BLOCK // LANES
    nblk = n_pages // ppb
    gates = []
    for n in range(nblk):
        tot = sm[n * ppb]
        for i in range(1, ppb):
            tot = tot + sm[n * ppb + i]
        gates.append(jnp.sum(tot, axis=-1, keepdims=True))
    s_pages = []
    for n in range(nblk):
        cnt = jnp.zeros((g, 1), jnp.int32)
        for n2 in range(nblk):
            if n2 == n:
                continue
            beats = (gates[n2] > gates[n]) | ((gates[n2] == gates[n]) & (n2 < n))
            cnt = cnt + jnp.where(beats, 1, 0)
        for i in range(ppb):
            p = n * ppb + i
            s_pages.append(jnp.where(cnt < MOBA_TOPK, sm[p] + bs_ref[0, p], NEG))
    s_pages.append(jnp.where(new_ok, sm[n_pages] + bs_ref[0, n_pages], NEG))
    for p, pr in enumerate(_softmax_pages(s_pages)):
        p_sc[p, 0:g, :] = pr.astype(BF16)

    lfn = lfn_ref[...]
    cn_row = jnp.zeros((g, 1), F32)
    bias_new = jnp.zeros((g, LANES), F32)
    for n in range(dec_seq):
        col = lfn[:, n:n + 1]
        cn_row = cn_row + jnp.where(n <= qrow, col, 0.0)
        bias_new = bias_new + jnp.where((lane < n) & (n <= qrow), col, 0.0)
    lfe = [_expand_rows(lf_refs[p][...], dec_seq) for p in range(n_pages)]
    suffix = _nn_split(jnp.concatenate(lfe, axis=0), sl, 3)
    sf = pages(g)
    s_pages = [None] * npg
    s_pages[n_pages] = jnp.where(new_ok, sf[n_pages] + bias_new, NEG)
    after = cn_row
    for p in range(n_pages - 1, -1, -1):
        s_pages[p] = sf[p] + (suffix[p * g:(p + 1) * g, :] + after)
        after = after + jnp.sum(lfe[p], axis=-1, keepdims=True)
    for p, pr in enumerate(_softmax_pages(s_pages)):
        p_sc[p, g:2 * g, :] = pr.astype(BF16)

    ss = pages(2 * g)
    lbs, lks = [], []
    for p in range(npg):
        lb = _log_sigmoid(ss[p])
        lk = lb - ss[p]
        if p == n_pages:
            lk = jnp.where(new_past, lk, 0.0)
        lbs.append(lb)
        lks.append(lk)
    later_in = _nn_split(jnp.concatenate(lks, axis=0), sl, 2)
    after = jnp.zeros((g, 1), F32)
    for p in range(npg - 1, -1, -1):
        a = jnp.exp(lbs[p] + later_in[p * g:(p + 1) * g, :] + after)
        if p == n_pages:
            a = jnp.where(new_past, a, 0.0)
        p_sc[p, 2 * g:3 * g, :] = a.astype(BF16)
        after = after + jnp.sum(lks[p], axis=-1, keepdims=True)

    lam_val = _lam_value(lam_ref, lam_init)
    maps = []
    for mp in range(2):
        sd = pages((3 + mp) * g)
        s_pages = [sd[p] + bs_ref[1, p] for p in range(n_pages)]
        s_pages.append(jnp.where(new_ok, sd[n_pages] + bs_ref[1, n_pages], NEG))
        maps.append(_softmax_pages(s_pages))
    for p in range(npg):
        p_sc[p, 3 * g:4 * g, :] = (maps[0][p] - lam_val * maps[1][p]).astype(BF16)

    padv = jnp.zeros((LANES - vn_ref.shape[0], D_QKV), BF16)
    acc = _nn(p_sc[n_pages], jnp.concatenate([vn_ref[...], padv], axis=0))
    for p in range(n_pages):
        acc = acc + _nt(p_sc[p], v_refs[p][...].astype(BF16))
    rm = acc * omask_ref[...]
    y = rm[0:8, :]
    for i in range(1, N_HEADS * dec_seq // 8):
        y = y + rm[8 * i:8 * (i + 1), :]
    o8 = y + pltpu.roll(y, dec_seq, 0)
    ms = _nn_split(o8 * o8, gd1_ref[...], 2)
    bc = _nn_split(lax.rsqrt(ms + EPS), gd2_ref[...], 3)
    o8 = o8 * (bc * gmul_ref[...] + gadd_ref[...])
    o_ref[...] = o8[0:dec_seq, :]


def _sample_attention(layer, page_table, q8, kn, vn, lfn, bs, consts, lam, gmul, gadd,
                      cache_k, cache_v, cache_lft, lam_init):
    db, n_pages = page_table.shape
    dec_seq = lfn.shape[1] // HPM
    page = cache_k.shape[3]
    qmask, omask, sl, gd1, gd2 = consts
    rows = N_SLOTS * dec_seq

    def batch_spec(shape):
        nd = len(shape)
        return pl.BlockSpec((None,) + shape, lambda b, pt: (b,) + (0,) * nd)

    def const(shape):
        nd = len(shape)
        return pl.BlockSpec(shape, lambda b, pt: (0,) * nd)

    def paged(width_rows, width_cols):
        return [pl.BlockSpec((None, None, width_rows, width_cols),
                             functools.partial(lambda b, pt, j: (layer, pt[b, j], 0, 0), j=j))
                for j in range(n_pages)]

    in_specs = ([batch_spec((8, D_QKV)), batch_spec(kn.shape[1:]), batch_spec(vn.shape[1:]),
                 batch_spec(lfn.shape[1:]), const(bs.shape), const(qmask.shape), const(omask.shape),
                 const(sl.shape), const(lam.shape), const(gd1.shape), const(gd2.shape),
                 const(gmul.shape), const(gadd.shape)]
                + paged(D_QKV, page) + paged(D_QKV, page) + paged(HPM, page))
    grid_spec = pltpu.PrefetchScalarGridSpec(
        num_scalar_prefetch=1,
        grid=(db,),
        in_specs=in_specs,
        out_specs=pl.BlockSpec((None, dec_seq, D_QKV), lambda b, pt: (b, 0, 0)),
        scratch_shapes=[pltpu.VMEM((n_pages + 1, rows, LANES), F32),
                        pltpu.VMEM((n_pages + 1, N_HEADS * dec_seq, LANES), BF16)],
    )
    return pl.pallas_call(
        functools.partial(_sample_kernel, n_pages=n_pages, dec_seq=dec_seq, lam_init=lam_init),
        grid_spec=grid_spec,
        out_shape=jax.ShapeDtypeStruct((db, dec_seq, D_QKV), F32),
        compiler_params=_params("arbitrary"),
        name="sample_attention",
    )(page_table, q8, kn, vn, lfn, bs, qmask, omask, sl, lam, gd1, gd2, gmul, gadd,
      *([cache_k] * n_pages), *([cache_v] * n_pages), *([cache_lft] * n_pages))


MLP_CHUNK = 1024


def _merge_mlp_kernel(x_ref, o0_ref, o1_ref, o2_ref, o3_ref, gmix_ref, wg_ref, wbr_ref, wo_ref,
                      gmlp_ref, wup_ref, wdn_ref, y_ref):
    x = x_ref[...]
    h = (x * lax.rsqrt(jnp.mean(x * x, axis=-1, keepdims=True) + EPS)) * gmix_ref[...]
    hb = h.astype(BF16)
    acc = jnp.zeros(x.shape, F32)
    for m, o_ref in enumerate((o0_ref, o1_ref, o2_ref, o3_ref)):
        gate = 1.0 / (1.0 + jnp.exp(-_nt(hb, wg_ref[m * D_MODEL:(m + 1) * D_MODEL, :])))
        acc = acc + gate * _nn(o_ref[...].astype(BF16), wbr_ref[m])
    y = x + _nn(acc.astype(BF16), wo_ref[...])
    hm = ((y * lax.rsqrt(jnp.mean(y * y, axis=-1, keepdims=True) + EPS)) * gmlp_ref[...]).astype(BF16)
    for c in range(D_FF // MLP_CHUNK):
        u = jnp.maximum(_nn(hm, wup_ref[:, c * MLP_CHUNK:(c + 1) * MLP_CHUNK]), 0.0)
        y = y + _nn((u * u).astype(BF16), wdn_ref[c * MLP_CHUNK:(c + 1) * MLP_CHUNK, :])
    y_ref[...] = y


def _resident_spec(shape):
    nd = len(shape)
    return pl.BlockSpec(shape, lambda *_: (0,) * nd, pipeline_mode=pl.Buffered(1))


def _merge_mlp(x, os_, gmix, wg, wbr, wo, gmlp, wup, wdn, tm):
    rows = x.shape[0]
    row = lambda w: pl.BlockSpec((tm, w), lambda i: (i, 0))
    return pl.pallas_call(
        _merge_mlp_kernel,
        grid=(rows // tm,),
        in_specs=[row(D_MODEL)] + [row(MIX_WIDTH)] * 4
        + [_const_spec((1, D_MODEL)), _resident_spec((N_MIXERS * D_MODEL, D_MODEL)),
           _resident_spec((N_MIXERS, MIX_WIDTH, D_MODEL)), _resident_spec((D_MODEL, D_MODEL)),
           _const_spec((1, D_MODEL)), _resident_spec((D_MODEL, D_FF)), _resident_spec((D_FF, D_MODEL))],
        out_specs=row(D_MODEL),
        out_shape=jax.ShapeDtypeStruct((rows, D_MODEL), F32),
        compiler_params=_params("parallel"),
        name="merge_mlp",
    )(x, *os_, gmix, wg, wbr, wo, gmlp, wup, wdn)


def _qk_vectors(gq, gk, qscale):
    z = jnp.zeros((MIX_WIDTH,), F32)
    one = jnp.ones((MIX_WIDTH,), F32)
    t4 = lambda a: jnp.tile(a, HPM)
    sc = HEAD_DIM ** -0.5 * qscale
    qmul = jnp.concatenate([t4(gq[0]) * sc, t4(gq[1]) * sc, z, t4(gq[2]) * (DIFF_HALF ** -0.5 * qscale)])
    qadd = jnp.concatenate([z, z, one * sc, z])
    kmul = jnp.concatenate([t4(gk[0]), t4(gk[1]), z, t4(gk[2])])
    kadd = jnp.concatenate([z, z, one, z])
    return [a.reshape(1, D_QKV) for a in (qmul, qadd, kmul, kadd)]


def _bias_index_tiles(tq, n_pages, page, dec_seq):
    a = np.arange(tq)
    diag = _t5_bucket_np(a[None, :] - a[:, None])
    sub = _t5_bucket_np(tq + a[None, :] - a[:, None])
    idxp = np.stack([diag, sub]).astype(np.int32)
    past_len = n_pages * page
    qpos = past_len + (np.arange(HPM * dec_seq) % dec_seq)
    kpos = np.concatenate([np.arange(past_len), past_len + np.arange(page)])
    idxs = _t5_bucket_np(qpos[None, :, None] - kpos.reshape(n_pages + 1, 1, page))
    return idxp, idxs.astype(np.int32)


def _layer(l, x, past, w, consts, tq, tm, kv_all=None):
    (tab, bp, bs, g1, g2, su_q, tri_c, sl_p, sample_consts, gd) = consts
    (w_in, b_forget, g_q, g_k, lam, g_sub, w_branch, w_o, g_mix, g_mlp, w_up, w_down) = w
    b, t, _ = x.shape
    rows = b * t
    lam_init = 0.8 - 0.6 * math.exp(-0.3 * l)
    wt = w_in[l].T.astype(BF16)
    wqkv = wf = wt
    wg = wt[3 * D_QKV + HPM:]
    gmix = g_mix[l].reshape(1, D_MODEL)
    qk_vecs = _qk_vectors(g_q[l], g_k[l], LOG2E if past is None else 1.0)
    x2 = x.reshape(rows, D_MODEL)

    if past is None:
        cols = [jnp.broadcast_to(a.reshape(D_QKV, 1), (D_QKV, LANES)) for a in qk_vecs]
        bf_col = jnp.broadcast_to(jnp.pad(b_forget[l], (0, 8 - HPM))[:, None], (8, LANES))
        qt, kb3, vt, k_new, v_new, lf_t, kn2 = _inproj_t(x, gmix, wqkv, wf, bf_col, *cols, tm=tm, tq=tq,
                                                         layer=l, depth=w_in.shape[0], kv_all=kv_all)
        kn2 = kn2.reshape(b, t // tm, N_MIXERS, HPM, LANES)
        lf_new = jnp.swapaxes(lf_t[:, :HPM], 1, 2)
        c = _cumsum(lf_t, tri_c)
        means = _block_means(k_new, l)
        gsub_col = jnp.broadcast_to(g_sub[l][:, None], (HEAD_DIM, LANES))
        qkv = (qt, kb3, vt, k_new, kn2)
        o_moba = _moba_prompt(tab, qkv, l, means, bp, tq)
        o_fox = _fox_prompt(qkv, l, c, tq)
        o_sb = _sb_prompt(qkv, l, su_q, tq)
        o_dif = _diff_prompt(tab, qkv, l, bp, lam[l], gsub_col, lam_init, tq)
        outs = [a.reshape(rows, MIX_WIDTH) for a in (o_moba, o_fox, o_sb, o_dif)]
    else:
        cache_k, cache_v, cache_lft, page_table = past
        bf = jnp.pad(b_forget[l], (0, LANES - HPM)).reshape(1, LANES)
        qb, kb, vb, kf, vf, lf = _inproj(x2, gmix, wqkv, wf, bf, g1, g2, *qk_vecs, tm=min(tm, rows))
        k_new, v_new = kf.reshape(b, t, N_HEADS, HEAD_DIM), vf.reshape(b, t, N_HEADS, HEAD_DIM)
        lf_new = lf.reshape(b, t, HPM)
        q4 = qb.astype(F32).reshape(b, t, D_QKV)
        q8 = jnp.concatenate([q4] * (8 // t), axis=1)
        padn = ((0, 0), (0, 16 - t), (0, 0))
        kn = jnp.pad(kb.reshape(b, t, D_QKV), padn)
        vn = jnp.pad(vb.reshape(b, t, D_QKV), padn)
        lfn = jnp.swapaxes(lf.reshape(b, t, HPM), 1, 2)
        lfn = jnp.pad(jnp.repeat(lfn, t, axis=1), ((0, 0), (0, 0), (0, 8 - t)))
        gd1, gd2 = gd
        z3 = jnp.zeros((3 * MIX_WIDTH,), F32)
        gmul = jnp.concatenate([z3, jnp.tile(g_sub[l], HPM) * (1.0 - lam_init)]).reshape(1, D_QKV)
        gadd = jnp.concatenate([z3 + 1.0, jnp.zeros((MIX_WIDTH,), F32)]).reshape(1, D_QKV)
        o = _sample_attention(l, page_table, q8, kn, vn, lfn, bs, (*sample_consts, sl_p, gd1, gd2),
                              lam[l], gmul, gadd, cache_k, cache_v, cache_lft, lam_init)
        o = o.reshape(rows, D_QKV)
        outs = [o[:, m * MIX_WIDTH:(m + 1) * MIX_WIDTH] for m in range(N_MIXERS)]

    y = _merge_mlp(x2, outs, gmix, wg, w_branch[l].astype(BF16), w_o[l].astype(BF16),
                   g_mlp[l].reshape(1, D_MODEL), w_up[l].astype(BF16), w_down[l].astype(BF16), tm=min(tm, rows))
    return y.reshape(b, t, D_MODEL), k_new, v_new, lf_new


def _constants(rel_bias, tq, n_pages, page, dec_seq):
    idxp, idxs = _bias_index_tiles(tq, n_pages, page, dec_seq)
    tab = rel_bias.T
    bp, bs = _bias_tiles(tab, jnp.asarray(idxp), jnp.asarray(idxs), dec_seq)
    g1, g2 = _qk_group_mats()
    qmask, omask = _sample_masks(dec_seq)
    gd1 = np.zeros((D_QKV, LANES), np.float32)
    gd2 = np.zeros((LANES, D_QKV), np.float32)
    for c in range(3 * MIX_WIDTH, D_QKV):
        gd1[c, (c - 3 * MIX_WIDTH) // HEAD_DIM] = 1.0 / HEAD_DIM
        gd2[(c - 3 * MIX_WIDTH) // HEAD_DIM, c] = 1.0
    su_q = np.concatenate([_strict_lower(tq).T, np.ones((ONES_ROWS, tq), np.float32)], axis=0)
    bf = lambda a: jnp.asarray(a, BF16)
    return (tab, bp, bs, bf(g1), bf(g2), bf(su_q), bf(_upper_incl(tq)), bf(_strict_lower(page)),
            (jnp.asarray(qmask), jnp.asarray(omask)), (bf(gd1), bf(gd2)))


def kernel(x_prompt, x_sample, cache_k, cache_v, cache_logf, page_table, rel_bias, w_in, b_forget,
           g_q, g_k, lam, g_sub, w_branch, w_o, g_mix, g_mlp, w_up, w_down):
    depth = w_in.shape[0]
    tq = MOBA_BLOCK
    tm = 512
    n_pages = page_table.shape[1]
    page = cache_k.shape[2]
    dec_seq = x_sample.shape[1]
    n_phys = cache_k.shape[1]
    assert (n_pages * page) % MOBA_BLOCK == 0 and 8 % dec_seq == 0 and x_prompt.shape[1] % tq == 0

    consts = _constants(rel_bias, tq, n_pages, page, dec_seq)
    w = (w_in, b_forget, g_q, g_k, lam, g_sub, w_branch, w_o, g_mix, g_mlp, w_up, w_down)
    to_pages = lambda c: jnp.transpose(c, (0, 1, 3, 4, 2)).reshape(depth, n_phys, D_QKV, page)
    past = (to_pages(cache_k), to_pages(cache_v), jnp.swapaxes(cache_logf, 2, 3), page_table)

    hp, hs = x_prompt, x_sample
    outs = [[] for _ in range(4)]
    kv_all = None
    for l in range(depth):
        hp, kp, vp, fp = _layer(l, hp, None, w, consts, tq, tm, kv_all)
        kv_all = (kp, vp)
        hs, ks, vs, fs = _layer(l, hs, past, w, consts, tq, tm)
        for lst, a in zip(outs, (fp, ks, vs, fs)):
            lst.append(a)
    b, t = x_prompt.shape[:2]
    to_heads = lambda a: jnp.transpose(a.reshape(depth, b, N_HEADS, HEAD_DIM, t), (0, 1, 4, 2, 3))
    fp, ks, vs, fs = (jnp.stack(a) for a in outs)
    return hp, hs, to_heads(kv_all[0]), to_heads(kv_all[1]), fp, ks, vs, fs
```

```python
import functools
import math

import numpy as np
import jax
import jax.numpy as jnp
from jax import lax
from jax.experimental import pallas as pl
from jax.experimental.pallas import tpu as pltpu

F32 = jnp.float32
BF16 = jnp.bfloat16

D_MODEL = 1024
HEAD_DIM = 64
N_MIXERS = 4
HPM = 4
N_HEADS = 16
MIX_WIDTH = HPM * HEAD_DIM
D_QKV = N_HEADS * HEAD_DIM
DIFF_HALF = HEAD_DIM // 2
D_FF = 4 * D_MODEL
MOBA_BLOCK = 256
MOBA_TOPK = 3
N_BUCKETS = 32
MAX_DISTANCE = 128
EPS = 1e-6
NEG = -1e30
LOG2E = 1.4426950408889634
LANES = 128
PAIR = 2 * HEAD_DIM
VMEM_LIMIT = 56 * 1024 * 1024


def _params(*sem):
    return pltpu.CompilerParams(dimension_semantics=sem, vmem_limit_bytes=VMEM_LIMIT)


def _nt(a, b):
    return lax.dot_general(a, b, (((1,), (1,)), ((), ())), preferred_element_type=F32)


def _nn(a, b):
    return jnp.dot(a, b, preferred_element_type=F32)


def _split2(x):
    hi = x.astype(BF16)
    lo = (x - hi.astype(F32)).astype(BF16)
    return hi, lo


def _split3(x):
    hi = x.astype(BF16)
    r = x - hi.astype(F32)
    mid = r.astype(BF16)
    lo = (r - mid.astype(F32)).astype(BF16)
    return hi, mid, lo


def _nn_split(x, w, parts):
    pieces = _split2(x) if parts == 2 else _split3(x)
    out = _nn(pieces[0], w)
    for p in pieces[1:]:
        out = out + _nn(p, w)
    return out


def _log_sigmoid(z):
    return jnp.minimum(z, 0.0) - jnp.log1p(jnp.exp(-jnp.abs(z)))


def _const_spec(shape):
    nd = len(shape)
    return pl.BlockSpec(shape, lambda *_: (0,) * nd)


def _t5_bucket_np(rel):
    n = np.maximum(rel, 0)
    max_exact = N_BUCKETS // 2
    nf = np.maximum(n, 1).astype(np.float32)
    large = max_exact + (np.log(nf / np.float32(max_exact)) / np.float32(math.log(MAX_DISTANCE / max_exact))
                         * np.float32(N_BUCKETS - max_exact)).astype(np.int32)
    return np.where(n < max_exact, n, np.minimum(large, N_BUCKETS - 1)).astype(np.int32)


def _qk_group_mats():
    g1 = np.zeros((D_QKV, LANES), np.float32)
    g2 = np.zeros((LANES, D_QKV), np.float32)
    for c in range(D_QKV):
        h = c // HEAD_DIM
        if h < 2 * HPM:
            gid, size = h, HEAD_DIM
        elif h < 3 * HPM:
            continue
        else:
            gid, size = 2 * HPM + (c - 3 * MIX_WIDTH) // DIFF_HALF, DIFF_HALF
        g1[c, gid] = 1.0 / size
        g2[gid, c] = 1.0
    return g1, g2


def _strict_lower(n):
    r = np.arange(n)
    return (r[:, None] > r[None, :]).astype(np.float32)


def _upper_incl(n):
    r = np.arange(n)
    return (r[:, None] <= r[None, :]).astype(np.float32)


_WF_SPEC = pl.BlockSpec((LANES, D_MODEL), lambda *_: (3 * D_QKV // LANES, 0))


def _inproj_kernel(x_ref, gmix_ref, wqkv_ref, wf_ref, bf_ref, g1_ref, g2_ref,
                   qmul_ref, qadd_ref, kmul_ref, kadd_ref,
                   qb_ref, kb_ref, vb_ref, kf_ref, vf_ref, lf_ref):
    x = x_ref[...]
    h = (x * lax.rsqrt(jnp.mean(x * x, axis=-1, keepdims=True) + EPS)) * gmix_ref[...]
    hb = h.astype(BF16)
    g1 = g1_ref[...]
    g2 = g2_ref[...]

    def group_norm(a, mul_ref, add_ref):
        ms = _nn_split(a * a, g1, 2)
        inv = lax.rsqrt(ms + EPS)
        bc = _nn_split(inv, g2, 3)
        return a * (bc * mul_ref[...] + add_ref[...])

    q = _nt(hb, wqkv_ref[0:D_QKV, :])
    qb_ref[...] = group_norm(q, qmul_ref, qadd_ref).astype(BF16)
    k = _nt(hb, wqkv_ref[D_QKV:2 * D_QKV, :])
    kn = group_norm(k, kmul_ref, kadd_ref)
    kf_ref[...] = kn
    kb_ref[...] = kn.astype(BF16)
    v = _nt(hb, wqkv_ref[2 * D_QKV:3 * D_QKV, :])
    vf_ref[...] = v
    vb_ref[...] = v.astype(BF16)
    z = _nt(hb, wf_ref[...]) + bf_ref[...]
    lf_ref[...] = _log_sigmoid(z)[:, 0:HPM]


def _inproj(x, gmix, wqkv, wf, bf, g1, g2, qmul, qadd, kmul, kadd, tm):
    rows = x.shape[0]
    row = lambda w: pl.BlockSpec((tm, w), lambda i: (i, 0))
    vec = _const_spec((1, D_QKV))
    return pl.pallas_call(
        _inproj_kernel,
        grid=(rows // tm,),
        in_specs=[row(D_MODEL), _const_spec((1, D_MODEL)), _const_spec((3 * D_QKV, D_MODEL)),
                  _WF_SPEC, _const_spec((1, LANES)),
                  _const_spec((D_QKV, LANES)), _const_spec((LANES, D_QKV)), vec, vec, vec, vec],
        out_specs=[row(D_QKV), row(D_QKV), row(D_QKV), row(D_QKV), row(D_QKV), row(HPM)],
        out_shape=[jax.ShapeDtypeStruct((rows, D_QKV), BF16)] * 3
        + [jax.ShapeDtypeStruct((rows, D_QKV), F32)] * 2 + [jax.ShapeDtypeStruct((rows, HPM), F32)],
        compiler_params=_params("parallel"),
        name="inproj",
    )(x, gmix, wqkv, wf, bf, g1, g2, qmul, qadd, kmul, kadd)


def _inproj_t_kernel(x_ref, gmix_ref, wqkv_ref, wf_ref, bf_ref, qmul_ref, qadd_ref, kmul_ref, kadd_ref, *refs):
    qt_ref, kb_ref, vt_ref, kf_ref, vf_ref, lf_ref, kn2_ref = refs[-7:]
    x = x_ref[...]
    tm = x.shape[0]
    h = (x * lax.rsqrt(jnp.mean(x * x, axis=-1, keepdims=True) + EPS)) * gmix_ref[...]
    hb = h.astype(BF16)
    wide = lambda ref: jnp.concatenate([ref[...]] * (tm // LANES), axis=1)
    n64, n32 = 2 * MIX_WIDTH, MIX_WIDTH

    def group_norm(a, mul_ref, add_ref):
        sq = a * a

        def inv_rms(rows0, rows1, width):
            g = sq[rows0:rows1].reshape((rows1 - rows0) // width, width, tm)
            inv = lax.rsqrt(jnp.mean(g, axis=1, keepdims=True) + EPS)
            return jnp.broadcast_to(inv, g.shape).reshape(rows1 - rows0, tm)

        scale = jnp.concatenate([inv_rms(0, n64, HEAD_DIM), jnp.zeros((D_QKV - n64 - n32, tm), F32),
                                 inv_rms(D_QKV - n32, D_QKV, DIFF_HALF)], axis=0)
        return a * (scale * wide(mul_ref) + wide(add_ref))

    q = _nt(wqkv_ref[0:D_QKV, :], hb)
    qt_ref[...] = group_norm(q, qmul_ref, qadd_ref).astype(BF16)
    k = _nt(wqkv_ref[D_QKV:2 * D_QKV, :], hb)
    kn = group_norm(k, kmul_ref, kadd_ref)
    kf_ref[...] = kn
    kb_ref[...] = kn.T.astype(BF16)
    kq = kn.astype(BF16).astype(F32)
    n2 = jnp.sum((kq * kq).reshape(N_HEADS, HEAD_DIM, tm), axis=1)
    kn2_ref[...] = jnp.broadcast_to(jnp.max(n2, axis=1, keepdims=True), (N_HEADS, LANES))
    v = _nt(wqkv_ref[2 * D_QKV:3 * D_QKV, :], hb)
    vf_ref[...] = v
    vb = v.astype(BF16)
    tq = vt_ref.shape[2]
    for i in range(tm // tq):
        vt_ref[i] = vb[:, i * tq:(i + 1) * tq]
    z = _nt(wf_ref[...], hb)[0:8, :] + wide(bf_ref)
    lf_ref[...] = _log_sigmoid(z)


def _inproj_t(x, gmix, wqkv, wf, bf_col, qmul, qadd, kmul, kadd, tm, tq, layer, depth, kv_all):
    b, t, _ = x.shape
    col = _const_spec((D_QKV, LANES))
    feat = lambda rows: pl.BlockSpec((None, rows, tm), lambda bi, i: (bi, 0, i))
    layered = pl.BlockSpec((None, None, D_QKV, tm), lambda bi, i: (layer, bi, 0, i))
    operands = [x, gmix, wqkv, wf, bf_col, qmul, qadd, kmul, kadd]
    in_specs = [pl.BlockSpec((None, tm, D_MODEL), lambda bi, i: (bi, i, 0)), _const_spec((1, D_MODEL)),
                _const_spec((3 * D_QKV, D_MODEL)), _WF_SPEC, _const_spec((8, LANES)),
                col, col, col, col]
    aliases = {}
    if kv_all is not None:
        aliases = {len(operands): 3, len(operands) + 1: 4}
        operands += list(kv_all)
        in_specs += [pl.BlockSpec(memory_space=pl.ANY)] * 2
    return pl.pallas_call(
        _inproj_t_kernel,
        grid=(b, t // tm),
        in_specs=in_specs,
        out_specs=[feat(D_QKV), pl.BlockSpec((None, tm, D_QKV), lambda bi, i: (bi, i, 0)),
                   pl.BlockSpec((None, tm // tq, D_QKV, tq), lambda bi, i: (bi, i, 0, 0)),
                   layered, layered, feat(8),
                   pl.BlockSpec((None, None, N_HEADS, LANES), lambda bi, i: (bi, i, 0, 0))],
        out_shape=[jax.ShapeDtypeStruct((b, D_QKV, t), BF16), jax.ShapeDtypeStruct((b, t, D_QKV), BF16),
                   jax.ShapeDtypeStruct((b, t // tq, D_QKV, tq), BF16),
                   jax.ShapeDtypeStruct((depth, b, D_QKV, t), F32), jax.ShapeDtypeStruct((depth, b, D_QKV, t), F32),
                   jax.ShapeDtypeStruct((b, 8, t), F32), jax.ShapeDtypeStruct((b, t // tm, N_HEADS, LANES), F32)],
        input_output_aliases=aliases,
        compiler_params=_params("parallel", "parallel"),
        name="inproj_t",
    )(*operands)


def _cumsum_kernel(lf_ref, tri_ref, c_ref, *, t, blk):
    tri = tri_ref[...]
    carry = jnp.zeros((8, 1), F32)
    for i in range(t // blk):
        cs = _nn_split(lf_ref[:, i * blk:(i + 1) * blk], tri, 3) + carry
        c_ref[:, i * blk:(i + 1) * blk] = cs
        carry = cs[:, blk - 1:blk]


def _cumsum(lf_t, tri):
    b, r, t = lf_t.shape
    blk = tri.shape[0]
    return pl.pallas_call(
        functools.partial(_cumsum_kernel, t=t, blk=blk),
        grid=(b,),
        in_specs=[pl.BlockSpec((None, r, t), lambda i: (i, 0, 0)), _const_spec((blk, blk))],
        out_specs=pl.BlockSpec((None, r, t), lambda i: (i, 0, 0)),
        out_shape=jax.ShapeDtypeStruct((b, r, t), F32),
        compiler_params=_params("parallel"),
        name="logf_cumsum",
    )(lf_t, tri)


def _means_kernel(k_ref, o_ref, *, nb):
    lane = lax.broadcasted_iota(jnp.int32, (1, LANES), 1)
    acc = jnp.zeros(o_ref.shape, F32)
    for n in range(nb):
        col = jnp.mean(k_ref[:, n * MOBA_BLOCK:(n + 1) * MOBA_BLOCK], axis=1, keepdims=True)
        acc = jnp.where(lane == n, col, acc)
    o_ref[...] = acc


def _block_means(kf_all, layer):
    _, b, _, t = kf_all.shape
    nb = t // MOBA_BLOCK
    assert nb <= LANES
    return pl.pallas_call(
        functools.partial(_means_kernel, nb=nb),
        grid=(b,),
        in_specs=[pl.BlockSpec((None, None, MIX_WIDTH, t), lambda i: (layer, i, 0, 0))],
        out_specs=pl.BlockSpec((None, MIX_WIDTH, LANES), lambda i: (i, 0, 0)),
        out_shape=jax.ShapeDtypeStruct((b, MIX_WIDTH, LANES), F32),
        compiler_params=_params("parallel"),
        name="moba_block_means",
    )(kf_all)


def _bias_kernel(tab_ref, idxp_ref, idxs_ref, bp_ref, bs_ref, *, dec_seq):
    idxp = idxp_ref[...]
    idxs = idxs_ref[...]
    row = lax.broadcasted_iota(jnp.int32, idxs.shape, 1)
    for h in range(2 * HPM):
        acc = jnp.zeros(idxp.shape, F32)
        for b in range(N_BUCKETS):
            acc = jnp.where(idxp == b, tab_ref[h, b] * LOG2E, acc)
        bp_ref[h] = acc
    for m in range(2):
        acc = jnp.zeros(idxs.shape, F32)
        for hl in range(HPM):
            for b in range(N_BUCKETS):
                acc = jnp.where((idxs == b) & (row // dec_seq == hl), tab_ref[m * HPM + hl, b], acc)
        bs_ref[m] = acc


def _bias_tiles(tab, idxp, idxs, dec_seq):
    return pl.pallas_call(
        functools.partial(_bias_kernel, dec_seq=dec_seq),
        in_specs=[pl.BlockSpec(memory_space=pltpu.SMEM), pl.BlockSpec(memory_space=pltpu.VMEM),
                  pl.BlockSpec(memory_space=pltpu.VMEM)],
        out_specs=[pl.BlockSpec(memory_space=pltpu.VMEM), pl.BlockSpec(memory_space=pltpu.VMEM)],
        out_shape=[jax.ShapeDtypeStruct((2 * HPM,) + idxp.shape, F32),
                   jax.ShapeDtypeStruct((2,) + idxs.shape, F32)],
        compiler_params=pltpu.CompilerParams(vmem_limit_bytes=VMEM_LIMIT),
        name="t5_bias_tiles",
    )(tab, idxp, idxs)


ONES_ROWS = 16
ACC_ROWS = HEAD_DIM + ONES_ROWS
EXP2_DEAD = -200.0


def _rows(n):
    return lax.broadcasted_iota(jnp.int32, (n, 1), 0)


def _key_before_query(tk, tq, strict):
    s = lax.broadcasted_iota(jnp.int32, (tk, tq), 0)
    t = lax.broadcasted_iota(jnp.int32, (tk, tq), 1)
    return (s < t) if strict else (s <= t)


def _row_range(qt, lo, hi):
    r = _rows(qt.shape[0])
    return jnp.where((r >= lo) & (r < hi), qt, jnp.zeros_like(qt))


def _v_aug(vt, x):
    return jnp.concatenate([vt[x * HEAD_DIM:(x + 1) * HEAD_DIM, :],
                            jnp.ones((ONES_ROWS, vt.shape[1]), BF16)], axis=0)


def _extra_rows(by_row, n, width):
    r = _rows(n)
    out = jnp.zeros((n, width), F32)
    for row, val in by_row.items():
        out = jnp.where(r == row, val, out)
    return out


def _pieces(by_row_f32, base):
    out = {}
    for i, val in enumerate(by_row_f32):
        for j, piece in enumerate(_split3(val)):
            out[base + 3 * i + j] = piece.astype(F32)
    return out


def _softmax_step(i, s, va, m_sc, acc_sc, sel=None):
    m_old = m_sc[i]
    bm = jnp.max(s, axis=0, keepdims=True)
    if sel is not None:
        bm = jnp.where(sel, bm, NEG)
    m_new = jnp.maximum(m_old, bm)
    m_sub = m_new if sel is None else jnp.where(sel, m_new, -NEG)
    p = jnp.exp2(s - m_sub)
    acc_sc[i] = acc_sc[i] * jnp.exp2(m_old - m_new) + _nn(va, p.astype(BF16))
    m_sc[i] = m_new


def _normalized(acc_sc, i):
    a = acc_sc[i]
    return a[0:HEAD_DIM, :] / a[HEAD_DIM:HEAD_DIM + 1, :]


def _init(m_sc, acc_sc):
    m_sc[...] = jnp.full(m_sc.shape, NEG, F32)
    acc_sc[...] = jnp.zeros(acc_sc.shape, F32)


def _sweep(qi, group, near, far, alive=None, pin=None):
    nf = jnp.maximum(qi - 1, 0)

    def rest(pinned):
        @pl.when(qi == 0)
        def _():
            near(1, pinned)

        @pl.when(qi > 0)
        def _():
            near(2, pinned)

        if alive is None:
            def body(i, c):
                top = nf - 1 - group * i
                far([top - g for g in range(group)], pinned)
                return c

            lax.fori_loop(0, nf // group, body, 0)
            ok = None
        else:
            def body(c):
                i, _ = c
                top = nf - 1 - group * i
                far([top - g for g in range(group)], pinned)
                return i + 1, alive(top - group + 1)

            _, ok = lax.while_loop(lambda c: (c[0] < nf // group) & c[1], body, (0, alive(nf)))
        rem = nf % group
        size = group // 2
        while size >= 1:
            top = (rem & (2 * size - 1)) - 1
            cond = (rem & size) != 0
            if ok is not None:
                cond = cond & ok

            @pl.when(cond)
            def _(size=size, top=top):
                far([top - g for g in range(size)], pinned)

            if ok is not None and size > 1:
                ok = ok & alive(rem & (size - 1))
            size //= 2

    if pin is None:
        rest(False)
    else:
        pinned = pin()

        @pl.when(pinned)
        def _():
            rest(True)

        @pl.when(jnp.logical_not(pinned))
        def _():
            rest(False)


PIN_HEADROOM = 100.0
PIN_MAX_GAP = 220.0


def _pin_reference(bounds, lows, m_sc):
    gap = bounds[0] - lows[0]
    for i in range(1, len(bounds)):
        gap = jnp.maximum(gap, bounds[i] - lows[i])
    pinned = jnp.max(gap) <= PIN_MAX_GAP

    @pl.when(pinned)
    def _():
        for i, (bound, low) in enumerate(zip(bounds, lows)):
            m_sc[i] = jnp.maximum(low, bound - PIN_HEADROOM)

    return pinned


def _own_key_scores(qt_ref, kd_ref, rows_per_map):
    prod = qt_ref[...].astype(F32) * kd_ref[...].astype(BF16).astype(F32)
    tq = prod.shape[1]
    z = jnp.sum(prod.reshape(MIX_WIDTH // rows_per_map, rows_per_map, tq), axis=1, keepdims=True)
    return z - 0.01


def _softmax_blocks(items, scores, v_of, m_sc, acc_sc, tq, pinned=False):
    ss = [scores(j, kind, i) for (j, kind, i, x, sel) in items]
    if not pinned:
        for s, (j, kind, i, x, sel) in zip(ss, items):
            if kind == "diag":
                s = jnp.where(_key_before_query(tq, tq, False), s, NEG)
            _softmax_step(i, s, _v_aug(v_of(j), x), m_sc, acc_sc, sel(j, x) if sel is not None else None)
        return
    pvs = {}
    for s, (j, kind, i, x, sel) in zip(ss, items):
        if kind == "diag":
            s = jnp.where(_key_before_query(tq, tq, False), s, NEG)
        ref = m_sc[i]
        if sel is not None:
            ref = jnp.where(sel(j, x), ref, -NEG)
        pv = _nn(_v_aug(v_of(j), x), jnp.exp2(s - ref).astype(BF16))
        pvs[i] = pv if i not in pvs else pvs[i] + pv
    for i, pv in pvs.items():
        acc_sc[i] = acc_sc[i] + pv


def _max_key_norm(kn2_ref, h):
    return jnp.sqrt(jnp.max(kn2_ref[:, h, :], axis=0, keepdims=True)[:, 0:1])


def _score_bound(q, kn, bias_max):
    qn = jnp.sqrt(jnp.sum(jnp.square(q.astype(F32)), axis=0, keepdims=True))
    return qn * kn * 1.001 + (bias_max + 0.01)


def _table_max(tab_ref, h):
    m = tab_ref[h, 0]
    for b in range(1, N_BUCKETS):
        m = jnp.maximum(m, tab_ref[h, b])
    return m * LOG2E


def _fox_kernel(qt_ref, k_ref, vt_ref, kd_ref, kn2_ref, cq_ref, call_ref, cend_ref, o_ref, kx_sc, m_sc, acc_sc,
                *, tq):
    qi = pl.program_id(1)

    def head_row(c, h):
        return c[h:h + 1, :]

    @pl.when(qi == 0)
    def _():
        c = call_ref[...] * LOG2E
        for p in range(HPM // 2):
            by_row = _pieces([head_row(c, 2 * p), head_row(c, 2 * p + 1)], 0)
            by_row.update({6: 1.0, 7: 1.0, 8: 1.0})
            e = _extra_rows(by_row, 16, c.shape[1])
            e = jnp.concatenate([e, jnp.zeros((PAIR - 16, c.shape[1]), F32)], axis=0)
            kx_sc[p] = e.T.astype(BF16)

    cq = cq_ref[...] * LOG2E
    qps, cts, zbs = [], [], []
    for h in range(HPM):
        p, x = divmod(h, 2)
        ct = head_row(cq, h)
        by_row = {3 * x: -1.0, 3 * x + 1: -1.0, 3 * x + 2: -1.0}
        by_row.update(_pieces([ct], 6))
        qx = _extra_rows(by_row, PAIR, tq).astype(BF16)
        qh = _row_range(qt_ref[p * PAIR:(p + 1) * PAIR, :], x * HEAD_DIM, (x + 1) * HEAD_DIM)
        qps.append(jnp.concatenate([qh, qx], axis=0))
        cts.append(ct)
        qn = jnp.sqrt(jnp.sum(jnp.square(qh.astype(F32)), axis=0, keepdims=True))
        zbs.append(qn * _max_key_norm(kn2_ref, h) * 1.001 + 0.001)
    _init(m_sc, acc_sc)

    def alive(low):
        lane = lax.broadcasted_iota(jnp.int32, cend_ref.shape, 1)
        cl = jnp.sum(jnp.where(lane == low - 1, cend_ref[...] * LOG2E, 0.0), axis=1, keepdims=True)
        worst = zbs[0] + cts[0] - head_row(cl, 0) - m_sc[0]
        for h in range(1, HPM):
            worst = jnp.maximum(worst, zbs[h] + cts[h] - head_row(cl, h) - m_sc[h])
        return jnp.max(worst) > EXP2_DEAD

    def scores(j, kind, h):
        r0 = pl.multiple_of(j * tq, tq)
        p = h // 2
        kp = jnp.concatenate([k_ref[pl.ds(r0, tq), p * PAIR:(p + 1) * PAIR], kx_sc[p, pl.ds(r0, tq), :]], axis=1)
        return _nn(kp, qps[h])

    def run(blocks, pinned=False):
        items = [(j, kind, h, h, None) for j, kind in blocks for h in range(HPM)]
        _softmax_blocks(items, scores, lambda j: vt_ref[j], m_sc, acc_sc, tq, pinned)

    lows = _own_key_scores(qt_ref, kd_ref, HEAD_DIM)
    _sweep(qi, 4,
           lambda n, pinned: run([(qi, "diag")] + ([(qi - 1, "far")] if n == 2 else []), pinned),
           lambda js, pinned: run([(j, "far") for j in js], pinned), alive,
           lambda: _pin_reference([zb + 0.01 for zb in zbs], [lows[h] for h in range(HPM)], m_sc))
    o = jnp.concatenate([_normalized(acc_sc, h) for h in range(HPM)], axis=0)
    o_ref[...] = o.T.astype(o_ref.dtype)


def _moba_kernel(tab_ref, qt_ref, k_ref, vt_ref, kd_ref, kn2_ref, mean_ref, bt_ref, o_ref, m_sc, acc_sc, sel_sc,
                 *, tq, nb):
    qi = pl.program_id(1)
    ones_k = jnp.ones((tq, PAIR), BF16)
    nbp = sel_sc.shape[1]
    rb = _rows(nbp)
    qs, qps, bounds = [], [], []
    means = [_split2(mean_ref[p * PAIR:(p + 1) * PAIR, :].T) for p in range(HPM // 2)]
    for h in range(HPM):
        p, x = divmod(h, 2)
        qx = _row_range(qt_ref[p * PAIR:(p + 1) * PAIR, :], x * HEAD_DIM, (x + 1) * HEAD_DIM)
        qs.append(qx)
        bounds.append(_score_bound(qx, _max_key_norm(kn2_ref, h), _table_max(tab_ref, h)))
        far = jnp.full((1, tq), tab_ref[h, N_BUCKETS - 1] * LOG2E, F32)
        qps.append(jnp.concatenate([qx, _extra_rows(_pieces([far], 0), PAIR, tq).astype(BF16)], axis=0))
    gates = [(_nn(means[h // 2][0], qs[h]) + _nn(means[h // 2][1], qs[h]))[0:nbp, :] for h in range(HPM)]
    for h, gate in enumerate(gates):
        cnt = jnp.zeros((nbp, tq), jnp.int32)
        for n in range(nb):
            gn = gate[n:n + 1, :]
            beats = (gn > gate) | ((gn == gate) & (n < rb))
            cnt = cnt + jnp.where(beats, jnp.where(n < qi, 1, 0), 0)
        sel_sc[h] = jnp.where((cnt < MOBA_TOPK) & (rb < qi), 1.0, 0.0)
    _init(m_sc, acc_sc)

    def scores(j, kind, h):
        p = h // 2
        k = k_ref[pl.ds(pl.multiple_of(j * tq, tq), tq), p * PAIR:(p + 1) * PAIR]
        if kind == "far":
            return _nn(jnp.concatenate([k, ones_k], axis=1), qps[h])
        return _nn(k, qs[h]) + bt_ref[h, 0 if kind == "diag" else 1]

    def run(blocks, pinned=False):
        items = [(j, kind, h, h, None if kind == "diag" else (lambda j, h: sel_sc[h, pl.ds(j, 1), :] > 0.0))
                 for j, kind in blocks for h in range(HPM)]
        _softmax_blocks(items, scores, lambda j: vt_ref[j], m_sc, acc_sc, tq, pinned)

    own = _own_key_scores(qt_ref, kd_ref, HEAD_DIM)
    lows = [own[h] + tab_ref[h, 0] * LOG2E for h in range(HPM)]
    _sweep(qi, 4,
           lambda n, pinned: run([(qi, "diag")] + ([(qi - 1, "sub")] if n == 2 else []), pinned),
           lambda js, pinned: run([(j, "far") for j in js], pinned),
           pin=lambda: _pin_reference(bounds, lows, m_sc))
    o = jnp.concatenate([_normalized(acc_sc, h) for h in range(HPM)], axis=0)
    o_ref[...] = o.T.astype(o_ref.dtype)


def _sb_kernel(qt_ref, k_ref, vt_ref, su_ref, o_ref, r_sc, acc_sc, *, tq):
    qi = pl.program_id(1)
    qs = [_row_range(qt_ref[(h // 2) * PAIR:(h // 2 + 1) * PAIR, :], (h % 2) * HEAD_DIM, (h % 2 + 1) * HEAD_DIM)
          for h in range(HPM)]
    su = su_ref[...]
    r_sc[...] = jnp.zeros(r_sc.shape, F32)
    acc_sc[...] = jnp.zeros(acc_sc.shape, F32)

    def run(blocks):
        tiles = [(j, masked, h) for j, masked in blocks for h in range(HPM)]
        zs = [_nn(k_ref[pl.ds(pl.multiple_of(j * tq, tq), tq), (h // 2) * PAIR:(h // 2 + 1) * PAIR], qs[h])
              for j, masked, h in tiles]
        lbs, lats = [], []
        for z, (j, masked, h) in zip(zs, tiles):
            log_beta = jnp.minimum(z, 0.0) - jnp.log2(1.0 + jnp.exp2(-jnp.abs(z)))
            log_keep = log_beta - z
            if masked:
                log_keep = jnp.where(_key_before_query(tq, tq, True), log_keep, 0.0)
            lbs.append(log_beta)
            lats.append(_nn(su, log_keep.astype(BF16)))
        r = [r_sc[h] for h in range(HPM)]
        pvs = [jnp.zeros((HEAD_DIM, tq), F32)] * HPM
        for log_beta, lat, (j, masked, h) in zip(lbs, lats, tiles):
            a = jnp.exp2(log_beta + lat[0:tq, :] + r[h])
            if masked:
                a = jnp.where(_key_before_query(tq, tq, True), a, 0.0)
            r[h] = r[h] + lat[tq:tq + 1, :]
            pvs[h] = pvs[h] + _nn(vt_ref[j][h * HEAD_DIM:(h + 1) * HEAD_DIM, :], a.astype(BF16))
        for h in range(HPM):
            r_sc[h] = r[h]
            acc_sc[h] = acc_sc[h] + pvs[h]

    def alive(low):
        return jnp.max(r_sc[...]) > EXP2_DEAD

    _sweep(qi, 2,
           lambda n, pinned: run([(qi, True)] + ([(qi - 1, False)] if n == 2 else [])),
           lambda js, pinned: run([(j, False) for j in js]), alive)
    o = jnp.concatenate([acc_sc[h] for h in range(HPM)], axis=0)
    o_ref[...] = o.T.astype(o_ref.dtype)


def _lam_value(lam_ref, lam_init):
    lm = lam_ref[...]
    a = jnp.sum(lm[0:1] * lm[1:2], axis=-1, keepdims=True)
    b = jnp.sum(lm[2:3] * lm[3:4], axis=-1, keepdims=True)
    return jnp.exp(a) - jnp.exp(b) + lam_init


def _diff_kernel(tab_ref, qt_ref, k_ref, vt_ref, kd_ref, kn2_ref, bt_ref, lam_ref, gsub_ref, o_ref, m_sc, acc_sc,
                 *, tq, lam_init):
    qi = pl.program_id(1)
    ones_k = jnp.ones((tq, PAIR), BF16)
    qs, qps, bounds = [], [], []
    for i in range(2 * HPM):
        p, r = divmod(i * DIFF_HALF, PAIR)
        qx = _row_range(qt_ref[p * PAIR:(p + 1) * PAIR, :], r, r + DIFF_HALF)
        qs.append(qx)
        bounds.append(_score_bound(qx, _max_key_norm(kn2_ref, i // 2), _table_max(tab_ref, HPM + i // 2)))
        far = jnp.full((1, tq), tab_ref[HPM + i // 2, N_BUCKETS - 1] * LOG2E, F32)
        qps.append(jnp.concatenate([qx, _extra_rows(_pieces([far], 0), PAIR, tq).astype(BF16)], axis=0))
    _init(m_sc, acc_sc)

    def scores(j, kind, i):
        p = i // 4
        k = k_ref[pl.ds(pl.multiple_of(j * tq, tq), tq), p * PAIR:(p + 1) * PAIR]
        if kind == "far":
            return _nn(jnp.concatenate([k, ones_k], axis=1), qps[i])
        return _nn(k, qs[i]) + bt_ref[i // 2, 0 if kind == "diag" else 1]

    def run(blocks, pinned=False):
        items = [(j, kind, i, i // 2, None) for j, kind in blocks for i in range(2 * HPM)]
        _softmax_blocks(items, scores, lambda j: vt_ref[j], m_sc, acc_sc, tq, pinned)

    own = _own_key_scores(qt_ref, kd_ref, DIFF_HALF)
    lows = [own[i] + tab_ref[HPM + i // 2, 0] * LOG2E for i in range(2 * HPM)]
    _sweep(qi, 2,
           lambda n, pinned: run([(qi, "diag")] + ([(qi - 1, "sub")] if n == 2 else []), pinned),
           lambda js, pinned: run([(j, "far") for j in js], pinned),
           pin=lambda: _pin_reference(bounds, lows, m_sc))
    lam_val = _lam_value(lam_ref, lam_init)
    gcol = jnp.concatenate([gsub_ref[...]] * (tq // LANES), axis=1)
    outs = []
    for x in range(HPM):
        o = _normalized(acc_sc, 2 * x) - lam_val * _normalized(acc_sc, 2 * x + 1)
        inv = lax.rsqrt(jnp.mean(o * o, axis=0, keepdims=True) + EPS)
        outs.append(((o * inv) * gcol) * (1.0 - lam_init))
    o_ref[...] = jnp.concatenate(outs, axis=0).T.astype(o_ref.dtype)


def _mixer_specs(t, tq, mixer, layer, kn2_tiles):
    nk = t // tq
    qspec = pl.BlockSpec((None, MIX_WIDTH, tq), lambda b, qi: (b, mixer, qi))
    kspec = pl.BlockSpec((None, t, MIX_WIDTH), lambda b, qi: (b, 0, mixer))
    vspec = pl.BlockSpec((None, nk, MIX_WIDTH, tq), lambda b, qi: (b, 0, mixer, 0))
    kdspec = pl.BlockSpec((None, None, MIX_WIDTH, tq), lambda b, qi: (layer, b, mixer, qi))
    knspec = pl.BlockSpec((None, kn2_tiles, None, HPM, LANES), lambda b, qi: (b, 0, mixer, 0, 0))
    ospec = pl.BlockSpec((None, tq, MIX_WIDTH), lambda b, qi: (b, qi, 0))
    return [qspec, kspec, vspec, kdspec, knspec], ospec


def _attn_out(b, t):
    return jax.ShapeDtypeStruct((b, t, MIX_WIDTH), BF16)


def _stat_scratch(n_maps, tq, rows):
    return [pltpu.VMEM((n_maps, 1, tq), F32), pltpu.VMEM((n_maps, rows, tq), F32)]


def _fox_prompt(qkv, layer, c, tq):
    b, t, _ = qkv[1].shape
    specs, ospec = _mixer_specs(t, tq, 1, layer, qkv[4].shape[1])
    cq = pl.BlockSpec((None, 8, tq), lambda bi, qi: (bi, 0, qi))
    call = pl.BlockSpec((None, 8, t), lambda bi, qi: (bi, 0, 0))
    cend = c[:, :, tq - 1::tq]
    cends = pl.BlockSpec((None, 8, t // tq), lambda bi, qi: (bi, 0, 0))
    return pl.pallas_call(
        functools.partial(_fox_kernel, tq=tq),
        grid=(b, t // tq),
        in_specs=specs + [cq, call, cends],
        out_specs=ospec,
        out_shape=_attn_out(b, t),
        scratch_shapes=[pltpu.VMEM((HPM // 2, t, PAIR), BF16)] + _stat_scratch(HPM, tq, ACC_ROWS),
        compiler_params=_params("parallel", "arbitrary"),
        name="fox_prompt",
    )(*qkv, c, c, cend)


def _moba_prompt(tab, qkv, layer, means, bp, tq):
    b, t, _ = qkv[1].shape
    nb = t // MOBA_BLOCK
    nbp = -(-nb // 8) * 8
    specs, ospec = _mixer_specs(t, tq, 0, layer, qkv[4].shape[1])
    mspec = pl.BlockSpec((None, MIX_WIDTH, LANES), lambda bi, qi: (bi, 0, 0))
    bt = pl.BlockSpec((HPM, 2, tq, tq), lambda bi, qi: (0, 0, 0, 0))
    return pl.pallas_call(
        functools.partial(_moba_kernel, tq=tq, nb=nb),
        grid=(b, t // tq),
        in_specs=[pl.BlockSpec(memory_space=pltpu.SMEM)] + specs + [mspec, bt],
        out_specs=ospec,
        out_shape=_attn_out(b, t),
        scratch_shapes=_stat_scratch(HPM, tq, ACC_ROWS) + [pltpu.VMEM((HPM, nbp, tq), F32)],
        compiler_params=_params("parallel", "arbitrary"),
        name="moba_prompt",
    )(tab, *qkv, means, bp)


def _sb_prompt(qkv, layer, su, tq):
    b, t, _ = qkv[1].shape
    specs, ospec = _mixer_specs(t, tq, 2, layer, qkv[4].shape[1])
    return pl.pallas_call(
        functools.partial(_sb_kernel, tq=tq),
        grid=(b, t // tq),
        in_specs=specs[:3] + [_const_spec(su.shape)],
        out_specs=ospec,
        out_shape=_attn_out(b, t),
        scratch_shapes=_stat_scratch(HPM, tq, HEAD_DIM),
        compiler_params=_params("parallel", "arbitrary"),
        name="sb_prompt",
    )(*qkv[:3], su)


def _diff_prompt(tab, qkv, layer, bp, lam, gsub_col, lam_init, tq):
    b, t, _ = qkv[1].shape
    specs, ospec = _mixer_specs(t, tq, 3, layer, qkv[4].shape[1])
    bt = pl.BlockSpec((HPM, 2, tq, tq), lambda bi, qi: (1, 0, 0, 0))
    return pl.pallas_call(
        functools.partial(_diff_kernel, tq=tq, lam_init=lam_init),
        grid=(b, t // tq),
        in_specs=[pl.BlockSpec(memory_space=pltpu.SMEM)] + specs
        + [bt, _const_spec((4, DIFF_HALF)), _const_spec((HEAD_DIM, LANES))],
        out_specs=ospec,
        out_shape=_attn_out(b, t),
        scratch_shapes=_stat_scratch(2 * HPM, tq, ACC_ROWS),
        compiler_params=_params("parallel", "arbitrary"),
        name="diff_prompt",
    )(tab, *qkv, bp, lam, gsub_col)


N_SLOTS = 20


def _sample_masks(dec_seq):
    rows = N_SLOTS * dec_seq
    qmask = np.zeros((rows, D_QKV), np.float32)
    for r in range(rows):
        slot = r // dec_seq
        if slot < 3 * HPM:
            qmask[r, slot * HEAD_DIM:(slot + 1) * HEAD_DIM] = 1.0
        elif slot < 4 * HPM:
            qmask[r, slot * HEAD_DIM:slot * HEAD_DIM + DIFF_HALF] = 1.0
        else:
            c0 = (slot - HPM) * HEAD_DIM + DIFF_HALF
            qmask[r, c0:c0 + DIFF_HALF] = 1.0
    omask = np.zeros((N_HEADS * dec_seq, D_QKV), np.float32)
    for r in range(N_HEADS * dec_seq):
        omask[r, (r // dec_seq) * HEAD_DIM:(r // dec_seq + 1) * HEAD_DIM] = 1.0
    return qmask, omask


def _expand_rows(a, dec_seq):
    rows = HPM * dec_seq
    r = lax.broadcasted_iota(jnp.int32, (rows, 1), 0)
    out = jnp.broadcast_to(a[HPM - 1:HPM, :], (rows, a.shape[1]))
    for h in range(HPM - 2, -1, -1):
        out = jnp.where(r < (h + 1) * dec_seq, a[h:h + 1, :], out)
    return out


def _softmax_pages(s_pages):
    m = s_pages[0].max(axis=-1, keepdims=True)
    for s in s_pages[1:]:
        m = jnp.maximum(m, s.max(axis=-1, keepdims=True))
    ps = [jnp.exp(s - m) for s in s_pages]
    l = ps[0].sum(axis=-1, keepdims=True)
    for p in ps[1:]:
        l = l + p.sum(axis=-1, keepdims=True)
    return [p / l for p in ps]


def _sample_kernel(*refs, n_pages, dec_seq, lam_init):
    pt_ref = refs[0]
    del pt_ref
    (q_ref, kn_ref, vn_ref, lfn_ref, bs_ref, qmask_ref, omask_ref, sl_ref, lam_ref,
     gd1_ref, gd2_ref, gmul_ref, gadd_ref) = refs[1:14]
    k_refs = refs[14:14 + n_pages]
    v_refs = refs[14 + n_pages:14 + 2 * n_pages]
    lf_refs = refs[14 + 2 * n_pages:14 + 3 * n_pages]
    o_ref = refs[14 + 3 * n_pages]
    s_sc, p_sc = refs[14 + 3 * n_pages + 1:]

    g = HPM * dec_seq
    npg = n_pages + 1
    q = q_ref[...]
    qbd = (jnp.concatenate([q] * (N_SLOTS * dec_seq // 8), axis=0) * qmask_ref[...]).astype(BF16)
    pad = jnp.zeros((LANES - kn_ref.shape[0], D_QKV), BF16)
    for p in range(n_pages):
        s_sc[p] = _nn(qbd, k_refs[p][...].astype(BF16))
    s_sc[n_pages] = _nt(qbd, jnp.concatenate([kn_ref[...], pad], axis=0))

    lane = lax.broadcasted_iota(jnp.int32, (1, LANES), 1)
    qrow = lax.broadcasted_iota(jnp.int32, (g, 1), 0) % dec_seq
    new_ok = lane <= qrow
    new_past = lane < qrow
    sl = sl_ref[...]

    def pages(r0):
        return [s_sc[p, r0:r0 + g, :] for p in range(npg)]

    sm = pages(0)
    ppb = MOBA_BLOCK // LANES
    nblk = n_pages // ppb
    gates = []
    for n in range(nblk):
        tot = sm[n * ppb]
        for i in range(1, ppb):
            tot = tot + sm[n * ppb + i]
        gates.append(jnp.sum(tot, axis=-1, keepdims=True))
    s_pages = []
    for n in range(nblk):
        cnt = jnp.zeros((g, 1), jnp.int32)
        for n2 in range(nblk):
            if n2 == n:
                continue
            beats = (gates[n2] > gates[n]) | ((gates[n2] == gates[n]) & (n2 < n))
            cnt = cnt + jnp.where(beats, 1, 0)
        for i in range(ppb):
            p = n * ppb + i
            s_pages.append(jnp.where(cnt < MOBA_TOPK, sm[p] + bs_ref[0, p], NEG))
    s_pages.append(jnp.where(new_ok, sm[n_pages] + bs_ref[0, n_pages], NEG))
    for p, pr in enumerate(_softmax_pages(s_pages)):
        p_sc[p, 0:g, :] = pr.astype(BF16)

    lfn = lfn_ref[...]
    cn_row = jnp.zeros((g, 1), F32)
    bias_new = jnp.zeros((g, LANES), F32)
    for n in range(dec_seq):
        col = lfn[:, n:n + 1]
        cn_row = cn_row + jnp.where(n <= qrow, col, 0.0)
        bias_new = bias_new + jnp.where((lane < n) & (n <= qrow), col, 0.0)
    lfe = [_expand_rows(lf_refs[p][...], dec_seq) for p in range(n_pages)]
    suffix = _nn_split(jnp.concatenate(lfe, axis=0), sl, 3)
    sf = pages(g)
    s_pages = [None] * npg
    s_pages[n_pages] = jnp.where(new_ok, sf[n_pages] + bias_new, NEG)
    after = cn_row
    for p in range(n_pages - 1, -1, -1):
        s_pages[p] = sf[p] + (suffix[p * g:(p + 1) * g, :] + after)
        after = after + jnp.sum(lfe[p], axis=-1, keepdims=True)
    for p, pr in enumerate(_softmax_pages(s_pages)):
        p_sc[p, g:2 * g, :] = pr.astype(BF16)

    ss = pages(2 * g)
    lbs, lks = [], []
    for p in range(npg):
        lb = _log_sigmoid(ss[p])
        lk = lb - ss[p]
        if p == n_pages:
            lk = jnp.where(new_past, lk, 0.0)
        lbs.append(lb)
        lks.append(lk)
    later_in = _nn_split(jnp.concatenate(lks, axis=0), sl, 2)
    after = jnp.zeros((g, 1), F32)
    for p in range(npg - 1, -1, -1):
        a = jnp.exp(lbs[p] + later_in[p * g:(p + 1) * g, :] + after)
        if p == n_pages:
            a = jnp.where(new_past, a, 0.0)
        p_sc[p, 2 * g:3 * g, :] = a.astype(BF16)
        after = after + jnp.sum(lks[p], axis=-1, keepdims=True)

    lam_val = _lam_value(lam_ref, lam_init)
    maps = []
    for mp in range(2):
        sd = pages((3 + mp) * g)
        s_pages = [sd[p] + bs_ref[1, p] for p in range(n_pages)]
        s_pages.append(jnp.where(new_ok, sd[n_pages] + bs_ref[1, n_pages], NEG))
        maps.append(_softmax_pages(s_pages))
    for p in range(npg):
        p_sc[p, 3 * g:4 * g, :] = (maps[0][p] - lam_val * maps[1][p]).astype(BF16)

    padv = jnp.zeros((LANES - vn_ref.shape[0], D_QKV), BF16)
    acc = _nn(p_sc[n_pages], jnp.concatenate([vn_ref[...], padv], axis=0))
    for p in range(n_pages):
        acc = acc + _nt(p_sc[p], v_refs[p][...].astype(BF16))
    rm = acc * omask_ref[...]
    y = rm[0:8, :]
    for i in range(1, N_HEADS * dec_seq // 8):
        y = y + rm[8 * i:8 * (i + 1), :]
    o8 = y + pltpu.roll(y, dec_seq, 0)
    ms = _nn_split(o8 * o8, gd1_ref[...], 2)
    bc = _nn_split(lax.rsqrt(ms + EPS), gd2_ref[...], 3)
    o8 = o8 * (bc * gmul_ref[...] + gadd_ref[...])
    o_ref[...] = o8[0:dec_seq, :]


def _sample_attention(layer, page_table, q8, kn, vn, lfn, bs, consts, lam, gmul, gadd,
                      cache_k, cache_v, cache_lft, lam_init):
    db, n_pages = page_table.shape
    dec_seq = lfn.shape[1] // HPM
    page = cache_k.shape[3]
    qmask, omask, sl, gd1, gd2 = consts
    rows = N_SLOTS * dec_seq

    def batch_spec(shape):
        nd = len(shape)
        return pl.BlockSpec((None,) + shape, lambda b, pt: (b,) + (0,) * nd)

    def const(shape):
        nd = len(shape)
        return pl.BlockSpec(shape, lambda b, pt: (0,) * nd)

    def paged(width_rows, width_cols):
        return [pl.BlockSpec((None, None, width_rows, width_cols),
                             functools.partial(lambda b, pt, j: (layer, pt[b, j], 0, 0), j=j))
                for j in range(n_pages)]

    in_specs = ([batch_spec((8, D_QKV)), batch_spec(kn.shape[1:]), batch_spec(vn.shape[1:]),
                 batch_spec(lfn.shape[1:]), const(bs.shape), const(qmask.shape), const(omask.shape),
                 const(sl.shape), const(lam.shape), const(gd1.shape), const(gd2.shape),
                 const(gmul.shape), const(gadd.shape)]
                + paged(D_QKV, page) + paged(D_QKV, page) + paged(HPM, page))
    grid_spec = pltpu.PrefetchScalarGridSpec(
        num_scalar_prefetch=1,
        grid=(db,),
        in_specs=in_specs,
        out_specs=pl.BlockSpec((None, dec_seq, D_QKV), lambda b, pt: (b, 0, 0)),
        scratch_shapes=[pltpu.VMEM((n_pages + 1, rows, LANES), F32),
                        pltpu.VMEM((n_pages + 1, N_HEADS * dec_seq, LANES), BF16)],
    )
    return pl.pallas_call(
        functools.partial(_sample_kernel, n_pages=n_pages, dec_seq=dec_seq, lam_init=lam_init),
        grid_spec=grid_spec,
        out_shape=jax.ShapeDtypeStruct((db, dec_seq, D_QKV), F32),
        compiler_params=_params("arbitrary"),
        name="sample_attention",
    )(page_table, q8, kn, vn, lfn, bs, qmask, omask, sl, lam, gd1, gd2, gmul, gadd,
      *([cache_k] * n_pages), *([cache_v] * n_pages), *([cache_lft] * n_pages))


MLP_CHUNK = 1024


def _merge_mlp_kernel(x_ref, o0_ref, o1_ref, o2_ref, o3_ref, gmix_ref, wg_ref, wbr_ref, wo_ref,
                      gmlp_ref, wup_ref, wdn_ref, y_ref):
    x = x_ref[...]
    h = (x * lax.rsqrt(jnp.mean(x * x, axis=-1, keepdims=True) + EPS)) * gmix_ref[...]
    hb = h.astype(BF16)
    acc = jnp.zeros(x.shape, F32)
    for m, o_ref in enumerate((o0_ref, o1_ref, o2_ref, o3_ref)):
        gate = 1.0 / (1.0 + jnp.exp(-_nt(hb, wg_ref[m * D_MODEL:(m + 1) * D_MODEL, :])))
        acc = acc + gate * _nn(o_ref[...].astype(BF16), wbr_ref[m])
    y = x + _nn(acc.astype(BF16), wo_ref[...])
    hm = ((y * lax.rsqrt(jnp.mean(y * y, axis=-1, keepdims=True) + EPS)) * gmlp_ref[...]).astype(BF16)
    for c in range(D_FF // MLP_CHUNK):
        u = jnp.maximum(_nn(hm, wup_ref[:, c * MLP_CHUNK:(c + 1) * MLP_CHUNK]), 0.0)
        y = y + _nn((u * u).astype(BF16), wdn_ref[c * MLP_CHUNK:(c + 1) * MLP_CHUNK, :])
    y_ref[...] = y


def _resident_spec(shape):
    nd = len(shape)
    return pl.BlockSpec(shape, lambda *_: (0,) * nd, pipeline_mode=pl.Buffered(1))


def _merge_mlp(x, os_, gmix, wg, wbr, wo, gmlp, wup, wdn, tm):
    rows = x.shape[0]
    row = lambda w: pl.BlockSpec((tm, w), lambda i: (i, 0))
    return pl.pallas_call(
        _merge_mlp_kernel,
        grid=(rows // tm,),
        in_specs=[row(D_MODEL)] + [row(MIX_WIDTH)] * 4
        + [_const_spec((1, D_MODEL)), _resident_spec((N_MIXERS * D_MODEL, D_MODEL)),
           _resident_spec((N_MIXERS, MIX_WIDTH, D_MODEL)), _resident_spec((D_MODEL, D_MODEL)),
           _const_spec((1, D_MODEL)), _resident_spec((D_MODEL, D_FF)), _resident_spec((D_FF, D_MODEL))],
        out_specs=row(D_MODEL),
        out_shape=jax.ShapeDtypeStruct((rows, D_MODEL), F32),
        compiler_params=_params("parallel"),
        name="merge_mlp",
    )(x, *os_, gmix, wg, wbr, wo, gmlp, wup, wdn)


def _qk_vectors(gq, gk, qscale):
    z = jnp.zeros((MIX_WIDTH,), F32)
    one = jnp.ones((MIX_WIDTH,), F32)
    t4 = lambda a: jnp.tile(a, HPM)
    sc = HEAD_DIM ** -0.5 * qscale
    qmul = jnp.concatenate([t4(gq[0]) * sc, t4(gq[1]) * sc, z, t4(gq[2]) * (DIFF_HALF ** -0.5 * qscale)])
    qadd = jnp.concatenate([z, z, one * sc, z])
    kmul = jnp.concatenate([t4(gk[0]), t4(gk[1]), z, t4(gk[2])])
    kadd = jnp.concatenate([z, z, one, z])
    return [a.reshape(1, D_QKV) for a in (qmul, qadd, kmul, kadd)]


def _bias_index_tiles(tq, n_pages, page, dec_seq):
    a = np.arange(tq)
    diag = _t5_bucket_np(a[None, :] - a[:, None])
    sub = _t5_bucket_np(tq + a[None, :] - a[:, None])
    idxp = np.stack([diag, sub]).astype(np.int32)
    past_len = n_pages * page
    qpos = past_len + (np.arange(HPM * dec_seq) % dec_seq)
    kpos = np.concatenate([np.arange(past_len), past_len + np.arange(page)])
    idxs = _t5_bucket_np(qpos[None, :, None] - kpos.reshape(n_pages + 1, 1, page))
    return idxp, idxs.astype(np.int32)


def _layer(l, x, past, w, consts, tq, tm, kv_all=None):
    (tab, bp, bs, g1, g2, su_q, tri_c, sl_p, sample_consts, gd) = consts
    (b_forget, g_q, g_k, lam, g_sub, g_mix, g_mlp, dense) = w
    wt, wg, wbr, wo, wup, wdn = dense[l]
    depth = len(dense)
    b, t, _ = x.shape
    rows = b * t
    lam_init = 0.8 - 0.6 * math.exp(-0.3 * l)
    wqkv = wf = wt
    gmix = g_mix[l].reshape(1, D_MODEL)
    qk_vecs = _qk_vectors(g_q[l], g_k[l], LOG2E if past is None else 1.0)
    x2 = x.reshape(rows, D_MODEL)

    if past is None:
        cols = [jnp.broadcast_to(a.reshape(D_QKV, 1), (D_QKV, LANES)) for a in qk_vecs]
        bf_col = jnp.broadcast_to(jnp.pad(b_forget[l], (0, 8 - HPM))[:, None], (8, LANES))
        qt, kb3, vt, k_new, v_new, lf_t, kn2 = _inproj_t(x, gmix, wqkv, wf, bf_col, *cols, tm=tm, tq=tq,
                                                         layer=l, depth=depth, kv_all=kv_all)
        kn2 = kn2.reshape(b, t // tm, N_MIXERS, HPM, LANES)
        lf_new = jnp.swapaxes(lf_t[:, :HPM], 1, 2)
        c = _cumsum(lf_t, tri_c)
        means = _block_means(k_new, l)
        gsub_col = jnp.broadcast_to(g_sub[l][:, None], (HEAD_DIM, LANES))
        qkv = (qt, kb3, vt, k_new, kn2)
        o_moba = _moba_prompt(tab, qkv, l, means, bp, tq)
        o_fox = _fox_prompt(qkv, l, c, tq)
        o_sb = _sb_prompt(qkv, l, su_q, tq)
        o_dif = _diff_prompt(tab, qkv, l, bp, lam[l], gsub_col, lam_init, tq)
        outs = [a.reshape(rows, MIX_WIDTH) for a in (o_moba, o_fox, o_sb, o_dif)]
    else:
        cache_k, cache_v, cache_lft, page_table = past
        bf = jnp.pad(b_forget[l], (0, LANES - HPM)).reshape(1, LANES)
        qb, kb, vb, kf, vf, lf = _inproj(x2, gmix, wqkv, wf, bf, g1, g2, *qk_vecs, tm=min(tm, rows))
        k_new, v_new = kf.reshape(b, t, N_HEADS, HEAD_DIM), vf.reshape(b, t, N_HEADS, HEAD_DIM)
        lf_new = lf.reshape(b, t, HPM)
        q4 = qb.astype(F32).reshape(b, t, D_QKV)
        q8 = jnp.concatenate([q4] * (8 // t), axis=1)
        padn = ((0, 0), (0, 16 - t), (0, 0))
        kn = jnp.pad(kb.reshape(b, t, D_QKV), padn)
        vn = jnp.pad(vb.reshape(b, t, D_QKV), padn)
        lfn = jnp.swapaxes(lf.reshape(b, t, HPM), 1, 2)
        lfn = jnp.pad(jnp.repeat(lfn, t, axis=1), ((0, 0), (0, 0), (0, 8 - t)))
        gd1, gd2 = gd
        z3 = jnp.zeros((3 * MIX_WIDTH,), F32)
        gmul = jnp.concatenate([z3, jnp.tile(g_sub[l], HPM) * (1.0 - lam_init)]).reshape(1, D_QKV)
        gadd = jnp.concatenate([z3 + 1.0, jnp.zeros((MIX_WIDTH,), F32)]).reshape(1, D_QKV)
        o = _sample_attention(l, page_table, q8, kn, vn, lfn, bs, (*sample_consts, sl_p, gd1, gd2),
                              lam[l], gmul, gadd, cache_k, cache_v, cache_lft, lam_init)
        o = o.reshape(rows, D_QKV)
        outs = [o[:, m * MIX_WIDTH:(m + 1) * MIX_WIDTH] for m in range(N_MIXERS)]

    y = _merge_mlp(x2, outs, gmix, wg, wbr, wo, g_mlp[l].reshape(1, D_MODEL), wup, wdn, tm=min(tm, rows))
    return y.reshape(b, t, D_MODEL), k_new, v_new, lf_new


def _constants(rel_bias, tq, n_pages, page, dec_seq):
    idxp, idxs = _bias_index_tiles(tq, n_pages, page, dec_seq)
    tab = rel_bias.T
    bp, bs = _bias_tiles(tab, jnp.asarray(idxp), jnp.asarray(idxs), dec_seq)
    g1, g2 = _qk_group_mats()
    qmask, omask = _sample_masks(dec_seq)
    gd1 = np.zeros((D_QKV, LANES), np.float32)
    gd2 = np.zeros((LANES, D_QKV), np.float32)
    for c in range(3 * MIX_WIDTH, D_QKV):
        gd1[c, (c - 3 * MIX_WIDTH) // HEAD_DIM] = 1.0 / HEAD_DIM
        gd2[(c - 3 * MIX_WIDTH) // HEAD_DIM, c] = 1.0
    su_q = np.concatenate([_strict_lower(tq).T, np.ones((ONES_ROWS, tq), np.float32)], axis=0)
    bf = lambda a: jnp.asarray(a, BF16)
    return (tab, bp, bs, bf(g1), bf(g2), bf(su_q), bf(_upper_incl(tq)), bf(_strict_lower(page)),
            (jnp.asarray(qmask), jnp.asarray(omask)), (bf(gd1), bf(gd2)))


def kernel(x_prompt, x_sample, cache_k, cache_v, cache_logf, page_table, rel_bias, w_in, b_forget,
           g_q, g_k, lam, g_sub, w_branch, w_o, g_mix, g_mlp, w_up, w_down):
    depth = w_in.shape[0]
    tq = MOBA_BLOCK
    tm = 512
    n_pages = page_table.shape[1]
    page = cache_k.shape[2]
    dec_seq = x_sample.shape[1]
    n_phys = cache_k.shape[1]
    assert (n_pages * page) % MOBA_BLOCK == 0 and 8 % dec_seq == 0 and x_prompt.shape[1] % tq == 0

    consts = _constants(rel_bias, tq, n_pages, page, dec_seq)
    dense = []
    for l in range(depth):
        wt = w_in[l].T.astype(BF16)
        dense.append((wt, wt[3 * D_QKV + HPM:], w_branch[l].astype(BF16), w_o[l].astype(BF16),
                      w_up[l].astype(BF16), w_down[l].astype(BF16)))
    w = (b_forget, g_q, g_k, lam, g_sub, g_mix, g_mlp, dense)
    to_pages = lambda c: jnp.transpose(c, (0, 1, 3, 4, 2)).reshape(depth, n_phys, D_QKV, page)
    past = (to_pages(cache_k), to_pages(cache_v), jnp.swapaxes(cache_logf, 2, 3), page_table)

    hp, hs = x_prompt, x_sample
    outs = [[] for _ in range(4)]
    kv_all = None
    for l in range(depth):
        hp, kp, vp, fp = _layer(l, hp, None, w, consts, tq, tm, kv_all)
        kv_all = (kp, vp)
        hs, ks, vs, fs = _layer(l, hs, past, w, consts, tq, tm)
        for lst, a in zip(outs, (fp, ks, vs, fs)):
            lst.append(a)
    b, t = x_prompt.shape[:2]
    to_heads = lambda a: jnp.transpose(a.reshape(depth, b, N_HEADS, HEAD_DIM, t), (0, 1, 4, 2, 3))
    fp, ks, vs, fs = (jnp.stack(a) for a in outs)
    return hp, hs, to_heads(kv_all[0]), to_heads(kv_all[1]), fp, ks, vs, fs
```

```python
import functools
import math

import numpy as np
import jax
import jax.numpy as jnp
from jax import lax
from jax.experimental import pallas as pl
from jax.experimental.pallas import tpu as pltpu

F32 = jnp.float32
BF16 = jnp.bfloat16

D_MODEL = 1024
HEAD_DIM = 64
N_MIXERS = 4
HPM = 4
N_HEADS = 16
MIX_WIDTH = HPM * HEAD_DIM
D_QKV = N_HEADS * HEAD_DIM
DIFF_HALF = HEAD_DIM // 2
D_FF = 4 * D_MODEL
MOBA_BLOCK = 256
MOBA_TOPK = 3
N_BUCKETS = 32
MAX_DISTANCE = 128
EPS = 1e-6
NEG = -1e30
LOG2E = 1.4426950408889634
LANES = 128
PAIR = 2 * HEAD_DIM
VMEM_LIMIT = 56 * 1024 * 1024


def _params(*sem):
    return pltpu.CompilerParams(dimension_semantics=sem, vmem_limit_bytes=VMEM_LIMIT)


def _nt(a, b):
    return lax.dot_general(a, b, (((1,), (1,)), ((), ())), preferred_element_type=F32)


def _nn(a, b):
    return jnp.dot(a, b, preferred_element_type=F32)


def _split2(x):
    hi = x.astype(BF16)
    lo = (x - hi.astype(F32)).astype(BF16)
    return hi, lo


def _split3(x):
    hi = x.astype(BF16)
    r = x - hi.astype(F32)
    mid = r.astype(BF16)
    lo = (r - mid.astype(F32)).astype(BF16)
    return hi, mid, lo


def _nn_split(x, w, parts):
    pieces = _split2(x) if parts == 2 else _split3(x)
    out = _nn(pieces[0], w)
    for p in pieces[1:]:
        out = out + _nn(p, w)
    return out


def _log_sigmoid(z):
    return jnp.minimum(z, 0.0) - jnp.log1p(jnp.exp(-jnp.abs(z)))


def _const_spec(shape):
    nd = len(shape)
    return pl.BlockSpec(shape, lambda *_: (0,) * nd)


def _t5_bucket_np(rel):
    n = np.maximum(rel, 0)
    max_exact = N_BUCKETS // 2
    nf = np.maximum(n, 1).astype(np.float32)
    large = max_exact + (np.log(nf / np.float32(max_exact)) / np.float32(math.log(MAX_DISTANCE / max_exact))
                         * np.float32(N_BUCKETS - max_exact)).astype(np.int32)
    return np.where(n < max_exact, n, np.minimum(large, N_BUCKETS - 1)).astype(np.int32)


def _qk_group_mats():
    g1 = np.zeros((D_QKV, LANES), np.float32)
    g2 = np.zeros((LANES, D_QKV), np.float32)
    for c in range(D_QKV):
        h = c // HEAD_DIM
        if h < 2 * HPM:
            gid, size = h, HEAD_DIM
        elif h < 3 * HPM:
            continue
        else:
            gid, size = 2 * HPM + (c - 3 * MIX_WIDTH) // DIFF_HALF, DIFF_HALF
        g1[c, gid] = 1.0 / size
        g2[gid, c] = 1.0
    return g1, g2


def _strict_lower(n):
    r = np.arange(n)
    return (r[:, None] > r[None, :]).astype(np.float32)


def _upper_incl(n):
    r = np.arange(n)
    return (r[:, None] <= r[None, :]).astype(np.float32)


_WF_SPEC = pl.BlockSpec((LANES, D_MODEL), lambda *_: (3 * D_QKV // LANES, 0))


def _inproj_kernel(x_ref, gmix_ref, wqkv_ref, wf_ref, bf_ref, g1_ref, g2_ref,
                   qmul_ref, qadd_ref, kmul_ref, kadd_ref,
                   qb_ref, kb_ref, vb_ref, kf_ref, vf_ref, lf_ref):
    x = x_ref[...]
    h = (x * lax.rsqrt(jnp.mean(x * x, axis=-1, keepdims=True) + EPS)) * gmix_ref[...]
    hb = h.astype(BF16)
    g1 = g1_ref[...]
    g2 = g2_ref[...]

    def group_norm(a, mul_ref, add_ref):
        ms = _nn_split(a * a, g1, 2)
        inv = lax.rsqrt(ms + EPS)
        bc = _nn_split(inv, g2, 3)
        return a * (bc * mul_ref[...] + add_ref[...])

    q = _nt(hb, wqkv_ref[0:D_QKV, :])
    qb_ref[...] = group_norm(q, qmul_ref, qadd_ref).astype(BF16)
    k = _nt(hb, wqkv_ref[D_QKV:2 * D_QKV, :])
    kn = group_norm(k, kmul_ref, kadd_ref)
    kf_ref[...] = kn
    kb_ref[...] = kn.astype(BF16)
    v = _nt(hb, wqkv_ref[2 * D_QKV:3 * D_QKV, :])
    vf_ref[...] = v
    vb_ref[...] = v.astype(BF16)
    z = _nt(hb, wf_ref[...]) + bf_ref[...]
    lf_ref[...] = _log_sigmoid(z)[:, 0:HPM]


def _inproj(x, gmix, wqkv, wf, bf, g1, g2, qmul, qadd, kmul, kadd, tm):
    rows = x.shape[0]
    row = lambda w: pl.BlockSpec((tm, w), lambda i: (i, 0))
    vec = _const_spec((1, D_QKV))
    return pl.pallas_call(
        _inproj_kernel,
        grid=(rows // tm,),
        in_specs=[row(D_MODEL), _const_spec((1, D_MODEL)), _const_spec((3 * D_QKV, D_MODEL)),
                  _WF_SPEC, _const_spec((1, LANES)),
                  _const_spec((D_QKV, LANES)), _const_spec((LANES, D_QKV)), vec, vec, vec, vec],
        out_specs=[row(D_QKV), row(D_QKV), row(D_QKV), row(D_QKV), row(D_QKV), row(HPM)],
        out_shape=[jax.ShapeDtypeStruct((rows, D_QKV), BF16)] * 3
        + [jax.ShapeDtypeStruct((rows, D_QKV), F32)] * 2 + [jax.ShapeDtypeStruct((rows, HPM), F32)],
        compiler_params=_params("parallel"),
        name="inproj",
    )(x, gmix, wqkv, wf, bf, g1, g2, qmul, qadd, kmul, kadd)


def _inproj_t_kernel(x_ref, gmix_ref, wqkv_ref, wf_ref, bf_ref, qmul_ref, qadd_ref, kmul_ref, kadd_ref, *refs):
    qt_ref, kb_ref, vt_ref, kf_ref, vf_ref, lf_ref, kn2_ref = refs[-7:]
    x = x_ref[...]
    tm = x.shape[0]
    h = (x * lax.rsqrt(jnp.mean(x * x, axis=-1, keepdims=True) + EPS)) * gmix_ref[...]
    hb = h.astype(BF16)
    wide = lambda ref: jnp.concatenate([ref[...]] * (tm // LANES), axis=1)
    n64, n32 = 2 * MIX_WIDTH, MIX_WIDTH

    def group_norm(a, mul_ref, add_ref):
        sq = a * a

        def inv_rms(rows0, rows1, width):
            g = sq[rows0:rows1].reshape((rows1 - rows0) // width, width, tm)
            inv = lax.rsqrt(jnp.mean(g, axis=1, keepdims=True) + EPS)
            return jnp.broadcast_to(inv, g.shape).reshape(rows1 - rows0, tm)

        scale = jnp.concatenate([inv_rms(0, n64, HEAD_DIM), jnp.zeros((D_QKV - n64 - n32, tm), F32),
                                 inv_rms(D_QKV - n32, D_QKV, DIFF_HALF)], axis=0)
        return a * (scale * wide(mul_ref) + wide(add_ref))

    q = _nt(wqkv_ref[0:D_QKV, :], hb)
    qt_ref[...] = group_norm(q, qmul_ref, qadd_ref).astype(BF16)
    k = _nt(wqkv_ref[D_QKV:2 * D_QKV, :], hb)
    kn = group_norm(k, kmul_ref, kadd_ref)
    kf_ref[...] = kn
    kb_ref[...] = kn.T.astype(BF16)
    kq = kn.astype(BF16).astype(F32)
    n2 = jnp.sum((kq * kq).reshape(N_HEADS, HEAD_DIM, tm), axis=1)
    kn2_ref[...] = jnp.broadcast_to(jnp.max(n2, axis=1, keepdims=True), (N_HEADS, LANES))
    v = _nt(wqkv_ref[2 * D_QKV:3 * D_QKV, :], hb)
    vf_ref[...] = v
    vb = v.astype(BF16)
    tq = vt_ref.shape[2]
    for i in range(tm // tq):
        vt_ref[i] = vb[:, i * tq:(i + 1) * tq]
    z = _nt(wf_ref[...], hb)[0:8, :] + wide(bf_ref)
    lf_ref[...] = _log_sigmoid(z)


def _inproj_t(x, gmix, wqkv, wf, bf_col, qmul, qadd, kmul, kadd, tm, tq, layer, depth, kv_all):
    b, t, _ = x.shape
    col = _const_spec((D_QKV, LANES))
    feat = lambda rows: pl.BlockSpec((None, rows, tm), lambda bi, i: (bi, 0, i))
    layered = pl.BlockSpec((None, None, D_QKV, tm), lambda bi, i: (layer, bi, 0, i))
    operands = [x, gmix, wqkv, wf, bf_col, qmul, qadd, kmul, kadd]
    in_specs = [pl.BlockSpec((None, tm, D_MODEL), lambda bi, i: (bi, i, 0)), _const_spec((1, D_MODEL)),
                _const_spec((3 * D_QKV, D_MODEL)), _WF_SPEC, _const_spec((8, LANES)),
                col, col, col, col]
    aliases = {}
    if kv_all is not None:
        aliases = {len(operands): 3, len(operands) + 1: 4}
        operands += list(kv_all)
        in_specs += [pl.BlockSpec(memory_space=pl.ANY)] * 2
    return pl.pallas_call(
        _inproj_t_kernel,
        grid=(b, t // tm),
        in_specs=in_specs,
        out_specs=[feat(D_QKV), pl.BlockSpec((None, tm, D_QKV), lambda bi, i: (bi, i, 0)),
                   pl.BlockSpec((None, tm // tq, D_QKV, tq), lambda bi, i: (bi, i, 0, 0)),
                   layered, layered, feat(8),
                   pl.BlockSpec((None, None, N_HEADS, LANES), lambda bi, i: (bi, i, 0, 0))],
        out_shape=[jax.ShapeDtypeStruct((b, D_QKV, t), BF16), jax.ShapeDtypeStruct((b, t, D_QKV), BF16),
                   jax.ShapeDtypeStruct((b, t // tq, D_QKV, tq), BF16),
                   jax.ShapeDtypeStruct((depth, b, D_QKV, t), F32), jax.ShapeDtypeStruct((depth, b, D_QKV, t), F32),
                   jax.ShapeDtypeStruct((b, 8, t), F32), jax.ShapeDtypeStruct((b, t // tm, N_HEADS, LANES), F32)],
        input_output_aliases=aliases,
        compiler_params=_params("parallel", "parallel"),
        name="inproj_t",
    )(*operands)


def _cumsum_kernel(lf_ref, tri_ref, c_ref, *, t, blk):
    tri = tri_ref[...]
    carry = jnp.zeros((8, 1), F32)
    for i in range(t // blk):
        cs = _nn_split(lf_ref[:, i * blk:(i + 1) * blk], tri, 3) + carry
        c_ref[:, i * blk:(i + 1) * blk] = cs
        carry = cs[:, blk - 1:blk]


def _cumsum(lf_t, tri):
    b, r, t = lf_t.shape
    blk = tri.shape[0]
    return pl.pallas_call(
        functools.partial(_cumsum_kernel, t=t, blk=blk),
        grid=(b,),
        in_specs=[pl.BlockSpec((None, r, t), lambda i: (i, 0, 0)), _const_spec((blk, blk))],
        out_specs=pl.BlockSpec((None, r, t), lambda i: (i, 0, 0)),
        out_shape=jax.ShapeDtypeStruct((b, r, t), F32),
        compiler_params=_params("parallel"),
        name="logf_cumsum",
    )(lf_t, tri)


def _means_kernel(k_ref, o_ref, *, nb):
    lane = lax.broadcasted_iota(jnp.int32, (1, LANES), 1)
    acc = jnp.zeros(o_ref.shape, F32)
    for n in range(nb):
        col = jnp.mean(k_ref[:, n * MOBA_BLOCK:(n + 1) * MOBA_BLOCK], axis=1, keepdims=True)
        acc = jnp.where(lane == n, col, acc)
    o_ref[...] = acc


def _block_means(kf_all, layer):
    _, b, _, t = kf_all.shape
    nb = t // MOBA_BLOCK
    assert nb <= LANES
    return pl.pallas_call(
        functools.partial(_means_kernel, nb=nb),
        grid=(b,),
        in_specs=[pl.BlockSpec((None, None, MIX_WIDTH, t), lambda i: (layer, i, 0, 0))],
        out_specs=pl.BlockSpec((None, MIX_WIDTH, LANES), lambda i: (i, 0, 0)),
        out_shape=jax.ShapeDtypeStruct((b, MIX_WIDTH, LANES), F32),
        compiler_params=_params("parallel"),
        name="moba_block_means",
    )(kf_all)


def _bias_kernel(tab_ref, idxp_ref, idxs_ref, bp_ref, bs_ref, *, dec_seq):
    idxp = idxp_ref[...]
    idxs = idxs_ref[...]
    row = lax.broadcasted_iota(jnp.int32, idxs.shape, 1)
    for h in range(2 * HPM):
        acc = jnp.zeros(idxp.shape, F32)
        for b in range(N_BUCKETS):
            acc = jnp.where(idxp == b, tab_ref[h, b] * LOG2E, acc)
        bp_ref[h] = acc
    for m in range(2):
        acc = jnp.zeros(idxs.shape, F32)
        for hl in range(HPM):
            for b in range(N_BUCKETS):
                acc = jnp.where((idxs == b) & (row // dec_seq == hl), tab_ref[m * HPM + hl, b], acc)
        bs_ref[m] = acc


def _bias_tiles(tab, idxp, idxs, dec_seq):
    return pl.pallas_call(
        functools.partial(_bias_kernel, dec_seq=dec_seq),
        in_specs=[pl.BlockSpec(memory_space=pltpu.SMEM), pl.BlockSpec(memory_space=pltpu.VMEM),
                  pl.BlockSpec(memory_space=pltpu.VMEM)],
        out_specs=[pl.BlockSpec(memory_space=pltpu.VMEM), pl.BlockSpec(memory_space=pltpu.VMEM)],
        out_shape=[jax.ShapeDtypeStruct((2 * HPM,) + idxp.shape, F32),
                   jax.ShapeDtypeStruct((2,) + idxs.shape, F32)],
        compiler_params=pltpu.CompilerParams(vmem_limit_bytes=VMEM_LIMIT),
        name="t5_bias_tiles",
    )(tab, idxp, idxs)


ONES_ROWS = 16
ACC_ROWS = HEAD_DIM + ONES_ROWS
EXP2_DEAD = -200.0


def _rows(n):
    return lax.broadcasted_iota(jnp.int32, (n, 1), 0)


def _key_before_query(tk, tq, strict):
    s = lax.broadcasted_iota(jnp.int32, (tk, tq), 0)
    t = lax.broadcasted_iota(jnp.int32, (tk, tq), 1)
    return (s < t) if strict else (s <= t)


def _row_range(qt, lo, hi):
    r = _rows(qt.shape[0])
    return jnp.where((r >= lo) & (r < hi), qt, jnp.zeros_like(qt))


def _v_aug(vt, x):
    return jnp.concatenate([vt[x * HEAD_DIM:(x + 1) * HEAD_DIM, :],
                            jnp.ones((ONES_ROWS, vt.shape[1]), BF16)], axis=0)


def _extra_rows(by_row, n, width):
    r = _rows(n)
    out = jnp.zeros((n, width), F32)
    for row, val in by_row.items():
        out = jnp.where(r == row, val, out)
    return out


def _pieces(values, base):
    out = {}
    for i, val in enumerate(values):
        for j, piece in enumerate(_split3(val)):
            out[base + 3 * i + j] = piece.astype(F32)
    return out


def _softmax_step(i, s, va, m_sc, acc_sc, sel=None):
    m_old = m_sc[i]
    bm = jnp.max(s, axis=0, keepdims=True)
    if sel is not None:
        bm = jnp.where(sel, bm, NEG)
    m_new = jnp.maximum(m_old, bm)
    m_sub = m_new if sel is None else jnp.where(sel, m_new, -NEG)
    p = jnp.exp2(s - m_sub)
    acc_sc[i] = acc_sc[i] * jnp.exp2(m_old - m_new) + _nn(va, p.astype(BF16))
    m_sc[i] = m_new


def _normalized(acc_sc, i):
    a = acc_sc[i]
    return a[0:HEAD_DIM, :] / a[HEAD_DIM:HEAD_DIM + 1, :]


def _init(m_sc, acc_sc):
    m_sc[...] = jnp.full(m_sc.shape, NEG, F32)
    acc_sc[...] = jnp.zeros(acc_sc.shape, F32)


def _sweep(qi, group, near, far, alive=None, pin=None):
    nf = jnp.maximum(qi - 1, 0)

    def rest(pinned):
        @pl.when(qi == 0)
        def _():
            near(1, pinned)

        @pl.when(qi > 0)
        def _():
            near(2, pinned)

        if alive is None:
            def body(i, c):
                top = nf - 1 - group * i
                far([top - g for g in range(group)], pinned)
                return c

            lax.fori_loop(0, nf // group, body, 0)
            ok = None
        else:
            def body(c):
                i, _ = c
                top = nf - 1 - group * i
                far([top - g for g in range(group)], pinned)
                return i + 1, alive(top - group + 1)

            _, ok = lax.while_loop(lambda c: (c[0] < nf // group) & c[1], body, (0, alive(nf)))
        rem = nf % group
        size = group // 2
        while size >= 1:
            top = (rem & (2 * size - 1)) - 1
            cond = (rem & size) != 0
            if ok is not None:
                cond = cond & ok

            @pl.when(cond)
            def _(size=size, top=top):
                far([top - g for g in range(size)], pinned)

            if ok is not None and size > 1:
                ok = ok & alive(rem & (size - 1))
            size //= 2

    if pin is None:
        rest(False)
    else:
        pinned = pin()

        @pl.when(pinned)
        def _():
            rest(True)

        @pl.when(jnp.logical_not(pinned))
        def _():
            rest(False)


PIN_HEADROOM = 100.0
PIN_MAX_GAP = 220.0


def _pin_reference(bounds, lows, m_sc):
    gap = bounds[0] - lows[0]
    for i in range(1, len(bounds)):
        gap = jnp.maximum(gap, bounds[i] - lows[i])
    pinned = jnp.max(gap) <= PIN_MAX_GAP

    @pl.when(pinned)
    def _():
        for i, (bound, low) in enumerate(zip(bounds, lows)):
            m_sc[i] = jnp.maximum(low, bound - PIN_HEADROOM)

    return pinned


def _own_key_scores(qt_ref, kd_ref, rows_per_map):
    prod = qt_ref[...].astype(F32) * kd_ref[...].astype(BF16).astype(F32)
    tq = prod.shape[1]
    z = jnp.sum(prod.reshape(MIX_WIDTH // rows_per_map, rows_per_map, tq), axis=1, keepdims=True)
    return z - 0.01


def _softmax_blocks(items, scores, v_of, m_sc, acc_sc, tq, pinned=False):
    ss = [scores(j, kind, i) for (j, kind, i, x, sel) in items]
    if not pinned:
        for s, (j, kind, i, x, sel) in zip(ss, items):
            if kind == "diag":
                s = jnp.where(_key_before_query(tq, tq, False), s, NEG)
            _softmax_step(i, s, _v_aug(v_of(j), x), m_sc, acc_sc, sel(j, x) if sel is not None else None)
        return
    pvs = {}
    for s, (j, kind, i, x, sel) in zip(ss, items):
        if kind == "diag":
            s = jnp.where(_key_before_query(tq, tq, False), s, NEG)
        ref = m_sc[i]
        if sel is not None:
            ref = jnp.where(sel(j, x), ref, -NEG)
        pv = _nn(_v_aug(v_of(j), x), jnp.exp2(s - ref).astype(BF16))
        pvs[i] = pv if i not in pvs else pvs[i] + pv
    for i, pv in pvs.items():
        acc_sc[i] = acc_sc[i] + pv


def _max_key_norm(kn2_ref, h):
    return jnp.sqrt(jnp.max(kn2_ref[:, h, :], axis=0, keepdims=True)[:, 0:1])


def _score_bound(q, kn, bias_max):
    qn = jnp.sqrt(jnp.sum(jnp.square(q.astype(F32)), axis=0, keepdims=True))
    return qn * kn * 1.001 + (bias_max + 0.01)


def _table_max(tab_ref, h):
    m = tab_ref[h, 0]
    for b in range(1, N_BUCKETS):
        m = jnp.maximum(m, tab_ref[h, b])
    return m * LOG2E


def _fox_kernel(qt_ref, k_ref, vt_ref, kd_ref, kn2_ref, cq_ref, call_ref, cend_ref, o_ref, kx_sc, m_sc, acc_sc,
                *, tq):
    qi = pl.program_id(1)

    def head_row(c, h):
        return c[h:h + 1, :]

    @pl.when(qi == 0)
    def _():
        c = call_ref[...] * LOG2E
        for p in range(HPM // 2):
            by_row = _pieces([head_row(c, 2 * p), head_row(c, 2 * p + 1)], 0)
            by_row.update({6: 1.0, 7: 1.0, 8: 1.0})
            e = _extra_rows(by_row, 16, c.shape[1])
            e = jnp.concatenate([e, jnp.zeros((PAIR - 16, c.shape[1]), F32)], axis=0)
            kx_sc[p] = e.T.astype(BF16)

    cq = cq_ref[...] * LOG2E
    qps, cts, zbs = [], [], []
    for h in range(HPM):
        p, x = divmod(h, 2)
        ct = head_row(cq, h)
        by_row = {3 * x: -1.0, 3 * x + 1: -1.0, 3 * x + 2: -1.0}
        by_row.update(_pieces([ct], 6))
        qx = _extra_rows(by_row, PAIR, tq).astype(BF16)
        qh = _row_range(qt_ref[p * PAIR:(p + 1) * PAIR, :], x * HEAD_DIM, (x + 1) * HEAD_DIM)
        qps.append(jnp.concatenate([qh, qx], axis=0))
        cts.append(ct)
        qn = jnp.sqrt(jnp.sum(jnp.square(qh.astype(F32)), axis=0, keepdims=True))
        zbs.append(qn * _max_key_norm(kn2_ref, h) * 1.001 + 0.001)
    _init(m_sc, acc_sc)

    def alive(low):
        lane = lax.broadcasted_iota(jnp.int32, cend_ref.shape, 1)
        cl = jnp.sum(jnp.where(lane == low - 1, cend_ref[...] * LOG2E, 0.0), axis=1, keepdims=True)
        worst = zbs[0] + cts[0] - head_row(cl, 0) - m_sc[0]
        for h in range(1, HPM):
            worst = jnp.maximum(worst, zbs[h] + cts[h] - head_row(cl, h) - m_sc[h])
        return jnp.max(worst) > EXP2_DEAD

    def scores(j, kind, h):
        r0 = pl.multiple_of(j * tq, tq)
        p = h // 2
        kp = jnp.concatenate([k_ref[pl.ds(r0, tq), p * PAIR:(p + 1) * PAIR], kx_sc[p, pl.ds(r0, tq), :]], axis=1)
        return _nn(kp, qps[h])

    def run(blocks, pinned=False):
        items = [(j, kind, h, h, None) for j, kind in blocks for h in range(HPM)]
        _softmax_blocks(items, scores, lambda j: vt_ref[j], m_sc, acc_sc, tq, pinned)

    lows = _own_key_scores(qt_ref, kd_ref, HEAD_DIM)
    _sweep(qi, 4,
           lambda n, pinned: run([(qi, "diag")] + ([(qi - 1, "far")] if n == 2 else []), pinned),
           lambda js, pinned: run([(j, "far") for j in js], pinned), alive,
           lambda: _pin_reference([zb + 0.01 for zb in zbs], [lows[h] for h in range(HPM)], m_sc))
    o = jnp.concatenate([_normalized(acc_sc, h) for h in range(HPM)], axis=0)
    o_ref[...] = o.T.astype(o_ref.dtype)


def _moba_kernel(tab_ref, qt_ref, k_ref, vt_ref, kd_ref, kn2_ref, mean_ref, bt_ref, o_ref, m_sc, acc_sc, sel_sc,
                 *, tq, nb):
    qi = pl.program_id(1)
    ones_k = jnp.ones((tq, PAIR), BF16)
    nbp = sel_sc.shape[1]
    rb = _rows(nbp)
    qs, qps, bounds = [], [], []
    means = [_split2(mean_ref[p * PAIR:(p + 1) * PAIR, :].T) for p in range(HPM // 2)]
    for h in range(HPM):
        p, x = divmod(h, 2)
        qx = _row_range(qt_ref[p * PAIR:(p + 1) * PAIR, :], x * HEAD_DIM, (x + 1) * HEAD_DIM)
        qs.append(qx)
        bounds.append(_score_bound(qx, _max_key_norm(kn2_ref, h), _table_max(tab_ref, h)))
        far = jnp.full((1, tq), tab_ref[h, N_BUCKETS - 1] * LOG2E, F32)
        qps.append(jnp.concatenate([qx, _extra_rows(_pieces([far], 0), PAIR, tq).astype(BF16)], axis=0))
    gates = [(_nn(means[h // 2][0], qs[h]) + _nn(means[h // 2][1], qs[h]))[0:nbp, :] for h in range(HPM)]
    for h, gate in enumerate(gates):
        cnt = jnp.zeros((nbp, tq), jnp.int32)
        for n in range(nb):
            gn = gate[n:n + 1, :]
            beats = (gn > gate) | ((gn == gate) & (n < rb))
            cnt = cnt + jnp.where(beats, jnp.where(n < qi, 1, 0), 0)
        sel_sc[h] = jnp.where((cnt < MOBA_TOPK) & (rb < qi), 1.0, 0.0)
    _init(m_sc, acc_sc)

    def scores(j, kind, h):
        p = h // 2
        k = k_ref[pl.ds(pl.multiple_of(j * tq, tq), tq), p * PAIR:(p + 1) * PAIR]
        if kind == "far":
            return _nn(jnp.concatenate([k, ones_k], axis=1), qps[h])
        return _nn(k, qs[h]) + bt_ref[h, 0 if kind == "diag" else 1]

    def run(blocks, pinned=False):
        items = [(j, kind, h, h, None if kind == "diag" else (lambda j, h: sel_sc[h, pl.ds(j, 1), :] > 0.0))
                 for j, kind in blocks for h in range(HPM)]
        _softmax_blocks(items, scores, lambda j: vt_ref[j], m_sc, acc_sc, tq, pinned)

    own = _own_key_scores(qt_ref, kd_ref, HEAD_DIM)
    lows = [own[h] + tab_ref[h, 0] * LOG2E for h in range(HPM)]
    _sweep(qi, 4,
           lambda n, pinned: run([(qi, "diag")] + ([(qi - 1, "sub")] if n == 2 else []), pinned),
           lambda js, pinned: run([(j, "far") for j in js], pinned),
           pin=lambda: _pin_reference(bounds, lows, m_sc))
    o = jnp.concatenate([_normalized(acc_sc, h) for h in range(HPM)], axis=0)
    o_ref[...] = o.T.astype(o_ref.dtype)


def _sb_kernel(qt_ref, k_ref, vt_ref, su_ref, o_ref, r_sc, acc_sc, *, tq):
    qi = pl.program_id(1)
    qs = [_row_range(qt_ref[(h // 2) * PAIR:(h // 2 + 1) * PAIR, :], (h % 2) * HEAD_DIM, (h % 2 + 1) * HEAD_DIM)
          for h in range(HPM)]
    su = su_ref[...]
    r_sc[...] = jnp.zeros(r_sc.shape, F32)
    acc_sc[...] = jnp.zeros(acc_sc.shape, F32)

    def run(blocks):
        tiles = [(j, masked, h) for j, masked in blocks for h in range(HPM)]
        zs = [_nn(k_ref[pl.ds(pl.multiple_of(j * tq, tq), tq), (h // 2) * PAIR:(h // 2 + 1) * PAIR], qs[h])
              for j, masked, h in tiles]
        lbs, lats = [], []
        for z, (j, masked, h) in zip(zs, tiles):
            log_beta = jnp.minimum(z, 0.0) - jnp.log2(1.0 + jnp.exp2(-jnp.abs(z)))
            log_keep = log_beta - z
            if masked:
                log_keep = jnp.where(_key_before_query(tq, tq, True), log_keep, 0.0)
            lbs.append(log_beta)
            lats.append(_nn(su, log_keep.astype(BF16)))
        r = [r_sc[h] for h in range(HPM)]
        pvs = [jnp.zeros((HEAD_DIM, tq), F32)] * HPM
        for log_beta, lat, (j, masked, h) in zip(lbs, lats, tiles):
            a = jnp.exp2(log_beta + lat[0:tq, :] + r[h])
            if masked:
                a = jnp.where(_key_before_query(tq, tq, True), a, 0.0)
            r[h] = r[h] + lat[tq:tq + 1, :]
            pvs[h] = pvs[h] + _nn(vt_ref[j][h * HEAD_DIM:(h + 1) * HEAD_DIM, :], a.astype(BF16))
        for h in range(HPM):
            r_sc[h] = r[h]
            acc_sc[h] = acc_sc[h] + pvs[h]

    def alive(low):
        return jnp.max(r_sc[...]) > EXP2_DEAD

    _sweep(qi, 2,
           lambda n, pinned: run([(qi, True)] + ([(qi - 1, False)] if n == 2 else [])),
           lambda js, pinned: run([(j, False) for j in js]), alive)
    o = jnp.concatenate([acc_sc[h] for h in range(HPM)], axis=0)
    o_ref[...] = o.T.astype(o_ref.dtype)


def _lam_value(lam_ref, lam_init):
    lm = lam_ref[...]
    a = jnp.sum(lm[0:1] * lm[1:2], axis=-1, keepdims=True)
    b = jnp.sum(lm[2:3] * lm[3:4], axis=-1, keepdims=True)
    return jnp.exp(a) - jnp.exp(b) + lam_init


def _diff_kernel(tab_ref, qt_ref, k_ref, vt_ref, kd_ref, kn2_ref, bt_ref, lam_ref, gsub_ref, o_ref, m_sc, acc_sc,
                 *, tq, lam_init):
    qi = pl.program_id(1)
    ones_k = jnp.ones((tq, PAIR), BF16)
    qs, qps, bounds = [], [], []
    for i in range(2 * HPM):
        p, r = divmod(i * DIFF_HALF, PAIR)
        qx = _row_range(qt_ref[p * PAIR:(p + 1) * PAIR, :], r, r + DIFF_HALF)
        qs.append(qx)
        bounds.append(_score_bound(qx, _max_key_norm(kn2_ref, i // 2), _table_max(tab_ref, HPM + i // 2)))
        far = jnp.full((1, tq), tab_ref[HPM + i // 2, N_BUCKETS - 1] * LOG2E, F32)
        qps.append(jnp.concatenate([qx, _extra_rows(_pieces([far], 0), PAIR, tq).astype(BF16)], axis=0))
    _init(m_sc, acc_sc)

    def scores(j, kind, i):
        p = i // 4
        k = k_ref[pl.ds(pl.multiple_of(j * tq, tq), tq), p * PAIR:(p + 1) * PAIR]
        if kind == "far":
            return _nn(jnp.concatenate([k, ones_k], axis=1), qps[i])
        return _nn(k, qs[i]) + bt_ref[i // 2, 0 if kind == "diag" else 1]

    def run(blocks, pinned=False):
        items = [(j, kind, i, i // 2, None) for j, kind in blocks for i in range(2 * HPM)]
        _softmax_blocks(items, scores, lambda j: vt_ref[j], m_sc, acc_sc, tq, pinned)

    own = _own_key_scores(qt_ref, kd_ref, DIFF_HALF)
    lows = [own[i] + tab_ref[HPM + i // 2, 0] * LOG2E for i in range(2 * HPM)]
    _sweep(qi, 4,
           lambda n, pinned: run([(qi, "diag")] + ([(qi - 1, "sub")] if n == 2 else []), pinned),
           lambda js, pinned: run([(j, "far") for j in js], pinned),
           pin=lambda: _pin_reference(bounds, lows, m_sc))
    lam_val = _lam_value(lam_ref, lam_init)
    gcol = jnp.concatenate([gsub_ref[...]] * (tq // LANES), axis=1)
    outs = []
    for x in range(HPM):
        o = _normalized(acc_sc, 2 * x) - lam_val * _normalized(acc_sc, 2 * x + 1)
        inv = lax.rsqrt(jnp.mean(o * o, axis=0, keepdims=True) + EPS)
        outs.append(((o * inv) * gcol) * (1.0 - lam_init))
    o_ref[...] = jnp.concatenate(outs, axis=0).T.astype(o_ref.dtype)


def _mixer_specs(t, tq, mixer, layer, kn2_tiles):
    nk = t // tq
    qspec = pl.BlockSpec((None, MIX_WIDTH, tq), lambda b, qi: (b, mixer, qi))
    kspec = pl.BlockSpec((None, t, MIX_WIDTH), lambda b, qi: (b, 0, mixer))
    vspec = pl.BlockSpec((None, nk, MIX_WIDTH, tq), lambda b, qi: (b, 0, mixer, 0))
    kdspec = pl.BlockSpec((None, None, MIX_WIDTH, tq), lambda b, qi: (layer, b, mixer, qi))
    knspec = pl.BlockSpec((None, kn2_tiles, None, HPM, LANES), lambda b, qi: (b, 0, mixer, 0, 0))
    ospec = pl.BlockSpec((None, tq, MIX_WIDTH), lambda b, qi: (b, qi, 0))
    return [qspec, kspec, vspec, kdspec, knspec], ospec


def _attn_out(b, t):
    return jax.ShapeDtypeStruct((b, t, MIX_WIDTH), BF16)


def _stat_scratch(n_maps, tq, rows):
    return [pltpu.VMEM((n_maps, 1, tq), F32), pltpu.VMEM((n_maps, rows, tq), F32)]


def _fox_prompt(qkv, layer, c, tq):
    b, t, _ = qkv[1].shape
    specs, ospec = _mixer_specs(t, tq, 1, layer, qkv[4].shape[1])
    cq = pl.BlockSpec((None, 8, tq), lambda bi, qi: (bi, 0, qi))
    call = pl.BlockSpec((None, 8, t), lambda bi, qi: (bi, 0, 0))
    cend = c[:, :, tq - 1::tq]
    cends = pl.BlockSpec((None, 8, t // tq), lambda bi, qi: (bi, 0, 0))
    return pl.pallas_call(
        functools.partial(_fox_kernel, tq=tq),
        grid=(b, t // tq),
        in_specs=specs + [cq, call, cends],
        out_specs=ospec,
        out_shape=_attn_out(b, t),
        scratch_shapes=[pltpu.VMEM((HPM // 2, t, PAIR), BF16)] + _stat_scratch(HPM, tq, ACC_ROWS),
        compiler_params=_params("parallel", "arbitrary"),
        name="fox_prompt",
    )(*qkv, c, c, cend)


def _moba_prompt(tab, qkv, layer, means, bp, tq):
    b, t, _ = qkv[1].shape
    nb = t // MOBA_BLOCK
    nbp = -(-nb // 8) * 8
    specs, ospec = _mixer_specs(t, tq, 0, layer, qkv[4].shape[1])
    mspec = pl.BlockSpec((None, MIX_WIDTH, LANES), lambda bi, qi: (bi, 0, 0))
    bt = pl.BlockSpec((HPM, 2, tq, tq), lambda bi, qi: (0, 0, 0, 0))
    return pl.pallas_call(
        functools.partial(_moba_kernel, tq=tq, nb=nb),
        grid=(b, t // tq),
        in_specs=[pl.BlockSpec(memory_space=pltpu.SMEM)] + specs + [mspec, bt],
        out_specs=ospec,
        out_shape=_attn_out(b, t),
        scratch_shapes=_stat_scratch(HPM, tq, ACC_ROWS) + [pltpu.VMEM((HPM, nbp, tq), F32)],
        compiler_params=_params("parallel", "arbitrary"),
        name="moba_prompt",
    )(tab, *qkv, means, bp)


def _sb_prompt(qkv, layer, su, tq):
    b, t, _ = qkv[1].shape
    specs, ospec = _mixer_specs(t, tq, 2, layer, qkv[4].shape[1])
    return pl.pallas_call(
        functools.partial(_sb_kernel, tq=tq),
        grid=(b, t // tq),
        in_specs=specs[:3] + [_const_spec(su.shape)],
        out_specs=ospec,
        out_shape=_attn_out(b, t),
        scratch_shapes=_stat_scratch(HPM, tq, HEAD_DIM),
        compiler_params=_params("parallel", "arbitrary"),
        name="sb_prompt",
    )(*qkv[:3], su)


def _diff_prompt(tab, qkv, layer, bp, lam, gsub_col, lam_init, tq):
    b, t, _ = qkv[1].shape
    specs, ospec = _mixer_specs(t, tq, 3, layer, qkv[4].shape[1])
    bt = pl.BlockSpec((HPM, 2, tq, tq), lambda bi, qi: (1, 0, 0, 0))
    return pl.pallas_call(
        functools.partial(_diff_kernel, tq=tq, lam_init=lam_init),
        grid=(b, t // tq),
        in_specs=[pl.BlockSpec(memory_space=pltpu.SMEM)] + specs
        + [bt, _const_spec((4, DIFF_HALF)), _const_spec((HEAD_DIM, LANES))],
        out_specs=ospec,
        out_shape=_attn_out(b, t),
        scratch_shapes=_stat_scratch(2 * HPM, tq, ACC_ROWS),
        compiler_params=_params("parallel", "arbitrary"),
        name="diff_prompt",
    )(tab, *qkv, bp, lam, gsub_col)


N_SLOTS = 20


def _sample_masks(dec_seq):
    rows = N_SLOTS * dec_seq
    qmask = np.zeros((rows, D_QKV), np.float32)
    for r in range(rows):
        slot = r // dec_seq
        if slot < 3 * HPM:
            qmask[r, slot * HEAD_DIM:(slot + 1) * HEAD_DIM] = 1.0
        elif slot < 4 * HPM:
            qmask[r, slot * HEAD_DIM:slot * HEAD_DIM + DIFF_HALF] = 1.0
        else:
            c0 = (slot - HPM) * HEAD_DIM + DIFF_HALF
            qmask[r, c0:c0 + DIFF_HALF] = 1.0
    omask = np.zeros((N_HEADS * dec_seq, D_QKV), np.float32)
    for r in range(N_HEADS * dec_seq):
        omask[r, (r // dec_seq) * HEAD_DIM:(r // dec_seq + 1) * HEAD_DIM] = 1.0
    return qmask, omask


def _expand_rows(a, dec_seq):
    rows = HPM * dec_seq
    r = lax.broadcasted_iota(jnp.int32, (rows, 1), 0)
    out = jnp.broadcast_to(a[HPM - 1:HPM, :], (rows, a.shape[1]))
    for h in range(HPM - 2, -1, -1):
        out = jnp.where(r < (h + 1) * dec_seq, a[h:h + 1, :], out)
    return out


def _softmax_pages(s_pages):
    m = s_pages[0].max(axis=-1, keepdims=True)
    for s in s_pages[1:]:
        m = jnp.maximum(m, s.max(axis=-1, keepdims=True))
    ps = [jnp.exp(s - m) for s in s_pages]
    l = ps[0].sum(axis=-1, keepdims=True)
    for p in ps[1:]:
        l = l + p.sum(axis=-1, keepdims=True)
    return [p / l for p in ps]


def _sample_kernel(*refs, n_pages, dec_seq, lam_init):
    pt_ref = refs[0]
    del pt_ref
    (q_ref, kn_ref, vn_ref, lfn_ref, bs_ref, qmask_ref, omask_ref, sl_ref, lam_ref,
     gd1_ref, gd2_ref, gmul_ref, gadd_ref) = refs[1:14]
    k_refs = refs[14:14 + n_pages]
    v_refs = refs[14 + n_pages:14 + 2 * n_pages]
    lf_refs = refs[14 + 2 * n_pages:14 + 3 * n_pages]
    o_ref = refs[14 + 3 * n_pages]
    s_sc, p_sc = refs[14 + 3 * n_pages + 1:]

    g = HPM * dec_seq
    npg = n_pages + 1
    q = q_ref[...]
    qbd = (jnp.concatenate([q] * (N_SLOTS * dec_seq // 8), axis=0) * qmask_ref[...]).astype(BF16)
    pad = jnp.zeros((LANES - kn_ref.shape[0], D_QKV), BF16)
    for p in range(n_pages):
        s_sc[p] = _nn(qbd, k_refs[p][...].astype(BF16))
    s_sc[n_pages] = _nt(qbd, jnp.concatenate([kn_ref[...], pad], axis=0))

    lane = lax.broadcasted_iota(jnp.int32, (1, LANES), 1)
    qrow = lax.broadcasted_iota(jnp.int32, (g, 1), 0) % dec_seq
    new_ok = lane <= qrow
    new_past = lane < qrow
    sl = sl_ref[...]

    def pages(r0):
        return [s_sc[p, r0:r0 + g, :] for p in range(npg)]

    sm = pages(0)
    ppb = MOBA_BLOCK // LANES
    nblk = n_pages // ppb
    gates = []
    for n in range(nblk):
        tot = sm[n * ppb]
        for i in range(1, ppb):
            tot = tot + sm[n * ppb + i]
        gates.append(jnp.sum(tot, axis=-1, keepdims=True))
    s_pages = []
    for n in range(nblk):
        cnt = jnp.zeros((g, 1), jnp.int32)
        for n2 in range(nblk):
            if n2 == n:
                continue
            beats = (gates[n2] > gates[n]) | ((gates[n2] == gates[n]) & (n2 < n))
            cnt = cnt + jnp.where(beats, 1, 0)
        for i in range(ppb):
            p = n * ppb + i
            s_pages.append(jnp.where(cnt < MOBA_TOPK, sm[p] + bs_ref[0, p], NEG))
    s_pages.append(jnp.where(new_ok, sm[n_pages] + bs_ref[0, n_pages], NEG))
    for p, pr in enumerate(_softmax_pages(s_pages)):
        p_sc[p, 0:g, :] = pr.astype(BF16)

    lfn = lfn_ref[...]
    cn_row = jnp.zeros((g, 1), F32)
    bias_new = jnp.zeros((g, LANES), F32)
    for n in range(dec_seq):
        col = lfn[:, n:n + 1]
        cn_row = cn_row + jnp.where(n <= qrow, col, 0.0)
        bias_new = bias_new + jnp.where((lane < n) & (n <= qrow), col, 0.0)
    lfe = [_expand_rows(lf_refs[p][...], dec_seq) for p in range(n_pages)]
    suffix = _nn_split(jnp.concatenate(lfe, axis=0), sl, 3)
    sf = pages(g)
    s_pages = [None] * npg
    s_pages[n_pages] = jnp.where(new_ok, sf[n_pages] + bias_new, NEG)
    after = cn_row
    for p in range(n_pages - 1, -1, -1):
        s_pages[p] = sf[p] + (suffix[p * g:(p + 1) * g, :] + after)
        after = after + jnp.sum(lfe[p], axis=-1, keepdims=True)
    for p, pr in enumerate(_softmax_pages(s_pages)):
        p_sc[p, g:2 * g, :] = pr.astype(BF16)

    ss = pages(2 * g)
    lbs, lks = [], []
    for p in range(npg):
        lb = _log_sigmoid(ss[p])
        lk = lb - ss[p]
        if p == n_pages:
            lk = jnp.where(new_past, lk, 0.0)
        lbs.append(lb)
        lks.append(lk)
    later_in = _nn_split(jnp.concatenate(lks, axis=0), sl, 2)
    after = jnp.zeros((g, 1), F32)
    for p in range(npg - 1, -1, -1):
        a = jnp.exp(lbs[p] + later_in[p * g:(p + 1) * g, :] + after)
        if p == n_pages:
            a = jnp.where(new_past, a, 0.0)
        p_sc[p, 2 * g:3 * g, :] = a.astype(BF16)
        after = after + jnp.sum(lks[p], axis=-1, keepdims=True)

    lam_val = _lam_value(lam_ref, lam_init)
    maps = []
    for mp in range(2):
        sd = pages((3 + mp) * g)
        s_pages = [sd[p] + bs_ref[1, p] for p in range(n_pages)]
        s_pages.append(jnp.where(new_ok, sd[n_pages] + bs_ref[1, n_pages], NEG))
        maps.append(_softmax_pages(s_pages))
    for p in range(npg):
        p_sc[p, 3 * g:4 * g, :] = (maps[0][p] - lam_val * maps[1][p]).astype(BF16)

    padv = jnp.zeros((LANES - vn_ref.shape[0], D_QKV), BF16)
    acc = _nn(p_sc[n_pages], jnp.concatenate([vn_ref[...], padv], axis=0))
    for p in range(n_pages):
        acc = acc + _nt(p_sc[p], v_refs[p][...].astype(BF16))
    rm = acc * omask_ref[...]
    y = rm[0:8, :]
    for i in range(1, N_HEADS * dec_seq // 8):
        y = y + rm[8 * i:8 * (i + 1), :]
    o8 = y + pltpu.roll(y, dec_seq, 0)
    ms = _nn_split(o8 * o8, gd1_ref[...], 2)
    bc = _nn_split(lax.rsqrt(ms + EPS), gd2_ref[...], 3)
    o8 = o8 * (bc * gmul_ref[...] + gadd_ref[...])
    o_ref[...] = o8[0:dec_seq, :]


def _sample_attention(layer, page_table, q8, kn, vn, lfn, bs, consts, lam, gmul, gadd,
                      cache_k, cache_v, cache_lft, lam_init):
    db, n_pages = page_table.shape
    dec_seq = lfn.shape[1] // HPM
    page = cache_k.shape[3]
    qmask, omask, sl, gd1, gd2 = consts
    rows = N_SLOTS * dec_seq

    def batch_spec(shape):
        nd = len(shape)
        return pl.BlockSpec((None,) + shape, lambda b, pt: (b,) + (0,) * nd)

    def const(shape):
        nd = len(shape)
        return pl.BlockSpec(shape, lambda b, pt: (0,) * nd)

    def paged(width_rows, width_cols):
        return [pl.BlockSpec((None, None, width_rows, width_cols),
                             functools.partial(lambda b, pt, j: (layer, pt[b, j], 0, 0), j=j))
                for j in range(n_pages)]

    in_specs = ([batch_spec((8, D_QKV)), batch_spec(kn.shape[1:]), batch_spec(vn.shape[1:]),
                 batch_spec(lfn.shape[1:]), const(bs.shape), const(qmask.shape), const(omask.shape),
                 const(sl.shape), const(lam.shape), const(gd1.shape), const(gd2.shape),
                 const(gmul.shape), const(gadd.shape)]
                + paged(D_QKV, page) + paged(D_QKV, page) + paged(HPM, page))
    grid_spec = pltpu.PrefetchScalarGridSpec(
        num_scalar_prefetch=1,
        grid=(db,),
        in_specs=in_specs,
        out_specs=pl.BlockSpec((None, dec_seq, D_QKV), lambda b, pt: (b, 0, 0)),
        scratch_shapes=[pltpu.VMEM((n_pages + 1, rows, LANES), F32),
                        pltpu.VMEM((n_pages + 1, N_HEADS * dec_seq, LANES), BF16)],
    )
    return pl.pallas_call(
        functools.partial(_sample_kernel, n_pages=n_pages, dec_seq=dec_seq, lam_init=lam_init),
        grid_spec=grid_spec,
        out_shape=jax.ShapeDtypeStruct((db, dec_seq, D_QKV), F32),
        compiler_params=_params("arbitrary"),
        name="sample_attention",
    )(page_table, q8, kn, vn, lfn, bs, qmask, omask, sl, lam, gd1, gd2, gmul, gadd,
      *([cache_k] * n_pages), *([cache_v] * n_pages), *([cache_lft] * n_pages))


MLP_CHUNK = 1024


def _merge_mlp_kernel(x_ref, o0_ref, o1_ref, o2_ref, o3_ref, gmix_ref, wg_ref, wbr_ref, wo_ref,
                      gmlp_ref, wup_ref, wdn_ref, y_ref):
    x = x_ref[...]
    h = (x * lax.rsqrt(jnp.mean(x * x, axis=-1, keepdims=True) + EPS)) * gmix_ref[...]
    hb = h.astype(BF16)
    acc = jnp.zeros(x.shape, F32)
    for m, o_ref in enumerate((o0_ref, o1_ref, o2_ref, o3_ref)):
        gate = 1.0 / (1.0 + jnp.exp(-_nt(hb, wg_ref[m * D_MODEL:(m + 1) * D_MODEL, :])))
        acc = acc + gate * _nn(o_ref[...].astype(BF16), wbr_ref[m])
    y = x + _nn(acc.astype(BF16), wo_ref[...])
    hm = ((y * lax.rsqrt(jnp.mean(y * y, axis=-1, keepdims=True) + EPS)) * gmlp_ref[...]).astype(BF16)
    for c in range(D_FF // MLP_CHUNK):
        u = jnp.maximum(_nn(hm, wup_ref[:, c * MLP_CHUNK:(c + 1) * MLP_CHUNK]), 0.0)
        y = y + _nn((u * u).astype(BF16), wdn_ref[c * MLP_CHUNK:(c + 1) * MLP_CHUNK, :])
    y_ref[...] = y


def _resident_spec(shape):
    nd = len(shape)
    return pl.BlockSpec(shape, lambda *_: (0,) * nd, pipeline_mode=pl.Buffered(1))


def _merge_mlp(x, os_, gmix, wg, wbr, wo, gmlp, wup, wdn, tm):
    rows = x.shape[0]
    row = lambda w: pl.BlockSpec((tm, w), lambda i: (i, 0))
    return pl.pallas_call(
        _merge_mlp_kernel,
        grid=(rows // tm,),
        in_specs=[row(D_MODEL)] + [row(MIX_WIDTH)] * 4
        + [_const_spec((1, D_MODEL)), _resident_spec((N_MIXERS * D_MODEL, D_MODEL)),
           _resident_spec((N_MIXERS, MIX_WIDTH, D_MODEL)), _resident_spec((D_MODEL, D_MODEL)),
           _const_spec((1, D_MODEL)), _resident_spec((D_MODEL, D_FF)), _resident_spec((D_FF, D_MODEL))],
        out_specs=row(D_MODEL),
        out_shape=jax.ShapeDtypeStruct((rows, D_MODEL), F32),
        compiler_params=_params("parallel"),
        name="merge_mlp",
    )(x, *os_, gmix, wg, wbr, wo, gmlp, wup, wdn)


def _qk_vectors(gq, gk, qscale):
    z = jnp.zeros((MIX_WIDTH,), F32)
    one = jnp.ones((MIX_WIDTH,), F32)
    t4 = lambda a: jnp.tile(a, HPM)
    sc = HEAD_DIM ** -0.5 * qscale
    qmul = jnp.concatenate([t4(gq[0]) * sc, t4(gq[1]) * sc, z, t4(gq[2]) * (DIFF_HALF ** -0.5 * qscale)])
    qadd = jnp.concatenate([z, z, one * sc, z])
    kmul = jnp.concatenate([t4(gk[0]), t4(gk[1]), z, t4(gk[2])])
    kadd = jnp.concatenate([z, z, one, z])
    return [a.reshape(1, D_QKV) for a in (qmul, qadd, kmul, kadd)]


def _bias_index_tiles(tq, n_pages, page, dec_seq):
    a = np.arange(tq)
    diag = _t5_bucket_np(a[None, :] - a[:, None])
    sub = _t5_bucket_np(tq + a[None, :] - a[:, None])
    idxp = np.stack([diag, sub]).astype(np.int32)
    past_len = n_pages * page
    qpos = past_len + (np.arange(HPM * dec_seq) % dec_seq)
    kpos = np.concatenate([np.arange(past_len), past_len + np.arange(page)])
    idxs = _t5_bucket_np(qpos[None, :, None] - kpos.reshape(n_pages + 1, 1, page))
    return idxp, idxs.astype(np.int32)


def _layer(l, x, past, w, consts, tq, tm, kv_all=None):
    (tab, bp, bs, g1, g2, su_q, tri_c, sl_p, sample_consts, gd) = consts
    (b_forget, g_q, g_k, lam, g_sub, g_mix, g_mlp, dense) = w
    wt, wg, wbr, wo, wup, wdn = dense[l]
    depth = len(dense)
    b, t, _ = x.shape
    rows = b * t
    lam_init = 0.8 - 0.6 * math.exp(-0.3 * l)
    wqkv = wf = wt
    gmix = g_mix[l].reshape(1, D_MODEL)
    qk_vecs = _qk_vectors(g_q[l], g_k[l], LOG2E if past is None else 1.0)
    x2 = x.reshape(rows, D_MODEL)

    if past is None:
        cols = [jnp.broadcast_to(a.reshape(D_QKV, 1), (D_QKV, LANES)) for a in qk_vecs]
        bf_col = jnp.broadcast_to(jnp.pad(b_forget[l], (0, 8 - HPM))[:, None], (8, LANES))
        qt, kb3, vt, k_new, v_new, lf_t, kn2 = _inproj_t(x, gmix, wqkv, wf, bf_col, *cols, tm=tm, tq=tq,
                                                         layer=l, depth=depth, kv_all=kv_all)
        kn2 = kn2.reshape(b, t // tm, N_MIXERS, HPM, LANES)
        lf_new = jnp.swapaxes(lf_t[:, :HPM], 1, 2)
        c = _cumsum(lf_t, tri_c)
        means = _block_means(k_new, l)
        gsub_col = jnp.broadcast_to(g_sub[l][:, None], (HEAD_DIM, LANES))
        qkv = (qt, kb3, vt, k_new, kn2)
        o_moba = _moba_prompt(tab, qkv, l, means, bp, tq)
        o_fox = _fox_prompt(qkv, l, c, tq)
        o_sb = _sb_prompt(qkv, l, su_q, tq)
        o_dif = _diff_prompt(tab, qkv, l, bp, lam[l], gsub_col, lam_init, tq)
        outs = [a.reshape(rows, MIX_WIDTH) for a in (o_moba, o_fox, o_sb, o_dif)]
    else:
        cache_k, cache_v, cache_lft, page_table = past
        bf = jnp.pad(b_forget[l], (0, LANES - HPM)).reshape(1, LANES)
        qb, kb, vb, kf, vf, lf = _inproj(x2, gmix, wqkv, wf, bf, g1, g2, *qk_vecs, tm=min(tm, rows))
        k_new, v_new = kf.reshape(b, t, N_HEADS, HEAD_DIM), vf.reshape(b, t, N_HEADS, HEAD_DIM)
        lf_new = lf.reshape(b, t, HPM)
        q4 = qb.astype(F32).reshape(b, t, D_QKV)
        q8 = jnp.concatenate([q4] * (8 // t), axis=1)
        padn = ((0, 0), (0, 16 - t), (0, 0))
        kn = jnp.pad(kb.reshape(b, t, D_QKV), padn)
        vn = jnp.pad(vb.reshape(b, t, D_QKV), padn)
        lfn = jnp.swapaxes(lf.reshape(b, t, HPM), 1, 2)
        lfn = jnp.pad(jnp.repeat(lfn, t, axis=1), ((0, 0), (0, 0), (0, 8 - t)))
        gd1, gd2 = gd
        z3 = jnp.zeros((3 * MIX_WIDTH,), F32)
        gmul = jnp.concatenate([z3, jnp.tile(g_sub[l], HPM) * (1.0 - lam_init)]).reshape(1, D_QKV)
        gadd = jnp.concatenate([z3 + 1.0, jnp.zeros((MIX_WIDTH,), F32)]).reshape(1, D_QKV)
        o = _sample_attention(l, page_table, q8, kn, vn, lfn, bs, (*sample_consts, sl_p, gd1, gd2),
                              lam[l], gmul, gadd, cache_k, cache_v, cache_lft, lam_init)
        o = o.reshape(rows, D_QKV)
        outs = [o[:, m * MIX_WIDTH:(m + 1) * MIX_WIDTH] for m in range(N_MIXERS)]

    y = _merge_mlp(x2, outs, gmix, wg, wbr, wo, g_mlp[l].reshape(1, D_MODEL), wup, wdn, tm=min(tm, rows))
    return y.reshape(b, t, D_MODEL), k_new, v_new, lf_new


def _constants(rel_bias, tq, n_pages, page, dec_seq):
    idxp, idxs = _bias_index_tiles(tq, n_pages, page, dec_seq)
    tab = rel_bias.T
    bp, bs = _bias_tiles(tab, jnp.asarray(idxp), jnp.asarray(idxs), dec_seq)
    g1, g2 = _qk_group_mats()
    qmask, omask = _sample_masks(dec_seq)
    gd1 = np.zeros((D_QKV, LANES), np.float32)
    gd2 = np.zeros((LANES, D_QKV), np.float32)
    for c in range(3 * MIX_WIDTH, D_QKV):
        gd1[c, (c - 3 * MIX_WIDTH) // HEAD_DIM] = 1.0 / HEAD_DIM
        gd2[(c - 3 * MIX_WIDTH) // HEAD_DIM, c] = 1.0
    su_q = np.concatenate([_strict_lower(tq).T, np.ones((ONES_ROWS, tq), np.float32)], axis=0)
    bf = lambda a: jnp.asarray(a, BF16)
    return (tab, bp, bs, bf(g1), bf(g2), bf(su_q), bf(_upper_incl(tq)), bf(_strict_lower(page)),
            (jnp.asarray(qmask), jnp.asarray(omask)), (bf(gd1), bf(gd2)))


def kernel(x_prompt, x_sample, cache_k, cache_v, cache_logf, page_table, rel_bias, w_in, b_forget,
           g_q, g_k, lam, g_sub, w_branch, w_o, g_mix, g_mlp, w_up, w_down):
    depth = w_in.shape[0]
    tq = MOBA_BLOCK
    tm = 512
    n_pages = page_table.shape[1]
    page = cache_k.shape[2]
    dec_seq = x_sample.shape[1]
    n_phys = cache_k.shape[1]
    assert (n_pages * page) % MOBA_BLOCK == 0 and 8 % dec_seq == 0 and x_prompt.shape[1] % tq == 0

    consts = _constants(rel_bias, tq, n_pages, page, dec_seq)
    dense = []
    for l in range(depth):
        wt = w_in[l].T.astype(BF16)
        dense.append((wt, wt[3 * D_QKV + HPM:], w_branch[l].astype(BF16), w_o[l].astype(BF16),
                      w_up[l].astype(BF16), w_down[l].astype(BF16)))
    w = (b_forget, g_q, g_k, lam, g_sub, g_mix, g_mlp, dense)
    to_pages = lambda c: jnp.transpose(c, (0, 1, 3, 4, 2)).reshape(depth, n_phys, D_QKV, page)
    past = (to_pages(cache_k), to_pages(cache_v), jnp.swapaxes(cache_logf, 2, 3), page_table)

    hp, hs = x_prompt, x_sample
    outs = [[] for _ in range(4)]
    kv_all = None
    for l in range(depth):
        hp, kp, vp, fp = _layer(l, hp, None, w, consts, tq, tm, kv_all)
        kv_all = (kp, vp)
        hs, ks, vs, fs = _layer(l, hs, past, w, consts, tq, tm)
        for lst, a in zip(outs, (fp, ks, vs, fs)):
            lst.append(a)
    b, t = x_prompt.shape[:2]
    to_heads = lambda a: jnp.transpose(a.reshape(depth, b, N_HEADS, HEAD_DIM, t), (0, 1, 4, 2, 3))
    fp, ks, vs, fs = (jnp.stack(a) for a in outs)
    return hp, hs, to_heads(kv_all[0]), to_heads(kv_all[1]), fp, ks, vs, fs
```

```python
import functools
import math

import numpy as np
import jax
import jax.numpy as jnp
from jax import lax
from jax.experimental import pallas as pl
from jax.experimental.pallas import tpu as pltpu

F32 = jnp.float32
BF16 = jnp.bfloat16

D_MODEL = 1024
HEAD_DIM = 64
N_MIXERS = 4
HPM = 4
N_HEADS = 16
MIX_WIDTH = HPM * HEAD_DIM
D_QKV = N_HEADS * HEAD_DIM
DIFF_HALF = HEAD_DIM // 2
D_FF = 4 * D_MODEL
MOBA_BLOCK = 256
MOBA_TOPK = 3
N_BUCKETS = 32
MAX_DISTANCE = 128
EPS = 1e-6
NEG = -1e30
LOG2E = 1.4426950408889634
LANES = 128
PAIR = 2 * HEAD_DIM
VMEM_LIMIT = 56 * 1024 * 1024


def _params(*sem):
    return pltpu.CompilerParams(dimension_semantics=sem, vmem_limit_bytes=VMEM_LIMIT)


def _nt(a, b):
    return lax.dot_general(a, b, (((1,), (1,)), ((), ())), preferred_element_type=F32)


def _nn(a, b):
    return jnp.dot(a, b, preferred_element_type=F32)


def _split2(x):
    hi = x.astype(BF16)
    lo = (x - hi.astype(F32)).astype(BF16)
    return hi, lo


def _split3(x):
    hi = x.astype(BF16)
    r = x - hi.astype(F32)
    mid = r.astype(BF16)
    lo = (r - mid.astype(F32)).astype(BF16)
    return hi, mid, lo


def _nn_split(x, w, parts):
    pieces = _split2(x) if parts == 2 else _split3(x)
    out = _nn(pieces[0], w)
    for p in pieces[1:]:
        out = out + _nn(p, w)
    return out


def _log_sigmoid(z):
    return jnp.minimum(z, 0.0) - jnp.log1p(jnp.exp(-jnp.abs(z)))


def _const_spec(shape):
    nd = len(shape)
    return pl.BlockSpec(shape, lambda *_: (0,) * nd)


def _t5_bucket_np(rel):
    n = np.maximum(rel, 0)
    max_exact = N_BUCKETS // 2
    nf = np.maximum(n, 1).astype(np.float32)
    large = max_exact + (np.log(nf / np.float32(max_exact)) / np.float32(math.log(MAX_DISTANCE / max_exact))
                         * np.float32(N_BUCKETS - max_exact)).astype(np.int32)
    return np.where(n < max_exact, n, np.minimum(large, N_BUCKETS - 1)).astype(np.int32)


def _qk_group_mats():
    g1 = np.zeros((D_QKV, LANES), np.float32)
    g2 = np.zeros((LANES, D_QKV), np.float32)
    for c in range(D_QKV):
        h = c // HEAD_DIM
        if h < 2 * HPM:
            gid, size = h, HEAD_DIM
        elif h < 3 * HPM:
            continue
        else:
            gid, size = 2 * HPM + (c - 3 * MIX_WIDTH) // DIFF_HALF, DIFF_HALF
        g1[c, gid] = 1.0 / size
        g2[gid, c] = 1.0
    return g1, g2


def _strict_lower(n):
    r = np.arange(n)
    return (r[:, None] > r[None, :]).astype(np.float32)


def _upper_incl(n):
    r = np.arange(n)
    return (r[:, None] <= r[None, :]).astype(np.float32)


_WF_SPEC = pl.BlockSpec((LANES, D_MODEL), lambda *_: (3 * D_QKV // LANES, 0))


def _inproj_kernel(x_ref, gmix_ref, wqkv_ref, wf_ref, bf_ref, g1_ref, g2_ref,
                   qmul_ref, qadd_ref, kmul_ref, kadd_ref,
                   qb_ref, kb_ref, vb_ref, kf_ref, vf_ref, lf_ref):
    x = x_ref[...]
    h = (x * lax.rsqrt(jnp.mean(x * x, axis=-1, keepdims=True) + EPS)) * gmix_ref[...]
    hb = h.astype(BF16)
    g1 = g1_ref[...]
    g2 = g2_ref[...]

    def group_norm(a, mul_ref, add_ref):
        ms = _nn_split(a * a, g1, 2)
        inv = lax.rsqrt(ms + EPS)
        bc = _nn_split(inv, g2, 3)
        return a * (bc * mul_ref[...] + add_ref[...])

    q = _nt(hb, wqkv_ref[0:D_QKV, :])
    qb_ref[...] = group_norm(q, qmul_ref, qadd_ref).astype(BF16)
    k = _nt(hb, wqkv_ref[D_QKV:2 * D_QKV, :])
    kn = group_norm(k, kmul_ref, kadd_ref)
    kf_ref[...] = kn
    kb_ref[...] = kn.astype(BF16)
    v = _nt(hb, wqkv_ref[2 * D_QKV:3 * D_QKV, :])
    vf_ref[...] = v
    vb_ref[...] = v.astype(BF16)
    z = _nt(hb, wf_ref[...]) + bf_ref[...]
    lf_ref[...] = _log_sigmoid(z)[:, 0:HPM]


def _inproj(x, gmix, wqkv, wf, bf, g1, g2, qmul, qadd, kmul, kadd, tm):
    rows = x.shape[0]
    row = lambda w: pl.BlockSpec((tm, w), lambda i: (i, 0))
    vec = _const_spec((1, D_QKV))
    return pl.pallas_call(
        _inproj_kernel,
        grid=(rows // tm,),
        in_specs=[row(D_MODEL), _const_spec((1, D_MODEL)), _const_spec((3 * D_QKV, D_MODEL)),
                  _WF_SPEC, _const_spec((1, LANES)),
                  _const_spec((D_QKV, LANES)), _const_spec((LANES, D_QKV)), vec, vec, vec, vec],
        out_specs=[row(D_QKV), row(D_QKV), row(D_QKV), row(D_QKV), row(D_QKV), row(HPM)],
        out_shape=[jax.ShapeDtypeStruct((rows, D_QKV), BF16)] * 3
        + [jax.ShapeDtypeStruct((rows, D_QKV), F32)] * 2 + [jax.ShapeDtypeStruct((rows, HPM), F32)],
        compiler_params=_params("parallel"),
        name="inproj",
    )(x, gmix, wqkv, wf, bf, g1, g2, qmul, qadd, kmul, kadd)


def _inproj_t_kernel(x_ref, gmix_ref, wqkv_ref, wf_ref, bf_ref, qmul_ref, qadd_ref, kmul_ref, kadd_ref, *refs):
    qt_ref, kb_ref, vt_ref, kf_ref, vf_ref, lf_ref, kn2_ref = refs[-7:]
    x = x_ref[...]
    tm = x.shape[0]
    h = (x * lax.rsqrt(jnp.mean(x * x, axis=-1, keepdims=True) + EPS)) * gmix_ref[...]
    hb = h.astype(BF16)
    wide = lambda col: jnp.concatenate([col] * (tm // LANES), axis=1)
    tq = vt_ref.shape[2]

    def group_norm(a, m, mul_ref, add_ref):
        rows = slice(m * MIX_WIDTH, (m + 1) * MIX_WIDTH)
        mul, add = wide(mul_ref[rows, :]), wide(add_ref[rows, :])
        width = {0: HEAD_DIM, 1: HEAD_DIM, 3: DIFF_HALF}.get(m)
        if width is None:
            return a * add
        g = (a * a).reshape(MIX_WIDTH // width, width, tm)
        inv = lax.rsqrt(jnp.mean(g, axis=1, keepdims=True) + EPS)
        return a * (jnp.broadcast_to(inv, g.shape).reshape(MIX_WIDTH, tm) * mul + add)

    for m in range(N_MIXERS):
        rows = slice(m * MIX_WIDTH, (m + 1) * MIX_WIDTH)
        q = _nt(wqkv_ref[m * MIX_WIDTH:(m + 1) * MIX_WIDTH, :], hb)
        qt_ref[rows, :] = group_norm(q, m, qmul_ref, qadd_ref).astype(BF16)
        k = _nt(wqkv_ref[D_QKV + m * MIX_WIDTH:D_QKV + (m + 1) * MIX_WIDTH, :], hb)
        kn = group_norm(k, m, kmul_ref, kadd_ref)
        kf_ref[rows, :] = kn
        kb_ref[:, rows] = kn.T.astype(BF16)
        kq = kn.astype(BF16).astype(F32)
        n2 = jnp.sum((kq * kq).reshape(HPM, HEAD_DIM, tm), axis=1)
        kn2_ref[m * HPM:(m + 1) * HPM, :] = jnp.broadcast_to(jnp.max(n2, axis=1, keepdims=True), (HPM, LANES))
        v = _nt(wqkv_ref[2 * D_QKV + m * MIX_WIDTH:2 * D_QKV + (m + 1) * MIX_WIDTH, :], hb)
        vf_ref[rows, :] = v
        vb = v.astype(BF16)
        for i in range(tm // tq):
            vt_ref[i, rows, :] = vb[:, i * tq:(i + 1) * tq]
    z = _nt(wf_ref[...], hb)[0:8, :] + wide(bf_ref[...])
    lf_ref[...] = _log_sigmoid(z)


def _inproj_t(x, gmix, wqkv, wf, bf_col, qmul, qadd, kmul, kadd, tm, tq, layer, depth, kv_all):
    b, t, _ = x.shape
    col = _const_spec((D_QKV, LANES))
    feat = lambda rows: pl.BlockSpec((None, rows, tm), lambda bi, i: (bi, 0, i))
    layered = pl.BlockSpec((None, None, D_QKV, tm), lambda bi, i: (layer, bi, 0, i))
    operands = [x, gmix, wqkv, wf, bf_col, qmul, qadd, kmul, kadd]
    in_specs = [pl.BlockSpec((None, tm, D_MODEL), lambda bi, i: (bi, i, 0)), _const_spec((1, D_MODEL)),
                _const_spec((3 * D_QKV, D_MODEL)), _WF_SPEC, _const_spec((8, LANES)),
                col, col, col, col]
    aliases = {}
    if kv_all is not None:
        aliases = {len(operands): 3, len(operands) + 1: 4}
        operands += list(kv_all)
        in_specs += [pl.BlockSpec(memory_space=pl.ANY)] * 2
    return pl.pallas_call(
        _inproj_t_kernel,
        grid=(b, t // tm),
        in_specs=in_specs,
        out_specs=[feat(D_QKV), pl.BlockSpec((None, tm, D_QKV), lambda bi, i: (bi, i, 0)),
                   pl.BlockSpec((None, tm // tq, D_QKV, tq), lambda bi, i: (bi, i, 0, 0)),
                   layered, layered, feat(8),
                   pl.BlockSpec((None, None, N_HEADS, LANES), lambda bi, i: (bi, i, 0, 0))],
        out_shape=[jax.ShapeDtypeStruct((b, D_QKV, t), BF16), jax.ShapeDtypeStruct((b, t, D_QKV), BF16),
                   jax.ShapeDtypeStruct((b, t // tq, D_QKV, tq), BF16),
                   jax.ShapeDtypeStruct((depth, b, D_QKV, t), F32), jax.ShapeDtypeStruct((depth, b, D_QKV, t), F32),
                   jax.ShapeDtypeStruct((b, 8, t), F32), jax.ShapeDtypeStruct((b, t // tm, N_HEADS, LANES), F32)],
        input_output_aliases=aliases,
        compiler_params=_params("parallel", "parallel"),
        name="inproj_t",
    )(*operands)


def _cumsum_kernel(lf_ref, tri_ref, c_ref, *, t, blk):
    tri = tri_ref[...]
    carry = jnp.zeros((8, 1), F32)
    for i in range(t // blk):
        cs = _nn_split(lf_ref[:, i * blk:(i + 1) * blk], tri, 3) + carry
        c_ref[:, i * blk:(i + 1) * blk] = cs
        carry = cs[:, blk - 1:blk]


def _cumsum(lf_t, tri):
    b, r, t = lf_t.shape
    blk = tri.shape[0]
    return pl.pallas_call(
        functools.partial(_cumsum_kernel, t=t, blk=blk),
        grid=(b,),
        in_specs=[pl.BlockSpec((None, r, t), lambda i: (i, 0, 0)), _const_spec((blk, blk))],
        out_specs=pl.BlockSpec((None, r, t), lambda i: (i, 0, 0)),
        out_shape=jax.ShapeDtypeStruct((b, r, t), F32),
        compiler_params=_params("parallel"),
        name="logf_cumsum",
    )(lf_t, tri)


def _means_kernel(k_ref, o_ref, *, nb):
    lane = lax.broadcasted_iota(jnp.int32, (1, LANES), 1)
    acc = jnp.zeros(o_ref.shape, F32)
    for n in range(nb):
        col = jnp.mean(k_ref[:, n * MOBA_BLOCK:(n + 1) * MOBA_BLOCK], axis=1, keepdims=True)
        acc = jnp.where(lane == n, col, acc)
    o_ref[...] = acc


def _block_means(kf_all, layer):
    _, b, _, t = kf_all.shape
    nb = t // MOBA_BLOCK
    assert nb <= LANES
    return pl.pallas_call(
        functools.partial(_means_kernel, nb=nb),
        grid=(b,),
        in_specs=[pl.BlockSpec((None, None, MIX_WIDTH, t), lambda i: (layer, i, 0, 0))],
        out_specs=pl.BlockSpec((None, MIX_WIDTH, LANES), lambda i: (i, 0, 0)),
        out_shape=jax.ShapeDtypeStruct((b, MIX_WIDTH, LANES), F32),
        compiler_params=_params("parallel"),
        name="moba_block_means",
    )(kf_all)


def _bias_kernel(tab_ref, idxp_ref, idxs_ref, bp_ref, bs_ref, *, dec_seq):
    idxp = idxp_ref[...]
    idxs = idxs_ref[...]
    row = lax.broadcasted_iota(jnp.int32, idxs.shape, 1)
    for h in range(2 * HPM):
        acc = jnp.zeros(idxp.shape, F32)
        for b in range(N_BUCKETS):
            acc = jnp.where(idxp == b, tab_ref[h, b] * LOG2E, acc)
        bp_ref[h] = acc
    for m in range(2):
        acc = jnp.zeros(idxs.shape, F32)
        for hl in range(HPM):
            for b in range(N_BUCKETS):
                acc = jnp.where((idxs == b) & (row // dec_seq == hl), tab_ref[m * HPM + hl, b], acc)
        bs_ref[m] = acc


def _bias_tiles(tab, idxp, idxs, dec_seq):
    return pl.pallas_call(
        functools.partial(_bias_kernel, dec_seq=dec_seq),
        in_specs=[pl.BlockSpec(memory_space=pltpu.SMEM), pl.BlockSpec(memory_space=pltpu.VMEM),
                  pl.BlockSpec(memory_space=pltpu.VMEM)],
        out_specs=[pl.BlockSpec(memory_space=pltpu.VMEM), pl.BlockSpec(memory_space=pltpu.VMEM)],
        out_shape=[jax.ShapeDtypeStruct((2 * HPM,) + idxp.shape, F32),
                   jax.ShapeDtypeStruct((2,) + idxs.shape, F32)],
        compiler_params=pltpu.CompilerParams(vmem_limit_bytes=VMEM_LIMIT),
        name="t5_bias_tiles",
    )(tab, idxp, idxs)


ONES_ROWS = 16
ACC_ROWS = HEAD_DIM + ONES_ROWS
EXP2_DEAD = -200.0


def _rows(n):
    return lax.broadcasted_iota(jnp.int32, (n, 1), 0)


def _key_before_query(tk, tq, strict):
    s = lax.broadcasted_iota(jnp.int32, (tk, tq), 0)
    t = lax.broadcasted_iota(jnp.int32, (tk, tq), 1)
    return (s < t) if strict else (s <= t)


def _row_range(qt, lo, hi):
    r = _rows(qt.shape[0])
    return jnp.where((r >= lo) & (r < hi), qt, jnp.zeros_like(qt))


def _v_aug(vt, x):
    return jnp.concatenate([vt[x * HEAD_DIM:(x + 1) * HEAD_DIM, :],
                            jnp.ones((ONES_ROWS, vt.shape[1]), BF16)], axis=0)


def _extra_rows(by_row, n, width):
    r = _rows(n)
    out = jnp.zeros((n, width), F32)
    for row, val in by_row.items():
        out = jnp.where(r == row, val, out)
    return out


def _pieces(values, base):
    out = {}
    for i, val in enumerate(values):
        for j, piece in enumerate(_split3(val)):
            out[base + 3 * i + j] = piece.astype(F32)
    return out


def _softmax_step(i, s, va, m_sc, acc_sc, sel=None):
    m_old = m_sc[i]
    bm = jnp.max(s, axis=0, keepdims=True)
    if sel is not None:
        bm = jnp.where(sel, bm, NEG)
    m_new = jnp.maximum(m_old, bm)
    m_sub = m_new if sel is None else jnp.where(sel, m_new, -NEG)
    p = jnp.exp2(s - m_sub)
    acc_sc[i] = acc_sc[i] * jnp.exp2(m_old - m_new) + _nn(va, p.astype(BF16))
    m_sc[i] = m_new


def _normalized(acc_sc, i):
    a = acc_sc[i]
    return a[0:HEAD_DIM, :] / a[HEAD_DIM:HEAD_DIM + 1, :]


def _init(m_sc, acc_sc):
    m_sc[...] = jnp.full(m_sc.shape, NEG, F32)
    acc_sc[...] = jnp.zeros(acc_sc.shape, F32)


def _sweep(qi, group, near, far, alive=None, pin=None):
    nf = jnp.maximum(qi - 1, 0)

    def rest(pinned):
        @pl.when(qi == 0)
        def _():
            near(1, pinned)

        @pl.when(qi > 0)
        def _():
            near(2, pinned)

        if alive is None:
            def body(i, c):
                top = nf - 1 - group * i
                far([top - g for g in range(group)], pinned)
                return c

            lax.fori_loop(0, nf // group, body, 0)
            ok = None
        else:
            def body(c):
                i, _ = c
                top = nf - 1 - group * i
                far([top - g for g in range(group)], pinned)
                return i + 1, alive(top - group + 1)

            _, ok = lax.while_loop(lambda c: (c[0] < nf // group) & c[1], body, (0, alive(nf)))
        rem = nf % group
        size = group // 2
        while size >= 1:
            top = (rem & (2 * size - 1)) - 1
            cond = (rem & size) != 0
            if ok is not None:
                cond = cond & ok

            @pl.when(cond)
            def _(size=size, top=top):
                far([top - g for g in range(size)], pinned)

            if ok is not None and size > 1:
                ok = ok & alive(rem & (size - 1))
            size //= 2

    if pin is None:
        rest(False)
    else:
        pinned = pin()

        @pl.when(pinned)
        def _():
            rest(True)

        @pl.when(jnp.logical_not(pinned))
        def _():
            rest(False)


PIN_HEADROOM = 100.0
PIN_MAX_GAP = 220.0


def _pin_reference(bounds, lows, m_sc):
    gap = bounds[0] - lows[0]
    for i in range(1, len(bounds)):
        gap = jnp.maximum(gap, bounds[i] - lows[i])
    pinned = jnp.max(gap) <= PIN_MAX_GAP

    @pl.when(pinned)
    def _():
        for i, (bound, low) in enumerate(zip(bounds, lows)):
            m_sc[i] = jnp.maximum(low, bound - PIN_HEADROOM)

    return pinned


def _own_key_scores(qt_ref, kd_ref, rows_per_map):
    prod = qt_ref[...].astype(F32) * kd_ref[...].astype(BF16).astype(F32)
    tq = prod.shape[1]
    z = jnp.sum(prod.reshape(MIX_WIDTH // rows_per_map, rows_per_map, tq), axis=1, keepdims=True)
    return z - 0.01


def _softmax_blocks(items, scores, v_of, m_sc, acc_sc, tq, pinned=False):
    ss = [scores(j, kind, i) for (j, kind, i, x, sel) in items]
    if not pinned:
        for s, (j, kind, i, x, sel) in zip(ss, items):
            if kind == "diag":
                s = jnp.where(_key_before_query(tq, tq, False), s, NEG)
            _softmax_step(i, s, _v_aug(v_of(j), x), m_sc, acc_sc, sel(j, x) if sel is not None else None)
        return
    pvs = {}
    for s, (j, kind, i, x, sel) in zip(ss, items):
        if kind == "diag":
            s = jnp.where(_key_before_query(tq, tq, False), s, NEG)
        ref = m_sc[i]
        if sel is not None:
            ref = jnp.where(sel(j, x), ref, -NEG)
        pv = _nn(_v_aug(v_of(j), x), jnp.exp2(s - ref).astype(BF16))
        pvs[i] = pv if i not in pvs else pvs[i] + pv
    for i, pv in pvs.items():
        acc_sc[i] = acc_sc[i] + pv


def _max_key_norm(kn2_ref, h):
    return jnp.sqrt(jnp.max(kn2_ref[:, h, :], axis=0, keepdims=True)[:, 0:1])


def _score_bound(q, kn, bias_max):
    qn = jnp.sqrt(jnp.sum(jnp.square(q.astype(F32)), axis=0, keepdims=True))
    return qn * kn * 1.001 + (bias_max + 0.01)


def _table_max(tab_ref, h):
    m = tab_ref[h, 0]
    for b in range(1, N_BUCKETS):
        m = jnp.maximum(m, tab_ref[h, b])
    return m * LOG2E


def _fox_kernel(qt_ref, k_ref, vt_ref, kd_ref, kn2_ref, cq_ref, call_ref, cend_ref, o_ref, kx_sc, m_sc, acc_sc,
                *, tq):
    qi = pl.program_id(1)

    def head_row(c, h):
        return c[h:h + 1, :]

    @pl.when(qi == 0)
    def _():
        c = call_ref[...] * LOG2E
        for p in range(HPM // 2):
            by_row = _pieces([head_row(c, 2 * p), head_row(c, 2 * p + 1)], 0)
            by_row.update({6: 1.0, 7: 1.0, 8: 1.0})
            e = _extra_rows(by_row, 16, c.shape[1])
            e = jnp.concatenate([e, jnp.zeros((PAIR - 16, c.shape[1]), F32)], axis=0)
            kx_sc[p] = e.T.astype(BF16)

    cq = cq_ref[...] * LOG2E
    qps, cts, zbs = [], [], []
    for h in range(HPM):
        p, x = divmod(h, 2)
        ct = head_row(cq, h)
        by_row = {3 * x: -1.0, 3 * x + 1: -1.0, 3 * x + 2: -1.0}
        by_row.update(_pieces([ct], 6))
        qx = _extra_rows(by_row, PAIR, tq).astype(BF16)
        qh = _row_range(qt_ref[p * PAIR:(p + 1) * PAIR, :], x * HEAD_DIM, (x + 1) * HEAD_DIM)
        qps.append(jnp.concatenate([qh, qx], axis=0))
        cts.append(ct)
        qn = jnp.sqrt(jnp.sum(jnp.square(qh.astype(F32)), axis=0, keepdims=True))
        zbs.append(qn * _max_key_norm(kn2_ref, h) * 1.001 + 0.001)
    _init(m_sc, acc_sc)

    def alive(low):
        lane = lax.broadcasted_iota(jnp.int32, cend_ref.shape, 1)
        cl = jnp.sum(jnp.where(lane == low - 1, cend_ref[...] * LOG2E, 0.0), axis=1, keepdims=True)
        worst = zbs[0] + cts[0] - head_row(cl, 0) - m_sc[0]
        for h in range(1, HPM):
            worst = jnp.maximum(worst, zbs[h] + cts[h] - head_row(cl, h) - m_sc[h])
        return jnp.max(worst) > EXP2_DEAD

    def scores(j, kind, h):
        r0 = pl.multiple_of(j * tq, tq)
        p = h // 2
        kp = jnp.concatenate([k_ref[pl.ds(r0, tq), p * PAIR:(p + 1) * PAIR], kx_sc[p, pl.ds(r0, tq), :]], axis=1)
        return _nn(kp, qps[h])

    def run(blocks, pinned=False):
        items = [(j, kind, h, h, None) for j, kind in blocks for h in range(HPM)]
        _softmax_blocks(items, scores, lambda j: vt_ref[j], m_sc, acc_sc, tq, pinned)

    lows = _own_key_scores(qt_ref, kd_ref, HEAD_DIM)
    _sweep(qi, 4,
           lambda n, pinned: run([(qi, "diag")] + ([(qi - 1, "far")] if n == 2 else []), pinned),
           lambda js, pinned: run([(j, "far") for j in js], pinned), alive,
           lambda: _pin_reference([zb + 0.01 for zb in zbs], [lows[h] for h in range(HPM)], m_sc))
    o = jnp.concatenate([_normalized(acc_sc, h) for h in range(HPM)], axis=0)
    o_ref[...] = o.T.astype(o_ref.dtype)


def _moba_kernel(tab_ref, qt_ref, k_ref, vt_ref, kd_ref, kn2_ref, mean_ref, bt_ref, o_ref, m_sc, acc_sc, sel_sc,
                 *, tq, nb):
    qi = pl.program_id(1)
    ones_k = jnp.ones((tq, PAIR), BF16)
    nbp = sel_sc.shape[1]
    rb = _rows(nbp)
    qs, qps, bounds = [], [], []
    means = [_split2(mean_ref[p * PAIR:(p + 1) * PAIR, :].T) for p in range(HPM // 2)]
    for h in range(HPM):
        p, x = divmod(h, 2)
        qx = _row_range(qt_ref[p * PAIR:(p + 1) * PAIR, :], x * HEAD_DIM, (x + 1) * HEAD_DIM)
        qs.append(qx)
        bounds.append(_score_bound(qx, _max_key_norm(kn2_ref, h), _table_max(tab_ref, h)))
        far = jnp.full((1, tq), tab_ref[h, N_BUCKETS - 1] * LOG2E, F32)
        qps.append(jnp.concatenate([qx, _extra_rows(_pieces([far], 0), PAIR, tq).astype(BF16)], axis=0))
    gates = [(_nn(means[h // 2][0], qs[h]) + _nn(means[h // 2][1], qs[h]))[0:nbp, :] for h in range(HPM)]
    for h, gate in enumerate(gates):
        cnt = jnp.zeros((nbp, tq), jnp.int32)
        for n in range(nb):
            gn = gate[n:n + 1, :]
            beats = (gn > gate) | ((gn == gate) & (n < rb))
            cnt = cnt + jnp.where(beats, jnp.where(n < qi, 1, 0), 0)
        sel_sc[h] = jnp.where((cnt < MOBA_TOPK) & (rb < qi), 1.0, 0.0)
    _init(m_sc, acc_sc)

    def scores(j, kind, h):
        p = h // 2
        k = k_ref[pl.ds(pl.multiple_of(j * tq, tq), tq), p * PAIR:(p + 1) * PAIR]
        if kind == "far":
            return _nn(jnp.concatenate([k, ones_k], axis=1), qps[h])
        return _nn(k, qs[h]) + bt_ref[h, 0 if kind == "diag" else 1]

    def run(blocks, pinned=False):
        items = [(j, kind, h, h, None if kind == "diag" else (lambda j, h: sel_sc[h, pl.ds(j, 1), :] > 0.0))
                 for j, kind in blocks for h in range(HPM)]
        _softmax_blocks(items, scores, lambda j: vt_ref[j], m_sc, acc_sc, tq, pinned)

    own = _own_key_scores(qt_ref, kd_ref, HEAD_DIM)
    lows = [own[h] + tab_ref[h, 0] * LOG2E for h in range(HPM)]
    _sweep(qi, 4,
           lambda n, pinned: run([(qi, "diag")] + ([(qi - 1, "sub")] if n == 2 else []), pinned),
           lambda js, pinned: run([(j, "far") for j in js], pinned),
           pin=lambda: _pin_reference(bounds, lows, m_sc))
    o = jnp.concatenate([_normalized(acc_sc, h) for h in range(HPM)], axis=0)
    o_ref[...] = o.T.astype(o_ref.dtype)


def _sb_kernel(qt_ref, k_ref, vt_ref, su_ref, o_ref, r_sc, acc_sc, *, tq):
    qi = pl.program_id(1)
    qs = [_row_range(qt_ref[(h // 2) * PAIR:(h // 2 + 1) * PAIR, :], (h % 2) * HEAD_DIM, (h % 2 + 1) * HEAD_DIM)
          for h in range(HPM)]
    su = su_ref[...]
    r_sc[...] = jnp.zeros(r_sc.shape, F32)
    acc_sc[...] = jnp.zeros(acc_sc.shape, F32)

    def run(blocks):
        tiles = [(j, masked, h) for j, masked in blocks for h in range(HPM)]
        zs = [_nn(k_ref[pl.ds(pl.multiple_of(j * tq, tq), tq), (h // 2) * PAIR:(h // 2 + 1) * PAIR], qs[h])
              for j, masked, h in tiles]
        lbs, lats = [], []
        for z, (j, masked, h) in zip(zs, tiles):
            log_beta = jnp.minimum(z, 0.0) - jnp.log2(1.0 + jnp.exp2(-jnp.abs(z)))
            log_keep = log_beta - z
            if masked:
                log_keep = jnp.where(_key_before_query(tq, tq, True), log_keep, 0.0)
            lbs.append(log_beta)
            lats.append(_nn(su, log_keep.astype(BF16)))
        r = [r_sc[h] for h in range(HPM)]
        pvs = [jnp.zeros((HEAD_DIM, tq), F32)] * HPM
        for log_beta, lat, (j, masked, h) in zip(lbs, lats, tiles):
            a = jnp.exp2(log_beta + lat[0:tq, :] + r[h])
            if masked:
                a = jnp.where(_key_before_query(tq, tq, True), a, 0.0)
            r[h] = r[h] + lat[tq:tq + 1, :]
            pvs[h] = pvs[h] + _nn(vt_ref[j][h * HEAD_DIM:(h + 1) * HEAD_DIM, :], a.astype(BF16))
        for h in range(HPM):
            r_sc[h] = r[h]
            acc_sc[h] = acc_sc[h] + pvs[h]

    def alive(low):
        return jnp.max(r_sc[...]) > EXP2_DEAD

    _sweep(qi, 2,
           lambda n, pinned: run([(qi, True)] + ([(qi - 1, False)] if n == 2 else [])),
           lambda js, pinned: run([(j, False) for j in js]), alive)
    o = jnp.concatenate([acc_sc[h] for h in range(HPM)], axis=0)
    o_ref[...] = o.T.astype(o_ref.dtype)


def _lam_value(lam_ref, lam_init):
    lm = lam_ref[...]
    a = jnp.sum(lm[0:1] * lm[1:2], axis=-1, keepdims=True)
    b = jnp.sum(lm[2:3] * lm[3:4], axis=-1, keepdims=True)
    return jnp.exp(a) - jnp.exp(b) + lam_init


def _diff_kernel(tab_ref, qt_ref, k_ref, vt_ref, kd_ref, kn2_ref, bt_ref, lam_ref, gsub_ref, o_ref, m_sc, acc_sc,
                 *, tq, lam_init):
    qi = pl.program_id(1)
    ones_k = jnp.ones((tq, PAIR), BF16)
    qs, qps, bounds = [], [], []
    for i in range(2 * HPM):
        p, r = divmod(i * DIFF_HALF, PAIR)
        qx = _row_range(qt_ref[p * PAIR:(p + 1) * PAIR, :], r, r + DIFF_HALF)
        qs.append(qx)
        bounds.append(_score_bound(qx, _max_key_norm(kn2_ref, i // 2), _table_max(tab_ref, HPM + i // 2)))
        far = jnp.full((1, tq), tab_ref[HPM + i // 2, N_BUCKETS - 1] * LOG2E, F32)
        qps.append(jnp.concatenate([qx, _extra_rows(_pieces([far], 0), PAIR, tq).astype(BF16)], axis=0))
    _init(m_sc, acc_sc)

    def scores(j, kind, i):
        p = i // 4
        k = k_ref[pl.ds(pl.multiple_of(j * tq, tq), tq), p * PAIR:(p + 1) * PAIR]
        if kind == "far":
            return _nn(jnp.concatenate([k, ones_k], axis=1), qps[i])
        return _nn(k, qs[i]) + bt_ref[i // 2, 0 if kind == "diag" else 1]

    def run(blocks, pinned=False):
        items = [(j, kind, i, i // 2, None) for j, kind in blocks for i in range(2 * HPM)]
        _softmax_blocks(items, scores, lambda j: vt_ref[j], m_sc, acc_sc, tq, pinned)

    own = _own_key_scores(qt_ref, kd_ref, DIFF_HALF)
    lows = [own[i] + tab_ref[HPM + i // 2, 0] * LOG2E for i in range(2 * HPM)]
    _sweep(qi, 4,
           lambda n, pinned: run([(qi, "diag")] + ([(qi - 1, "sub")] if n == 2 else []), pinned),
           lambda js, pinned: run([(j, "far") for j in js], pinned),
           pin=lambda: _pin_reference(bounds, lows, m_sc))
    lam_val = _lam_value(lam_ref, lam_init)
    gcol = jnp.concatenate([gsub_ref[...]] * (tq // LANES), axis=1)
    outs = []
    for x in range(HPM):
        o = _normalized(acc_sc, 2 * x) - lam_val * _normalized(acc_sc, 2 * x + 1)
        inv = lax.rsqrt(jnp.mean(o * o, axis=0, keepdims=True) + EPS)
        outs.append(((o * inv) * gcol) * (1.0 - lam_init))
    o_ref[...] = jnp.concatenate(outs, axis=0).T.astype(o_ref.dtype)


def _mixer_specs(t, tq, mixer, layer, kn2_tiles):
    nk = t // tq
    qspec = pl.BlockSpec((None, MIX_WIDTH, tq), lambda b, qi: (b, mixer, qi))
    kspec = pl.BlockSpec((None, t, MIX_WIDTH), lambda b, qi: (b, 0, mixer))
    vspec = pl.BlockSpec((None, nk, MIX_WIDTH, tq), lambda b, qi: (b, 0, mixer, 0))
    kdspec = pl.BlockSpec((None, None, MIX_WIDTH, tq), lambda b, qi: (layer, b, mixer, qi))
    knspec = pl.BlockSpec((None, kn2_tiles, None, HPM, LANES), lambda b, qi: (b, 0, mixer, 0, 0))
    ospec = pl.BlockSpec((None, tq, MIX_WIDTH), lambda b, qi: (b, qi, 0))
    return [qspec, kspec, vspec, kdspec, knspec], ospec


def _attn_out(b, t):
    return jax.ShapeDtypeStruct((b, t, MIX_WIDTH), BF16)


def _stat_scratch(n_maps, tq, rows):
    return [pltpu.VMEM((n_maps, 1, tq), F32), pltpu.VMEM((n_maps, rows, tq), F32)]


def _fox_prompt(qkv, layer, c, tq):
    b, t, _ = qkv[1].shape
    specs, ospec = _mixer_specs(t, tq, 1, layer, qkv[4].shape[1])
    cq = pl.BlockSpec((None, 8, tq), lambda bi, qi: (bi, 0, qi))
    call = pl.BlockSpec((None, 8, t), lambda bi, qi: (bi, 0, 0))
    cend = c[:, :, tq - 1::tq]
    cends = pl.BlockSpec((None, 8, t // tq), lambda bi, qi: (bi, 0, 0))
    return pl.pallas_call(
        functools.partial(_fox_kernel, tq=tq),
        grid=(b, t // tq),
        in_specs=specs + [cq, call, cends],
        out_specs=ospec,
        out_shape=_attn_out(b, t),
        scratch_shapes=[pltpu.VMEM((HPM // 2, t, PAIR), BF16)] + _stat_scratch(HPM, tq, ACC_ROWS),
        compiler_params=_params("parallel", "arbitrary"),
        name="fox_prompt",
    )(*qkv, c, c, cend)


def _moba_prompt(tab, qkv, layer, means, bp, tq):
    b, t, _ = qkv[1].shape
    nb = t // MOBA_BLOCK
    nbp = -(-nb // 8) * 8
    specs, ospec = _mixer_specs(t, tq, 0, layer, qkv[4].shape[1])
    mspec = pl.BlockSpec((None, MIX_WIDTH, LANES), lambda bi, qi: (bi, 0, 0))
    bt = pl.BlockSpec((HPM, 2, tq, tq), lambda bi, qi: (0, 0, 0, 0))
    return pl.pallas_call(
        functools.partial(_moba_kernel, tq=tq, nb=nb),
        grid=(b, t // tq),
        in_specs=[pl.BlockSpec(memory_space=pltpu.SMEM)] + specs + [mspec, bt],
        out_specs=ospec,
        out_shape=_attn_out(b, t),
        scratch_shapes=_stat_scratch(HPM, tq, ACC_ROWS) + [pltpu.VMEM((HPM, nbp, tq), F32)],
        compiler_params=_params("parallel", "arbitrary"),
        name="moba_prompt",
    )(tab, *qkv, means, bp)


def _sb_prompt(qkv, layer, su, tq):
    b, t, _ = qkv[1].shape
    specs, ospec = _mixer_specs(t, tq, 2, layer, qkv[4].shape[1])
    return pl.pallas_call(
        functools.partial(_sb_kernel, tq=tq),
        grid=(b, t // tq),
        in_specs=specs[:3] + [_const_spec(su.shape)],
        out_specs=ospec,
        out_shape=_attn_out(b, t),
        scratch_shapes=_stat_scratch(HPM, tq, HEAD_DIM),
        compiler_params=_params("parallel", "arbitrary"),
        name="sb_prompt",
    )(*qkv[:3], su)


def _diff_prompt(tab, qkv, layer, bp, lam, gsub_col, lam_init, tq):
    b, t, _ = qkv[1].shape
    specs, ospec = _mixer_specs(t, tq, 3, layer, qkv[4].shape[1])
    bt = pl.BlockSpec((HPM, 2, tq, tq), lambda bi, qi: (1, 0, 0, 0))
    return pl.pallas_call(
        functools.partial(_diff_kernel, tq=tq, lam_init=lam_init),
        grid=(b, t // tq),
        in_specs=[pl.BlockSpec(memory_space=pltpu.SMEM)] + specs
        + [bt, _const_spec((4, DIFF_HALF)), _const_spec((HEAD_DIM, LANES))],
        out_specs=ospec,
        out_shape=_attn_out(b, t),
        scratch_shapes=_stat_scratch(2 * HPM, tq, ACC_ROWS),
        compiler_params=_params("parallel", "arbitrary"),
        name="diff_prompt",
    )(tab, *qkv, bp, lam, gsub_col)


N_SLOTS = 20


def _sample_masks(dec_seq):
    rows = N_SLOTS * dec_seq
    qmask = np.zeros((rows, D_QKV), np.float32)
    for r in range(rows):
        slot = r // dec_seq
        if slot < 3 * HPM:
            qmask[r, slot * HEAD_DIM:(slot + 1) * HEAD_DIM] = 1.0
        elif slot < 4 * HPM:
            qmask[r, slot * HEAD_DIM:slot * HEAD_DIM + DIFF_HALF] = 1.0
        else:
            c0 = (slot - HPM) * HEAD_DIM + DIFF_HALF
            qmask[r, c0:c0 + DIFF_HALF] = 1.0
    omask = np.zeros((N_HEADS * dec_seq, D_QKV), np.float32)
    for r in range(N_HEADS * dec_seq):
        omask[r, (r // dec_seq) * HEAD_DIM:(r // dec_seq + 1) * HEAD_DIM] = 1.0
    return qmask, omask


def _expand_rows(a, dec_seq):
    rows = HPM * dec_seq
    r = lax.broadcasted_iota(jnp.int32, (rows, 1), 0)
    out = jnp.broadcast_to(a[HPM - 1:HPM, :], (rows, a.shape[1]))
    for h in range(HPM - 2, -1, -1):
        out = jnp.where(r < (h + 1) * dec_seq, a[h:h + 1, :], out)
    return out


def _softmax_pages(s_pages):
    m = s_pages[0].max(axis=-1, keepdims=True)
    for s in s_pages[1:]:
        m = jnp.maximum(m, s.max(axis=-1, keepdims=True))
    ps = [jnp.exp(s - m) for s in s_pages]
    l = ps[0].sum(axis=-1, keepdims=True)
    for p in ps[1:]:
        l = l + p.sum(axis=-1, keepdims=True)
    return [p / l for p in ps]


def _sample_kernel(*refs, n_pages, dec_seq, lam_init):
    pt_ref = refs[0]
    del pt_ref
    (q_ref, kn_ref, vn_ref, lfn_ref, bs_ref, qmask_ref, omask_ref, sl_ref, lam_ref,
     gd1_ref, gd2_ref, gmul_ref, gadd_ref) = refs[1:14]
    k_refs = refs[14:14 + n_pages]
    v_refs = refs[14 + n_pages:14 + 2 * n_pages]
    lf_refs = refs[14 + 2 * n_pages:14 + 3 * n_pages]
    o_ref = refs[14 + 3 * n_pages]
    s_sc, p_sc = refs[14 + 3 * n_pages + 1:]

    g = HPM * dec_seq
    npg = n_pages + 1
    q = q_ref[...]
    qbd = (jnp.concatenate([q] * (N_SLOTS * dec_seq // 8), axis=0) * qmask_ref[...]).astype(BF16)
    pad = jnp.zeros((LANES - kn_ref.shape[0], D_QKV), BF16)
    for p in range(n_pages):
        s_sc[p] = _nn(qbd, k_refs[p][...].astype(BF16))
    s_sc[n_pages] = _nt(qbd, jnp.concatenate([kn_ref[...], pad], axis=0))

    lane = lax.broadcasted_iota(jnp.int32, (1, LANES), 1)
    qrow = lax.broadcasted_iota(jnp.int32, (g, 1), 0) % dec_seq
    new_ok = lane <= qrow
    new_past = lane < qrow
    sl = sl_ref[...]

    def pages(r0):
        return [s_sc[p, r0:r0 + g, :] for p in range(npg)]

    sm = pages(0)
    ppb = MOBA_BLOCK // LANES
    nblk = n_pages // ppb
    gates = []
    for n in range(nblk):
        tot = sm[n * ppb]
        for i in range(1, ppb):
            tot = tot + sm[n * ppb + i]
        gates.append(jnp.sum(tot, axis=-1, keepdims=True))
    s_pages = []
    for n in range(nblk):
        cnt = jnp.zeros((g, 1), jnp.int32)
        for n2 in range(nblk):
            if n2 == n:
                continue
            beats = (gates[n2] > gates[n]) | ((gates[n2] == gates[n]) & (n2 < n))
            cnt = cnt + jnp.where(beats, 1, 0)
        for i in range(ppb):
            p = n * ppb + i
            s_pages.append(jnp.where(cnt < MOBA_TOPK, sm[p] + bs_ref[0, p], NEG))
    s_pages.append(jnp.where(new_ok, sm[n_pages] + bs_ref[0, n_pages], NEG))
    for p, pr in enumerate(_softmax_pages(s_pages)):
        p_sc[p, 0:g, :] = pr.astype(BF16)

    lfn = lfn_ref[...]
    cn_row = jnp.zeros((g, 1), F32)
    bias_new = jnp.zeros((g, LANES), F32)
    for n in range(dec_seq):
        col = lfn[:, n:n + 1]
        cn_row = cn_row + jnp.where(n <= qrow, col, 0.0)
        bias_new = bias_new + jnp.where((lane < n) & (n <= qrow), col, 0.0)
    lfe = [_expand_rows(lf_refs[p][...], dec_seq) for p in range(n_pages)]
    suffix = _nn_split(jnp.concatenate(lfe, axis=0), sl, 3)
    sf = pages(g)
    s_pages = [None] * npg
    s_pages[n_pages] = jnp.where(new_ok, sf[n_pages] + bias_new, NEG)
    after = cn_row
    for p in range(n_pages - 1, -1, -1):
        s_pages[p] = sf[p] + (suffix[p * g:(p + 1) * g, :] + after)
        after = after + jnp.sum(lfe[p], axis=-1, keepdims=True)
    for p, pr in enumerate(_softmax_pages(s_pages)):
        p_sc[p, g:2 * g, :] = pr.astype(BF16)

    ss = pages(2 * g)
    lbs, lks = [], []
    for p in range(npg):
        lb = _log_sigmoid(ss[p])
        lk = lb - ss[p]
        if p == n_pages:
            lk = jnp.where(new_past, lk, 0.0)
        lbs.append(lb)
        lks.append(lk)
    later_in = _nn_split(jnp.concatenate(lks, axis=0), sl, 2)
    after = jnp.zeros((g, 1), F32)
    for p in range(npg - 1, -1, -1):
        a = jnp.exp(lbs[p] + later_in[p * g:(p + 1) * g, :] + after)
        if p == n_pages:
            a = jnp.where(new_past, a, 0.0)
        p_sc[p, 2 * g:3 * g, :] = a.astype(BF16)
        after = after + jnp.sum(lks[p], axis=-1, keepdims=True)

    lam_val = _lam_value(lam_ref, lam_init)
    maps = []
    for mp in range(2):
        sd = pages((3 + mp) * g)
        s_pages = [sd[p] + bs_ref[1, p] for p in range(n_pages)]
        s_pages.append(jnp.where(new_ok, sd[n_pages] + bs_ref[1, n_pages], NEG))
        maps.append(_softmax_pages(s_pages))
    for p in range(npg):
        p_sc[p, 3 * g:4 * g, :] = (maps[0][p] - lam_val * maps[1][p]).astype(BF16)

    padv = jnp.zeros((LANES - vn_ref.shape[0], D_QKV), BF16)
    acc = _nn(p_sc[n_pages], jnp.concatenate([vn_ref[...], padv], axis=0))
    for p in range(n_pages):
        acc = acc + _nt(p_sc[p], v_refs[p][...].astype(BF16))
    rm = acc * omask_ref[...]
    y = rm[0:8, :]
    for i in range(1, N_HEADS * dec_seq // 8):
        y = y + rm[8 * i:8 * (i + 1), :]
    o8 = y + pltpu.roll(y, dec_seq, 0)
    ms = _nn_split(o8 * o8, gd1_ref[...], 2)
    bc = _nn_split(lax.rsqrt(ms + EPS), gd2_ref[...], 3)
    o8 = o8 * (bc * gmul_ref[...] + gadd_ref[...])
    o_ref[...] = o8[0:dec_seq, :]


def _sample_attention(layer, page_table, q8, kn, vn, lfn, bs, consts, lam, gmul, gadd,
                      cache_k, cache_v, cache_lft, lam_init):
    db, n_pages = page_table.shape
    dec_seq = lfn.shape[1] // HPM
    page = cache_k.shape[3]
    qmask, omask, sl, gd1, gd2 = consts
    rows = N_SLOTS * dec_seq

    def batch_spec(shape):
        nd = len(shape)
        return pl.BlockSpec((None,) + shape, lambda b, pt: (b,) + (0,) * nd)

    def const(shape):
        nd = len(shape)
        return pl.BlockSpec(shape, lambda b, pt: (0,) * nd)

    def paged(width_rows, width_cols):
        return [pl.BlockSpec((None, None, width_rows, width_cols),
                             functools.partial(lambda b, pt, j: (layer, pt[b, j], 0, 0), j=j))
                for j in range(n_pages)]

    in_specs = ([batch_spec((8, D_QKV)), batch_spec(kn.shape[1:]), batch_spec(vn.shape[1:]),
                 batch_spec(lfn.shape[1:]), const(bs.shape), const(qmask.shape), const(omask.shape),
                 const(sl.shape), const(lam.shape), const(gd1.shape), const(gd2.shape),
                 const(gmul.shape), const(gadd.shape)]
                + paged(D_QKV, page) + paged(D_QKV, page) + paged(HPM, page))
    grid_spec = pltpu.PrefetchScalarGridSpec(
        num_scalar_prefetch=1,
        grid=(db,),
        in_specs=in_specs,
        out_specs=pl.BlockSpec((None, dec_seq, D_QKV), lambda b, pt: (b, 0, 0)),
        scratch_shapes=[pltpu.VMEM((n_pages + 1, rows, LANES), F32),
                        pltpu.VMEM((n_pages + 1, N_HEADS * dec_seq, LANES), BF16)],
    )
    return pl.pallas_call(
        functools.partial(_sample_kernel, n_pages=n_pages, dec_seq=dec_seq, lam_init=lam_init),
        grid_spec=grid_spec,
        out_shape=jax.ShapeDtypeStruct((db, dec_seq, D_QKV), F32),
        compiler_params=_params("arbitrary"),
        name="sample_attention",
    )(page_table, q8, kn, vn, lfn, bs, qmask, omask, sl, lam, gd1, gd2, gmul, gadd,
      *([cache_k] * n_pages), *([cache_v] * n_pages), *([cache_lft] * n_pages))


MLP_CHUNK = 1024


def _merge_mlp_kernel(x_ref, o0_ref, o1_ref, o2_ref, o3_ref, gmix_ref, wg_ref, wbr_ref, wo_ref,
                      gmlp_ref, wup_ref, wdn_ref, y_ref):
    x = x_ref[...]
    h = (x * lax.rsqrt(jnp.mean(x * x, axis=-1, keepdims=True) + EPS)) * gmix_ref[...]
    hb = h.astype(BF16)
    acc = jnp.zeros(x.shape, F32)
    for m, o_ref in enumerate((o0_ref, o1_ref, o2_ref, o3_ref)):
        gate = 1.0 / (1.0 + jnp.exp(-_nt(hb, wg_ref[m * D_MODEL:(m + 1) * D_MODEL, :])))
        acc = acc + gate * _nn(o_ref[...].astype(BF16), wbr_ref[m])
    y = x + _nn(acc.astype(BF16), wo_ref[...])
    hm = ((y * lax.rsqrt(jnp.mean(y * y, axis=-1, keepdims=True) + EPS)) * gmlp_ref[...]).astype(BF16)
    for c in range(D_FF // MLP_CHUNK):
        u = jnp.maximum(_nn(hm, wup_ref[:, c * MLP_CHUNK:(c + 1) * MLP_CHUNK]), 0.0)
        y = y + _nn((u * u).astype(BF16), wdn_ref[c * MLP_CHUNK:(c + 1) * MLP_CHUNK, :])
    y_ref[...] = y


def _resident_spec(shape):
    nd = len(shape)
    return pl.BlockSpec(shape, lambda *_: (0,) * nd, pipeline_mode=pl.Buffered(1))


def _merge_mlp(x, os_, gmix, wg, wbr, wo, gmlp, wup, wdn, tm):
    rows = x.shape[0]
    row = lambda w: pl.BlockSpec((tm, w), lambda i: (i, 0))
    return pl.pallas_call(
        _merge_mlp_kernel,
        grid=(rows // tm,),
        in_specs=[row(D_MODEL)] + [row(MIX_WIDTH)] * 4
        + [_const_spec((1, D_MODEL)), _resident_spec((N_MIXERS * D_MODEL, D_MODEL)),
           _resident_spec((N_MIXERS, MIX_WIDTH, D_MODEL)), _resident_spec((D_MODEL, D_MODEL)),
           _const_spec((1, D_MODEL)), _resident_spec((D_MODEL, D_FF)), _resident_spec((D_FF, D_MODEL))],
        out_specs=row(D_MODEL),
        out_shape=jax.ShapeDtypeStruct((rows, D_MODEL), F32),
        compiler_params=_params("parallel"),
        name="merge_mlp",
    )(x, *os_, gmix, wg, wbr, wo, gmlp, wup, wdn)


def _qk_vectors(gq, gk, qscale):
    z = jnp.zeros((MIX_WIDTH,), F32)
    one = jnp.ones((MIX_WIDTH,), F32)
    t4 = lambda a: jnp.tile(a, HPM)
    sc = HEAD_DIM ** -0.5 * qscale
    qmul = jnp.concatenate([t4(gq[0]) * sc, t4(gq[1]) * sc, z, t4(gq[2]) * (DIFF_HALF ** -0.5 * qscale)])
    qadd = jnp.concatenate([z, z, one * sc, z])
    kmul = jnp.concatenate([t4(gk[0]), t4(gk[1]), z, t4(gk[2])])
    kadd = jnp.concatenate([z, z, one, z])
    return [a.reshape(1, D_QKV) for a in (qmul, qadd, kmul, kadd)]


def _bias_index_tiles(tq, n_pages, page, dec_seq):
    a = np.arange(tq)
    diag = _t5_bucket_np(a[None, :] - a[:, None])
    sub = _t5_bucket_np(tq + a[None, :] - a[:, None])
    idxp = np.stack([diag, sub]).astype(np.int32)
    past_len = n_pages * page
    qpos = past_len + (np.arange(HPM * dec_seq) % dec_seq)
    kpos = np.concatenate([np.arange(past_len), past_len + np.arange(page)])
    idxs = _t5_bucket_np(qpos[None, :, None] - kpos.reshape(n_pages + 1, 1, page))
    return idxp, idxs.astype(np.int32)


def _layer(l, x, past, w, consts, tq, tm, kv_all=None):
    (tab, bp, bs, g1, g2, su_q, tri_c, sl_p, sample_consts, gd) = consts
    (b_forget, g_q, g_k, lam, g_sub, g_mix, g_mlp, dense) = w
    wt, wg, wbr, wo, wup, wdn = dense[l]
    depth = len(dense)
    b, t, _ = x.shape
    rows = b * t
    lam_init = 0.8 - 0.6 * math.exp(-0.3 * l)
    wqkv = wf = wt
    gmix = g_mix[l].reshape(1, D_MODEL)
    qk_vecs = _qk_vectors(g_q[l], g_k[l], LOG2E if past is None else 1.0)
    x2 = x.reshape(rows, D_MODEL)

    if past is None:
        cols = [jnp.broadcast_to(a.reshape(D_QKV, 1), (D_QKV, LANES)) for a in qk_vecs]
        bf_col = jnp.broadcast_to(jnp.pad(b_forget[l], (0, 8 - HPM))[:, None], (8, LANES))
        qt, kb3, vt, k_new, v_new, lf_t, kn2 = _inproj_t(x, gmix, wqkv, wf, bf_col, *cols, tm=tm, tq=tq,
                                                         layer=l, depth=depth, kv_all=kv_all)
        kn2 = kn2.reshape(b, t // tm, N_MIXERS, HPM, LANES)
        lf_new = jnp.swapaxes(lf_t[:, :HPM], 1, 2)
        c = _cumsum(lf_t, tri_c)
        means = _block_means(k_new, l)
        gsub_col = jnp.broadcast_to(g_sub[l][:, None], (HEAD_DIM, LANES))
        qkv = (qt, kb3, vt, k_new, kn2)
        o_moba = _moba_prompt(tab, qkv, l, means, bp, tq)
        o_fox = _fox_prompt(qkv, l, c, tq)
        o_sb = _sb_prompt(qkv, l, su_q, tq)
        o_dif = _diff_prompt(tab, qkv, l, bp, lam[l], gsub_col, lam_init, tq)
        outs = [a.reshape(rows, MIX_WIDTH) for a in (o_moba, o_fox, o_sb, o_dif)]
    else:
        cache_k, cache_v, cache_lft, page_table = past
        bf = jnp.pad(b_forget[l], (0, LANES - HPM)).reshape(1, LANES)
        qb, kb, vb, kf, vf, lf = _inproj(x2, gmix, wqkv, wf, bf, g1, g2, *qk_vecs, tm=min(tm, rows))
        k_new, v_new = kf.reshape(b, t, N_HEADS, HEAD_DIM), vf.reshape(b, t, N_HEADS, HEAD_DIM)
        lf_new = lf.reshape(b, t, HPM)
        q4 = qb.astype(F32).reshape(b, t, D_QKV)
        q8 = jnp.concatenate([q4] * (8 // t), axis=1)
        padn = ((0, 0), (0, 16 - t), (0, 0))
        kn = jnp.pad(kb.reshape(b, t, D_QKV), padn)
        vn = jnp.pad(vb.reshape(b, t, D_QKV), padn)
        lfn = jnp.swapaxes(lf.reshape(b, t, HPM), 1, 2)
        lfn = jnp.pad(jnp.repeat(lfn, t, axis=1), ((0, 0), (0, 0), (0, 8 - t)))
        gd1, gd2 = gd
        z3 = jnp.zeros((3 * MIX_WIDTH,), F32)
        gmul = jnp.concatenate([z3, jnp.tile(g_sub[l], HPM) * (1.0 - lam_init)]).reshape(1, D_QKV)
        gadd = jnp.concatenate([z3 + 1.0, jnp.zeros((MIX_WIDTH,), F32)]).reshape(1, D_QKV)
        o = _sample_attention(l, page_table, q8, kn, vn, lfn, bs, (*sample_consts, sl_p, gd1, gd2),
                              lam[l], gmul, gadd, cache_k, cache_v, cache_lft, lam_init)
        o = o.reshape(rows, D_QKV)
        outs = [o[:, m * MIX_WIDTH:(m + 1) * MIX_WIDTH] for m in range(N_MIXERS)]

    y = _merge_mlp(x2, outs, gmix, wg, wbr, wo, g_mlp[l].reshape(1, D_MODEL), wup, wdn, tm=min(tm, rows))
    return y.reshape(b, t, D_MODEL), k_new, v_new, lf_new


def _constants(rel_bias, tq, n_pages, page, dec_seq):
    idxp, idxs = _bias_index_tiles(tq, n_pages, page, dec_seq)
    tab = rel_bias.T
    bp, bs = _bias_tiles(tab, jnp.asarray(idxp), jnp.asarray(idxs), dec_seq)
    g1, g2 = _qk_group_mats()
    qmask, omask = _sample_masks(dec_seq)
    gd1 = np.zeros((D_QKV, LANES), np.float32)
    gd2 = np.zeros((LANES, D_QKV), np.float32)
    for c in range(3 * MIX_WIDTH, D_QKV):
        gd1[c, (c - 3 * MIX_WIDTH) // HEAD_DIM] = 1.0 / HEAD_DIM
        gd2[(c - 3 * MIX_WIDTH) // HEAD_DIM, c] = 1.0
    su_q = np.concatenate([_strict_lower(tq).T, np.ones((ONES_ROWS, tq), np.float32)], axis=0)
    bf = lambda a: jnp.asarray(a, BF16)
    return (tab, bp, bs, bf(g1), bf(g2), bf(su_q), bf(_upper_incl(tq)), bf(_strict_lower(page)),
            (jnp.asarray(qmask), jnp.asarray(omask)), (bf(gd1), bf(gd2)))


def kernel(x_prompt, x_sample, cache_k, cache_v, cache_logf, page_table, rel_bias, w_in, b_forget,
           g_q, g_k, lam, g_sub, w_branch, w_o, g_mix, g_mlp, w_up, w_down):
    depth = w_in.shape[0]
    tq = MOBA_BLOCK
    tm = 512
    n_pages = page_table.shape[1]
    page = cache_k.shape[2]
    dec_seq = x_sample.shape[1]
    n_phys = cache_k.shape[1]
    assert (n_pages * page) % MOBA_BLOCK == 0 and 8 % dec_seq == 0 and x_prompt.shape[1] % tq == 0

    consts = _constants(rel_bias, tq, n_pages, page, dec_seq)
    dense = []
    for l in range(depth):
        wt = w_in[l].T.astype(BF16)
        dense.append((wt, wt[3 * D_QKV + HPM:], w_branch[l].astype(BF16), w_o[l].astype(BF16),
                      w_up[l].astype(BF16), w_down[l].astype(BF16)))
    w = (b_forget, g_q, g_k, lam, g_sub, g_mix, g_mlp, dense)
    to_pages = lambda c: jnp.transpose(c, (0, 1, 3, 4, 2)).reshape(depth, n_phys, D_QKV, page)
    past = (to_pages(cache_k), to_pages(cache_v), jnp.swapaxes(cache_logf, 2, 3), page_table)

    hp, hs = x_prompt, x_sample
    outs = [[] for _ in range(4)]
    kv_all = None
    for l in range(depth):
        hp, kp, vp, fp = _layer(l, hp, None, w, consts, tq, tm, kv_all)
        kv_all = (kp, vp)
        hs, ks, vs, fs = _layer(l, hs, past, w, consts, tq, tm)
        for lst, a in zip(outs, (fp, ks, vs, fs)):
            lst.append(a)
    b, t = x_prompt.shape[:2]
    to_heads = lambda a: jnp.transpose(a.reshape(depth, b, N_HEADS, HEAD_DIM, t), (0, 1, 4, 2, 3))
    fp, ks, vs, fs = (jnp.stack(a) for a in outs)
    return hp, hs, to_heads(kv_all[0]), to_heads(kv_all[1]), fp, ks, vs, fs
```
